```python
import math
import jax, jax.numpy as jnp
from jax import lax
import numpy as np

D_MODEL = 1024
BATCH = 4
SEQ = 4096
DEPTH = 2

HEAD_DIM = 64
EPS = 1e-6
DA_HEADS = 4
DA_QK = HEAD_DIM
DA_V = 2 * HEAD_DIM
DA_WIDTH = DA_HEADS * DA_V
Q_BLOCK = 128
ROPE_THETA = 500000.0
ROT_DIM = HEAD_DIM // 4
SC_WIDTH = 256
SC_GROUPS = 4
CONV_W = 3
GLA_HEADS = 4
GLA_DK = 32
GLA_DV = 64
GLA_KW = GLA_HEADS * GLA_DK
GLA_VW = GLA_HEADS * GLA_DV
GLA_GATE_RANK = 16
GLA_GATE_TEMP = 16.0
GLA_CHUNK = 64
MIX_WIDTH = DA_WIDTH + SC_WIDTH + GLA_VW
IN_SIZES = (DA_HEADS * 2 * DA_QK, DA_HEADS * 2 * DA_QK, DA_WIDTH,
            SC_WIDTH, SC_WIDTH, SC_WIDTH,
            GLA_KW, GLA_KW, GLA_VW, GLA_VW, GLA_GATE_RANK)
IN_WIDTH = 3088
PEER_HEADS = 8
PEER_NKEYS = 128
PEER_EXPERTS = PEER_NKEYS * PEER_NKEYS
PEER_HALF = 128
PEER_QDIM = 2 * PEER_HALF
PEER_TOPK = 16
PEER_TOKEN_BLOCK = 128

kernel_name = "hymba_style_diffattn_shortconv_gla_peer"


def rmsnorm(x, g):
    xf = x.astype(jnp.float32)
    y = xf * lax.rsqrt(jnp.mean(xf * xf, axis=-1, keepdims=True) + EPS)
    return (y * g.astype(jnp.float32)).astype(x.dtype)


def rope_tables(positions):
    inv = ROPE_THETA ** (-jnp.arange(0, ROT_DIM, 2, dtype=jnp.float32) / ROT_DIM)
    ang = positions.astype(jnp.float32)[..., None] * inv
    return jnp.cos(ang), jnp.sin(ang)


def apply_partial_rope(t, cos, sin):
    rot, rest = t[..., :ROT_DIM], t[..., ROT_DIM:]
    r1, r2 = rot[..., :ROT_DIM // 2], rot[..., ROT_DIM // 2:]
    c = cos[:, :, None, None, :].astype(t.dtype)
    s = sin[:, :, None, None, :].astype(t.dtype)
    return jnp.concatenate([r1 * c - r2 * s, r1 * s + r2 * c, rest], axis=-1)


def diff_attention(q, k, v, lam, out_scale, gain):
    B, S = q.shape[0], q.shape[1]
    nblk = S // Q_BLOCK
    qb = q.reshape(B, nblk, Q_BLOCK, DA_HEADS, 2, DA_QK).transpose(1, 0, 2, 3, 4, 5)
    kpos = jnp.arange(S)
    scale = DA_QK ** -0.5

    def block(args):
        qi, bi = args
        s = jnp.einsum('bqhmd,bkhmd->bhmqk', qi, k).astype(jnp.float32) * scale
        qpos = bi * Q_BLOCK + jnp.arange(Q_BLOCK)
        mask = kpos[None, :] <= qpos[:, None]
        p = jax.nn.softmax(jnp.where(mask, s, -jnp.inf), axis=-1)
        a = p[:, :, 0] - lam * p[:, :, 1]
        return jnp.einsum('bhqk,bkhe->bqhe', a.astype(v.dtype), v)

    o = lax.map(block, (qb, jnp.arange(nblk)))
    o = o.transpose(1, 0, 2, 3, 4).reshape(B, S, DA_HEADS, DA_V)
    o = rmsnorm(o, gain) * out_scale
    return o.reshape(B, S, DA_WIDTH)


def short_conv(b_gate, c_gate, h_in, w):
    z = c_gate * h_in
    S = z.shape[1]
    zp = jnp.pad(z, ((0, 0), (CONV_W - 1, 0), (0, 0)))
    y = w[0] * zp[:, 0:S]
    for j in range(1, CONV_W):
        y = y + w[j] * zp[:, j:j + S]
    return b_gate * y


def gla(q, k, v, log_a):
    B, S = q.shape[0], q.shape[1]
    nc = S // GLA_CHUNK

    def chunks(t):
        return t.astype(jnp.float32).reshape(B, nc, GLA_CHUNK, GLA_HEADS, t.shape[-1]).transpose(1, 0, 3, 2, 4)

    qc, kc, vc, ac = chunks(q * (GLA_DK ** -0.5)), chunks(k), chunks(v), chunks(log_a)
    causal = jnp.tril(jnp.ones((GLA_CHUNK, GLA_CHUNK), dtype=bool))

    def step(state, inp):
        qi, ki, vi, ai = inp
        b = lax.cumsum(ai, axis=2)
        diff = b[:, :, :, None, :] - b[:, :, None, :, :]
        decay = jnp.exp(jnp.where(causal[:, :, None], diff, -jnp.inf))
        attn = jnp.sum(qi[:, :, :, None, :] * ki[:, :, None, :, :] * decay, axis=-1)
        o = jnp.einsum('bhij,bhje->bhie', attn, vi) + jnp.einsum('bhid,bhde->bhie', qi * jnp.exp(b), state)
        b_last = b[:, :, -1:, :]
        state = jnp.exp(b_last[:, :, 0, :, None]) * state + jnp.einsum(
            'bhjd,bhje->bhde', ki * jnp.exp(b_last - b), vi)
        return state, o

    state0 = jnp.zeros((B, GLA_HEADS, GLA_DK, GLA_DV), jnp.float32)
    _, o = lax.scan(step, state0, (qc, kc, vc, ac))
    o = o.transpose(1, 0, 3, 2, 4).reshape(B, S, GLA_HEADS, GLA_DV)
    return o.astype(v.dtype)


def peer(xn, w_q, keys1, keys2, u, v):
    B, S, D = xn.shape
    T = B * S
    xt = xn.reshape(T, D)
    q = (xt @ w_q).reshape(T, PEER_HEADS, 2, PEER_HALF)
    s1 = jnp.einsum('thd,nd->thn', q[:, :, 0], keys1).astype(jnp.float32)
    s2 = jnp.einsum('thd,nd->thn', q[:, :, 1], keys2).astype(jnp.float32)
    v1, i1 = lax.top_k(s1, PEER_TOPK)
    v2, i2 = lax.top_k(s2, PEER_TOPK)
    cand = (v1[..., :, None] + v2[..., None, :]).reshape(T, PEER_HEADS, PEER_TOPK * PEER_TOPK)
    sc, ci = lax.top_k(cand, PEER_TOPK)
    e1 = jnp.take_along_axis(i1, ci // PEER_TOPK, axis=-1)
    e2 = jnp.take_along_axis(i2, ci % PEER_TOPK, axis=-1)
    hk = PEER_HEADS * PEER_TOPK
    experts = (e1 * PEER_NKEYS + e2).reshape(T, hk)
    gates = jax.nn.softmax(sc, axis=-1).reshape(T, hk).astype(xn.dtype)
    nb = T // PEER_TOKEN_BLOCK

    def block(args):
        xb, eb, gb = args
        ub = jnp.take(u, eb, axis=0)
        hb = jax.nn.gelu(jnp.einsum('tkd,td->tk', ub, xb), approximate=False)
        vb = jnp.take(v, eb, axis=0)
        return jnp.einsum('tk,tkd->td', gb * hb, vb)

    out = lax.map(block, (xt.reshape(nb, PEER_TOKEN_BLOCK, D),
                          experts.reshape(nb, PEER_TOKEN_BLOCK, hk),
                          gates.reshape(nb, PEER_TOKEN_BLOCK, hk)))
    return out.reshape(B, S, D)


def setup_inputs(seed: int = 0) -> dict:
    key = jax.random.key(seed)
    ks = jax.random.split(key, 24)
    f32 = jnp.float32
    L, D = DEPTH, D_MODEL
    nrm = lambda k, shape, s: jax.random.normal(k, shape, f32) * s
    return {
        "x": jax.random.normal(ks[0], (BATCH, SEQ, D), f32),
        "positions": jnp.broadcast_to(jnp.arange(SEQ, dtype=jnp.int32), (BATCH, SEQ)),
        "norm_mix": 1.0 + nrm(ks[1], (L, D), 0.02),
        "w_in": nrm(ks[2], (L, D, IN_WIDTH), D ** -0.5),
        "lam_q1": nrm(ks[3], (L, DA_QK), 0.1),
        "lam_k1": nrm(ks[4], (L, DA_QK), 0.1),
        "lam_q2": nrm(ks[5], (L, DA_QK), 0.1),
        "lam_k2": nrm(ks[6], (L, DA_QK), 0.1),
        "diff_norm": 1.0 + nrm(ks[7], (L, DA_V), 0.02),
        "conv_w": nrm(ks[8], (L, CONV_W, SC_WIDTH), CONV_W ** -0.5),
        "gla_w_gate2": nrm(ks[9], (L, GLA_GATE_RANK, GLA_KW), GLA_GATE_RANK ** -0.5),
        "gla_b_gate": nrm(ks[10], (L, GLA_KW), 0.1),
        "gla_norm": 1.0 + nrm(ks[11], (L, GLA_DV), 0.02),
        "w_out": nrm(ks[12], (L, MIX_WIDTH, D), MIX_WIDTH ** -0.5),
        "norm_ffn": 1.0 + nrm(ks[13], (L, D), 0.02),
        "peer_w_q": nrm(ks[14], (L, D, PEER_HEADS * PEER_QDIM), D ** -0.5),
        "peer_keys1": nrm(ks[15], (L, PEER_NKEYS, PEER_HALF), PEER_HALF ** -0.5),
        "peer_keys2": nrm(ks[16], (L, PEER_NKEYS, PEER_HALF), PEER_HALF ** -0.5),
        "peer_u": nrm(ks[17], (L, PEER_EXPERTS, D), D ** -0.5),
        "peer_v": nrm(ks[18], (L, PEER_EXPERTS, D), (PEER_HEADS * PEER_TOPK) ** -0.5),
        "norm_final": 1.0 + nrm(ks[19], (D,), 0.02),
    }


def reference(x, positions, norm_mix, w_in, lam_q1, lam_k1, lam_q2, lam_k2, diff_norm,
              conv_w, gla_w_gate2, gla_b_gate, gla_norm, w_out, norm_ffn, peer_w_q,
              peer_keys1, peer_keys2, peer_u, peer_v, norm_final):
    B, S, _ = x.shape
    cos, sin = rope_tables(positions)
    split_points = [int(p) for p in np.cumsum(IN_SIZES)[:-1]]
    for i in range(DEPTH):
        h = rmsnorm(x, norm_mix[i])
        proj = h @ w_in[i]
        (da_q, da_k, da_v, sc_b, sc_c, sc_h,
         g_q, g_k, g_v, g_r, g_lr) = jnp.split(proj, split_points, axis=-1)

        lam_init = 0.8 - 0.6 * math.exp(-0.3 * i)
        lam = (jnp.exp(jnp.sum(lam_q1[i].astype(jnp.float32) * lam_k1[i].astype(jnp.float32)))
               - jnp.exp(jnp.sum(lam_q2[i].astype(jnp.float32) * lam_k2[i].astype(jnp.float32)))
               + lam_init)
        q = apply_partial_rope(da_q.reshape(B, S, DA_HEADS, 2, DA_QK), cos, sin)
        k = apply_partial_rope(da_k.reshape(B, S, DA_HEADS, 2, DA_QK), cos, sin)
        o_da = diff_attention(q, k, da_v.reshape(B, S, DA_HEADS, DA_V), lam,
                              1.0 - lam_init, diff_norm[i])

        o_sc = short_conv(sc_b, sc_c, sc_h, conv_w[i])

        log_a = jax.nn.log_sigmoid((g_lr @ gla_w_gate2[i] + gla_b_gate[i]).astype(jnp.float32)) / GLA_GATE_TEMP
        o_g = gla(g_q.reshape(B, S, GLA_HEADS, GLA_DK), g_k.reshape(B, S, GLA_HEADS, GLA_DK),
                  g_v.reshape(B, S, GLA_HEADS, GLA_DV), log_a.reshape(B, S, GLA_HEADS, GLA_DK))
        o_g = rmsnorm(o_g, gla_norm[i]) * jax.nn.silu(g_r.reshape(B, S, GLA_HEADS, GLA_DV))
        o_g = o_g.reshape(B, S, GLA_VW)

        mixed = jnp.concatenate([o_da, o_sc, o_g], axis=-1)
        x = x + mixed @ w_out[i]

        x = x + peer(rmsnorm(x, norm_ffn[i]), peer_w_q[i], peer_keys1[i], peer_keys2[i],
                     peer_u[i], peer_v[i])
    return rmsnorm(x, norm_final)
```

```python
import functools
import math

import jax
import jax.numpy as jnp
from jax import lax
from jax.experimental import pallas as pl
from jax.experimental.pallas import tpu as pltpu

F32 = jnp.float32
BF16 = jnp.bfloat16

D_MODEL = 1024
EPS = 1e-6
HEAD_DIM = 64
DA_HEADS = 4
DA_WIDTH = 512
ROPE_THETA = 500000.0
ROT_DIM = 16
SC_WIDTH = 256
CONV_W = 3
GLA_HEADS = 4
GLA_DK = 32
GLA_DV = 64
GLA_KW = 128
GLA_VW = 256
GLA_GATE_RANK = 16
GLA_GATE_TEMP = 16.0
GLA_CHUNK = 64
IN_WIDTH = 3088
IN_PAD = 3200
PEER_HEADS = 8
PEER_NKEYS = 128
PEER_EXPERTS = PEER_NKEYS * PEER_NKEYS
PEER_TOPK = 16
LANES = 128
NEG_INF = float("-inf")

COL_DA_Q, COL_DA_K, COL_DA_V = 0, 1, 2
COL_SC_B, COL_SC_C, COL_SC_H = 6, 7, 8
COL_G_V, COL_G_R = 10, 11
COL_G_Q, COL_G_K, COL_G_LR = 18, 19, 24


def _cparams(sem, vmem_mb=40):
    return pltpu.CompilerParams(dimension_semantics=sem, vmem_limit_bytes=vmem_mb * 1024 * 1024)


def _rms(x, g):
    return x * lax.rsqrt(jnp.mean(x * x, axis=-1, keepdims=True) + EPS) * g


def _rms_matmul_kernel(x_ref, g_ref, w_ref, o_ref):
    xn = _rms(x_ref[...], g_ref[...]).astype(BF16)
    o_ref[...] = jnp.dot(xn, w_ref[...], preferred_element_type=F32)


def _rms_matmul(x, g, w, tm):
    t, d = x.shape
    n = w.shape[1]
    return pl.pallas_call(
        _rms_matmul_kernel,
        grid=(t // tm,),
        in_specs=[pl.BlockSpec((tm, d), lambda i: (i, 0)),
                  pl.BlockSpec((1, d), lambda i: (0, 0)),
                  pl.BlockSpec((d, n), lambda i: (0, 0))],
        out_specs=pl.BlockSpec((tm, n), lambda i: (i, 0)),
        out_shape=jax.ShapeDtypeStruct((t, n), F32),
        compiler_params=_cparams(("arbitrary",)),
    )(x, g.reshape(1, d), w)


def _rope_table_kernel(pos_ref, inv_ref, mrot_ref, m1_ref, m2_ref, c_ref, s1_ref, s2_ref):
    ang = pos_ref[...] * inv_ref[...]
    c = jnp.cos(ang)
    s = jnp.sin(ang)
    mrot = mrot_ref[...]
    c_ref[...] = mrot * c + (1.0 - mrot)
    s1_ref[...] = -(m1_ref[...] * s)
    s2_ref[...] = m2_ref[...] * s


def _rope_tables(positions, tm):
    t = positions.size
    pos = jnp.broadcast_to(positions.reshape(t, 1).astype(F32), (t, LANES))
    d = jnp.arange(LANES) % HEAD_DIM
    half = ROT_DIM // 2
    inv = ROPE_THETA ** (-jnp.arange(0, ROT_DIM, 2, dtype=F32) / ROT_DIM)
    inv_row = jnp.where(d < ROT_DIM, inv[d % half], 0.0).astype(F32).reshape(1, LANES)
    mrot = (d < ROT_DIM).astype(F32).reshape(1, LANES)
    m1 = (d < half).astype(F32).reshape(1, LANES)
    m2 = ((d >= half) & (d < ROT_DIM)).astype(F32).reshape(1, LANES)
    row = pl.BlockSpec((1, LANES), lambda i: (0, 0))
    blk = pl.BlockSpec((tm, LANES), lambda i: (i, 0))
    return pl.pallas_call(
        _rope_table_kernel,
        grid=(t // tm,),
        in_specs=[blk, row, row, row, row],
        out_specs=[blk, blk, blk],
        out_shape=[jax.ShapeDtypeStruct((t, LANES), F32)] * 3,
        compiler_params=_cparams(("arbitrary",)),
    )(pos, inv_row, mrot, m1, m2)


def _rope_prep_kernel(q_ref, k_ref, v_ref, c_ref, s1_ref, s2_ref, qo_ref, ko_ref, vo_ref, *, qscale):
    c = c_ref[...]
    s1 = s1_ref[...]
    s2 = s2_ref[...]
    half = ROT_DIM // 2
    for g in range(DA_WIDTH // LANES):
        sl = slice(LANES * g, LANES * (g + 1))
        for src, dst, sc in ((q_ref, qo_ref, qscale), (k_ref, ko_ref, 1.0)):
            x = src[:, sl]
            y = x * c + pltpu.roll(x, LANES - half, 1) * s1 + pltpu.roll(x, half, 1) * s2
            dst[:, sl] = (y * sc).astype(BF16)
    vo_ref[...] = v_ref[...].astype(BF16)


def _rope_prep(proj, tabs, tm):
    t = proj.shape[0]
    c, s1, s2 = tabs
    col = lambda j: pl.BlockSpec((tm, DA_WIDTH), lambda i, j=j: (i, j))
    tab = pl.BlockSpec((tm, LANES), lambda i: (i, 0))
    out = pl.BlockSpec((tm, DA_WIDTH), lambda i: (i, 0))
    return pl.pallas_call(
        functools.partial(_rope_prep_kernel, qscale=HEAD_DIM ** -0.5),
        grid=(t // tm,),
        in_specs=[col(COL_DA_Q), col(COL_DA_K), col(COL_DA_V), tab, tab, tab],
        out_specs=[out, out, out],
        out_shape=[jax.ShapeDtypeStruct((t, DA_WIDTH), BF16)] * 3,
        compiler_params=_cparams(("arbitrary",)),
    )(proj, proj, proj, c, s1, s2)


def _attn_kernel(q_ref, k_ref, v_ref, lq1_ref, lk1_ref, lq2_ref, lk2_ref, gain_ref, o_ref,
                 *, tq, lam_init):
    qi = pl.program_id(2)
    q = q_ref[...]
    lane = lax.broadcasted_iota(jnp.int32, (1, LANES), 1)
    zero = jnp.zeros_like(q)
    qa = jnp.where(lane < HEAD_DIM, q, zero)
    qb = jnp.where(lane >= HEAD_DIM, q, zero)
    dn = (((1,), (1,)), ((), ()))

    def update(s, m, l, a, vv):
        mn = jnp.maximum(m, jnp.max(s, axis=-1, keepdims=True))
        alpha = jnp.exp(m - mn)
        p = jnp.exp(s - mn)
        l = alpha * l + jnp.sum(p, axis=-1, keepdims=True)
        a = alpha * a + jnp.dot(p.astype(BF16), vv, preferred_element_type=F32)
        return mn, l, a

    def step(j, carry, masked):
        m1, l1, a1, m2, l2, a2 = carry
        off = pl.multiple_of(j * tq, tq)
        kk = k_ref[pl.ds(off, tq), :]
        vv = v_ref[pl.ds(off, tq), :]
        s1 = lax.dot_general(qa, kk, dn, preferred_element_type=F32)
        s2 = lax.dot_general(qb, kk, dn, preferred_element_type=F32)
        if masked:
            r = lax.broadcasted_iota(jnp.int32, (tq, tq), 0)
            c = lax.broadcasted_iota(jnp.int32, (tq, tq), 1)
            keep = c <= r
            s1 = jnp.where(keep, s1, NEG_INF)
            s2 = jnp.where(keep, s2, NEG_INF)
        m1, l1, a1 = update(s1, m1, l1, a1, vv)
        m2, l2, a2 = update(s2, m2, l2, a2, vv)
        return m1, l1, a1, m2, l2, a2

    col = lambda v: jnp.full((tq, 1), v, F32)
    acc = jnp.zeros((tq, LANES), F32)
    init = (col(NEG_INF), col(0.0), acc, col(NEG_INF), col(0.0), acc)
    carry = lax.fori_loop(0, qi, lambda j, cr: step(j, cr, False), init)
    m1, l1, a1, m2, l2, a2 = step(qi, carry, True)

    lam = (jnp.exp(jnp.sum(lq1_ref[...] * lk1_ref[...], axis=-1, keepdims=True))
           - jnp.exp(jnp.sum(lq2_ref[...] * lk2_ref[...], axis=-1, keepdims=True)) + lam_init)
    o = a1 / l1 - lam * (a2 / l2)
    o_ref[...] = _rms(o, gain_ref[...]) * (1.0 - lam_init)


def _diff_attention(qr, kr, vb, lq1, lk1, lq2, lk2, gain, batch, seq, lam_init, tq):
    t = qr.shape[0]
    nq = seq // tq
    vec = lambda n: pl.BlockSpec((1, n), lambda b, h, i: (0, 0))
    return pl.pallas_call(
        functools.partial(_attn_kernel, tq=tq, lam_init=lam_init),
        grid=(batch, DA_HEADS, nq),
        in_specs=[pl.BlockSpec((tq, LANES), lambda b, h, i: (b * nq + i, h)),
                  pl.BlockSpec((seq, LANES), lambda b, h, i: (b, h)),
                  pl.BlockSpec((seq, LANES), lambda b, h, i: (b, h)),
                  vec(HEAD_DIM), vec(HEAD_DIM), vec(HEAD_DIM), vec(HEAD_DIM), vec(LANES)],
        out_specs=pl.BlockSpec((tq, LANES), lambda b, h, i: (b * nq + i, h)),
        out_shape=jax.ShapeDtypeStruct((t, DA_WIDTH), F32),
        compiler_params=_cparams(("arbitrary", "arbitrary", "arbitrary")),
    )(qr, kr, vb, lq1.reshape(1, -1), lk1.reshape(1, -1), lq2.reshape(1, -1), lk2.reshape(1, -1),
      gain.reshape(1, -1))


def _short_conv_kernel(b_ref, c_ref, h_ref, cp_ref, hp_ref, w_ref, o_ref, *, tm, blocks_per_seq):
    i = pl.program_id(0)
    z = c_ref[...] * h_ref[...]
    zp = cp_ref[...] * hp_ref[...]
    first = (i % blocks_per_seq) == 0
    zp = jnp.where(first, jnp.zeros_like(zp), zp)
    row = lax.broadcasted_iota(jnp.int32, (tm, 1), 0)
    z1 = jnp.where(row == 0, zp[7:8, :], pltpu.roll(z, 1, 0))
    z2 = pltpu.roll(z, 2, 0)
    z2 = jnp.where(row == 0, zp[6:7, :], jnp.where(row == 1, zp[7:8, :], z2))
    w = w_ref[...]
    y = w[0:1, :] * z2 + w[1:2, :] * z1 + w[2:3, :] * z
    o_ref[...] = b_ref[...] * y


def _short_conv(proj, w, seq, tm):
    t = proj.shape[0]
    rb = tm // 8
    cur = lambda j: pl.BlockSpec((tm, SC_WIDTH), lambda i, j=j: (i, j))
    prev = lambda j: pl.BlockSpec((8, SC_WIDTH), lambda i, j=j: (jnp.maximum(i * rb - 1, 0), j))
    return pl.pallas_call(
        functools.partial(_short_conv_kernel, tm=tm, blocks_per_seq=seq // tm),
        grid=(t // tm,),
        in_specs=[cur(COL_SC_B), cur(COL_SC_C), cur(COL_SC_H), prev(COL_SC_C), prev(COL_SC_H),
                  pl.BlockSpec((CONV_W, SC_WIDTH), lambda i: (0, 0))],
        out_specs=pl.BlockSpec((tm, SC_WIDTH), lambda i: (i, 0)),
        out_shape=jax.ShapeDtypeStruct((t, SC_WIDTH), F32),
        compiler_params=_cparams(("arbitrary",)),
    )(proj, proj, proj, proj, proj, w)


def _split3(x):
    hi = x.astype(BF16)
    r = x - hi.astype(F32)
    mid = r.astype(BF16)
    lo = (r - mid.astype(F32)).astype(BF16)
    return hi, mid, lo


def _gla_kernel(q_ref, k_ref, v_ref, r_ref, lr_ref, wg_ref, bg_ref, gn_ref, o_ref, st_ref, *, tc):
    ch = GLA_CHUNK

    @pl.when(pl.program_id(1) == 0)
    def _():
        st_ref[...] = jnp.zeros_like(st_ref)

    ii = lax.broadcasted_iota(jnp.int32, (ch, ch), 0)
    jj = lax.broadcasted_iota(jnp.int32, (ch, ch), 1)
    tri = (ii >= jj).astype(BF16)
    rk = lax.broadcasted_iota(jnp.int32, (GLA_VW, GLA_KW), 0)
    ck = lax.broadcasted_iota(jnp.int32, (GLA_VW, GLA_KW), 1)
    bd_k = (rk // GLA_DV) == (ck // GLA_DK)
    rv = lax.broadcasted_iota(jnp.int32, (GLA_VW, GLA_VW), 0)
    cv = lax.broadcasted_iota(jnp.int32, (GLA_VW, GLA_VW), 1)
    bd_v = (rv // GLA_DV) == (cv // GLA_DV)
    seg = bd_v.astype(BF16)
    ri = lax.broadcasted_iota(jnp.int32, (ch, GLA_VW), 0)
    ci = lax.broadcasted_iota(jnp.int32, (ch, GLA_VW), 1)
    causal = (ci % ch) <= ri
    dn_t = (((1,), (1,)), ((), ()))
    scale = GLA_DK ** -0.5

    def chunk(c, carry):
        off = pl.multiple_of(c * ch, ch)
        rows = pl.ds(off, ch)
        x = jnp.dot(lr_ref[rows, :].astype(BF16), wg_ref[...], preferred_element_type=F32) + bg_ref[...]
        la = (jnp.minimum(x, 0.0) - jnp.log1p(jnp.exp(-jnp.abs(x)))) * (1.0 / GLA_GATE_TEMP)
        hi, mid, lo = _split3(la)
        b = (jnp.dot(tri, hi, preferred_element_type=F32) + jnp.dot(tri, mid, preferred_element_type=F32)
             + jnp.dot(tri, lo, preferred_element_type=F32))
        b_mid = b[ch // 2 - 1:ch // 2, :]
        b_last = b[ch - 1:ch, :]
        q = q_ref[rows, :] * scale
        k = k_ref[rows, :]
        v = v_ref[rows, :]
        qt = (q * jnp.exp(b - b_mid)).astype(BF16)
        kt = (k * jnp.exp(b_mid - b)).astype(BF16)
        kbd = jnp.where(bd_k, jnp.concatenate([kt] * GLA_HEADS, axis=0), jnp.zeros((), BF16))
        att = lax.dot_general(qt, kbd, dn_t, preferred_element_type=F32)
        att = jnp.where(causal, att, 0.0).astype(BF16)
        vb = v.astype(BF16)
        vbd = jnp.where(bd_v, jnp.concatenate([vb] * GLA_HEADS, axis=0), jnp.zeros((), BF16))
        o = jnp.dot(att, vbd, preferred_element_type=F32)
        st = st_ref[...]
        bq = (q * jnp.exp(b)).astype(BF16)
        o = o + lax.dot_general(bq, st.astype(BF16), dn_t, preferred_element_type=F32)
        kl = (k * jnp.exp(b_last - b)).astype(BF16)
        ut = jnp.dot(v.T.astype(BF16), kl, preferred_element_type=F32)
        st_ref[...] = st * jnp.exp(b_last) + jnp.where(bd_k, ut, 0.0)
        o2 = o * o
        o2h = o2.astype(BF16)
        o2l = (o2 - o2h.astype(F32)).astype(BF16)
        ms = (jnp.dot(o2h, seg, preferred_element_type=F32)
              + jnp.dot(o2l, seg, preferred_element_type=F32)) * (1.0 / GLA_DV)
        r = r_ref[rows, :]
        gate = r * (1.0 / (1.0 + jnp.exp(-r)))
        o_ref[rows, :] = o * lax.rsqrt(ms + EPS) * gn_ref[...] * gate
        return carry

    lax.fori_loop(0, tc // ch, chunk, 0)


def _gla(proj, wg, bg, gn, batch, seq, tc):
    t = proj.shape[0]
    nb = seq // tc
    blk = lambda w, j: pl.BlockSpec((tc, w), lambda b, i, j=j: (b * nb + i, j))
    const = lambda r, c: pl.BlockSpec((r, c), lambda b, i: (0, 0))
    wg_pad = jnp.zeros((LANES, GLA_KW), F32).at[:GLA_GATE_RANK].set(wg).astype(BF16)
    gn_row = jnp.tile(gn, GLA_HEADS).reshape(1, GLA_VW)
    return pl.pallas_call(
        functools.partial(_gla_kernel, tc=tc),
        grid=(batch, nb),
        in_specs=[blk(GLA_KW, COL_G_Q), blk(GLA_KW, COL_G_K), blk(GLA_VW, COL_G_V), blk(GLA_VW, COL_G_R),
                  blk(LANES, COL_G_LR), const(LANES, GLA_KW), const(1, GLA_KW), const(1, GLA_VW)],
        out_specs=pl.BlockSpec((tc, GLA_VW), lambda b, i: (b * nb + i, 0)),
        out_shape=jax.ShapeDtypeStruct((t, GLA_VW), F32),
        scratch_shapes=[pltpu.VMEM((GLA_VW, GLA_KW), F32)],
        compiler_params=_cparams(("arbitrary", "arbitrary")),
    )(proj, proj, proj, proj, proj, wg_pad, bg.reshape(1, GLA_KW), gn_row)


def _out_proj_kernel(x_ref, a_ref, s_ref, g_ref, w_ref, o_ref):
    acc = jnp.dot(a_ref[...].astype(BF16), w_ref[0:DA_WIDTH, :], preferred_element_type=F32)
    acc += jnp.dot(s_ref[...].astype(BF16), w_ref[DA_WIDTH:DA_WIDTH + SC_WIDTH, :],
                   preferred_element_type=F32)
    acc += jnp.dot(g_ref[...].astype(BF16), w_ref[DA_WIDTH + SC_WIDTH:, :], preferred_element_type=F32)
    o_ref[...] = x_ref[...] + acc


def _out_proj(x, o_da, o_sc, o_g, w, tm):
    t, d = x.shape
    blk = lambda n: pl.BlockSpec((tm, n), lambda i: (i, 0))
    return pl.pallas_call(
        _out_proj_kernel,
        grid=(t // tm,),
        in_specs=[blk(d), blk(DA_WIDTH), blk(SC_WIDTH), blk(GLA_VW),
                  pl.BlockSpec(w.shape, lambda i: (0, 0))],
        out_specs=blk(d),
        out_shape=jax.ShapeDtypeStruct((t, d), F32),
        compiler_params=_cparams(("arbitrary",)),
    )(x, o_da, o_sc, o_g, w)


def _peer_scores_kernel(x_ref, g_ref, wq_ref, k1_ref, k2_ref, s1_ref, s2_ref):
    xn = _rms(x_ref[...], g_ref[...]).astype(BF16)
    q = jnp.dot(xn, wq_ref[...], preferred_element_type=F32)
    dn = (((1,), (1,)), ((), ()))
    for h in range(PEER_HEADS):
        qa = q[:, (2 * h) * LANES:(2 * h + 1) * LANES].astype(BF16)
        qb = q[:, (2 * h + 1) * LANES:(2 * h + 2) * LANES].astype(BF16)
        s1_ref[h] = lax.dot_general(k1_ref[...], qa, dn, preferred_element_type=F32)
        s2_ref[h] = lax.dot_general(k2_ref[...], qb, dn, preferred_element_type=F32)


def _peer_scores(x, g, wq, k1, k2, tm):
    t, d = x.shape
    out = pl.BlockSpec((PEER_HEADS, PEER_NKEYS, tm), lambda i: (0, 0, i))
    return pl.pallas_call(
        _peer_scores_kernel,
        grid=(t // tm,),
        in_specs=[pl.BlockSpec((tm, d), lambda i: (i, 0)),
                  pl.BlockSpec((1, d), lambda i: (0, 0)),
                  pl.BlockSpec(wq.shape, lambda i: (0, 0)),
                  pl.BlockSpec(k1.shape, lambda i: (0, 0)),
                  pl.BlockSpec(k2.shape, lambda i: (0, 0))],
        out_specs=[out, out],
        out_shape=[jax.ShapeDtypeStruct((PEER_HEADS, PEER_NKEYS, t), F32)] * 2,
        compiler_params=_cparams(("arbitrary",)),
    )(x, g.reshape(1, d), wq, k1, k2)


def _top_desc(x, n):
    vals = []
    cur = x
    for r in range(n):
        m = jnp.max(cur, axis=0, keepdims=True)
        vals.append(m)
        if r + 1 < n:
            cur = jnp.where(cur == m, NEG_INF, cur)
    return vals


def _peer_tables_kernel(s1_ref, s2_ref, a1_ref, a2_ref, thr_ref):
    k = PEER_TOPK
    tt = s1_ref.shape[-1]
    for h in range(PEER_HEADS):
        s1 = s1_ref[h]
        s2 = s2_ref[h]
        v1 = _top_desc(s1, k)
        v2 = _top_desc(s2, k)
        cands = [v1[i] + v2[j] for i in range(k) for j in range(k // (i + 1))]
        pad = (-len(cands)) % 8
        cands += [jnp.full((1, tt), NEG_INF, F32)] * pad
        top = _top_desc(jnp.concatenate(cands, axis=0), k)
        z = jnp.ones((1, tt), F32)
        for r in range(1, k):
            z = z + jnp.exp(top[r] - top[0])
        thr_ref[h:h + 1, :] = top[k - 1]
        a1_ref[h] = jnp.exp(s1 - v1[0]) * (1.0 / z)
        a2_ref[h] = jnp.exp(s2 - v2[0])


def _peer_tables(s1t, s2t, tt):
    t = s1t.shape[-1]
    blk = pl.BlockSpec((PEER_HEADS, PEER_NKEYS, tt), lambda i: (0, 0, i))
    return pl.pallas_call(
        _peer_tables_kernel,
        grid=(t // tt,),
        in_specs=[blk, blk],
        out_specs=[blk, blk, pl.BlockSpec((PEER_HEADS, tt), lambda i: (0, i))],
        out_shape=[jax.ShapeDtypeStruct(s1t.shape, F32), jax.ShapeDtypeStruct(s1t.shape, F32),
                   jax.ShapeDtypeStruct((PEER_HEADS, t), F32)],
        compiler_params=_cparams(("arbitrary",)),
    )(s1t, s2t)


def _peer_dense_kernel(x_ref, g_ref, u_ref, vt_ref, s1_ref, a1_ref, s2_ref, a2_ref, thr_ref, o_ref,
                       xnt_ref, pt_ref, acc_ref, *, n1_per_tile):
    e = pl.program_id(1)

    @pl.when(e == 0)
    def _():
        xn = _rms(x_ref[...], g_ref[...])
        xnt_ref[...] = xn.T.astype(BF16)
        acc_ref[...] = jnp.zeros_like(acc_ref)

    ht = jnp.dot(u_ref[...], xnt_ref[...], preferred_element_type=F32)
    inv_sqrt2 = 1.0 / math.sqrt(2.0)
    for r in range(n1_per_tile):
        rows = slice(r * PEER_NKEYS, (r + 1) * PEER_NKEYS)
        hb = ht[rows, :]
        w = jnp.zeros_like(hb)
        for h in range(PEER_HEADS):
            sel = (s1_ref[h, r:r + 1, :] + s2_ref[h]) >= thr_ref[h:h + 1, :]
            w = w + a1_ref[h, r:r + 1, :] * jnp.where(sel, a2_ref[h], 0.0)
        gelu = 0.5 * hb * (1.0 + lax.erf(hb * inv_sqrt2))
        pt_ref[rows, :] = (w * gelu).astype(BF16)
    acc_ref[...] += jnp.dot(vt_ref[...], pt_ref[...], preferred_element_type=F32)

    @pl.when(e == pl.num_programs(1) - 1)
    def _():
        o_ref[...] = x_ref[...] + acc_ref[...].T


def _peer_dense(x, g, u, vt, s1t, a1t, s2t, a2t, thr, tt, et):
    t, d = x.shape
    n_exp = u.shape[0]
    n1_per_tile = et // PEER_NKEYS
    full = pl.BlockSpec((PEER_HEADS, PEER_NKEYS, tt), lambda i, e: (0, 0, i))
    part = pl.BlockSpec((PEER_HEADS, n1_per_tile, tt), lambda i, e: (0, e, i))
    return pl.pallas_call(
        functools.partial(_peer_dense_kernel, n1_per_tile=n1_per_tile),
        grid=(t // tt, n_exp // et),
        in_specs=[pl.BlockSpec((tt, d), lambda i, e: (i, 0)),
                  pl.BlockSpec((1, d), lambda i, e: (0, 0)),
                  pl.BlockSpec((et, d), lambda i, e: (e, 0)),
                  pl.BlockSpec((d, et), lambda i, e: (0, e)),
                  part, part, full, full,
                  pl.BlockSpec((PEER_HEADS, tt), lambda i, e: (0, i))],
        out_specs=pl.BlockSpec((tt, d), lambda i, e: (i, 0)),
        out_shape=jax.ShapeDtypeStruct((t, d), F32),
        scratch_shapes=[pltpu.VMEM((d, tt), BF16), pltpu.VMEM((et, tt), BF16), pltpu.VMEM((d, tt), F32)],
        compiler_params=_cparams(("arbitrary", "arbitrary"), vmem_mb=48),
    )(x, g.reshape(1, d), u, vt, s1t, a1t, s2t, a2t, thr)


def _rmsnorm_kernel(x_ref, g_ref, o_ref):
    o_ref[...] = _rms(x_ref[...], g_ref[...])


def _rmsnorm(x, g, tm):
    t, d = x.shape
    return pl.pallas_call(
        _rmsnorm_kernel,
        grid=(t // tm,),
        in_specs=[pl.BlockSpec((tm, d), lambda i: (i, 0)), pl.BlockSpec((1, d), lambda i: (0, 0))],
        out_specs=pl.BlockSpec((tm, d), lambda i: (i, 0)),
        out_shape=jax.ShapeDtypeStruct((t, d), F32),
        compiler_params=_cparams(("arbitrary",)),
    )(x, g.reshape(1, d))


def _tiles(seq):
    pick = lambda want: min(want, seq)
    return dict(tm=pick(256), tq=pick(512), tconv=pick(512), tgla=pick(512), tt=pick(256), et=1024)


def kernel(x, positions, norm_mix, w_in, lam_q1, lam_k1, lam_q2, lam_k2, diff_norm, conv_w, gla_w_gate2, gla_b_gate, gla_norm, w_out, norm_ffn, peer_w_q, peer_keys1, peer_keys2, peer_u, peer_v, norm_final):
    batch, seq, d = x.shape
    t = batch * seq
    depth = w_in.shape[0]
    tl = _tiles(seq)
    xt = x.reshape(t, d)
    tabs = _rope_tables(positions, tl["tm"])
    for i in range(depth):
        w_in_p = jnp.pad(w_in[i], ((0, 0), (0, IN_PAD - IN_WIDTH))).astype(BF16)
        proj = _rms_matmul(xt, norm_mix[i], w_in_p, tl["tm"])
        qr, kr, vb = _rope_prep(proj, tabs, tl["tm"])
        lam_init = 0.8 - 0.6 * math.exp(-0.3 * i)
        o_da = _diff_attention(qr, kr, vb, lam_q1[i], lam_k1[i], lam_q2[i], lam_k2[i], diff_norm[i],
                               batch, seq, lam_init, tl["tq"])
        o_sc = _short_conv(proj, conv_w[i], seq, tl["tconv"])
        o_g = _gla(proj, gla_w_gate2[i], gla_b_gate[i], gla_norm[i], batch, seq, tl["tgla"])
        xt = _out_proj(xt, o_da, o_sc, o_g, w_out[i].astype(BF16), tl["tm"])
        s1t, s2t = _peer_scores(xt, norm_ffn[i], peer_w_q[i].astype(BF16), peer_keys1[i].astype(BF16),
                                peer_keys2[i].astype(BF16), tl["tm"])
        a1t, a2t, thr = _peer_tables(s1t, s2t, tl["tt"])
        xt = _peer_dense(xt, norm_ffn[i], peer_u[i].astype(BF16), peer_v[i].T.astype(BF16),
                         s1t, a1t, s2t, a2t, thr, tl["tt"], tl["et"])
    return _rmsnorm(xt, norm_final, tl["tm"]).reshape(batch, seq, d)
```

```python
import functools
import math

import jax
import jax.numpy as jnp
from jax import lax
from jax.experimental import pallas as pl
from jax.experimental.pallas import tpu as pltpu

F32 = jnp.float32
BF16 = jnp.bfloat16

D_MODEL = 1024
EPS = 1e-6
HEAD_DIM = 64
DA_HEADS = 4
DA_WIDTH = 512
ROPE_THETA = 500000.0
ROT_DIM = 16
SC_WIDTH = 256
CONV_W = 3
GLA_HEADS = 4
GLA_DK = 32
GLA_DV = 64
GLA_KW = 128
GLA_VW = 256
GLA_GATE_RANK = 16
GLA_GATE_TEMP = 16.0
GLA_CHUNK = 64
IN_WIDTH = 3088
IN_PAD = 3200
PEER_HEADS = 8
PEER_NKEYS = 128
PEER_EXPERTS = PEER_NKEYS * PEER_NKEYS
PEER_TOPK = 16
LANES = 128
BF16_SUBLANES = 16
NEG_INF = float("-inf")

COL_DA_Q, COL_DA_K, COL_DA_V = 0, 1, 2
COL_SC_B, COL_SC_C, COL_SC_H = 6, 7, 8
COL_G_V, COL_G_R = 10, 11
COL_G_Q, COL_G_K, COL_G_LR = 18, 19, 24


def _cparams(sem, vmem_mb=40):
    return pltpu.CompilerParams(dimension_semantics=sem, vmem_limit_bytes=vmem_mb * 1024 * 1024)


def _rms(x, g):
    return x * lax.rsqrt(jnp.mean(x * x, axis=-1, keepdims=True) + EPS) * g


def _rms_matmul_kernel(x_ref, g_ref, w_ref, o_ref):
    xn = _rms(x_ref[...], g_ref[...]).astype(BF16)
    o_ref[...] = jnp.dot(xn, w_ref[...], preferred_element_type=F32)


def _rms_matmul(x, g, w, tm):
    t, d = x.shape
    n = w.shape[1]
    return pl.pallas_call(
        _rms_matmul_kernel,
        grid=(t // tm,),
        in_specs=[pl.BlockSpec((tm, d), lambda i: (i, 0)),
                  pl.BlockSpec((1, d), lambda i: (0, 0)),
                  pl.BlockSpec((d, n), lambda i: (0, 0))],
        out_specs=pl.BlockSpec((tm, n), lambda i: (i, 0)),
        out_shape=jax.ShapeDtypeStruct((t, n), F32),
        compiler_params=_cparams(("arbitrary",)),
    )(x, g.reshape(1, d), w)


def _rope_table_kernel(pos_ref, inv_ref, mrot_ref, m1_ref, m2_ref, c_ref, s1_ref, s2_ref):
    ang = pos_ref[...] * inv_ref[...]
    c = jnp.cos(ang)
    s = jnp.sin(ang)
    mrot = mrot_ref[...]
    c_ref[...] = mrot * c + (1.0 - mrot)
    s1_ref[...] = -(m1_ref[...] * s)
    s2_ref[...] = m2_ref[...] * s


def _rope_tables(positions, tm):
    t = positions.size
    pos = jnp.broadcast_to(positions.reshape(t, 1).astype(F32), (t, LANES))
    d = jnp.arange(LANES) % HEAD_DIM
    half = ROT_DIM // 2
    inv = ROPE_THETA ** (-jnp.arange(0, ROT_DIM, 2, dtype=F32) / ROT_DIM)
    inv_row = jnp.where(d < ROT_DIM, inv[d % half], 0.0).astype(F32).reshape(1, LANES)
    mrot = (d < ROT_DIM).astype(F32).reshape(1, LANES)
    m1 = (d < half).astype(F32).reshape(1, LANES)
    m2 = ((d >= half) & (d < ROT_DIM)).astype(F32).reshape(1, LANES)
    row = pl.BlockSpec((1, LANES), lambda i: (0, 0))
    blk = pl.BlockSpec((tm, LANES), lambda i: (i, 0))
    return pl.pallas_call(
        _rope_table_kernel,
        grid=(t // tm,),
        in_specs=[blk, row, row, row, row],
        out_specs=[blk, blk, blk],
        out_shape=[jax.ShapeDtypeStruct((t, LANES), F32)] * 3,
        compiler_params=_cparams(("arbitrary",)),
    )(pos, inv_row, mrot, m1, m2)


def _rope_prep_kernel(q_ref, k_ref, v_ref, c_ref, s1_ref, s2_ref, qo_ref, ko_ref, vo_ref, *, qscale):
    c = c_ref[...]
    s1 = s1_ref[...]
    s2 = s2_ref[...]
    half = ROT_DIM // 2
    for g in range(DA_WIDTH // LANES):
        sl = slice(LANES * g, LANES * (g + 1))
        for src, dst, sc in ((q_ref, qo_ref, qscale), (k_ref, ko_ref, 1.0)):
            x = src[:, sl]
            y = x * c + pltpu.roll(x, LANES - half, 1) * s1 + pltpu.roll(x, half, 1) * s2
            dst[:, sl] = (y * sc).astype(BF16)
    vo_ref[...] = v_ref[...].astype(BF16)


def _rope_prep(proj, tabs, tm):
    t = proj.shape[0]
    c, s1, s2 = tabs
    col = lambda j: pl.BlockSpec((tm, DA_WIDTH), lambda i, j=j: (i, j))
    tab = pl.BlockSpec((tm, LANES), lambda i: (i, 0))
    out = pl.BlockSpec((tm, DA_WIDTH), lambda i: (i, 0))
    return pl.pallas_call(
        functools.partial(_rope_prep_kernel, qscale=HEAD_DIM ** -0.5),
        grid=(t // tm,),
        in_specs=[col(COL_DA_Q), col(COL_DA_K), col(COL_DA_V), tab, tab, tab],
        out_specs=[out, out, out],
        out_shape=[jax.ShapeDtypeStruct((t, DA_WIDTH), BF16)] * 3,
        compiler_params=_cparams(("arbitrary",)),
    )(proj, proj, proj, c, s1, s2)


def _attn_kernel(q_ref, k_ref, v_ref, lq1_ref, lk1_ref, lq2_ref, lk2_ref, gain_ref, o_ref,
                 *, tq, lam_init):
    qi = pl.program_id(2)
    q = q_ref[...]
    lane = lax.broadcasted_iota(jnp.int32, (1, LANES), 1)
    zero = jnp.zeros_like(q)
    qa = jnp.where(lane < HEAD_DIM, q, zero)
    qb = jnp.where(lane >= HEAD_DIM, q, zero)
    dn = (((1,), (1,)), ((), ()))

    def update(s, m, l, a, vv):
        mn = jnp.maximum(m, jnp.max(s, axis=-1, keepdims=True))
        alpha = jnp.exp(m - mn)
        p = jnp.exp(s - mn)
        l = alpha * l + jnp.sum(p, axis=-1, keepdims=True)
        a = alpha * a + jnp.dot(p.astype(BF16), vv, preferred_element_type=F32)
        return mn, l, a

    def step(j, carry, masked):
        m1, l1, a1, m2, l2, a2 = carry
        off = pl.multiple_of(j * tq, tq)
        kk = k_ref[pl.ds(off, tq), :]
        vv = v_ref[pl.ds(off, tq), :]
        s1 = lax.dot_general(qa, kk, dn, preferred_element_type=F32)
        s2 = lax.dot_general(qb, kk, dn, preferred_element_type=F32)
        if masked:
            r = lax.broadcasted_iota(jnp.int32, (tq, tq), 0)
            c = lax.broadcasted_iota(jnp.int32, (tq, tq), 1)
            keep = c <= r
            s1 = jnp.where(keep, s1, NEG_INF)
            s2 = jnp.where(keep, s2, NEG_INF)
        m1, l1, a1 = update(s1, m1, l1, a1, vv)
        m2, l2, a2 = update(s2, m2, l2, a2, vv)
        return m1, l1, a1, m2, l2, a2

    col = lambda v: jnp.full((tq, 1), v, F32)
    acc = jnp.zeros((tq, LANES), F32)
    init = (col(NEG_INF), col(0.0), acc, col(NEG_INF), col(0.0), acc)
    carry = lax.fori_loop(0, qi, lambda j, cr: step(j, cr, False), init)
    m1, l1, a1, m2, l2, a2 = step(qi, carry, True)

    lam = (jnp.exp(jnp.sum(lq1_ref[...] * lk1_ref[...], axis=-1, keepdims=True))
           - jnp.exp(jnp.sum(lq2_ref[...] * lk2_ref[...], axis=-1, keepdims=True)) + lam_init)
    o = a1 / l1 - lam * (a2 / l2)
    o_ref[...] = _rms(o, gain_ref[...]) * (1.0 - lam_init)


def _diff_attention(qr, kr, vb, lq1, lk1, lq2, lk2, gain, batch, seq, lam_init, tq):
    t = qr.shape[0]
    nq = seq // tq
    vec = lambda n: pl.BlockSpec((1, n), lambda b, h, i: (0, 0))
    return pl.pallas_call(
        functools.partial(_attn_kernel, tq=tq, lam_init=lam_init),
        grid=(batch, DA_HEADS, nq),
        in_specs=[pl.BlockSpec((tq, LANES), lambda b, h, i: (b * nq + i, h)),
                  pl.BlockSpec((seq, LANES), lambda b, h, i: (b, h)),
                  pl.BlockSpec((seq, LANES), lambda b, h, i: (b, h)),
                  vec(HEAD_DIM), vec(HEAD_DIM), vec(HEAD_DIM), vec(HEAD_DIM), vec(LANES)],
        out_specs=pl.BlockSpec((tq, LANES), lambda b, h, i: (b * nq + i, h)),
        out_shape=jax.ShapeDtypeStruct((t, DA_WIDTH), F32),
        compiler_params=_cparams(("arbitrary", "arbitrary", "arbitrary")),
    )(qr, kr, vb, lq1.reshape(1, -1), lk1.reshape(1, -1), lq2.reshape(1, -1), lk2.reshape(1, -1),
      gain.reshape(1, -1))


def _short_conv_kernel(b_ref, c_ref, h_ref, cp_ref, hp_ref, w_ref, o_ref, *, tm, blocks_per_seq):
    i = pl.program_id(0)
    z = c_ref[...] * h_ref[...]
    zp = cp_ref[...] * hp_ref[...]
    first = (i % blocks_per_seq) == 0
    zp = jnp.where(first, jnp.zeros_like(zp), zp)
    row = lax.broadcasted_iota(jnp.int32, (tm, 1), 0)
    z1 = jnp.where(row == 0, zp[7:8, :], pltpu.roll(z, 1, 0))
    z2 = pltpu.roll(z, 2, 0)
    z2 = jnp.where(row == 0, zp[6:7, :], jnp.where(row == 1, zp[7:8, :], z2))
    w = w_ref[...]
    y = w[0:1, :] * z2 + w[1:2, :] * z1 + w[2:3, :] * z
    o_ref[...] = b_ref[...] * y


def _short_conv(proj, w, seq, tm):
    t = proj.shape[0]
    rb = tm // 8
    cur = lambda j: pl.BlockSpec((tm, SC_WIDTH), lambda i, j=j: (i, j))
    prev = lambda j: pl.BlockSpec((8, SC_WIDTH), lambda i, j=j: (jnp.maximum(i * rb - 1, 0), j))
    return pl.pallas_call(
        functools.partial(_short_conv_kernel, tm=tm, blocks_per_seq=seq // tm),
        grid=(t // tm,),
        in_specs=[cur(COL_SC_B), cur(COL_SC_C), cur(COL_SC_H), prev(COL_SC_C), prev(COL_SC_H),
                  pl.BlockSpec((CONV_W, SC_WIDTH), lambda i: (0, 0))],
        out_specs=pl.BlockSpec((tm, SC_WIDTH), lambda i: (i, 0)),
        out_shape=jax.ShapeDtypeStruct((t, SC_WIDTH), F32),
        compiler_params=_cparams(("arbitrary",)),
    )(proj, proj, proj, proj, proj, w)


def _split3(x):
    hi = x.astype(BF16)
    r = x - hi.astype(F32)
    mid = r.astype(BF16)
    lo = (r - mid.astype(F32)).astype(BF16)
    return hi, mid, lo


def _gla_kernel(q_ref, k_ref, v_ref, r_ref, lr_ref, wg_ref, bg_ref, gn_ref, o_ref, st_ref, *, tc):
    ch = GLA_CHUNK

    @pl.when(pl.program_id(1) == 0)
    def _():
        st_ref[...] = jnp.zeros_like(st_ref)

    ii = lax.broadcasted_iota(jnp.int32, (ch, ch), 0)
    jj = lax.broadcasted_iota(jnp.int32, (ch, ch), 1)
    tri = (ii >= jj).astype(BF16)
    rk = lax.broadcasted_iota(jnp.int32, (GLA_VW, GLA_KW), 0)
    ck = lax.broadcasted_iota(jnp.int32, (GLA_VW, GLA_KW), 1)
    bd_k = (rk // GLA_DV) == (ck // GLA_DK)
    rv = lax.broadcasted_iota(jnp.int32, (GLA_VW, GLA_VW), 0)
    cv = lax.broadcasted_iota(jnp.int32, (GLA_VW, GLA_VW), 1)
    bd_v = (rv // GLA_DV) == (cv // GLA_DV)
    seg = bd_v.astype(BF16)
    ri = lax.broadcasted_iota(jnp.int32, (ch, GLA_VW), 0)
    ci = lax.broadcasted_iota(jnp.int32, (ch, GLA_VW), 1)
    causal = (ci % ch) <= ri
    dn_t = (((1,), (1,)), ((), ()))
    scale = GLA_DK ** -0.5

    def chunk(c, carry):
        off = pl.multiple_of(c * ch, ch)
        rows = pl.ds(off, ch)
        x = jnp.dot(lr_ref[rows, :].astype(BF16), wg_ref[...], preferred_element_type=F32) + bg_ref[...]
        la = (jnp.minimum(x, 0.0) - jnp.log1p(jnp.exp(-jnp.abs(x)))) * (1.0 / GLA_GATE_TEMP)
        hi, mid, lo = _split3(la)
        b = (jnp.dot(tri, hi, preferred_element_type=F32) + jnp.dot(tri, mid, preferred_element_type=F32)
             + jnp.dot(tri, lo, preferred_element_type=F32))
        b_mid = b[ch // 2 - 1:ch // 2, :]
        b_last = b[ch - 1:ch, :]
        q = q_ref[rows, :] * scale
        k = k_ref[rows, :]
        v = v_ref[rows, :]
        qt = (q * jnp.exp(b - b_mid)).astype(BF16)
        kt = (k * jnp.exp(b_mid - b)).astype(BF16)
        kbd = jnp.where(bd_k, jnp.concatenate([kt] * GLA_HEADS, axis=0), jnp.zeros((), BF16))
        att = lax.dot_general(qt, kbd, dn_t, preferred_element_type=F32)
        att = jnp.where(causal, att, 0.0).astype(BF16)
        vb = v.astype(BF16)
        vbd = jnp.where(bd_v, jnp.concatenate([vb] * GLA_HEADS, axis=0), jnp.zeros((), BF16))
        o = jnp.dot(att, vbd, preferred_element_type=F32)
        st = st_ref[...]
        bq = (q * jnp.exp(b)).astype(BF16)
        o = o + lax.dot_general(bq, st.astype(BF16), dn_t, preferred_element_type=F32)
        kl = (k * jnp.exp(b_last - b)).astype(BF16)
        ut = jnp.dot(v.T.astype(BF16), kl, preferred_element_type=F32)
        st_ref[...] = st * jnp.exp(b_last) + jnp.where(bd_k, ut, 0.0)
        o2 = o * o
        o2h = o2.astype(BF16)
        o2l = (o2 - o2h.astype(F32)).astype(BF16)
        ms = (jnp.dot(o2h, seg, preferred_element_type=F32)
              + jnp.dot(o2l, seg, preferred_element_type=F32)) * (1.0 / GLA_DV)
        r = r_ref[rows, :]
        gate = r * (1.0 / (1.0 + jnp.exp(-r)))
        o_ref[rows, :] = o * lax.rsqrt(ms + EPS) * gn_ref[...] * gate
        return carry

    lax.fori_loop(0, tc // ch, chunk, 0)


def _gla(proj, wg, bg, gn, batch, seq, tc):
    t = proj.shape[0]
    nb = seq // tc
    blk = lambda w, j: pl.BlockSpec((tc, w), lambda b, i, j=j: (b * nb + i, j))
    const = lambda r, c: pl.BlockSpec((r, c), lambda b, i: (0, 0))
    wg_pad = jnp.zeros((LANES, GLA_KW), F32).at[:GLA_GATE_RANK].set(wg).astype(BF16)
    gn_row = jnp.tile(gn, GLA_HEADS).reshape(1, GLA_VW)
    return pl.pallas_call(
        functools.partial(_gla_kernel, tc=tc),
        grid=(batch, nb),
        in_specs=[blk(GLA_KW, COL_G_Q), blk(GLA_KW, COL_G_K), blk(GLA_VW, COL_G_V), blk(GLA_VW, COL_G_R),
                  blk(LANES, COL_G_LR), const(LANES, GLA_KW), const(1, GLA_KW), const(1, GLA_VW)],
        out_specs=pl.BlockSpec((tc, GLA_VW), lambda b, i: (b * nb + i, 0)),
        out_shape=jax.ShapeDtypeStruct((t, GLA_VW), F32),
        scratch_shapes=[pltpu.VMEM((GLA_VW, GLA_KW), F32)],
        compiler_params=_cparams(("arbitrary", "arbitrary")),
    )(proj, proj, proj, proj, proj, wg_pad, bg.reshape(1, GLA_KW), gn_row)


def _out_proj_kernel(x_ref, a_ref, s_ref, g_ref, w_ref, o_ref):
    acc = jnp.dot(a_ref[...].astype(BF16), w_ref[0:DA_WIDTH, :], preferred_element_type=F32)
    acc += jnp.dot(s_ref[...].astype(BF16), w_ref[DA_WIDTH:DA_WIDTH + SC_WIDTH, :],
                   preferred_element_type=F32)
    acc += jnp.dot(g_ref[...].astype(BF16), w_ref[DA_WIDTH + SC_WIDTH:, :], preferred_element_type=F32)
    o_ref[...] = x_ref[...] + acc


def _out_proj(x, o_da, o_sc, o_g, w, tm):
    t, d = x.shape
    blk = lambda n: pl.BlockSpec((tm, n), lambda i: (i, 0))
    return pl.pallas_call(
        _out_proj_kernel,
        grid=(t // tm,),
        in_specs=[blk(d), blk(DA_WIDTH), blk(SC_WIDTH), blk(GLA_VW),
                  pl.BlockSpec(w.shape, lambda i: (0, 0))],
        out_specs=blk(d),
        out_shape=jax.ShapeDtypeStruct((t, d), F32),
        compiler_params=_cparams(("arbitrary",)),
    )(x, o_da, o_sc, o_g, w)


def _peer_scores_kernel(x_ref, g_ref, wq_ref, k1_ref, k2_ref, s1_ref, s2_ref):
    xn = _rms(x_ref[...], g_ref[...]).astype(BF16)
    q = jnp.dot(xn, wq_ref[...], preferred_element_type=F32)
    dn = (((1,), (1,)), ((), ()))
    for h in range(PEER_HEADS):
        qa = q[:, (2 * h) * LANES:(2 * h + 1) * LANES].astype(BF16)
        qb = q[:, (2 * h + 1) * LANES:(2 * h + 2) * LANES].astype(BF16)
        s1_ref[h] = lax.dot_general(k1_ref[...], qa, dn, preferred_element_type=F32)
        s2_ref[h] = lax.dot_general(k2_ref[...], qb, dn, preferred_element_type=F32)


def _peer_scores(x, g, wq, k1, k2, tm):
    t, d = x.shape
    out = pl.BlockSpec((PEER_HEADS, PEER_NKEYS, tm), lambda i: (0, 0, i))
    return pl.pallas_call(
        _peer_scores_kernel,
        grid=(t // tm,),
        in_specs=[pl.BlockSpec((tm, d), lambda i: (i, 0)),
                  pl.BlockSpec((1, d), lambda i: (0, 0)),
                  pl.BlockSpec(wq.shape, lambda i: (0, 0)),
                  pl.BlockSpec(k1.shape, lambda i: (0, 0)),
                  pl.BlockSpec(k2.shape, lambda i: (0, 0))],
        out_specs=[out, out],
        out_shape=[jax.ShapeDtypeStruct((PEER_HEADS, PEER_NKEYS, t), F32)] * 2,
        compiler_params=_cparams(("arbitrary",)),
    )(x, g.reshape(1, d), wq, k1, k2)


def _top_desc(x, n):
    vals = []
    cur = x
    for r in range(n):
        m = jnp.max(cur, axis=0, keepdims=True)
        vals.append(m)
        if r + 1 < n:
            cur = jnp.where(cur == m, NEG_INF, cur)
    return vals


def _top_desc_ranked(x, n):
    vals = []
    cur = x
    rank = jnp.full(x.shape, float(n), F32)
    for r in range(n):
        m = jnp.max(cur, axis=0, keepdims=True)
        vals.append(m)
        hit = cur == m
        rank = jnp.where(hit, float(r), rank)
        if r + 1 < n:
            cur = jnp.where(hit, NEG_INF, cur)
    return vals, rank


def _peer_tables_kernel(s1_ref, s2_ref, a1_ref, cnt_ref, a2_ref, rank2_ref):
    k = PEER_TOPK
    tt = s1_ref.shape[-1]
    for h in range(PEER_HEADS):
        s1 = s1_ref[h]
        s2 = s2_ref[h]
        v1 = _top_desc(s1, k)
        v2, rank2 = _top_desc_ranked(s2, k)
        cands = [v1[i] + v2[j] for i in range(k) for j in range(k // (i + 1))]
        pad = (-len(cands)) % 8
        cands += [jnp.full((1, tt), NEG_INF, F32)] * pad
        top = _top_desc(jnp.concatenate(cands, axis=0), k)
        z = jnp.ones((1, tt), F32)
        for r in range(1, k):
            z = z + jnp.exp(top[r] - top[0])
        thr = top[k - 1]
        cnt = jnp.zeros_like(s1)
        for j in range(k):
            cnt = cnt + ((s1 + v2[j]) >= thr).astype(F32)
        cnt_ref[h] = jnp.where(s1 >= v1[k - 1], cnt, 0.0)
        rank2_ref[h] = rank2.astype(BF16)
        a1_ref[h] = jnp.exp(s1 - v1[0]) * (1.0 / z)
        a2_ref[h] = jnp.exp(s2 - v2[0]).astype(BF16)


def _peer_tables(s1t, s2t, tt):
    t = s1t.shape[-1]
    blk = pl.BlockSpec((PEER_HEADS, PEER_NKEYS, tt), lambda i: (0, 0, i))
    f32 = jax.ShapeDtypeStruct(s1t.shape, F32)
    bf16 = jax.ShapeDtypeStruct(s1t.shape, BF16)
    return pl.pallas_call(
        _peer_tables_kernel,
        grid=(t // tt,),
        in_specs=[blk, blk],
        out_specs=[blk, blk, blk, blk],
        out_shape=[f32, f32, bf16, bf16],
        compiler_params=_cparams(("arbitrary",)),
    )(s1t, s2t)


def _peer_dense_kernel(x_ref, g_ref, u_ref, vt_ref, a1_ref, cnt_ref, a2_ref, rank2_ref, o_ref,
                       xnt_ref, acc_ref, *, n1_per_tile, job_rows, sub):
    e = pl.program_id(1)

    @pl.when(e == 0)
    def _():
        xn = _rms(x_ref[...], g_ref[...])
        xnt_ref[...] = xn.T.astype(BF16)
        acc_ref[...] = jnp.zeros_like(acc_ref)

    inv_sqrt2 = 1.0 / math.sqrt(2.0)
    zero = jnp.zeros((), BF16)
    tt = xnt_ref.shape[1]
    n1_per_job = job_rows // PEER_NKEYS
    pack = BF16_SUBLANES
    groups = PEER_NKEYS // pack
    jobs = [(q, c) for c in range(tt // sub) for q in range(n1_per_tile // n1_per_job)]

    def matmul1(q, c):
        return jnp.dot(u_ref[q * job_rows:(q + 1) * job_rows, :], xnt_ref[:, c * sub:(c + 1) * sub],
                       preferred_element_type=F32)

    def gate(q, c, ht):
        cols = slice(c * sub, (c + 1) * sub)
        pts = []
        for rl in range(n1_per_job):
            r = q * n1_per_job + rl
            w = None
            for h in range(PEER_HEADS):
                cnt = jnp.broadcast_to(cnt_ref[h, r:r + 1, cols], (pack, sub)).astype(BF16)
                a1 = jnp.broadcast_to(a1_ref[h, r:r + 1, cols], (pack, sub)).astype(BF16)
                sel = rank2_ref[h, :, :, cols] < cnt[None]
                term = a1[None] * jnp.where(sel, a2_ref[h, :, :, cols], zero)
                w = term if w is None else w + term
            hb = ht[rl * PEER_NKEYS:(rl + 1) * PEER_NKEYS, :]
            gelu = 0.5 * hb * (1.0 + lax.erf(hb * inv_sqrt2))
            p = w * gelu.astype(BF16).reshape(groups, pack, sub)
            pts.append(p.reshape(PEER_NKEYS, sub))
        return jnp.concatenate(pts, axis=0)

    def matmul2(q, c, pt):
        cols = slice(c * sub, (c + 1) * sub)
        acc_ref[:, cols] += jnp.dot(vt_ref[:, q * job_rows:(q + 1) * job_rows], pt,
                                    preferred_element_type=F32)

    ht_next = matmul1(*jobs[0])
    for j, (q, c) in enumerate(jobs):
        ht = ht_next
        if j + 1 < len(jobs):
            ht_next = matmul1(*jobs[j + 1])
        matmul2(q, c, gate(q, c, ht))

    @pl.when(e == pl.num_programs(1) - 1)
    def _():
        o_ref[...] = x_ref[...] + acc_ref[...].T


def _peer_dense(x, g, u, vt, a1t, cnt, a2t, rank2, tt, et):
    t, d = x.shape
    n_exp = u.shape[0]
    n1_per_tile = et // PEER_NKEYS
    groups = PEER_NKEYS // BF16_SUBLANES
    grouped = (PEER_HEADS, groups, BF16_SUBLANES, t)
    full = pl.BlockSpec((PEER_HEADS, groups, BF16_SUBLANES, tt), lambda i, e: (0, 0, 0, i))
    part = pl.BlockSpec((PEER_HEADS, n1_per_tile, tt), lambda i, e: (0, e, i))
    return pl.pallas_call(
        functools.partial(_peer_dense_kernel, n1_per_tile=n1_per_tile, job_rows=min(et, 512),
                          sub=min(tt, 256)),
        grid=(t // tt, n_exp // et),
        in_specs=[pl.BlockSpec((tt, d), lambda i, e: (i, 0)),
                  pl.BlockSpec((1, d), lambda i, e: (0, 0)),
                  pl.BlockSpec((et, d), lambda i, e: (e, 0)),
                  pl.BlockSpec((d, et), lambda i, e: (0, e)),
                  part, part, full, full],
        out_specs=pl.BlockSpec((tt, d), lambda i, e: (i, 0)),
        out_shape=jax.ShapeDtypeStruct((t, d), F32),
        scratch_shapes=[pltpu.VMEM((d, tt), BF16), pltpu.VMEM((d, tt), F32)],
        compiler_params=_cparams(("arbitrary", "arbitrary"), vmem_mb=52),
    )(x, g.reshape(1, d), u, vt, a1t, cnt, a2t.reshape(grouped), rank2.reshape(grouped))


def _rmsnorm_kernel(x_ref, g_ref, o_ref):
    o_ref[...] = _rms(x_ref[...], g_ref[...])


def _rmsnorm(x, g, tm):
    t, d = x.shape
    return pl.pallas_call(
        _rmsnorm_kernel,
        grid=(t // tm,),
        in_specs=[pl.BlockSpec((tm, d), lambda i: (i, 0)), pl.BlockSpec((1, d), lambda i: (0, 0))],
        out_specs=pl.BlockSpec((tm, d), lambda i: (i, 0)),
        out_shape=jax.ShapeDtypeStruct((t, d), F32),
        compiler_params=_cparams(("arbitrary",)),
    )(x, g.reshape(1, d))


def _tiles(seq):
    pick = lambda want: min(want, seq)
    return dict(tm=pick(256), tq=pick(512), tconv=pick(512), tgla=pick(512), tt=pick(256), td=pick(512),
                et=2048)


def kernel(x, positions, norm_mix, w_in, lam_q1, lam_k1, lam_q2, lam_k2, diff_norm, conv_w, gla_w_gate2, gla_b_gate, gla_norm, w_out, norm_ffn, peer_w_q, peer_keys1, peer_keys2, peer_u, peer_v, norm_final):
    batch, seq, d = x.shape
    t = batch * seq
    depth = w_in.shape[0]
    tl = _tiles(seq)
    xt = x.reshape(t, d)
    tabs = _rope_tables(positions, tl["tm"])
    for i in range(depth):
        w_in_p = jnp.pad(w_in[i], ((0, 0), (0, IN_PAD - IN_WIDTH))).astype(BF16)
        proj = _rms_matmul(xt, norm_mix[i], w_in_p, tl["tm"])
        qr, kr, vb = _rope_prep(proj, tabs, tl["tm"])
        lam_init = 0.8 - 0.6 * math.exp(-0.3 * i)
        o_da = _diff_attention(qr, kr, vb, lam_q1[i], lam_k1[i], lam_q2[i], lam_k2[i], diff_norm[i],
                               batch, seq, lam_init, tl["tq"])
        o_sc = _short_conv(proj, conv_w[i], seq, tl["tconv"])
        o_g = _gla(proj, gla_w_gate2[i], gla_b_gate[i], gla_norm[i], batch, seq, tl["tgla"])
        xt = _out_proj(xt, o_da, o_sc, o_g, w_out[i].astype(BF16), tl["tm"])
        s1t, s2t = _peer_scores(xt, norm_ffn[i], peer_w_q[i].astype(BF16), peer_keys1[i].astype(BF16),
                                peer_keys2[i].astype(BF16), tl["tm"])
        a1t, cnt, a2t, rank2 = _peer_tables(s1t, s2t, tl["tt"])
        xt = _peer_dense(xt, norm_ffn[i], peer_u[i].astype(BF16), peer_v[i].T.astype(BF16),
                         a1t, cnt, a2t, rank2, tl["td"], tl["et"])
    return _rmsnorm(xt, norm_final, tl["tm"]).reshape(batch, seq, d)
```

```python
import functools
import math

import jax
import jax.numpy as jnp
from jax import lax
from jax.experimental import pallas as pl
from jax.experimental.pallas import tpu as pltpu

F32 = jnp.float32
BF16 = jnp.bfloat16

D_MODEL = 1024
EPS = 1e-6
HEAD_DIM = 64
DA_HEADS = 4
DA_WIDTH = 512
ROPE_THETA = 500000.0
ROT_DIM = 16
SC_WIDTH = 256
CONV_W = 3
GLA_HEADS = 4
GLA_DK = 32
GLA_DV = 64
GLA_KW = 128
GLA_VW = 256
GLA_GATE_RANK = 16
GLA_GATE_TEMP = 16.0
GLA_CHUNK = 64
IN_WIDTH = 3088
IN_PAD = 3200
PEER_HEADS = 8
PEER_NKEYS = 128
PEER_EXPERTS = PEER_NKEYS * PEER_NKEYS
PEER_TOPK = 16
LANES = 128
BF16_SUBLANES = 16
NEG_INF = float("-inf")

COL_DA_Q, COL_DA_K, COL_DA_V = 0, 1, 2
COL_SC_B, COL_SC_C, COL_SC_H = 6, 7, 8
COL_G_V, COL_G_R = 10, 11
COL_G_Q, COL_G_K, COL_G_LR = 18, 19, 24


def _cparams(sem, vmem_mb=40):
    return pltpu.CompilerParams(dimension_semantics=sem, vmem_limit_bytes=vmem_mb * 1024 * 1024)


def _rms(x, g):
    return x * lax.rsqrt(jnp.mean(x * x, axis=-1, keepdims=True) + EPS) * g


def _rms_matmul_kernel(x_ref, g_ref, w_ref, o_ref):
    xn = _rms(x_ref[...], g_ref[...]).astype(BF16)
    o_ref[...] = jnp.dot(xn, w_ref[...], preferred_element_type=F32)


def _rms_matmul(x, g, w, tm):
    t, d = x.shape
    n = w.shape[1]
    return pl.pallas_call(
        _rms_matmul_kernel,
        grid=(t // tm,),
        in_specs=[pl.BlockSpec((tm, d), lambda i: (i, 0)),
                  pl.BlockSpec((1, d), lambda i: (0, 0)),
                  pl.BlockSpec((d, n), lambda i: (0, 0))],
        out_specs=pl.BlockSpec((tm, n), lambda i: (i, 0)),
        out_shape=jax.ShapeDtypeStruct((t, n), F32),
        compiler_params=_cparams(("arbitrary",)),
    )(x, g.reshape(1, d), w)


def _rope_table_kernel(pos_ref, inv_ref, mrot_ref, m1_ref, m2_ref, c_ref, s1_ref, s2_ref):
    ang = pos_ref[...] * inv_ref[...]
    c = jnp.cos(ang)
    s = jnp.sin(ang)
    mrot = mrot_ref[...]
    c_ref[...] = mrot * c + (1.0 - mrot)
    s1_ref[...] = -(m1_ref[...] * s)
    s2_ref[...] = m2_ref[...] * s


def _rope_tables(positions, tm):
    t = positions.size
    pos = jnp.broadcast_to(positions.reshape(t, 1).astype(F32), (t, LANES))
    d = jnp.arange(LANES) % HEAD_DIM
    half = ROT_DIM // 2
    inv = ROPE_THETA ** (-jnp.arange(0, ROT_DIM, 2, dtype=F32) / ROT_DIM)
    inv_row = jnp.where(d < ROT_DIM, inv[d % half], 0.0).astype(F32).reshape(1, LANES)
    mrot = (d < ROT_DIM).astype(F32).reshape(1, LANES)
    m1 = (d < half).astype(F32).reshape(1, LANES)
    m2 = ((d >= half) & (d < ROT_DIM)).astype(F32).reshape(1, LANES)
    row = pl.BlockSpec((1, LANES), lambda i: (0, 0))
    blk = pl.BlockSpec((tm, LANES), lambda i: (i, 0))
    return pl.pallas_call(
        _rope_table_kernel,
        grid=(t // tm,),
        in_specs=[blk, row, row, row, row],
        out_specs=[blk, blk, blk],
        out_shape=[jax.ShapeDtypeStruct((t, LANES), F32)] * 3,
        compiler_params=_cparams(("arbitrary",)),
    )(pos, inv_row, mrot, m1, m2)


def _rope_prep_kernel(q_ref, k_ref, v_ref, c_ref, s1_ref, s2_ref, qo_ref, ko_ref, vo_ref, *, qscale):
    c = c_ref[...]
    s1 = s1_ref[...]
    s2 = s2_ref[...]
    half = ROT_DIM // 2
    for g in range(DA_WIDTH // LANES):
        sl = slice(LANES * g, LANES * (g + 1))
        for src, dst, sc in ((q_ref, qo_ref, qscale), (k_ref, ko_ref, 1.0)):
            x = src[:, sl]
            y = x * c + pltpu.roll(x, LANES - half, 1) * s1 + pltpu.roll(x, half, 1) * s2
            dst[:, sl] = (y * sc).astype(BF16)
    vo_ref[...] = v_ref[...].astype(BF16)


def _rope_prep(proj, tabs, tm):
    t = proj.shape[0]
    c, s1, s2 = tabs
    col = lambda j: pl.BlockSpec((tm, DA_WIDTH), lambda i, j=j: (i, j))
    tab = pl.BlockSpec((tm, LANES), lambda i: (i, 0))
    out = pl.BlockSpec((tm, DA_WIDTH), lambda i: (i, 0))
    return pl.pallas_call(
        functools.partial(_rope_prep_kernel, qscale=HEAD_DIM ** -0.5),
        grid=(t // tm,),
        in_specs=[col(COL_DA_Q), col(COL_DA_K), col(COL_DA_V), tab, tab, tab],
        out_specs=[out, out, out],
        out_shape=[jax.ShapeDtypeStruct((t, DA_WIDTH), BF16)] * 3,
        compiler_params=_cparams(("arbitrary",)),
    )(proj, proj, proj, c, s1, s2)


def _attn_kernel(q_ref, k_ref, v_ref, lq1_ref, lk1_ref, lq2_ref, lk2_ref, gain_ref, o_ref,
                 *, tq, lam_init):
    qi = pl.program_id(2)
    q = q_ref[...]
    lane = lax.broadcasted_iota(jnp.int32, (1, LANES), 1)
    zero = jnp.zeros_like(q)
    qa = jnp.where(lane < HEAD_DIM, q, zero)
    qb = jnp.where(lane >= HEAD_DIM, q, zero)
    dn = (((1,), (1,)), ((), ()))

    def update(s, m, l, a, vv):
        mn = jnp.maximum(m, jnp.max(s, axis=-1, keepdims=True))
        alpha = jnp.exp(m - mn)
        p = jnp.exp(s - mn)
        l = alpha * l + jnp.sum(p, axis=-1, keepdims=True)
        a = alpha * a + jnp.dot(p.astype(BF16), vv, preferred_element_type=F32)
        return mn, l, a

    def step(j, carry, masked):
        m1, l1, a1, m2, l2, a2 = carry
        off = pl.multiple_of(j * tq, tq)
        kk = k_ref[pl.ds(off, tq), :]
        vv = v_ref[pl.ds(off, tq), :]
        s1 = lax.dot_general(qa, kk, dn, preferred_element_type=F32)
        s2 = lax.dot_general(qb, kk, dn, preferred_element_type=F32)
        if masked:
            r = lax.broadcasted_iota(jnp.int32, (tq, tq), 0)
            c = lax.broadcasted_iota(jnp.int32, (tq, tq), 1)
            keep = c <= r
            s1 = jnp.where(keep, s1, NEG_INF)
            s2 = jnp.where(keep, s2, NEG_INF)
        m1, l1, a1 = update(s1, m1, l1, a1, vv)
        m2, l2, a2 = update(s2, m2, l2, a2, vv)
        return m1, l1, a1, m2, l2, a2

    col = lambda v: jnp.full((tq, 1), v, F32)
    acc = jnp.zeros((tq, LANES), F32)
    init = (col(NEG_INF), col(0.0), acc, col(NEG_INF), col(0.0), acc)
    carry = lax.fori_loop(0, qi, lambda j, cr: step(j, cr, False), init)
    m1, l1, a1, m2, l2, a2 = step(qi, carry, True)

    lam = (jnp.exp(jnp.sum(lq1_ref[...] * lk1_ref[...], axis=-1, keepdims=True))
           - jnp.exp(jnp.sum(lq2_ref[...] * lk2_ref[...], axis=-1, keepdims=True)) + lam_init)
    o = a1 / l1 - lam * (a2 / l2)
    o_ref[...] = _rms(o, gain_ref[...]) * (1.0 - lam_init)


def _diff_attention(qr, kr, vb, lq1, lk1, lq2, lk2, gain, batch, seq, lam_init, tq):
    t = qr.shape[0]
    nq = seq // tq
    vec = lambda n: pl.BlockSpec((1, n), lambda b, h, i: (0, 0))
    return pl.pallas_call(
        functools.partial(_attn_kernel, tq=tq, lam_init=lam_init),
        grid=(batch, DA_HEADS, nq),
        in_specs=[pl.BlockSpec((tq, LANES), lambda b, h, i: (b * nq + i, h)),
                  pl.BlockSpec((seq, LANES), lambda b, h, i: (b, h)),
                  pl.BlockSpec((seq, LANES), lambda b, h, i: (b, h)),
                  vec(HEAD_DIM), vec(HEAD_DIM), vec(HEAD_DIM), vec(HEAD_DIM), vec(LANES)],
        out_specs=pl.BlockSpec((tq, LANES), lambda b, h, i: (b * nq + i, h)),
        out_shape=jax.ShapeDtypeStruct((t, DA_WIDTH), F32),
        compiler_params=_cparams(("arbitrary", "arbitrary", "arbitrary")),
    )(qr, kr, vb, lq1.reshape(1, -1), lk1.reshape(1, -1), lq2.reshape(1, -1), lk2.reshape(1, -1),
      gain.reshape(1, -1))


def _short_conv_kernel(b_ref, c_ref, h_ref, cp_ref, hp_ref, w_ref, o_ref, *, tm, blocks_per_seq):
    i = pl.program_id(0)
    z = c_ref[...] * h_ref[...]
    zp = cp_ref[...] * hp_ref[...]
    first = (i % blocks_per_seq) == 0
    zp = jnp.where(first, jnp.zeros_like(zp), zp)
    row = lax.broadcasted_iota(jnp.int32, (tm, 1), 0)
    z1 = jnp.where(row == 0, zp[7:8, :], pltpu.roll(z, 1, 0))
    z2 = pltpu.roll(z, 2, 0)
    z2 = jnp.where(row == 0, zp[6:7, :], jnp.where(row == 1, zp[7:8, :], z2))
    w = w_ref[...]
    y = w[0:1, :] * z2 + w[1:2, :] * z1 + w[2:3, :] * z
    o_ref[...] = b_ref[...] * y


def _short_conv(proj, w, seq, tm):
    t = proj.shape[0]
    rb = tm // 8
    cur = lambda j: pl.BlockSpec((tm, SC_WIDTH), lambda i, j=j: (i, j))
    prev = lambda j: pl.BlockSpec((8, SC_WIDTH), lambda i, j=j: (jnp.maximum(i * rb - 1, 0), j))
    return pl.pallas_call(
        functools.partial(_short_conv_kernel, tm=tm, blocks_per_seq=seq // tm),
        grid=(t // tm,),
        in_specs=[cur(COL_SC_B), cur(COL_SC_C), cur(COL_SC_H), prev(COL_SC_C), prev(COL_SC_H),
                  pl.BlockSpec((CONV_W, SC_WIDTH), lambda i: (0, 0))],
        out_specs=pl.BlockSpec((tm, SC_WIDTH), lambda i: (i, 0)),
        out_shape=jax.ShapeDtypeStruct((t, SC_WIDTH), F32),
        compiler_params=_cparams(("arbitrary",)),
    )(proj, proj, proj, proj, proj, w)


def _split3(x):
    hi = x.astype(BF16)
    r = x - hi.astype(F32)
    mid = r.astype(BF16)
    lo = (r - mid.astype(F32)).astype(BF16)
    return hi, mid, lo


def _gla_kernel(q_ref, k_ref, v_ref, r_ref, lr_ref, wg_ref, bg_ref, gn_ref, o_ref, st_ref, *, tc):
    ch = GLA_CHUNK

    @pl.when(pl.program_id(1) == 0)
    def _():
        st_ref[...] = jnp.zeros_like(st_ref)

    ii = lax.broadcasted_iota(jnp.int32, (ch, ch), 0)
    jj = lax.broadcasted_iota(jnp.int32, (ch, ch), 1)
    tri = (ii >= jj).astype(BF16)
    rk = lax.broadcasted_iota(jnp.int32, (GLA_VW, GLA_KW), 0)
    ck = lax.broadcasted_iota(jnp.int32, (GLA_VW, GLA_KW), 1)
    bd_k = (rk // GLA_DV) == (ck // GLA_DK)
    rv = lax.broadcasted_iota(jnp.int32, (GLA_VW, GLA_VW), 0)
    cv = lax.broadcasted_iota(jnp.int32, (GLA_VW, GLA_VW), 1)
    bd_v = (rv // GLA_DV) == (cv // GLA_DV)
    seg = bd_v.astype(BF16)
    ri = lax.broadcasted_iota(jnp.int32, (ch, GLA_VW), 0)
    ci = lax.broadcasted_iota(jnp.int32, (ch, GLA_VW), 1)
    causal = (ci % ch) <= ri
    dn_t = (((1,), (1,)), ((), ()))
    scale = GLA_DK ** -0.5

    def chunk(c, carry):
        off = pl.multiple_of(c * ch, ch)
        rows = pl.ds(off, ch)
        x = jnp.dot(lr_ref[rows, :].astype(BF16), wg_ref[...], preferred_element_type=F32) + bg_ref[...]
        la = (jnp.minimum(x, 0.0) - jnp.log1p(jnp.exp(-jnp.abs(x)))) * (1.0 / GLA_GATE_TEMP)
        hi, mid, lo = _split3(la)
        b = (jnp.dot(tri, hi, preferred_element_type=F32) + jnp.dot(tri, mid, preferred_element_type=F32)
             + jnp.dot(tri, lo, preferred_element_type=F32))
        b_mid = b[ch // 2 - 1:ch // 2, :]
        b_last = b[ch - 1:ch, :]
        q = q_ref[rows, :] * scale
        k = k_ref[rows, :]
        v = v_ref[rows, :]
        qt = (q * jnp.exp(b - b_mid)).astype(BF16)
        kt = (k * jnp.exp(b_mid - b)).astype(BF16)
        kbd = jnp.where(bd_k, jnp.concatenate([kt] * GLA_HEADS, axis=0), jnp.zeros((), BF16))
        att = lax.dot_general(qt, kbd, dn_t, preferred_element_type=F32)
        att = jnp.where(causal, att, 0.0).astype(BF16)
        vb = v.astype(BF16)
        vbd = jnp.where(bd_v, jnp.concatenate([vb] * GLA_HEADS, axis=0), jnp.zeros((), BF16))
        o = jnp.dot(att, vbd, preferred_element_type=F32)
        st = st_ref[...]
        bq = (q * jnp.exp(b)).astype(BF16)
        o = o + lax.dot_general(bq, st.astype(BF16), dn_t, preferred_element_type=F32)
        kl = (k * jnp.exp(b_last - b)).astype(BF16)
        ut = jnp.dot(v.T.astype(BF16), kl, preferred_element_type=F32)
        st_ref[...] = st * jnp.exp(b_last) + jnp.where(bd_k, ut, 0.0)
        o2 = o * o
        o2h = o2.astype(BF16)
        o2l = (o2 - o2h.astype(F32)).astype(BF16)
        ms = (jnp.dot(o2h, seg, preferred_element_type=F32)
              + jnp.dot(o2l, seg, preferred_element_type=F32)) * (1.0 / GLA_DV)
        r = r_ref[rows, :]
        gate = r * (1.0 / (1.0 + jnp.exp(-r)))
        o_ref[rows, :] = o * lax.rsqrt(ms + EPS) * gn_ref[...] * gate
        return carry

    lax.fori_loop(0, tc // ch, chunk, 0)


def _gla(proj, wg, bg, gn, batch, seq, tc):
    t = proj.shape[0]
    nb = seq // tc
    blk = lambda w, j: pl.BlockSpec((tc, w), lambda b, i, j=j: (b * nb + i, j))
    const = lambda r, c: pl.BlockSpec((r, c), lambda b, i: (0, 0))
    wg_pad = jnp.zeros((LANES, GLA_KW), F32).at[:GLA_GATE_RANK].set(wg).astype(BF16)
    gn_row = jnp.tile(gn, GLA_HEADS).reshape(1, GLA_VW)
    return pl.pallas_call(
        functools.partial(_gla_kernel, tc=tc),
        grid=(batch, nb),
        in_specs=[blk(GLA_KW, COL_G_Q), blk(GLA_KW, COL_G_K), blk(GLA_VW, COL_G_V), blk(GLA_VW, COL_G_R),
                  blk(LANES, COL_G_LR), const(LANES, GLA_KW), const(1, GLA_KW), const(1, GLA_VW)],
        out_specs=pl.BlockSpec((tc, GLA_VW), lambda b, i: (b * nb + i, 0)),
        out_shape=jax.ShapeDtypeStruct((t, GLA_VW), F32),
        scratch_shapes=[pltpu.VMEM((GLA_VW, GLA_KW), F32)],
        compiler_params=_cparams(("arbitrary", "arbitrary")),
    )(proj, proj, proj, proj, proj, wg_pad, bg.reshape(1, GLA_KW), gn_row)


def _out_proj_kernel(x_ref, a_ref, s_ref, g_ref, w_ref, o_ref):
    acc = jnp.dot(a_ref[...].astype(BF16), w_ref[0:DA_WIDTH, :], preferred_element_type=F32)
    acc += jnp.dot(s_ref[...].astype(BF16), w_ref[DA_WIDTH:DA_WIDTH + SC_WIDTH, :],
                   preferred_element_type=F32)
    acc += jnp.dot(g_ref[...].astype(BF16), w_ref[DA_WIDTH + SC_WIDTH:, :], preferred_element_type=F32)
    o_ref[...] = x_ref[...] + acc


def _out_proj(x, o_da, o_sc, o_g, w, tm):
    t, d = x.shape
    blk = lambda n: pl.BlockSpec((tm, n), lambda i: (i, 0))
    return pl.pallas_call(
        _out_proj_kernel,
        grid=(t // tm,),
        in_specs=[blk(d), blk(DA_WIDTH), blk(SC_WIDTH), blk(GLA_VW),
                  pl.BlockSpec(w.shape, lambda i: (0, 0))],
        out_specs=blk(d),
        out_shape=jax.ShapeDtypeStruct((t, d), F32),
        compiler_params=_cparams(("arbitrary",)),
    )(x, o_da, o_sc, o_g, w)


def _peer_scores_kernel(x_ref, g_ref, wq_ref, k1_ref, k2_ref, s1_ref, s2_ref):
    xn = _rms(x_ref[...], g_ref[...]).astype(BF16)
    q = jnp.dot(xn, wq_ref[...], preferred_element_type=F32)
    dn = (((1,), (1,)), ((), ()))
    for h in range(PEER_HEADS):
        qa = q[:, (2 * h) * LANES:(2 * h + 1) * LANES].astype(BF16)
        qb = q[:, (2 * h + 1) * LANES:(2 * h + 2) * LANES].astype(BF16)
        s1_ref[h] = lax.dot_general(k1_ref[...], qa, dn, preferred_element_type=F32)
        s2_ref[h] = lax.dot_general(k2_ref[...], qb, dn, preferred_element_type=F32)


def _peer_scores(x, g, wq, k1, k2, tm):
    t, d = x.shape
    out = pl.BlockSpec((PEER_HEADS, PEER_NKEYS, tm), lambda i: (0, 0, i))
    return pl.pallas_call(
        _peer_scores_kernel,
        grid=(t // tm,),
        in_specs=[pl.BlockSpec((tm, d), lambda i: (i, 0)),
                  pl.BlockSpec((1, d), lambda i: (0, 0)),
                  pl.BlockSpec(wq.shape, lambda i: (0, 0)),
                  pl.BlockSpec(k1.shape, lambda i: (0, 0)),
                  pl.BlockSpec(k2.shape, lambda i: (0, 0))],
        out_specs=[out, out],
        out_shape=[jax.ShapeDtypeStruct((PEER_HEADS, PEER_NKEYS, t), F32)] * 2,
        compiler_params=_cparams(("arbitrary",)),
    )(x, g.reshape(1, d), wq, k1, k2)


def _top_desc(x, n):
    vals = []
    cur = x
    for r in range(n):
        m = jnp.max(cur, axis=0, keepdims=True)
        vals.append(m)
        if r + 1 < n:
            cur = jnp.where(cur == m, NEG_INF, cur)
    return vals


def _top_desc_ranked(x, n):
    vals = []
    cur = x
    rank = jnp.full(x.shape, float(n), F32)
    for r in range(n):
        m = jnp.max(cur, axis=0, keepdims=True)
        vals.append(m)
        hit = cur == m
        rank = jnp.where(hit, float(r), rank)
        if r + 1 < n:
            cur = jnp.where(hit, NEG_INF, cur)
    return vals, rank


def _peer_tables_kernel(s1_ref, s2_ref, a1_ref, cnt_ref, a2_ref, rank2_ref):
    k = PEER_TOPK
    tt = s1_ref.shape[-1]
    for h in range(PEER_HEADS):
        s1 = s1_ref[h]
        s2 = s2_ref[h]
        v1 = _top_desc(s1, k)
        v2, rank2 = _top_desc_ranked(s2, k)
        cands = [v1[i] + v2[j] for i in range(k) for j in range(k // (i + 1))]
        pad = (-len(cands)) % 8
        cands += [jnp.full((1, tt), NEG_INF, F32)] * pad
        top = _top_desc(jnp.concatenate(cands, axis=0), k)
        z = jnp.ones((1, tt), F32)
        for r in range(1, k):
            z = z + jnp.exp(top[r] - top[0])
        thr = top[k - 1]
        cnt = jnp.zeros_like(s1)
        for j in range(k):
            cnt = cnt + ((s1 + v2[j]) >= thr).astype(F32)
        cnt_ref[h] = jnp.where(s1 >= v1[k - 1], cnt, 0.0)
        rank2_ref[h] = rank2.astype(BF16)
        a1_ref[h] = jnp.exp(s1 - v1[0]) * (1.0 / z)
        a2_ref[h] = jnp.exp(s2 - v2[0]).astype(BF16)


def _peer_tables(s1t, s2t, tt):
    t = s1t.shape[-1]
    blk = pl.BlockSpec((PEER_HEADS, PEER_NKEYS, tt), lambda i: (0, 0, i))
    f32 = jax.ShapeDtypeStruct(s1t.shape, F32)
    bf16 = jax.ShapeDtypeStruct(s1t.shape, BF16)
    return pl.pallas_call(
        _peer_tables_kernel,
        grid=(t // tt,),
        in_specs=[blk, blk],
        out_specs=[blk, blk, blk, blk],
        out_shape=[f32, f32, bf16, bf16],
        compiler_params=_cparams(("arbitrary",)),
    )(s1t, s2t)


def _peer_dense_kernel(x_ref, g_ref, u_ref, vt_ref, a1_ref, cnt_ref, a2_ref, rank2_ref, o_ref,
                       xnt_ref, acc_ref, hta_ref, htb_ref, pta_ref, ptb_ref):
    e = pl.program_id(1)

    @pl.when(e == 0)
    def _():
        xn = _rms(x_ref[...], g_ref[...])
        xnt_ref[...] = xn.T.astype(BF16)
        acc_ref[...] = jnp.zeros_like(acc_ref)

    inv_sqrt2 = 1.0 / math.sqrt(2.0)
    zero = jnp.zeros((), BF16)
    n_q, job_rows, _ = u_ref.shape
    sub = xnt_ref.shape[1] // 2
    n1_per_job = job_rows // PEER_NKEYS
    pack = BF16_SUBLANES
    groups = PEER_NKEYS // pack
    col_a, col_b = slice(0, sub), slice(sub, 2 * sub)
    ht_refs = {col_a: hta_ref, col_b: htb_ref}
    pt_refs = {col_a: pta_ref, col_b: ptb_ref}

    def matmul1(q, cols):
        ht_refs[cols][...] = jnp.dot(u_ref[q], xnt_ref[:, cols], preferred_element_type=F32)

    def gate(q, cols):
        ht_ref, pt_ref = ht_refs[cols], pt_refs[cols]
        for rl in range(n1_per_job):
            r = q * n1_per_job + rl
            w = None
            for h in range(PEER_HEADS):
                cnt = jnp.broadcast_to(cnt_ref[h, pl.ds(r, 1), cols], (pack, sub)).astype(BF16)
                a1 = jnp.broadcast_to(a1_ref[h, pl.ds(r, 1), cols], (pack, sub)).astype(BF16)
                sel = rank2_ref[h, :, :, cols] < cnt[None]
                term = a1[None] * jnp.where(sel, a2_ref[h, :, :, cols], zero)
                w = term if w is None else w + term
            rows = slice(rl * PEER_NKEYS, (rl + 1) * PEER_NKEYS)
            hb = ht_ref[rows, :]
            gelu = 0.5 * hb * (1.0 + lax.erf(hb * inv_sqrt2))
            p = w * gelu.astype(BF16).reshape(groups, pack, sub)
            pt_ref[rows, :] = p.reshape(PEER_NKEYS, sub)

    def matmul2(q, cols):
        acc_ref[:, cols] += jnp.dot(vt_ref[q], pt_refs[cols][...], preferred_element_type=F32)

    def pair(q, has_prev, has_next):
        matmul1(q, col_b)
        gate(q, col_a)
        if has_prev:
            matmul2(q - 1, col_b)
        if has_next:
            matmul1(q + 1, col_a)
        gate(q, col_b)
        matmul2(q, col_a)

    matmul1(0, col_a)
    if n_q == 1:
        pair(0, False, False)
    else:
        pair(0, False, True)

        def body(q, carry):
            pair(q, True, True)
            return carry

        lax.fori_loop(1, n_q - 1, body, 0)
        pair(n_q - 1, True, False)
    matmul2(n_q - 1, col_b)

    @pl.when(e == pl.num_programs(1) - 1)
    def _():
        o_ref[...] = x_ref[...] + acc_ref[...].T


PEER_JOB_ROWS = 512


def _expert_jobs(u, v):
    n_exp, d = u.shape
    n_jobs = n_exp // PEER_JOB_ROWS
    u3 = u.astype(BF16).reshape(n_jobs, PEER_JOB_ROWS, d)
    vt3 = jnp.transpose(v.astype(BF16).reshape(n_jobs, PEER_JOB_ROWS, d), (0, 2, 1))
    return u3, vt3


def _peer_dense(x, g, u, vt, a1t, cnt, a2t, rank2, tt, et):
    t, d = x.shape
    n_jobs, job_rows, _ = u.shape
    n_q = et // job_rows
    n1_per_tile = et // PEER_NKEYS
    groups = PEER_NKEYS // BF16_SUBLANES
    grouped = (PEER_HEADS, groups, BF16_SUBLANES, t)
    full = pl.BlockSpec((PEER_HEADS, groups, BF16_SUBLANES, tt), lambda i, e: (0, 0, 0, i))
    part = pl.BlockSpec((PEER_HEADS, n1_per_tile, tt), lambda i, e: (0, e, i))
    sub = tt // 2
    return pl.pallas_call(
        _peer_dense_kernel,
        grid=(t // tt, n_jobs // n_q),
        in_specs=[pl.BlockSpec((tt, d), lambda i, e: (i, 0)),
                  pl.BlockSpec((1, d), lambda i, e: (0, 0)),
                  pl.BlockSpec((n_q, job_rows, d), lambda i, e: (e, 0, 0)),
                  pl.BlockSpec((n_q, d, job_rows), lambda i, e: (e, 0, 0)),
                  part, part, full, full],
        out_specs=pl.BlockSpec((tt, d), lambda i, e: (i, 0)),
        out_shape=jax.ShapeDtypeStruct((t, d), F32),
        scratch_shapes=[pltpu.VMEM((d, tt), BF16), pltpu.VMEM((d, tt), F32),
                        pltpu.VMEM((job_rows, sub), F32), pltpu.VMEM((job_rows, sub), F32),
                        pltpu.VMEM((job_rows, sub), BF16), pltpu.VMEM((job_rows, sub), BF16)],
        compiler_params=_cparams(("arbitrary", "arbitrary"), vmem_mb=52),
    )(x, g.reshape(1, d), u, vt, a1t, cnt, a2t.reshape(grouped), rank2.reshape(grouped))


def _rmsnorm_kernel(x_ref, g_ref, o_ref):
    o_ref[...] = _rms(x_ref[...], g_ref[...])


def _rmsnorm(x, g, tm):
    t, d = x.shape
    return pl.pallas_call(
        _rmsnorm_kernel,
        grid=(t // tm,),
        in_specs=[pl.BlockSpec((tm, d), lambda i: (i, 0)), pl.BlockSpec((1, d), lambda i: (0, 0))],
        out_specs=pl.BlockSpec((tm, d), lambda i: (i, 0)),
        out_shape=jax.ShapeDtypeStruct((t, d), F32),
        compiler_params=_cparams(("arbitrary",)),
    )(x, g.reshape(1, d))


def _tiles(seq):
    pick = lambda want: min(want, seq)
    return dict(tm=pick(256), tq=pick(512), tconv=pick(512), tgla=pick(512), tt=pick(256), td=pick(512),
                et=2048)


def kernel(x, positions, norm_mix, w_in, lam_q1, lam_k1, lam_q2, lam_k2, diff_norm, conv_w, gla_w_gate2, gla_b_gate, gla_norm, w_out, norm_ffn, peer_w_q, peer_keys1, peer_keys2, peer_u, peer_v, norm_final):
    batch, seq, d = x.shape
    t = batch * seq
    depth = w_in.shape[0]
    tl = _tiles(seq)
    xt = x.reshape(t, d)
    tabs = _rope_tables(positions, tl["tm"])
    for i in range(depth):
        w_in_p = jnp.pad(w_in[i], ((0, 0), (0, IN_PAD - IN_WIDTH))).astype(BF16)
        proj = _rms_matmul(xt, norm_mix[i], w_in_p, tl["tm"])
        qr, kr, vb = _rope_prep(proj, tabs, tl["tm"])
        lam_init = 0.8 - 0.6 * math.exp(-0.3 * i)
        o_da = _diff_attention(qr, kr, vb, lam_q1[i], lam_k1[i], lam_q2[i], lam_k2[i], diff_norm[i],
                               batch, seq, lam_init, tl["tq"])
        o_sc = _short_conv(proj, conv_w[i], seq, tl["tconv"])
        o_g = _gla(proj, gla_w_gate2[i], gla_b_gate[i], gla_norm[i], batch, seq, tl["tgla"])
        xt = _out_proj(xt, o_da, o_sc, o_g, w_out[i].astype(BF16), tl["tm"])
        s1t, s2t = _peer_scores(xt, norm_ffn[i], peer_w_q[i].astype(BF16), peer_keys1[i].astype(BF16),
                                peer_keys2[i].astype(BF16), tl["tm"])
        a1t, cnt, a2t, rank2 = _peer_tables(s1t, s2t, tl["tt"])
        u3, vt3 = _expert_jobs(peer_u[i], peer_v[i])
        xt = _peer_dense(xt, norm_ffn[i], u3, vt3, a1t, cnt, a2t, rank2, tl["td"], tl["et"])
    return _rmsnorm(xt, norm_final, tl["tm"]).reshape(batch, seq, d)
```

```python
import functools
import math

import jax
import jax.numpy as jnp
from jax import lax
from jax.experimental import pallas as pl
from jax.experimental.pallas import tpu as pltpu

F32 = jnp.float32
BF16 = jnp.bfloat16

D_MODEL = 1024
EPS = 1e-6
HEAD_DIM = 64
DA_HEADS = 4
DA_WIDTH = 512
ROPE_THETA = 500000.0
ROT_DIM = 16
SC_WIDTH = 256
CONV_W = 3
GLA_HEADS = 4
GLA_DK = 32
GLA_DV = 64
GLA_KW = 128
GLA_VW = 256
GLA_GATE_RANK = 16
GLA_GATE_TEMP = 16.0
GLA_CHUNK = 64
IN_WIDTH = 3088
IN_PAD = 3200
PEER_HEADS = 8
PEER_NKEYS = 128
PEER_EXPERTS = PEER_NKEYS * PEER_NKEYS
PEER_TOPK = 16
LANES = 128
BF16_SUBLANES = 16
NEG_INF = float("-inf")

COL_DA_Q, COL_DA_K, COL_DA_V = 0, 1, 2
COL_SC_B, COL_SC_C, COL_SC_H = 6, 7, 8
COL_G_V, COL_G_R = 10, 11
COL_G_Q, COL_G_K, COL_G_LR = 18, 19, 24


def _cparams(sem, vmem_mb=40):
    return pltpu.CompilerParams(dimension_semantics=sem, vmem_limit_bytes=vmem_mb * 1024 * 1024)


def _rms(x, g):
    return x * lax.rsqrt(jnp.mean(x * x, axis=-1, keepdims=True) + EPS) * g


def _rms_matmul_kernel(x_ref, g_ref, w_ref, o_ref):
    xn = _rms(x_ref[...], g_ref[...]).astype(BF16)
    o_ref[...] = jnp.dot(xn, w_ref[...], preferred_element_type=F32)


def _rms_matmul(x, g, w, tm):
    t, d = x.shape
    n = w.shape[1]
    return pl.pallas_call(
        _rms_matmul_kernel,
        grid=(t // tm,),
        in_specs=[pl.BlockSpec((tm, d), lambda i: (i, 0)),
                  pl.BlockSpec((1, d), lambda i: (0, 0)),
                  pl.BlockSpec((d, n), lambda i: (0, 0))],
        out_specs=pl.BlockSpec((tm, n), lambda i: (i, 0)),
        out_shape=jax.ShapeDtypeStruct((t, n), F32),
        compiler_params=_cparams(("arbitrary",)),
    )(x, g.reshape(1, d), w)


def _rope_table_kernel(pos_ref, inv_ref, mrot_ref, m1_ref, m2_ref, c_ref, s1_ref, s2_ref):
    ang = pos_ref[...] * inv_ref[...]
    c = jnp.cos(ang)
    s = jnp.sin(ang)
    mrot = mrot_ref[...]
    c_ref[...] = mrot * c + (1.0 - mrot)
    s1_ref[...] = -(m1_ref[...] * s)
    s2_ref[...] = m2_ref[...] * s


def _rope_tables(positions, tm):
    t = positions.size
    pos = jnp.broadcast_to(positions.reshape(t, 1).astype(F32), (t, LANES))
    d = jnp.arange(LANES) % HEAD_DIM
    half = ROT_DIM // 2
    inv = ROPE_THETA ** (-jnp.arange(0, ROT_DIM, 2, dtype=F32) / ROT_DIM)
    inv_row = jnp.where(d < ROT_DIM, inv[d % half], 0.0).astype(F32).reshape(1, LANES)
    mrot = (d < ROT_DIM).astype(F32).reshape(1, LANES)
    m1 = (d < half).astype(F32).reshape(1, LANES)
    m2 = ((d >= half) & (d < ROT_DIM)).astype(F32).reshape(1, LANES)
    row = pl.BlockSpec((1, LANES), lambda i: (0, 0))
    blk = pl.BlockSpec((tm, LANES), lambda i: (i, 0))
    return pl.pallas_call(
        _rope_table_kernel,
        grid=(t // tm,),
        in_specs=[blk, row, row, row, row],
        out_specs=[blk, blk, blk],
        out_shape=[jax.ShapeDtypeStruct((t, LANES), F32)] * 3,
        compiler_params=_cparams(("arbitrary",)),
    )(pos, inv_row, mrot, m1, m2)


def _rope_prep_kernel(q_ref, k_ref, v_ref, c_ref, s1_ref, s2_ref, qt_ref, ko_ref, vt_ref, *, qscale):
    c = c_ref[...]
    s1 = s1_ref[...]
    s2 = s2_ref[...]
    half = ROT_DIM // 2

    def rope(x):
        return x * c + pltpu.roll(x, LANES - half, 1) * s1 + pltpu.roll(x, half, 1) * s2

    for g in range(DA_WIDTH // LANES):
        sl = slice(LANES * g, LANES * (g + 1))
        qt_ref[sl, :] = (rope(q_ref[:, sl]) * qscale).T.astype(BF16)
        ko_ref[:, sl] = rope(k_ref[:, sl]).astype(BF16)
        vt_ref[0, sl, :] = v_ref[:, sl].T.astype(BF16)


def _rope_prep(proj, tabs, tm):
    t = proj.shape[0]
    c, s1, s2 = tabs
    col = lambda j: pl.BlockSpec((tm, DA_WIDTH), lambda i, j=j: (i, j))
    tab = pl.BlockSpec((tm, LANES), lambda i: (i, 0))
    return pl.pallas_call(
        functools.partial(_rope_prep_kernel, qscale=HEAD_DIM ** -0.5 * math.log2(math.e)),
        grid=(t // tm,),
        in_specs=[col(COL_DA_Q), col(COL_DA_K), col(COL_DA_V), tab, tab, tab],
        out_specs=[pl.BlockSpec((DA_WIDTH, tm), lambda i: (0, i)),
                   pl.BlockSpec((tm, DA_WIDTH), lambda i: (i, 0)),
                   pl.BlockSpec((1, DA_WIDTH, tm), lambda i: (i, 0, 0))],
        out_shape=[jax.ShapeDtypeStruct((DA_WIDTH, t), BF16),
                   jax.ShapeDtypeStruct((t, DA_WIDTH), BF16),
                   jax.ShapeDtypeStruct((t // tm, DA_WIDTH, tm), BF16)],
        compiler_params=_cparams(("arbitrary",)),
    )(proj, proj, proj, c, s1, s2)


def _attn_kernel(qt_ref, k_ref, vt_ref, lq1_ref, lk1_ref, lq2_ref, lk2_ref, gain_ref, o_ref,
                 acc1_ref, acc2_ref, m1_ref, l1_ref, m2_ref, l2_ref, s1a_ref, s2a_ref, s1b_ref, s2b_ref,
                 *, tq, lam_init):
    qi = pl.program_id(2)
    qt = qt_ref[...]
    row = lax.broadcasted_iota(jnp.int32, (LANES, 1), 0)
    zero = jnp.zeros_like(qt)
    qa = jnp.where(row < HEAD_DIM, qt, zero)
    qb = jnp.where(row >= HEAD_DIM, qt, zero)
    acc1_ref[...] = jnp.zeros_like(acc1_ref)
    acc2_ref[...] = jnp.zeros_like(acc2_ref)

    for st_ref in (m1_ref, m2_ref):
        st_ref[...] = jnp.full(st_ref.shape, NEG_INF, F32)
    for st_ref in (l1_ref, l2_ref):
        st_ref[...] = jnp.zeros_like(st_ref)
    maps = ((qa, m1_ref, l1_ref, acc1_ref), (qb, m2_ref, l2_ref, acc2_ref))

    def scores(j, bufs):
        off = pl.multiple_of(j * tq, tq)
        kk = k_ref[pl.ds(off, tq), :]
        for (qm, _, _, _), s_ref in zip(maps, bufs):
            s_ref[...] = jnp.dot(kk, qm, preferred_element_type=F32)

    def absorb(j, bufs, masked):
        vt = vt_ref[j]
        for (_, m_ref, l_ref, acc_ref), s_ref in zip(maps, bufs):
            s = s_ref[...]
            if masked:
                r = lax.broadcasted_iota(jnp.int32, (tq, tq), 0)
                c = lax.broadcasted_iota(jnp.int32, (tq, tq), 1)
                s = jnp.where(r <= c, s, NEG_INF)
            m = m_ref[...]
            mn = jnp.maximum(m, jnp.max(s, axis=0, keepdims=True))
            alpha = jnp.exp2(m - mn)
            p = jnp.exp2(s - mn)
            m_ref[...] = mn
            l_ref[...] = alpha * l_ref[...] + jnp.sum(p, axis=0, keepdims=True)
            acc_ref[...] = alpha * acc_ref[...] + jnp.dot(vt, p.astype(BF16),
                                                          preferred_element_type=F32)

    buf_a, buf_b = (s1a_ref, s2a_ref), (s1b_ref, s2b_ref)
    scores(0, buf_a)

    def pair(i, carry):
        j = 2 * i
        scores(j + 1, buf_b)
        absorb(j, buf_a, False)
        scores(j + 2, buf_a)
        absorb(j + 1, buf_b, False)
        return carry

    lax.fori_loop(0, qi // 2, pair, 0)

    @pl.when(qi % 2 == 1)
    def _():
        scores(qi, buf_b)
        absorb(qi - 1, buf_a, False)
        absorb(qi, buf_b, True)

    @pl.when(qi % 2 == 0)
    def _():
        absorb(qi, buf_a, True)

    l1, l2 = l1_ref[...], l2_ref[...]

    lam = (jnp.exp(jnp.sum(lq1_ref[...] * lk1_ref[...], axis=-1, keepdims=True))
           - jnp.exp(jnp.sum(lq2_ref[...] * lk2_ref[...], axis=-1, keepdims=True)) + lam_init)
    o = acc1_ref[...] / l1 - lam * (acc2_ref[...] / l2)
    y = o * lax.rsqrt(jnp.mean(o * o, axis=0, keepdims=True) + EPS) * gain_ref[...]
    o_ref[...] = (y * (1.0 - lam_init)).T


def _diff_attention(qt, kr, vt3, lq1, lk1, lq2, lk2, gain, batch, seq, lam_init, tq):
    t = kr.shape[0]
    nq = seq // tq
    vec = lambda n: pl.BlockSpec((1, n), lambda b, h, i: (0, 0))
    return pl.pallas_call(
        functools.partial(_attn_kernel, tq=tq, lam_init=lam_init),
        grid=(batch, DA_HEADS, nq),
        in_specs=[pl.BlockSpec((LANES, tq), lambda b, h, i: (h, b * nq + i)),
                  pl.BlockSpec((seq, LANES), lambda b, h, i: (b, h)),
                  pl.BlockSpec((nq, LANES, tq), lambda b, h, i: (b, h, 0)),
                  vec(HEAD_DIM), vec(HEAD_DIM), vec(HEAD_DIM), vec(HEAD_DIM),
                  pl.BlockSpec((LANES, 1), lambda b, h, i: (0, 0))],
        out_specs=pl.BlockSpec((tq, LANES), lambda b, h, i: (b * nq + i, h)),
        out_shape=jax.ShapeDtypeStruct((t, DA_WIDTH), F32),
        scratch_shapes=([pltpu.VMEM((LANES, tq), F32)] * 2 + [pltpu.VMEM((1, tq), F32)] * 4
                        + [pltpu.VMEM((tq, tq), F32)] * 4),
        compiler_params=_cparams(("arbitrary", "arbitrary", "arbitrary")),
    )(qt, kr, vt3, lq1.reshape(1, -1), lk1.reshape(1, -1), lq2.reshape(1, -1), lk2.reshape(1, -1),
      gain.reshape(-1, 1))


def _short_conv_kernel(b_ref, c_ref, h_ref, cp_ref, hp_ref, w_ref, o_ref, *, tm, blocks_per_seq):
    i = pl.program_id(0)
    z = c_ref[...] * h_ref[...]
    zp = cp_ref[...] * hp_ref[...]
    first = (i % blocks_per_seq) == 0
    zp = jnp.where(first, jnp.zeros_like(zp), zp)
    row = lax.broadcasted_iota(jnp.int32, (tm, 1), 0)
    z1 = jnp.where(row == 0, zp[7:8, :], pltpu.roll(z, 1, 0))
    z2 = pltpu.roll(z, 2, 0)
    z2 = jnp.where(row == 0, zp[6:7, :], jnp.where(row == 1, zp[7:8, :], z2))
    w = w_ref[...]
    y = w[0:1, :] * z2 + w[1:2, :] * z1 + w[2:3, :] * z
    o_ref[...] = b_ref[...] * y


def _short_conv(proj, w, seq, tm):
    t = proj.shape[0]
    rb = tm // 8
    cur = lambda j: pl.BlockSpec((tm, SC_WIDTH), lambda i, j=j: (i, j))
    prev = lambda j: pl.BlockSpec((8, SC_WIDTH), lambda i, j=j: (jnp.maximum(i * rb - 1, 0), j))
    return pl.pallas_call(
        functools.partial(_short_conv_kernel, tm=tm, blocks_per_seq=seq // tm),
        grid=(t // tm,),
        in_specs=[cur(COL_SC_B), cur(COL_SC_C), cur(COL_SC_H), prev(COL_SC_C), prev(COL_SC_H),
                  pl.BlockSpec((CONV_W, SC_WIDTH), lambda i: (0, 0))],
        out_specs=pl.BlockSpec((tm, SC_WIDTH), lambda i: (i, 0)),
        out_shape=jax.ShapeDtypeStruct((t, SC_WIDTH), F32),
        compiler_params=_cparams(("arbitrary",)),
    )(proj, proj, proj, proj, proj, w)


def _split3(x):
    hi = x.astype(BF16)
    r = x - hi.astype(F32)
    mid = r.astype(BF16)
    lo = (r - mid.astype(F32)).astype(BF16)
    return hi, mid, lo


def _gla_kernel(q_ref, k_ref, v_ref, r_ref, lr_ref, wg_ref, bg_ref, gn_ref, o_ref, st_ref, *, tc):
    ch = GLA_CHUNK

    @pl.when(pl.program_id(1) == 0)
    def _():
        st_ref[...] = jnp.zeros_like(st_ref)

    ii = lax.broadcasted_iota(jnp.int32, (ch, ch), 0)
    jj = lax.broadcasted_iota(jnp.int32, (ch, ch), 1)
    tri = (ii >= jj).astype(BF16)
    rk = lax.broadcasted_iota(jnp.int32, (GLA_VW, GLA_KW), 0)
    ck = lax.broadcasted_iota(jnp.int32, (GLA_VW, GLA_KW), 1)
    bd_k = (rk // GLA_DV) == (ck // GLA_DK)
    rv = lax.broadcasted_iota(jnp.int32, (GLA_VW, GLA_VW), 0)
    cv = lax.broadcasted_iota(jnp.int32, (GLA_VW, GLA_VW), 1)
    bd_v = (rv // GLA_DV) == (cv // GLA_DV)
    seg = bd_v.astype(BF16)
    ri = lax.broadcasted_iota(jnp.int32, (ch, GLA_VW), 0)
    ci = lax.broadcasted_iota(jnp.int32, (ch, GLA_VW), 1)
    causal = (ci % ch) <= ri
    dn_t = (((1,), (1,)), ((), ()))
    scale = GLA_DK ** -0.5

    def chunk(c, carry):
        off = pl.multiple_of(c * ch, ch)
        rows = pl.ds(off, ch)
        x = jnp.dot(lr_ref[rows, :].astype(BF16), wg_ref[...], preferred_element_type=F32) + bg_ref[...]
        la = (jnp.minimum(x, 0.0) - jnp.log1p(jnp.exp(-jnp.abs(x)))) * (1.0 / GLA_GATE_TEMP)
        hi, mid, lo = _split3(la)
        b = (jnp.dot(tri, hi, preferred_element_type=F32) + jnp.dot(tri, mid, preferred_element_type=F32)
             + jnp.dot(tri, lo, preferred_element_type=F32))
        b_mid = b[ch // 2 - 1:ch // 2, :]
        b_last = b[ch - 1:ch, :]
        q = q_ref[rows, :] * scale
        k = k_ref[rows, :]
        v = v_ref[rows, :]
        qt = (q * jnp.exp(b - b_mid)).astype(BF16)
        kt = (k * jnp.exp(b_mid - b)).astype(BF16)
        kbd = jnp.where(bd_k, jnp.concatenate([kt] * GLA_HEADS, axis=0), jnp.zeros((), BF16))
        att = lax.dot_general(qt, kbd, dn_t, preferred_element_type=F32)
        att = jnp.where(causal, att, 0.0).astype(BF16)
        vb = v.astype(BF16)
        vbd = jnp.where(bd_v, jnp.concatenate([vb] * GLA_HEADS, axis=0), jnp.zeros((), BF16))
        o = jnp.dot(att, vbd, preferred_element_type=F32)
        st = st_ref[...]
        bq = (q * jnp.exp(b)).astype(BF16)
        o = o + lax.dot_general(bq, st.astype(BF16), dn_t, preferred_element_type=F32)
        kl = (k * jnp.exp(b_last - b)).astype(BF16)
        ut = jnp.dot(v.T.astype(BF16), kl, preferred_element_type=F32)
        st_ref[...] = st * jnp.exp(b_last) + jnp.where(bd_k, ut, 0.0)
        o2 = o * o
        o2h = o2.astype(BF16)
        o2l = (o2 - o2h.astype(F32)).astype(BF16)
        ms = (jnp.dot(o2h, seg, preferred_element_type=F32)
              + jnp.dot(o2l, seg, preferred_element_type=F32)) * (1.0 / GLA_DV)
        r = r_ref[rows, :]
        gate = r * (1.0 / (1.0 + jnp.exp(-r)))
        o_ref[rows, :] = o * lax.rsqrt(ms + EPS) * gn_ref[...] * gate
        return carry

    lax.fori_loop(0, tc // ch, chunk, 0)


def _gla(proj, wg, bg, gn, batch, seq, tc):
    t = proj.shape[0]
    nb = seq // tc
    blk = lambda w, j: pl.BlockSpec((tc, w), lambda b, i, j=j: (b * nb + i, j))
    const = lambda r, c: pl.BlockSpec((r, c), lambda b, i: (0, 0))
    wg_pad = jnp.zeros((LANES, GLA_KW), F32).at[:GLA_GATE_RANK].set(wg).astype(BF16)
    gn_row = jnp.tile(gn, GLA_HEADS).reshape(1, GLA_VW)
    return pl.pallas_call(
        functools.partial(_gla_kernel, tc=tc),
        grid=(batch, nb),
        in_specs=[blk(GLA_KW, COL_G_Q), blk(GLA_KW, COL_G_K), blk(GLA_VW, COL_G_V), blk(GLA_VW, COL_G_R),
                  blk(LANES, COL_G_LR), const(LANES, GLA_KW), const(1, GLA_KW), const(1, GLA_VW)],
        out_specs=pl.BlockSpec((tc, GLA_VW), lambda b, i: (b * nb + i, 0)),
        out_shape=jax.ShapeDtypeStruct((t, GLA_VW), F32),
        scratch_shapes=[pltpu.VMEM((GLA_VW, GLA_KW), F32)],
        compiler_params=_cparams(("arbitrary", "arbitrary")),
    )(proj, proj, proj, proj, proj, wg_pad, bg.reshape(1, GLA_KW), gn_row)


def _out_proj_kernel(x_ref, a_ref, s_ref, g_ref, w_ref, o_ref):
    acc = jnp.dot(a_ref[...].astype(BF16), w_ref[0:DA_WIDTH, :], preferred_element_type=F32)
    acc += jnp.dot(s_ref[...].astype(BF16), w_ref[DA_WIDTH:DA_WIDTH + SC_WIDTH, :],
                   preferred_element_type=F32)
    acc += jnp.dot(g_ref[...].astype(BF16), w_ref[DA_WIDTH + SC_WIDTH:, :], preferred_element_type=F32)
    o_ref[...] = x_ref[...] + acc


def _out_proj(x, o_da, o_sc, o_g, w, tm):
    t, d = x.shape
    blk = lambda n: pl.BlockSpec((tm, n), lambda i: (i, 0))
    return pl.pallas_call(
        _out_proj_kernel,
        grid=(t // tm,),
        in_specs=[blk(d), blk(DA_WIDTH), blk(SC_WIDTH), blk(GLA_VW),
                  pl.BlockSpec(w.shape, lambda i: (0, 0))],
        out_specs=blk(d),
        out_shape=jax.ShapeDtypeStruct((t, d), F32),
        compiler_params=_cparams(("arbitrary",)),
    )(x, o_da, o_sc, o_g, w)


def _peer_scores_kernel(x_ref, g_ref, wq_ref, k1_ref, k2_ref, s1_ref, s2_ref):
    xn = _rms(x_ref[...], g_ref[...]).astype(BF16)
    q = jnp.dot(xn, wq_ref[...], preferred_element_type=F32)
    dn = (((1,), (1,)), ((), ()))
    for h in range(PEER_HEADS):
        qa = q[:, (2 * h) * LANES:(2 * h + 1) * LANES].astype(BF16)
        qb = q[:, (2 * h + 1) * LANES:(2 * h + 2) * LANES].astype(BF16)
        s1_ref[h] = lax.dot_general(k1_ref[...], qa, dn, preferred_element_type=F32)
        s2_ref[h] = lax.dot_general(k2_ref[...], qb, dn, preferred_element_type=F32)


def _peer_scores(x, g, wq, k1, k2, tm):
    t, d = x.shape
    out = pl.BlockSpec((PEER_HEADS, PEER_NKEYS, tm), lambda i: (0, 0, i))
    return pl.pallas_call(
        _peer_scores_kernel,
        grid=(t // tm,),
        in_specs=[pl.BlockSpec((tm, d), lambda i: (i, 0)),
                  pl.BlockSpec((1, d), lambda i: (0, 0)),
                  pl.BlockSpec(wq.shape, lambda i: (0, 0)),
                  pl.BlockSpec(k1.shape, lambda i: (0, 0)),
                  pl.BlockSpec(k2.shape, lambda i: (0, 0))],
        out_specs=[out, out],
        out_shape=[jax.ShapeDtypeStruct((PEER_HEADS, PEER_NKEYS, t), F32)] * 2,
        compiler_params=_cparams(("arbitrary",)),
    )(x, g.reshape(1, d), wq, k1, k2)


def _top_desc(x, n):
    vals = []
    cur = x
    for r in range(n):
        m = jnp.max(cur, axis=0, keepdims=True)
        vals.append(m)
        if r + 1 < n:
            cur = jnp.where(cur == m, NEG_INF, cur)
    return vals


def _top_desc_ranked(x, n):
    vals = []
    cur = x
    rank = jnp.full(x.shape, float(n), F32)
    for r in range(n):
        m = jnp.max(cur, axis=0, keepdims=True)
        vals.append(m)
        hit = cur == m
        rank = jnp.where(hit, float(r), rank)
        if r + 1 < n:
            cur = jnp.where(hit, NEG_INF, cur)
    return vals, rank


def _peer_tables_kernel(s1_ref, s2_ref, a1_ref, cnt_ref, a2_ref, rank2_ref):
    k = PEER_TOPK
    tt = s1_ref.shape[-1]
    for h in range(PEER_HEADS):
        s1 = s1_ref[h]
        s2 = s2_ref[h]
        v1 = _top_desc(s1, k)
        v2, rank2 = _top_desc_ranked(s2, k)
        cands = [v1[i] + v2[j] for i in range(k) for j in range(k // (i + 1))]
        pad = (-len(cands)) % 8
        cands += [jnp.full((1, tt), NEG_INF, F32)] * pad
        top = _top_desc(jnp.concatenate(cands, axis=0), k)
        z = jnp.ones((1, tt), F32)
        for r in range(1, k):
            z = z + jnp.exp(top[r] - top[0])
        thr = top[k - 1]
        cnt = jnp.zeros_like(s1)
        for j in range(k):
            cnt = cnt + ((s1 + v2[j]) >= thr).astype(F32)
        cnt_ref[h] = jnp.where(s1 >= v1[k - 1], cnt, 0.0)
        rank2_ref[h] = rank2.astype(BF16)
        a1_ref[h] = jnp.exp(s1 - v1[0]) * (1.0 / z)
        a2_ref[h] = jnp.exp(s2 - v2[0]).astype(BF16)


def _peer_tables(s1t, s2t, tt):
    t = s1t.shape[-1]
    blk = pl.BlockSpec((PEER_HEADS, PEER_NKEYS, tt), lambda i: (0, 0, i))
    f32 = jax.ShapeDtypeStruct(s1t.shape, F32)
    bf16 = jax.ShapeDtypeStruct(s1t.shape, BF16)
    return pl.pallas_call(
        _peer_tables_kernel,
        grid=(t // tt,),
        in_specs=[blk, blk],
        out_specs=[blk, blk, blk, blk],
        out_shape=[f32, f32, bf16, bf16],
        compiler_params=_cparams(("arbitrary",)),
    )(s1t, s2t)


def _peer_dense_kernel(x_ref, g_ref, u_ref, vt_ref, a1_ref, cnt_ref, a2_ref, rank2_ref, o_ref,
                       xnt_ref, acc_ref, hta_ref, htb_ref, pta_ref, ptb_ref):
    e = pl.program_id(1)

    @pl.when(e == 0)
    def _():
        xn = _rms(x_ref[...], g_ref[...])
        xnt_ref[...] = xn.T.astype(BF16)
        acc_ref[...] = jnp.zeros_like(acc_ref)

    inv_sqrt2 = 1.0 / math.sqrt(2.0)
    zero = jnp.zeros((), BF16)
    n_q, job_rows, _ = u_ref.shape
    sub = xnt_ref.shape[1] // 2
    n1_per_job = job_rows // PEER_NKEYS
    pack = BF16_SUBLANES
    groups = PEER_NKEYS // pack
    col_a, col_b = slice(0, sub), slice(sub, 2 * sub)
    ht_refs = {col_a: hta_ref, col_b: htb_ref}
    pt_refs = {col_a: pta_ref, col_b: ptb_ref}

    def matmul1(q, cols):
        ht_refs[cols][...] = jnp.dot(u_ref[q], xnt_ref[:, cols], preferred_element_type=F32)

    def gate(q, cols):
        ht_ref, pt_ref = ht_refs[cols], pt_refs[cols]
        for rl in range(n1_per_job):
            r = q * n1_per_job + rl
            w = None
            for h in range(PEER_HEADS):
                cnt = jnp.broadcast_to(cnt_ref[h, pl.ds(r, 1), cols], (pack, sub)).astype(BF16)
                a1 = jnp.broadcast_to(a1_ref[h, pl.ds(r, 1), cols], (pack, sub)).astype(BF16)
                sel = rank2_ref[h, :, :, cols] < cnt[None]
                term = a1[None] * jnp.where(sel, a2_ref[h, :, :, cols], zero)
                w = term if w is None else w + term
            rows = slice(rl * PEER_NKEYS, (rl + 1) * PEER_NKEYS)
            hb = ht_ref[rows, :]
            gelu = 0.5 * hb * (1.0 + lax.erf(hb * inv_sqrt2))
            p = w * gelu.astype(BF16).reshape(groups, pack, sub)
            pt_ref[rows, :] = p.reshape(PEER_NKEYS, sub)

    def matmul2(q, cols):
        acc_ref[:, cols] += jnp.dot(vt_ref[q], pt_refs[cols][...], preferred_element_type=F32)

    def pair(q, has_prev, has_next):
        matmul1(q, col_b)
        gate(q, col_a)
        if has_prev:
            matmul2(q - 1, col_b)
        if has_next:
            matmul1(q + 1, col_a)
        gate(q, col_b)
        matmul2(q, col_a)

    matmul1(0, col_a)
    if n_q == 1:
        pair(0, False, False)
    else:
        pair(0, False, True)

        def body(q, carry):
            pair(q, True, True)
            return carry

        lax.fori_loop(1, n_q - 1, body, 0)
        pair(n_q - 1, True, False)
    matmul2(n_q - 1, col_b)

    @pl.when(e == pl.num_programs(1) - 1)
    def _():
        o_ref[...] = x_ref[...] + acc_ref[...].T


PEER_JOB_ROWS = 512


def _expert_jobs(u, v):
    n_exp, d = u.shape
    n_jobs = n_exp // PEER_JOB_ROWS
    u3 = u.astype(BF16).reshape(n_jobs, PEER_JOB_ROWS, d)
    vt3 = jnp.transpose(v.astype(BF16).reshape(n_jobs, PEER_JOB_ROWS, d), (0, 2, 1))
    return u3, vt3


def _peer_dense(x, g, u, vt, a1t, cnt, a2t, rank2, tt, et):
    t, d = x.shape
    n_jobs, job_rows, _ = u.shape
    n_q = et // job_rows
    n1_per_tile = et // PEER_NKEYS
    groups = PEER_NKEYS // BF16_SUBLANES
    grouped = (PEER_HEADS, groups, BF16_SUBLANES, t)
    full = pl.BlockSpec((PEER_HEADS, groups, BF16_SUBLANES, tt), lambda i, e: (0, 0, 0, i))
    part = pl.BlockSpec((PEER_HEADS, n1_per_tile, tt), lambda i, e: (0, e, i))
    sub = tt // 2
    return pl.pallas_call(
        _peer_dense_kernel,
        grid=(t // tt, n_jobs // n_q),
        in_specs=[pl.BlockSpec((tt, d), lambda i, e: (i, 0)),
                  pl.BlockSpec((1, d), lambda i, e: (0, 0)),
                  pl.BlockSpec((n_q, job_rows, d), lambda i, e: (e, 0, 0)),
                  pl.BlockSpec((n_q, d, job_rows), lambda i, e: (e, 0, 0)),
                  part, part, full, full],
        out_specs=pl.BlockSpec((tt, d), lambda i, e: (i, 0)),
        out_shape=jax.ShapeDtypeStruct((t, d), F32),
        scratch_shapes=[pltpu.VMEM((d, tt), BF16), pltpu.VMEM((d, tt), F32),
                        pltpu.VMEM((job_rows, sub), F32), pltpu.VMEM((job_rows, sub), F32),
                        pltpu.VMEM((job_rows, sub), BF16), pltpu.VMEM((job_rows, sub), BF16)],
        compiler_params=_cparams(("arbitrary", "arbitrary"), vmem_mb=52),
    )(x, g.reshape(1, d), u, vt, a1t, cnt, a2t.reshape(grouped), rank2.reshape(grouped))


def _rmsnorm_kernel(x_ref, g_ref, o_ref):
    o_ref[...] = _rms(x_ref[...], g_ref[...])


def _rmsnorm(x, g, tm):
    t, d = x.shape
    return pl.pallas_call(
        _rmsnorm_kernel,
        grid=(t // tm,),
        in_specs=[pl.BlockSpec((tm, d), lambda i: (i, 0)), pl.BlockSpec((1, d), lambda i: (0, 0))],
        out_specs=pl.BlockSpec((tm, d), lambda i: (i, 0)),
        out_shape=jax.ShapeDtypeStruct((t, d), F32),
        compiler_params=_cparams(("arbitrary",)),
    )(x, g.reshape(1, d))


def _tiles(seq):
    pick = lambda want: min(want, seq)
    return dict(tm=pick(256), tq=pick(512), tconv=pick(512), tgla=pick(512), tt=pick(256), td=pick(512),
                et=2048)


def kernel(x, positions, norm_mix, w_in, lam_q1, lam_k1, lam_q2, lam_k2, diff_norm, conv_w, gla_w_gate2, gla_b_gate, gla_norm, w_out, norm_ffn, peer_w_q, peer_keys1, peer_keys2, peer_u, peer_v, norm_final):
    batch, seq, d = x.shape
    t = batch * seq
    depth = w_in.shape[0]
    tl = _tiles(seq)
    xt = x.reshape(t, d)
    tabs = _rope_tables(positions, tl["tm"])
    for i in range(depth):
        w_in_p = jnp.pad(w_in[i], ((0, 0), (0, IN_PAD - IN_WIDTH))).astype(BF16)
        proj = _rms_matmul(xt, norm_mix[i], w_in_p, tl["tm"])
        qt, kr, vt3 = _rope_prep(proj, tabs, tl["tq"])
        lam_init = 0.8 - 0.6 * math.exp(-0.3 * i)
        o_da = _diff_attention(qt, kr, vt3, lam_q1[i], lam_k1[i], lam_q2[i], lam_k2[i], diff_norm[i],
                               batch, seq, lam_init, tl["tq"])
        o_sc = _short_conv(proj, conv_w[i], seq, tl["tconv"])
        o_g = _gla(proj, gla_w_gate2[i], gla_b_gate[i], gla_norm[i], batch, seq, tl["tgla"])
        xt = _out_proj(xt, o_da, o_sc, o_g, w_out[i].astype(BF16), tl["tm"])
        s1t, s2t = _peer_scores(xt, norm_ffn[i], peer_w_q[i].astype(BF16), peer_keys1[i].astype(BF16),
                                peer_keys2[i].astype(BF16), tl["tm"])
        a1t, cnt, a2t, rank2 = _peer_tables(s1t, s2t, tl["tt"])
        u3, vt3 = _expert_jobs(peer_u[i], peer_v[i])
        xt = _peer_dense(xt, norm_ffn[i], u3, vt3, a1t, cnt, a2t, rank2, tl["td"], tl["et"])
    return _rmsnorm(xt, norm_final, tl["tm"]).reshape(batch, seq, d)
```

```python
import functools
import math

import jax
import jax.numpy as jnp
from jax import lax
from jax.experimental import pallas as pl
from jax.experimental.pallas import tpu as pltpu

F32 = jnp.float32
BF16 = jnp.bfloat16

D_MODEL = 1024
EPS = 1e-6
HEAD_DIM = 64
DA_HEADS = 4
DA_WIDTH = 512
ROPE_THETA = 500000.0
ROT_DIM = 16
SC_WIDTH = 256
CONV_W = 3
GLA_HEADS = 4
GLA_DK = 32
GLA_DV = 64
GLA_KW = 128
GLA_VW = 256
GLA_GATE_RANK = 16
GLA_GATE_TEMP = 16.0
GLA_CHUNK = 64
IN_WIDTH = 3088
IN_PAD = 3200
PEER_HEADS = 8
PEER_NKEYS = 128
PEER_EXPERTS = PEER_NKEYS * PEER_NKEYS
PEER_TOPK = 16
LANES = 128
BF16_SUBLANES = 16
NEG_INF = float("-inf")

COL_DA_Q, COL_DA_K, COL_DA_V = 0, 1, 2
COL_SC_B, COL_SC_C, COL_SC_H = 6, 7, 8
COL_G_V, COL_G_R = 10, 11
COL_G_Q, COL_G_K, COL_G_LR = 18, 19, 24


def _cparams(sem, vmem_mb=40):
    return pltpu.CompilerParams(dimension_semantics=sem, vmem_limit_bytes=vmem_mb * 1024 * 1024)


def _rms(x, g):
    return x * lax.rsqrt(jnp.mean(x * x, axis=-1, keepdims=True) + EPS) * g


def _rms_matmul_kernel(x_ref, g_ref, w_ref, o_ref):
    xn = _rms(x_ref[...], g_ref[...]).astype(BF16)
    o_ref[...] = jnp.dot(xn, w_ref[...], preferred_element_type=F32)


def _rms_matmul(x, g, w, tm):
    t, d = x.shape
    n = w.shape[1]
    return pl.pallas_call(
        _rms_matmul_kernel,
        grid=(t // tm,),
        in_specs=[pl.BlockSpec((tm, d), lambda i: (i, 0)),
                  pl.BlockSpec((1, d), lambda i: (0, 0)),
                  pl.BlockSpec((d, n), lambda i: (0, 0))],
        out_specs=pl.BlockSpec((tm, n), lambda i: (i, 0)),
        out_shape=jax.ShapeDtypeStruct((t, n), F32),
        compiler_params=_cparams(("arbitrary",)),
    )(x, g.reshape(1, d), w)


def _rope_table_kernel(pos_ref, inv_ref, mrot_ref, m1_ref, m2_ref, c_ref, s1_ref, s2_ref):
    ang = pos_ref[...] * inv_ref[...]
    c = jnp.cos(ang)
    s = jnp.sin(ang)
    mrot = mrot_ref[...]
    c_ref[...] = mrot * c + (1.0 - mrot)
    s1_ref[...] = -(m1_ref[...] * s)
    s2_ref[...] = m2_ref[...] * s


def _rope_tables(positions, tm):
    t = positions.size
    pos = jnp.broadcast_to(positions.reshape(t, 1).astype(F32), (t, LANES))
    d = jnp.arange(LANES) % HEAD_DIM
    half = ROT_DIM // 2
    inv = ROPE_THETA ** (-jnp.arange(0, ROT_DIM, 2, dtype=F32) / ROT_DIM)
    inv_row = jnp.where(d < ROT_DIM, inv[d % half], 0.0).astype(F32).reshape(1, LANES)
    mrot = (d < ROT_DIM).astype(F32).reshape(1, LANES)
    m1 = (d < half).astype(F32).reshape(1, LANES)
    m2 = ((d >= half) & (d < ROT_DIM)).astype(F32).reshape(1, LANES)
    row = pl.BlockSpec((1, LANES), lambda i: (0, 0))
    blk = pl.BlockSpec((tm, LANES), lambda i: (i, 0))
    return pl.pallas_call(
        _rope_table_kernel,
        grid=(t // tm,),
        in_specs=[blk, row, row, row, row],
        out_specs=[blk, blk, blk],
        out_shape=[jax.ShapeDtypeStruct((t, LANES), F32)] * 3,
        compiler_params=_cparams(("arbitrary",)),
    )(pos, inv_row, mrot, m1, m2)


def _rope_prep_kernel(q_ref, k_ref, v_ref, c_ref, s1_ref, s2_ref, qt_ref, ko_ref, vt_ref, *, qscale):
    c = c_ref[...]
    s1 = s1_ref[...]
    s2 = s2_ref[...]
    half = ROT_DIM // 2

    def rope(x):
        return x * c + pltpu.roll(x, LANES - half, 1) * s1 + pltpu.roll(x, half, 1) * s2

    for g in range(DA_WIDTH // LANES):
        sl = slice(LANES * g, LANES * (g + 1))
        qt_ref[sl, :] = (rope(q_ref[:, sl]) * qscale).T.astype(BF16)
        ko_ref[:, sl] = rope(k_ref[:, sl]).astype(BF16)
        vt_ref[0, sl, :] = v_ref[:, sl].T.astype(BF16)


def _rope_prep(proj, tabs, tm):
    t = proj.shape[0]
    c, s1, s2 = tabs
    col = lambda j: pl.BlockSpec((tm, DA_WIDTH), lambda i, j=j: (i, j))
    tab = pl.BlockSpec((tm, LANES), lambda i: (i, 0))
    return pl.pallas_call(
        functools.partial(_rope_prep_kernel, qscale=HEAD_DIM ** -0.5 * math.log2(math.e)),
        grid=(t // tm,),
        in_specs=[col(COL_DA_Q), col(COL_DA_K), col(COL_DA_V), tab, tab, tab],
        out_specs=[pl.BlockSpec((DA_WIDTH, tm), lambda i: (0, i)),
                   pl.BlockSpec((tm, DA_WIDTH), lambda i: (i, 0)),
                   pl.BlockSpec((1, DA_WIDTH, tm), lambda i: (i, 0, 0))],
        out_shape=[jax.ShapeDtypeStruct((DA_WIDTH, t), BF16),
                   jax.ShapeDtypeStruct((t, DA_WIDTH), BF16),
                   jax.ShapeDtypeStruct((t // tm, DA_WIDTH, tm), BF16)],
        compiler_params=_cparams(("arbitrary",)),
    )(proj, proj, proj, c, s1, s2)


def _attn_kernel(qt_ref, k_ref, vt_ref, lq1_ref, lk1_ref, lq2_ref, lk2_ref, gain_ref, o_ref,
                 acc1_ref, acc2_ref, m1_ref, l1_ref, m2_ref, l2_ref, s1a_ref, s2a_ref, s1b_ref, s2b_ref,
                 *, tq, lam_init):
    qi = pl.program_id(2)
    qt = qt_ref[...]
    row = lax.broadcasted_iota(jnp.int32, (LANES, 1), 0)
    zero = jnp.zeros_like(qt)
    qa = jnp.where(row < HEAD_DIM, qt, zero)
    qb = jnp.where(row >= HEAD_DIM, qt, zero)
    acc1_ref[...] = jnp.zeros_like(acc1_ref)
    acc2_ref[...] = jnp.zeros_like(acc2_ref)

    for st_ref in (m1_ref, m2_ref):
        st_ref[...] = jnp.full(st_ref.shape, NEG_INF, F32)
    for st_ref in (l1_ref, l2_ref):
        st_ref[...] = jnp.zeros_like(st_ref)
    maps = ((qa, m1_ref, l1_ref, acc1_ref), (qb, m2_ref, l2_ref, acc2_ref))

    def scores(j, bufs):
        off = pl.multiple_of(j * tq, tq)
        kk = k_ref[pl.ds(off, tq), :]
        for (qm, _, _, _), s_ref in zip(maps, bufs):
            s_ref[...] = jnp.dot(kk, qm, preferred_element_type=F32)

    def absorb(j, bufs, masked):
        vt = vt_ref[j]
        for (_, m_ref, l_ref, acc_ref), s_ref in zip(maps, bufs):
            s = s_ref[...]
            if masked:
                r = lax.broadcasted_iota(jnp.int32, (tq, tq), 0)
                c = lax.broadcasted_iota(jnp.int32, (tq, tq), 1)
                s = jnp.where(r <= c, s, NEG_INF)
            m = m_ref[...]
            mn = jnp.maximum(m, jnp.max(s, axis=0, keepdims=True))
            alpha = jnp.exp2(m - mn)
            p = jnp.exp2(s - mn)
            m_ref[...] = mn
            l_ref[...] = alpha * l_ref[...] + jnp.sum(p, axis=0, keepdims=True)
            acc_ref[...] = alpha * acc_ref[...] + jnp.dot(vt, p.astype(BF16),
                                                          preferred_element_type=F32)

    buf_a, buf_b = (s1a_ref, s2a_ref), (s1b_ref, s2b_ref)
    scores(0, buf_a)

    def pair(i, carry):
        j = 2 * i
        scores(j + 1, buf_b)
        absorb(j, buf_a, False)
        scores(j + 2, buf_a)
        absorb(j + 1, buf_b, False)
        return carry

    lax.fori_loop(0, qi // 2, pair, 0)

    @pl.when(qi % 2 == 1)
    def _():
        scores(qi, buf_b)
        absorb(qi - 1, buf_a, False)
        absorb(qi, buf_b, True)

    @pl.when(qi % 2 == 0)
    def _():
        absorb(qi, buf_a, True)

    l1, l2 = l1_ref[...], l2_ref[...]

    lam = (jnp.exp(jnp.sum(lq1_ref[...] * lk1_ref[...], axis=-1, keepdims=True))
           - jnp.exp(jnp.sum(lq2_ref[...] * lk2_ref[...], axis=-1, keepdims=True)) + lam_init)
    o = acc1_ref[...] / l1 - lam * (acc2_ref[...] / l2)
    y = o * lax.rsqrt(jnp.mean(o * o, axis=0, keepdims=True) + EPS) * gain_ref[...]
    o_ref[...] = (y * (1.0 - lam_init)).T


def _diff_attention(qt, kr, vt3, lq1, lk1, lq2, lk2, gain, batch, seq, lam_init, tq):
    t = kr.shape[0]
    nq = seq // tq
    vec = lambda n: pl.BlockSpec((1, n), lambda b, h, i: (0, 0))
    return pl.pallas_call(
        functools.partial(_attn_kernel, tq=tq, lam_init=lam_init),
        grid=(batch, DA_HEADS, nq),
        in_specs=[pl.BlockSpec((LANES, tq), lambda b, h, i: (h, b * nq + i)),
                  pl.BlockSpec((seq, LANES), lambda b, h, i: (b, h)),
                  pl.BlockSpec((nq, LANES, tq), lambda b, h, i: (b, h, 0)),
                  vec(HEAD_DIM), vec(HEAD_DIM), vec(HEAD_DIM), vec(HEAD_DIM),
                  pl.BlockSpec((LANES, 1), lambda b, h, i: (0, 0))],
        out_specs=pl.BlockSpec((tq, LANES), lambda b, h, i: (b * nq + i, h)),
        out_shape=jax.ShapeDtypeStruct((t, DA_WIDTH), F32),
        scratch_shapes=([pltpu.VMEM((LANES, tq), F32)] * 2 + [pltpu.VMEM((1, tq), F32)] * 4
                        + [pltpu.VMEM((tq, tq), F32)] * 4),
        compiler_params=_cparams(("arbitrary", "arbitrary", "arbitrary")),
    )(qt, kr, vt3, lq1.reshape(1, -1), lk1.reshape(1, -1), lq2.reshape(1, -1), lk2.reshape(1, -1),
      gain.reshape(-1, 1))


def _short_conv_kernel(b_ref, c_ref, h_ref, cp_ref, hp_ref, w_ref, o_ref, *, tm, blocks_per_seq):
    i = pl.program_id(0)
    z = c_ref[...] * h_ref[...]
    zp = cp_ref[...] * hp_ref[...]
    first = (i % blocks_per_seq) == 0
    zp = jnp.where(first, jnp.zeros_like(zp), zp)
    row = lax.broadcasted_iota(jnp.int32, (tm, 1), 0)
    z1 = jnp.where(row == 0, zp[7:8, :], pltpu.roll(z, 1, 0))
    z2 = pltpu.roll(z, 2, 0)
    z2 = jnp.where(row == 0, zp[6:7, :], jnp.where(row == 1, zp[7:8, :], z2))
    w = w_ref[...]
    y = w[0:1, :] * z2 + w[1:2, :] * z1 + w[2:3, :] * z
    o_ref[...] = b_ref[...] * y


def _short_conv(proj, w, seq, tm):
    t = proj.shape[0]
    rb = tm // 8
    cur = lambda j: pl.BlockSpec((tm, SC_WIDTH), lambda i, j=j: (i, j))
    prev = lambda j: pl.BlockSpec((8, SC_WIDTH), lambda i, j=j: (jnp.maximum(i * rb - 1, 0), j))
    return pl.pallas_call(
        functools.partial(_short_conv_kernel, tm=tm, blocks_per_seq=seq // tm),
        grid=(t // tm,),
        in_specs=[cur(COL_SC_B), cur(COL_SC_C), cur(COL_SC_H), prev(COL_SC_C), prev(COL_SC_H),
                  pl.BlockSpec((CONV_W, SC_WIDTH), lambda i: (0, 0))],
        out_specs=pl.BlockSpec((tm, SC_WIDTH), lambda i: (i, 0)),
        out_shape=jax.ShapeDtypeStruct((t, SC_WIDTH), F32),
        compiler_params=_cparams(("arbitrary",)),
    )(proj, proj, proj, proj, proj, w)


def _split3(x):
    hi = x.astype(BF16)
    r = x - hi.astype(F32)
    mid = r.astype(BF16)
    lo = (r - mid.astype(F32)).astype(BF16)
    return hi, mid, lo


def _gla_kernel(q_ref, k_ref, v_ref, r_ref, lr_ref, wg_ref, bg_ref, gn_ref, o_ref, st_ref, *, tc):
    ch = GLA_CHUNK

    @pl.when(pl.program_id(1) == 0)
    def _():
        st_ref[...] = jnp.zeros_like(st_ref)

    ii = lax.broadcasted_iota(jnp.int32, (ch, ch), 0)
    jj = lax.broadcasted_iota(jnp.int32, (ch, ch), 1)
    tri = (ii >= jj).astype(BF16)
    rk = lax.broadcasted_iota(jnp.int32, (GLA_VW, GLA_KW), 0)
    ck = lax.broadcasted_iota(jnp.int32, (GLA_VW, GLA_KW), 1)
    bd_k = (rk // GLA_DV) == (ck // GLA_DK)
    rv = lax.broadcasted_iota(jnp.int32, (GLA_VW, GLA_VW), 0)
    cv = lax.broadcasted_iota(jnp.int32, (GLA_VW, GLA_VW), 1)
    bd_v = (rv // GLA_DV) == (cv // GLA_DV)
    seg = bd_v.astype(BF16)
    ri = lax.broadcasted_iota(jnp.int32, (ch, GLA_VW), 0)
    ci = lax.broadcasted_iota(jnp.int32, (ch, GLA_VW), 1)
    causal = (ci % ch) <= ri
    dn_t = (((1,), (1,)), ((), ()))
    scale = GLA_DK ** -0.5

    def chunk(c, carry):
        off = pl.multiple_of(c * ch, ch)
        rows = pl.ds(off, ch)
        x = jnp.dot(lr_ref[rows, :].astype(BF16), wg_ref[...], preferred_element_type=F32) + bg_ref[...]
        la = (jnp.minimum(x, 0.0) - jnp.log1p(jnp.exp(-jnp.abs(x)))) * (1.0 / GLA_GATE_TEMP)
        hi, mid, lo = _split3(la)
        b = (jnp.dot(tri, hi, preferred_element_type=F32) + jnp.dot(tri, mid, preferred_element_type=F32)
             + jnp.dot(tri, lo, preferred_element_type=F32))
        b_mid = b[ch // 2 - 1:ch // 2, :]
        b_last = b[ch - 1:ch, :]
        q = q_ref[rows, :] * scale
        k = k_ref[rows, :]
        v = v_ref[rows, :]
        qt = (q * jnp.exp(b - b_mid)).astype(BF16)
        kt = (k * jnp.exp(b_mid - b)).astype(BF16)
        kbd = jnp.where(bd_k, jnp.concatenate([kt] * GLA_HEADS, axis=0), jnp.zeros((), BF16))
        att = lax.dot_general(qt, kbd, dn_t, preferred_element_type=F32)
        att = jnp.where(causal, att, 0.0).astype(BF16)
        vb = v.astype(BF16)
        vbd = jnp.where(bd_v, jnp.concatenate([vb] * GLA_HEADS, axis=0), jnp.zeros((), BF16))
        o = jnp.dot(att, vbd, preferred_element_type=F32)
        st = st_ref[...]
        bq = (q * jnp.exp(b)).astype(BF16)
        o = o + lax.dot_general(bq, st.astype(BF16), dn_t, preferred_element_type=F32)
        kl = (k * jnp.exp(b_last - b)).astype(BF16)
        ut = jnp.dot(v.T.astype(BF16), kl, preferred_element_type=F32)
        st_ref[...] = st * jnp.exp(b_last) + jnp.where(bd_k, ut, 0.0)
        o2 = o * o
        o2h = o2.astype(BF16)
        o2l = (o2 - o2h.astype(F32)).astype(BF16)
        ms = (jnp.dot(o2h, seg, preferred_element_type=F32)
              + jnp.dot(o2l, seg, preferred_element_type=F32)) * (1.0 / GLA_DV)
        r = r_ref[rows, :]
        gate = r * (1.0 / (1.0 + jnp.exp(-r)))
        o_ref[rows, :] = o * lax.rsqrt(ms + EPS) * gn_ref[...] * gate
        return carry

    lax.fori_loop(0, tc // ch, chunk, 0)


def _gla(proj, wg, bg, gn, batch, seq, tc):
    t = proj.shape[0]
    nb = seq // tc
    blk = lambda w, j: pl.BlockSpec((tc, w), lambda b, i, j=j: (b * nb + i, j))
    const = lambda r, c: pl.BlockSpec((r, c), lambda b, i: (0, 0))
    wg_pad = jnp.zeros((LANES, GLA_KW), F32).at[:GLA_GATE_RANK].set(wg).astype(BF16)
    gn_row = jnp.tile(gn, GLA_HEADS).reshape(1, GLA_VW)
    return pl.pallas_call(
        functools.partial(_gla_kernel, tc=tc),
        grid=(batch, nb),
        in_specs=[blk(GLA_KW, COL_G_Q), blk(GLA_KW, COL_G_K), blk(GLA_VW, COL_G_V), blk(GLA_VW, COL_G_R),
                  blk(LANES, COL_G_LR), const(LANES, GLA_KW), const(1, GLA_KW), const(1, GLA_VW)],
        out_specs=pl.BlockSpec((tc, GLA_VW), lambda b, i: (b * nb + i, 0)),
        out_shape=jax.ShapeDtypeStruct((t, GLA_VW), F32),
        scratch_shapes=[pltpu.VMEM((GLA_VW, GLA_KW), F32)],
        compiler_params=_cparams(("arbitrary", "arbitrary")),
    )(proj, proj, proj, proj, proj, wg_pad, bg.reshape(1, GLA_KW), gn_row)


def _out_proj_kernel(x_ref, a_ref, s_ref, g_ref, w_ref, o_ref):
    acc = jnp.dot(a_ref[...].astype(BF16), w_ref[0:DA_WIDTH, :], preferred_element_type=F32)
    acc += jnp.dot(s_ref[...].astype(BF16), w_ref[DA_WIDTH:DA_WIDTH + SC_WIDTH, :],
                   preferred_element_type=F32)
    acc += jnp.dot(g_ref[...].astype(BF16), w_ref[DA_WIDTH + SC_WIDTH:, :], preferred_element_type=F32)
    o_ref[...] = x_ref[...] + acc


def _out_proj(x, o_da, o_sc, o_g, w, tm):
    t, d = x.shape
    blk = lambda n: pl.BlockSpec((tm, n), lambda i: (i, 0))
    return pl.pallas_call(
        _out_proj_kernel,
        grid=(t // tm,),
        in_specs=[blk(d), blk(DA_WIDTH), blk(SC_WIDTH), blk(GLA_VW),
                  pl.BlockSpec(w.shape, lambda i: (0, 0))],
        out_specs=blk(d),
        out_shape=jax.ShapeDtypeStruct((t, d), F32),
        compiler_params=_cparams(("arbitrary",)),
    )(x, o_da, o_sc, o_g, w)


def _top_desc(x, n):
    vals = []
    cur = x
    for r in range(n):
        m = jnp.max(cur, axis=0, keepdims=True)
        vals.append(m)
        if r + 1 < n:
            cur = jnp.where(cur == m, NEG_INF, cur)
    return vals


def _top_desc_ranked(x, n):
    vals = []
    cur = x
    rank = jnp.full(x.shape, float(n), F32)
    for r in range(n):
        m = jnp.max(cur, axis=0, keepdims=True)
        vals.append(m)
        hit = cur == m
        rank = jnp.where(hit, float(r), rank)
        if r + 1 < n:
            cur = jnp.where(hit, NEG_INF, cur)
    return vals, rank


def _peer_route_kernel(x_ref, g_ref, wq_ref, k1_ref, k2_ref, a1_ref, cnt_ref, a2_ref, rank2_ref):
    k = PEER_TOPK
    tt = x_ref.shape[0]
    xn = _rms(x_ref[...], g_ref[...]).astype(BF16)
    q = jnp.dot(xn, wq_ref[...], preferred_element_type=F32)
    dn = (((1,), (1,)), ((), ()))
    for h in range(PEER_HEADS):
        qa = q[:, (2 * h) * LANES:(2 * h + 1) * LANES].astype(BF16)
        qb = q[:, (2 * h + 1) * LANES:(2 * h + 2) * LANES].astype(BF16)
        s1 = lax.dot_general(k1_ref[...], qa, dn, preferred_element_type=F32)
        s2 = lax.dot_general(k2_ref[...], qb, dn, preferred_element_type=F32)
        v1 = _top_desc(s1, k)
        v2, rank2 = _top_desc_ranked(s2, k)
        cands = [v1[i] + v2[j] for i in range(k) for j in range(k // (i + 1))]
        pad = (-len(cands)) % 8
        cands += [jnp.full((1, tt), NEG_INF, F32)] * pad
        top = _top_desc(jnp.concatenate(cands, axis=0), k)
        z = jnp.ones((1, tt), F32)
        for r in range(1, k):
            z = z + jnp.exp(top[r] - top[0])
        thr = top[k - 1]
        v1s = jnp.concatenate(v1, axis=0)
        cnt_rank = jnp.zeros_like(v1s)
        for j in range(k):
            cnt_rank = cnt_rank + ((v1s + v2[j]) >= thr).astype(F32)
        cnt = jnp.zeros_like(s1)
        for i in range(k):
            cnt = jnp.where(s1 == v1[i], cnt_rank[i:i + 1, :], cnt)
        cnt_ref[h] = cnt
        rank2_ref[h] = rank2.astype(BF16)
        a1_ref[h] = jnp.exp(s1 - v1[0]) * (1.0 / z)
        a2_ref[h] = jnp.exp(s2 - v2[0]).astype(BF16)


def _peer_route(x, g, wq, k1, k2, tt):
    t, d = x.shape
    shape = (PEER_HEADS, PEER_NKEYS, t)
    out = pl.BlockSpec((PEER_HEADS, PEER_NKEYS, tt), lambda i: (0, 0, i))
    f32 = jax.ShapeDtypeStruct(shape, F32)
    bf16 = jax.ShapeDtypeStruct(shape, BF16)
    return pl.pallas_call(
        _peer_route_kernel,
        grid=(t // tt,),
        in_specs=[pl.BlockSpec((tt, d), lambda i: (i, 0)),
                  pl.BlockSpec((1, d), lambda i: (0, 0)),
                  pl.BlockSpec(wq.shape, lambda i: (0, 0)),
                  pl.BlockSpec(k1.shape, lambda i: (0, 0)),
                  pl.BlockSpec(k2.shape, lambda i: (0, 0))],
        out_specs=[out, out, out, out],
        out_shape=[f32, f32, bf16, bf16],
        compiler_params=_cparams(("arbitrary",)),
    )(x, g.reshape(1, d), wq, k1, k2)


def _peer_dense_kernel(x_ref, g_ref, u_ref, vt_ref, a1_ref, cnt_ref, a2_ref, rank2_ref, o_ref,
                       xnt_ref, acc_ref, hta_ref, htb_ref, pta_ref, ptb_ref):
    e = pl.program_id(1)

    @pl.when(e == 0)
    def _():
        xn = _rms(x_ref[...], g_ref[...])
        xnt_ref[...] = xn.T.astype(BF16)
        acc_ref[...] = jnp.zeros_like(acc_ref)

    inv_sqrt2 = 1.0 / math.sqrt(2.0)
    zero = jnp.zeros((), BF16)
    n_q, job_rows, _ = u_ref.shape
    sub = xnt_ref.shape[1] // 2
    n1_per_job = job_rows // PEER_NKEYS
    pack = BF16_SUBLANES
    groups = PEER_NKEYS // pack
    col_a, col_b = slice(0, sub), slice(sub, 2 * sub)
    ht_refs = {col_a: hta_ref, col_b: htb_ref}
    pt_refs = {col_a: pta_ref, col_b: ptb_ref}

    def matmul1(q, cols):
        ht_refs[cols][...] = jnp.dot(u_ref[q], xnt_ref[:, cols], preferred_element_type=F32)

    def gate(q, cols):
        ht_ref, pt_ref = ht_refs[cols], pt_refs[cols]
        for rl in range(n1_per_job):
            r = q * n1_per_job + rl
            w = None
            for h in range(PEER_HEADS):
                cnt = jnp.broadcast_to(cnt_ref[h, pl.ds(r, 1), cols], (pack, sub)).astype(BF16)
                a1 = jnp.broadcast_to(a1_ref[h, pl.ds(r, 1), cols], (pack, sub)).astype(BF16)
                sel = rank2_ref[h, :, :, cols] < cnt[None]
                term = a1[None] * jnp.where(sel, a2_ref[h, :, :, cols], zero)
                w = term if w is None else w + term
            rows = slice(rl * PEER_NKEYS, (rl + 1) * PEER_NKEYS)
            hb = ht_ref[rows, :]
            gelu = 0.5 * hb * (1.0 + lax.erf(hb * inv_sqrt2))
            p = w * gelu.astype(BF16).reshape(groups, pack, sub)
            pt_ref[rows, :] = p.reshape(PEER_NKEYS, sub)

    def matmul2(q, cols):
        acc_ref[:, cols] += jnp.dot(vt_ref[q], pt_refs[cols][...], preferred_element_type=F32)

    def pair(q, has_prev, has_next):
        matmul1(q, col_b)
        gate(q, col_a)
        if has_prev:
            matmul2(q - 1, col_b)
        if has_next:
            matmul1(q + 1, col_a)
        gate(q, col_b)
        matmul2(q, col_a)

    matmul1(0, col_a)
    if n_q == 1:
        pair(0, False, False)
    else:
        pair(0, False, True)

        def body(q, carry):
            pair(q, True, True)
            return carry

        lax.fori_loop(1, n_q - 1, body, 0)
        pair(n_q - 1, True, False)
    matmul2(n_q - 1, col_b)

    @pl.when(e == pl.num_programs(1) - 1)
    def _():
        o_ref[...] = x_ref[...] + acc_ref[...].T


PEER_JOB_ROWS = 512


def _expert_jobs(u, v):
    n_exp, d = u.shape
    n_jobs = n_exp // PEER_JOB_ROWS
    u3 = u.astype(BF16).reshape(n_jobs, PEER_JOB_ROWS, d)
    vt3 = jnp.transpose(v.astype(BF16).reshape(n_jobs, PEER_JOB_ROWS, d), (0, 2, 1))
    return u3, vt3


def _peer_dense(x, g, u, vt, a1t, cnt, a2t, rank2, tt, et):
    t, d = x.shape
    n_jobs, job_rows, _ = u.shape
    n_q = et // job_rows
    n1_per_tile = et // PEER_NKEYS
    groups = PEER_NKEYS // BF16_SUBLANES
    grouped = (PEER_HEADS, groups, BF16_SUBLANES, t)
    full = pl.BlockSpec((PEER_HEADS, groups, BF16_SUBLANES, tt), lambda i, e: (0, 0, 0, i))
    part = pl.BlockSpec((PEER_HEADS, n1_per_tile, tt), lambda i, e: (0, e, i))
    sub = tt // 2
    return pl.pallas_call(
        _peer_dense_kernel,
        grid=(t // tt, n_jobs // n_q),
        in_specs=[pl.BlockSpec((tt, d), lambda i, e: (i, 0)),
                  pl.BlockSpec((1, d), lambda i, e: (0, 0)),
                  pl.BlockSpec((n_q, job_rows, d), lambda i, e: (e, 0, 0)),
                  pl.BlockSpec((n_q, d, job_rows), lambda i, e: (e, 0, 0)),
                  part, part, full, full],
        out_specs=pl.BlockSpec((tt, d), lambda i, e: (i, 0)),
        out_shape=jax.ShapeDtypeStruct((t, d), F32),
        scratch_shapes=[pltpu.VMEM((d, tt), BF16), pltpu.VMEM((d, tt), F32),
                        pltpu.VMEM((job_rows, sub), F32), pltpu.VMEM((job_rows, sub), F32),
                        pltpu.VMEM((job_rows, sub), BF16), pltpu.VMEM((job_rows, sub), BF16)],
        compiler_params=_cparams(("arbitrary", "arbitrary"), vmem_mb=52),
    )(x, g.reshape(1, d), u, vt, a1t, cnt, a2t.reshape(grouped), rank2.reshape(grouped))


def _rmsnorm_kernel(x_ref, g_ref, o_ref):
    o_ref[...] = _rms(x_ref[...], g_ref[...])


def _rmsnorm(x, g, tm):
    t, d = x.shape
    return pl.pallas_call(
        _rmsnorm_kernel,
        grid=(t // tm,),
        in_specs=[pl.BlockSpec((tm, d), lambda i: (i, 0)), pl.BlockSpec((1, d), lambda i: (0, 0))],
        out_specs=pl.BlockSpec((tm, d), lambda i: (i, 0)),
        out_shape=jax.ShapeDtypeStruct((t, d), F32),
        compiler_params=_cparams(("arbitrary",)),
    )(x, g.reshape(1, d))


def _tiles(seq):
    pick = lambda want: min(want, seq)
    return dict(tm=pick(256), tq=pick(512), tconv=pick(512), tgla=pick(512), tt=pick(256), td=pick(512),
                et=2048)


def kernel(x, positions, norm_mix, w_in, lam_q1, lam_k1, lam_q2, lam_k2, diff_norm, conv_w, gla_w_gate2, gla_b_gate, gla_norm, w_out, norm_ffn, peer_w_q, peer_keys1, peer_keys2, peer_u, peer_v, norm_final):
    batch, seq, d = x.shape
    t = batch * seq
    depth = w_in.shape[0]
    tl = _tiles(seq)
    xt = x.reshape(t, d)
    tabs = _rope_tables(positions, tl["tm"])
    for i in range(depth):
        w_in_p = jnp.pad(w_in[i], ((0, 0), (0, IN_PAD - IN_WIDTH))).astype(BF16)
        proj = _rms_matmul(xt, norm_mix[i], w_in_p, tl["tm"])
        qt, kr, vt3 = _rope_prep(proj, tabs, tl["tq"])
        lam_init = 0.8 - 0.6 * math.exp(-0.3 * i)
        o_da = _diff_attention(qt, kr, vt3, lam_q1[i], lam_k1[i], lam_q2[i], lam_k2[i], diff_norm[i],
                               batch, seq, lam_init, tl["tq"])
        o_sc = _short_conv(proj, conv_w[i], seq, tl["tconv"])
        o_g = _gla(proj, gla_w_gate2[i], gla_b_gate[i], gla_norm[i], batch, seq, tl["tgla"])
        xt = _out_proj(xt, o_da, o_sc, o_g, w_out[i].astype(BF16), tl["tm"])
        a1t, cnt, a2t, rank2 = _peer_route(xt, norm_ffn[i], peer_w_q[i].astype(BF16),
                                           peer_keys1[i].astype(BF16), peer_keys2[i].astype(BF16), tl["tt"])
        u3, vt3 = _expert_jobs(peer_u[i], peer_v[i])
        xt = _peer_dense(xt, norm_ffn[i], u3, vt3, a1t, cnt, a2t, rank2, tl["td"], tl["et"])
    return _rmsnorm(xt, norm_final, tl["tm"]).reshape(batch, seq, d)
```

```python
import functools
import math

import jax
import jax.numpy as jnp
from jax import lax
from jax.experimental import pallas as pl
from jax.experimental.pallas import tpu as pltpu

F32 = jnp.float32
BF16 = jnp.bfloat16

D_MODEL = 1024
EPS = 1e-6
HEAD_DIM = 64
DA_HEADS = 4
DA_WIDTH = 512
ROPE_THETA = 500000.0
ROT_DIM = 16
SC_WIDTH = 256
CONV_W = 3
GLA_HEADS = 4
GLA_DK = 32
GLA_DV = 64
GLA_KW = 128
GLA_VW = 256
GLA_GATE_RANK = 16
GLA_GATE_TEMP = 16.0
GLA_CHUNK = 64
IN_WIDTH = 3088
IN_PAD = 3200
PEER_HEADS = 8
PEER_NKEYS = 128
PEER_EXPERTS = PEER_NKEYS * PEER_NKEYS
PEER_TOPK = 16
LANES = 128
BF16_SUBLANES = 16
NEG_INF = float("-inf")

COL_DA_Q, COL_DA_K, COL_DA_V = 0, 1, 2
COL_SC_B, COL_SC_C, COL_SC_H = 6, 7, 8
COL_G_V, COL_G_R = 10, 11
COL_G_Q, COL_G_K, COL_G_LR = 18, 19, 24


def _cparams(sem, vmem_mb=40):
    return pltpu.CompilerParams(dimension_semantics=sem, vmem_limit_bytes=vmem_mb * 1024 * 1024)


def _rms(x, g):
    return x * lax.rsqrt(jnp.mean(x * x, axis=-1, keepdims=True) + EPS) * g


def _rms_matmul_kernel(x_ref, g_ref, w_ref, o_ref):
    xn = _rms(x_ref[...], g_ref[...]).astype(BF16)
    o_ref[...] = jnp.dot(xn, w_ref[...], preferred_element_type=F32)


def _rms_matmul(x, g, w, tm):
    t, d = x.shape
    n = w.shape[1]
    return pl.pallas_call(
        _rms_matmul_kernel,
        grid=(t // tm,),
        in_specs=[pl.BlockSpec((tm, d), lambda i: (i, 0)),
                  pl.BlockSpec((1, d), lambda i: (0, 0)),
                  pl.BlockSpec((d, n), lambda i: (0, 0))],
        out_specs=pl.BlockSpec((tm, n), lambda i: (i, 0)),
        out_shape=jax.ShapeDtypeStruct((t, n), F32),
        compiler_params=_cparams(("arbitrary",)),
    )(x, g.reshape(1, d), w)


def _rope_table_kernel(pos_ref, inv_ref, mrot_ref, m1_ref, m2_ref, c_ref, s1_ref, s2_ref):
    ang = pos_ref[...] * inv_ref[...]
    c = jnp.cos(ang)
    s = jnp.sin(ang)
    mrot = mrot_ref[...]
    c_ref[...] = mrot * c + (1.0 - mrot)
    s1_ref[...] = -(m1_ref[...] * s)
    s2_ref[...] = m2_ref[...] * s


def _rope_tables(positions, tm):
    t = positions.size
    pos = jnp.broadcast_to(positions.reshape(t, 1).astype(F32), (t, LANES))
    d = jnp.arange(LANES) % HEAD_DIM
    half = ROT_DIM // 2
    inv = ROPE_THETA ** (-jnp.arange(0, ROT_DIM, 2, dtype=F32) / ROT_DIM)
    inv_row = jnp.where(d < ROT_DIM, inv[d % half], 0.0).astype(F32).reshape(1, LANES)
    mrot = (d < ROT_DIM).astype(F32).reshape(1, LANES)
    m1 = (d < half).astype(F32).reshape(1, LANES)
    m2 = ((d >= half) & (d < ROT_DIM)).astype(F32).reshape(1, LANES)
    row = pl.BlockSpec((1, LANES), lambda i: (0, 0))
    blk = pl.BlockSpec((tm, LANES), lambda i: (i, 0))
    return pl.pallas_call(
        _rope_table_kernel,
        grid=(t // tm,),
        in_specs=[blk, row, row, row, row],
        out_specs=[blk, blk, blk],
        out_shape=[jax.ShapeDtypeStruct((t, LANES), F32)] * 3,
        compiler_params=_cparams(("arbitrary",)),
    )(pos, inv_row, mrot, m1, m2)


def _rope_prep_kernel(q_ref, k_ref, v_ref, c_ref, s1_ref, s2_ref, qt_ref, ko_ref, vt_ref, *, qscale):
    c = c_ref[...]
    s1 = s1_ref[...]
    s2 = s2_ref[...]
    half = ROT_DIM // 2

    def rope(x):
        return x * c + pltpu.roll(x, LANES - half, 1) * s1 + pltpu.roll(x, half, 1) * s2

    for g in range(DA_WIDTH // LANES):
        sl = slice(LANES * g, LANES * (g + 1))
        qt_ref[sl, :] = (rope(q_ref[:, sl]) * qscale).T.astype(BF16)
        ko_ref[:, sl] = rope(k_ref[:, sl]).astype(BF16)
        vt_ref[0, sl, :] = v_ref[:, sl].T.astype(BF16)


def _rope_prep(proj, tabs, tm):
    t = proj.shape[0]
    c, s1, s2 = tabs
    col = lambda j: pl.BlockSpec((tm, DA_WIDTH), lambda i, j=j: (i, j))
    tab = pl.BlockSpec((tm, LANES), lambda i: (i, 0))
    return pl.pallas_call(
        functools.partial(_rope_prep_kernel, qscale=HEAD_DIM ** -0.5 * math.log2(math.e)),
        grid=(t // tm,),
        in_specs=[col(COL_DA_Q), col(COL_DA_K), col(COL_DA_V), tab, tab, tab],
        out_specs=[pl.BlockSpec((DA_WIDTH, tm), lambda i: (0, i)),
                   pl.BlockSpec((tm, DA_WIDTH), lambda i: (i, 0)),
                   pl.BlockSpec((1, DA_WIDTH, tm), lambda i: (i, 0, 0))],
        out_shape=[jax.ShapeDtypeStruct((DA_WIDTH, t), BF16),
                   jax.ShapeDtypeStruct((t, DA_WIDTH), BF16),
                   jax.ShapeDtypeStruct((t // tm, DA_WIDTH, tm), BF16)],
        compiler_params=_cparams(("arbitrary",)),
    )(proj, proj, proj, c, s1, s2)


def _attn_kernel(qt_ref, k_ref, vt_ref, lq1_ref, lk1_ref, lq2_ref, lk2_ref, gain_ref, o_ref,
                 acc1_ref, acc2_ref, m1_ref, l1_ref, m2_ref, l2_ref, s1a_ref, s2a_ref, s1b_ref, s2b_ref,
                 *, tq, lam_init):
    qi = pl.program_id(2)
    qt = qt_ref[...]
    row = lax.broadcasted_iota(jnp.int32, (LANES, 1), 0)
    zero = jnp.zeros_like(qt)
    qa = jnp.where(row < HEAD_DIM, qt, zero)
    qb = jnp.where(row >= HEAD_DIM, qt, zero)
    acc1_ref[...] = jnp.zeros_like(acc1_ref)
    acc2_ref[...] = jnp.zeros_like(acc2_ref)

    for st_ref in (m1_ref, m2_ref):
        st_ref[...] = jnp.full(st_ref.shape, NEG_INF, F32)
    for st_ref in (l1_ref, l2_ref):
        st_ref[...] = jnp.zeros_like(st_ref)
    maps = ((qa, m1_ref, l1_ref, acc1_ref), (qb, m2_ref, l2_ref, acc2_ref))

    def scores(j, bufs):
        off = pl.multiple_of(j * tq, tq)
        kk = k_ref[pl.ds(off, tq), :]
        for (qm, _, _, _), s_ref in zip(maps, bufs):
            s_ref[...] = jnp.dot(kk, qm, preferred_element_type=F32)

    def absorb(j, bufs, masked):
        vt = vt_ref[j]
        for (_, m_ref, l_ref, acc_ref), s_ref in zip(maps, bufs):
            s = s_ref[...]
            if masked:
                r = lax.broadcasted_iota(jnp.int32, (tq, tq), 0)
                c = lax.broadcasted_iota(jnp.int32, (tq, tq), 1)
                s = jnp.where(r <= c, s, NEG_INF)
            m = m_ref[...]
            mn = jnp.maximum(m, jnp.max(s, axis=0, keepdims=True))
            alpha = jnp.exp2(m - mn)
            p = jnp.exp2(s - mn)
            m_ref[...] = mn
            l_ref[...] = alpha * l_ref[...] + jnp.sum(p, axis=0, keepdims=True)
            acc_ref[...] = alpha * acc_ref[...] + jnp.dot(vt, p.astype(BF16),
                                                          preferred_element_type=F32)

    buf_a, buf_b = (s1a_ref, s2a_ref), (s1b_ref, s2b_ref)
    scores(0, buf_a)

    def pair(i, carry):
        j = 2 * i
        scores(j + 1, buf_b)
        absorb(j, buf_a, False)
        scores(j + 2, buf_a)
        absorb(j + 1, buf_b, False)
        return carry

    lax.fori_loop(0, qi // 2, pair, 0)

    @pl.when(qi % 2 == 1)
    def _():
        scores(qi, buf_b)
        absorb(qi - 1, buf_a, False)
        absorb(qi, buf_b, True)

    @pl.when(qi % 2 == 0)
    def _():
        absorb(qi, buf_a, True)

    l1, l2 = l1_ref[...], l2_ref[...]

    lam = (jnp.exp(jnp.sum(lq1_ref[...] * lk1_ref[...], axis=-1, keepdims=True))
           - jnp.exp(jnp.sum(lq2_ref[...] * lk2_ref[...], axis=-1, keepdims=True)) + lam_init)
    o = acc1_ref[...] / l1 - lam * (acc2_ref[...] / l2)
    y = o * lax.rsqrt(jnp.mean(o * o, axis=0, keepdims=True) + EPS) * gain_ref[...]
    o_ref[...] = (y * (1.0 - lam_init)).T


def _diff_attention(qt, kr, vt3, lq1, lk1, lq2, lk2, gain, batch, seq, lam_init, tq):
    t = kr.shape[0]
    nq = seq // tq
    vec = lambda n: pl.BlockSpec((1, n), lambda b, h, i: (0, 0))
    return pl.pallas_call(
        functools.partial(_attn_kernel, tq=tq, lam_init=lam_init),
        grid=(batch, DA_HEADS, nq),
        in_specs=[pl.BlockSpec((LANES, tq), lambda b, h, i: (h, b * nq + i)),
                  pl.BlockSpec((seq, LANES), lambda b, h, i: (b, h)),
                  pl.BlockSpec((nq, LANES, tq), lambda b, h, i: (b, h, 0)),
                  vec(HEAD_DIM), vec(HEAD_DIM), vec(HEAD_DIM), vec(HEAD_DIM),
                  pl.BlockSpec((LANES, 1), lambda b, h, i: (0, 0))],
        out_specs=pl.BlockSpec((tq, LANES), lambda b, h, i: (b * nq + i, h)),
        out_shape=jax.ShapeDtypeStruct((t, DA_WIDTH), F32),
        scratch_shapes=([pltpu.VMEM((LANES, tq), F32)] * 2 + [pltpu.VMEM((1, tq), F32)] * 4
                        + [pltpu.VMEM((tq, tq), F32)] * 4),
        compiler_params=_cparams(("arbitrary", "arbitrary", "arbitrary")),
    )(qt, kr, vt3, lq1.reshape(1, -1), lk1.reshape(1, -1), lq2.reshape(1, -1), lk2.reshape(1, -1),
      gain.reshape(-1, 1))


def _short_conv_kernel(b_ref, c_ref, h_ref, cp_ref, hp_ref, w_ref, o_ref, *, tm, blocks_per_seq):
    i = pl.program_id(0)
    z = c_ref[...] * h_ref[...]
    zp = cp_ref[...] * hp_ref[...]
    first = (i % blocks_per_seq) == 0
    zp = jnp.where(first, jnp.zeros_like(zp), zp)
    row = lax.broadcasted_iota(jnp.int32, (tm, 1), 0)
    z1 = jnp.where(row == 0, zp[7:8, :], pltpu.roll(z, 1, 0))
    z2 = pltpu.roll(z, 2, 0)
    z2 = jnp.where(row == 0, zp[6:7, :], jnp.where(row == 1, zp[7:8, :], z2))
    w = w_ref[...]
    y = w[0:1, :] * z2 + w[1:2, :] * z1 + w[2:3, :] * z
    o_ref[...] = b_ref[...] * y


def _short_conv(proj, w, seq, tm):
    t = proj.shape[0]
    rb = tm // 8
    cur = lambda j: pl.BlockSpec((tm, SC_WIDTH), lambda i, j=j: (i, j))
    prev = lambda j: pl.BlockSpec((8, SC_WIDTH), lambda i, j=j: (jnp.maximum(i * rb - 1, 0), j))
    return pl.pallas_call(
        functools.partial(_short_conv_kernel, tm=tm, blocks_per_seq=seq // tm),
        grid=(t // tm,),
        in_specs=[cur(COL_SC_B), cur(COL_SC_C), cur(COL_SC_H), prev(COL_SC_C), prev(COL_SC_H),
                  pl.BlockSpec((CONV_W, SC_WIDTH), lambda i: (0, 0))],
        out_specs=pl.BlockSpec((tm, SC_WIDTH), lambda i: (i, 0)),
        out_shape=jax.ShapeDtypeStruct((t, SC_WIDTH), F32),
        compiler_params=_cparams(("arbitrary",)),
    )(proj, proj, proj, proj, proj, w)


def _split3(x):
    hi = x.astype(BF16)
    r = x - hi.astype(F32)
    mid = r.astype(BF16)
    lo = (r - mid.astype(F32)).astype(BF16)
    return hi, mid, lo


def _gla_kernel(q_ref, k_ref, v_ref, r_ref, lr_ref, wg_ref, bg_ref, gn_ref, o_ref, st_ref, *, tc):
    ch = GLA_CHUNK

    @pl.when(pl.program_id(1) == 0)
    def _():
        st_ref[...] = jnp.zeros_like(st_ref)

    ii = lax.broadcasted_iota(jnp.int32, (ch, ch), 0)
    jj = lax.broadcasted_iota(jnp.int32, (ch, ch), 1)
    tri = (ii >= jj).astype(BF16)
    rk = lax.broadcasted_iota(jnp.int32, (GLA_VW, GLA_KW), 0)
    ck = lax.broadcasted_iota(jnp.int32, (GLA_VW, GLA_KW), 1)
    bd_k = (rk // GLA_DV) == (ck // GLA_DK)
    rv = lax.broadcasted_iota(jnp.int32, (GLA_VW, GLA_VW), 0)
    cv = lax.broadcasted_iota(jnp.int32, (GLA_VW, GLA_VW), 1)
    bd_v = (rv // GLA_DV) == (cv // GLA_DV)
    seg = bd_v.astype(BF16)
    ri = lax.broadcasted_iota(jnp.int32, (ch, GLA_VW), 0)
    ci = lax.broadcasted_iota(jnp.int32, (ch, GLA_VW), 1)
    causal = (ci % ch) <= ri
    dn_t = (((1,), (1,)), ((), ()))
    scale = GLA_DK ** -0.5

    def chunk(c, carry):
        off = pl.multiple_of(c * ch, ch)
        rows = pl.ds(off, ch)
        x = jnp.dot(lr_ref[rows, :].astype(BF16), wg_ref[...], preferred_element_type=F32) + bg_ref[...]
        la = (jnp.minimum(x, 0.0) - jnp.log1p(jnp.exp(-jnp.abs(x)))) * (1.0 / GLA_GATE_TEMP)
        hi, mid, lo = _split3(la)
        b = (jnp.dot(tri, hi, preferred_element_type=F32) + jnp.dot(tri, mid, preferred_element_type=F32)
             + jnp.dot(tri, lo, preferred_element_type=F32))
        b_mid = b[ch // 2 - 1:ch // 2, :]
        b_last = b[ch - 1:ch, :]
        q = q_ref[rows, :] * scale
        k = k_ref[rows, :]
        v = v_ref[rows, :]
        qt = (q * jnp.exp(b - b_mid)).astype(BF16)
        kt = (k * jnp.exp(b_mid - b)).astype(BF16)
        kbd = jnp.where(bd_k, jnp.concatenate([kt] * GLA_HEADS, axis=0), jnp.zeros((), BF16))
        att = lax.dot_general(qt, kbd, dn_t, preferred_element_type=F32)
        att = jnp.where(causal, att, 0.0).astype(BF16)
        vb = v.astype(BF16)
        vbd = jnp.where(bd_v, jnp.concatenate([vb] * GLA_HEADS, axis=0), jnp.zeros((), BF16))
        o = jnp.dot(att, vbd, preferred_element_type=F32)
        st = st_ref[...]
        bq = (q * jnp.exp(b)).astype(BF16)
        o = o + lax.dot_general(bq, st.astype(BF16), dn_t, preferred_element_type=F32)
        kl = (k * jnp.exp(b_last - b)).astype(BF16)
        ut = jnp.dot(v.T.astype(BF16), kl, preferred_element_type=F32)
        st_ref[...] = st * jnp.exp(b_last) + jnp.where(bd_k, ut, 0.0)
        o2 = o * o
        o2h = o2.astype(BF16)
        o2l = (o2 - o2h.astype(F32)).astype(BF16)
        ms = (jnp.dot(o2h, seg, preferred_element_type=F32)
              + jnp.dot(o2l, seg, preferred_element_type=F32)) * (1.0 / GLA_DV)
        r = r_ref[rows, :]
        gate = r * (1.0 / (1.0 + jnp.exp(-r)))
        o_ref[rows, :] = o * lax.rsqrt(ms + EPS) * gn_ref[...] * gate
        return carry

    lax.fori_loop(0, tc // ch, chunk, 0)


def _gla(proj, wg, bg, gn, batch, seq, tc):
    t = proj.shape[0]
    nb = seq // tc
    blk = lambda w, j: pl.BlockSpec((tc, w), lambda b, i, j=j: (b * nb + i, j))
    const = lambda r, c: pl.BlockSpec((r, c), lambda b, i: (0, 0))
    wg_pad = jnp.zeros((LANES, GLA_KW), F32).at[:GLA_GATE_RANK].set(wg).astype(BF16)
    gn_row = jnp.tile(gn, GLA_HEADS).reshape(1, GLA_VW)
    return pl.pallas_call(
        functools.partial(_gla_kernel, tc=tc),
        grid=(batch, nb),
        in_specs=[blk(GLA_KW, COL_G_Q), blk(GLA_KW, COL_G_K), blk(GLA_VW, COL_G_V), blk(GLA_VW, COL_G_R),
                  blk(LANES, COL_G_LR), const(LANES, GLA_KW), const(1, GLA_KW), const(1, GLA_VW)],
        out_specs=pl.BlockSpec((tc, GLA_VW), lambda b, i: (b * nb + i, 0)),
        out_shape=jax.ShapeDtypeStruct((t, GLA_VW), F32),
        scratch_shapes=[pltpu.VMEM((GLA_VW, GLA_KW), F32)],
        compiler_params=_cparams(("arbitrary", "arbitrary")),
    )(proj, proj, proj, proj, proj, wg_pad, bg.reshape(1, GLA_KW), gn_row)


def _out_proj_kernel(x_ref, a_ref, s_ref, g_ref, w_ref, o_ref):
    acc = jnp.dot(a_ref[...].astype(BF16), w_ref[0:DA_WIDTH, :], preferred_element_type=F32)
    acc += jnp.dot(s_ref[...].astype(BF16), w_ref[DA_WIDTH:DA_WIDTH + SC_WIDTH, :],
                   preferred_element_type=F32)
    acc += jnp.dot(g_ref[...].astype(BF16), w_ref[DA_WIDTH + SC_WIDTH:, :], preferred_element_type=F32)
    o_ref[...] = x_ref[...] + acc


def _out_proj(x, o_da, o_sc, o_g, w, tm):
    t, d = x.shape
    blk = lambda n: pl.BlockSpec((tm, n), lambda i: (i, 0))
    return pl.pallas_call(
        _out_proj_kernel,
        grid=(t // tm,),
        in_specs=[blk(d), blk(DA_WIDTH), blk(SC_WIDTH), blk(GLA_VW),
                  pl.BlockSpec(w.shape, lambda i: (0, 0))],
        out_specs=blk(d),
        out_shape=jax.ShapeDtypeStruct((t, d), F32),
        compiler_params=_cparams(("arbitrary",)),
    )(x, o_da, o_sc, o_g, w)


SUBLANES = 8


def _merge_exchange_network(n):
    pairs = []
    t = max(1, math.ceil(math.log2(n)))
    p = 2 ** (t - 1)
    while p > 0:
        q, r, d = 2 ** (t - 1), 0, p
        while d > 0:
            pairs += [(i, i + d) for i in range(n - d) if (i & p) == r]
            d, q, r = q - p, q // 2, p
        p //= 2
    return pairs


def _top_desc(x, n):
    groups = [x[SUBLANES * i:SUBLANES * (i + 1), :] for i in range(x.shape[0] // SUBLANES)]
    for i, j in _merge_exchange_network(len(groups)):
        groups[i], groups[j] = jnp.maximum(groups[i], groups[j]), jnp.minimum(groups[i], groups[j])
    vals = []
    for r in range(n):
        m = jnp.max(groups[0], axis=0, keepdims=True)
        vals.append(m)
        hit = groups[0] == m
        live = min(len(groups), n - r)
        for k in range(live - 1):
            groups[k] = jnp.where(hit, groups[k + 1], groups[k])
        if live == len(groups) and live > 0 and r + 1 < n:
            groups[live - 1] = jnp.where(hit, NEG_INF, groups[live - 1])
    return vals


def _ranks_of(x, vals):
    rank = jnp.full(x.shape, float(len(vals)), F32)
    for r in reversed(range(len(vals))):
        rank = jnp.where(x == vals[r], float(r), rank)
    return rank


def _peer_route_kernel(x_ref, g_ref, wq_ref, k1_ref, k2_ref, a1_ref, cnt_ref, a2_ref, rank2_ref):
    k = PEER_TOPK
    tt = x_ref.shape[0]
    xn = _rms(x_ref[...], g_ref[...]).astype(BF16)
    q = jnp.dot(xn, wq_ref[...], preferred_element_type=F32)
    dn = (((1,), (1,)), ((), ()))
    for h in range(PEER_HEADS):
        qa = q[:, (2 * h) * LANES:(2 * h + 1) * LANES].astype(BF16)
        qb = q[:, (2 * h + 1) * LANES:(2 * h + 2) * LANES].astype(BF16)
        s1 = lax.dot_general(k1_ref[...], qa, dn, preferred_element_type=F32)
        s2 = lax.dot_general(k2_ref[...], qb, dn, preferred_element_type=F32)
        v1 = _top_desc(s1, k)
        v2 = _top_desc(s2, k)
        rank2 = _ranks_of(s2, v2)
        cands = [v1[i] + v2[j] for i in range(k) for j in range(k // (i + 1))]
        pad = (-len(cands)) % 8
        cands += [jnp.full((1, tt), NEG_INF, F32)] * pad
        top = _top_desc(jnp.concatenate(cands, axis=0), k)
        z = jnp.ones((1, tt), F32)
        for r in range(1, k):
            z = z + jnp.exp(top[r] - top[0])
        thr = top[k - 1]
        v1s = jnp.concatenate(v1, axis=0)
        cnt_rank = jnp.zeros_like(v1s)
        for j in range(k):
            cnt_rank = cnt_rank + ((v1s + v2[j]) >= thr).astype(F32)
        cnt = jnp.zeros_like(s1)
        for i in range(k):
            cnt = jnp.where(s1 == v1[i], cnt_rank[i:i + 1, :], cnt)
        cnt_ref[h] = cnt
        rank2_ref[h] = rank2.astype(BF16)
        a1_ref[h] = jnp.exp(s1 - v1[0]) * (1.0 / z)
        a2_ref[h] = jnp.exp(s2 - v2[0]).astype(BF16)


def _peer_route(x, g, wq, k1, k2, tt):
    t, d = x.shape
    shape = (PEER_HEADS, PEER_NKEYS, t)
    out = pl.BlockSpec((PEER_HEADS, PEER_NKEYS, tt), lambda i: (0, 0, i))
    f32 = jax.ShapeDtypeStruct(shape, F32)
    bf16 = jax.ShapeDtypeStruct(shape, BF16)
    return pl.pallas_call(
        _peer_route_kernel,
        grid=(t // tt,),
        in_specs=[pl.BlockSpec((tt, d), lambda i: (i, 0)),
                  pl.BlockSpec((1, d), lambda i: (0, 0)),
                  pl.BlockSpec(wq.shape, lambda i: (0, 0)),
                  pl.BlockSpec(k1.shape, lambda i: (0, 0)),
                  pl.BlockSpec(k2.shape, lambda i: (0, 0))],
        out_specs=[out, out, out, out],
        out_shape=[f32, f32, bf16, bf16],
        compiler_params=_cparams(("arbitrary",)),
    )(x, g.reshape(1, d), wq, k1, k2)


def _peer_dense_kernel(x_ref, g_ref, u_ref, vt_ref, a1_ref, cnt_ref, a2_ref, rank2_ref, o_ref,
                       xnt_ref, acc_ref, hta_ref, htb_ref, pta_ref, ptb_ref):
    e = pl.program_id(1)

    @pl.when(e == 0)
    def _():
        xn = _rms(x_ref[...], g_ref[...])
        xnt_ref[...] = xn.T.astype(BF16)
        acc_ref[...] = jnp.zeros_like(acc_ref)

    inv_sqrt2 = 1.0 / math.sqrt(2.0)
    zero = jnp.zeros((), BF16)
    n_q, job_rows, _ = u_ref.shape
    sub = xnt_ref.shape[1] // 2
    n1_per_job = job_rows // PEER_NKEYS
    pack = BF16_SUBLANES
    groups = PEER_NKEYS // pack
    col_a, col_b = slice(0, sub), slice(sub, 2 * sub)
    ht_refs = {col_a: hta_ref, col_b: htb_ref}
    pt_refs = {col_a: pta_ref, col_b: ptb_ref}

    def matmul1(q, cols):
        ht_refs[cols][...] = jnp.dot(u_ref[q], xnt_ref[:, cols], preferred_element_type=F32)

    def gate(q, cols):
        ht_ref, pt_ref = ht_refs[cols], pt_refs[cols]
        for rl in range(n1_per_job):
            r = q * n1_per_job + rl
            w = None
            for h in range(PEER_HEADS):
                cnt = jnp.broadcast_to(cnt_ref[h, pl.ds(r, 1), cols], (pack, sub)).astype(BF16)
                a1 = jnp.broadcast_to(a1_ref[h, pl.ds(r, 1), cols], (pack, sub)).astype(BF16)
                sel = rank2_ref[h, :, :, cols] < cnt[None]
                term = a1[None] * jnp.where(sel, a2_ref[h, :, :, cols], zero)
                w = term if w is None else w + term
            rows = slice(rl * PEER_NKEYS, (rl + 1) * PEER_NKEYS)
            hb = ht_ref[rows, :]
            gelu = 0.5 * hb * (1.0 + lax.erf(hb * inv_sqrt2))
            p = w * gelu.astype(BF16).reshape(groups, pack, sub)
            pt_ref[rows, :] = p.reshape(PEER_NKEYS, sub)

    def matmul2(q, cols):
        acc_ref[:, cols] += jnp.dot(vt_ref[q], pt_refs[cols][...], preferred_element_type=F32)

    def pair(q, has_prev, has_next):
        matmul1(q, col_b)
        gate(q, col_a)
        if has_prev:
            matmul2(q - 1, col_b)
        if has_next:
            matmul1(q + 1, col_a)
        gate(q, col_b)
        matmul2(q, col_a)

    matmul1(0, col_a)
    if n_q == 1:
        pair(0, False, False)
    else:
        pair(0, False, True)

        def body(q, carry):
            pair(q, True, True)
            return carry

        lax.fori_loop(1, n_q - 1, body, 0)
        pair(n_q - 1, True, False)
    matmul2(n_q - 1, col_b)

    @pl.when(e == pl.num_programs(1) - 1)
    def _():
        o_ref[...] = x_ref[...] + acc_ref[...].T


PEER_JOB_ROWS = 512


def _expert_jobs(u, v):
    n_exp, d = u.shape
    n_jobs = n_exp // PEER_JOB_ROWS
    u3 = u.astype(BF16).reshape(n_jobs, PEER_JOB_ROWS, d)
    vt3 = jnp.transpose(v.astype(BF16).reshape(n_jobs, PEER_JOB_ROWS, d), (0, 2, 1))
    return u3, vt3


def _peer_dense(x, g, u, vt, a1t, cnt, a2t, rank2, tt, et):
    t, d = x.shape
    n_jobs, job_rows, _ = u.shape
    n_q = et // job_rows
    n1_per_tile = et // PEER_NKEYS
    groups = PEER_NKEYS // BF16_SUBLANES
    grouped = (PEER_HEADS, groups, BF16_SUBLANES, t)
    full = pl.BlockSpec((PEER_HEADS, groups, BF16_SUBLANES, tt), lambda i, e: (0, 0, 0, i))
    part = pl.BlockSpec((PEER_HEADS, n1_per_tile, tt), lambda i, e: (0, e, i))
    sub = tt // 2
    return pl.pallas_call(
        _peer_dense_kernel,
        grid=(t // tt, n_jobs // n_q),
        in_specs=[pl.BlockSpec((tt, d), lambda i, e: (i, 0)),
                  pl.BlockSpec((1, d), lambda i, e: (0, 0)),
                  pl.BlockSpec((n_q, job_rows, d), lambda i, e: (e, 0, 0)),
                  pl.BlockSpec((n_q, d, job_rows), lambda i, e: (e, 0, 0)),
                  part, part, full, full],
        out_specs=pl.BlockSpec((tt, d), lambda i, e: (i, 0)),
        out_shape=jax.ShapeDtypeStruct((t, d), F32),
        scratch_shapes=[pltpu.VMEM((d, tt), BF16), pltpu.VMEM((d, tt), F32),
                        pltpu.VMEM((job_rows, sub), F32), pltpu.VMEM((job_rows, sub), F32),
                        pltpu.VMEM((job_rows, sub), BF16), pltpu.VMEM((job_rows, sub), BF16)],
        compiler_params=_cparams(("arbitrary", "arbitrary"), vmem_mb=52),
    )(x, g.reshape(1, d), u, vt, a1t, cnt, a2t.reshape(grouped), rank2.reshape(grouped))


def _rmsnorm_kernel(x_ref, g_ref, o_ref):
    o_ref[...] = _rms(x_ref[...], g_ref[...])


def _rmsnorm(x, g, tm):
    t, d = x.shape
    return pl.pallas_call(
        _rmsnorm_kernel,
        grid=(t // tm,),
        in_specs=[pl.BlockSpec((tm, d), lambda i: (i, 0)), pl.BlockSpec((1, d), lambda i: (0, 0))],
        out_specs=pl.BlockSpec((tm, d), lambda i: (i, 0)),
        out_shape=jax.ShapeDtypeStruct((t, d), F32),
        compiler_params=_cparams(("arbitrary",)),
    )(x, g.reshape(1, d))


def _tiles(seq):
    pick = lambda want: min(want, seq)
    return dict(tm=pick(256), tq=pick(512), tconv=pick(512), tgla=pick(512), tt=pick(256), td=pick(512),
                et=2048)


def kernel(x, positions, norm_mix, w_in, lam_q1, lam_k1, lam_q2, lam_k2, diff_norm, conv_w, gla_w_gate2, gla_b_gate, gla_norm, w_out, norm_ffn, peer_w_q, peer_keys1, peer_keys2, peer_u, peer_v, norm_final):
    batch, seq, d = x.shape
    t = batch * seq
    depth = w_in.shape[0]
    tl = _tiles(seq)
    xt = x.reshape(t, d)
    tabs = _rope_tables(positions, tl["tm"])
    for i in range(depth):
        w_in_p = jnp.pad(w_in[i], ((0, 0), (0, IN_PAD - IN_WIDTH))).astype(BF16)
        proj = _rms_matmul(xt, norm_mix[i], w_in_p, tl["tm"])
        qt, kr, vt3 = _rope_prep(proj, tabs, tl["tq"])
        lam_init = 0.8 - 0.6 * math.exp(-0.3 * i)
        o_da = _diff_attention(qt, kr, vt3, lam_q1[i], lam_k1[i], lam_q2[i], lam_k2[i], diff_norm[i],
                               batch, seq, lam_init, tl["tq"])
        o_sc = _short_conv(proj, conv_w[i], seq, tl["tconv"])
        o_g = _gla(proj, gla_w_gate2[i], gla_b_gate[i], gla_norm[i], batch, seq, tl["tgla"])
        xt = _out_proj(xt, o_da, o_sc, o_g, w_out[i].astype(BF16), tl["tm"])
        a1t, cnt, a2t, rank2 = _peer_route(xt, norm_ffn[i], peer_w_q[i].astype(BF16),
                                           peer_keys1[i].astype(BF16), peer_keys2[i].astype(BF16), tl["tt"])
        u3, vt3 = _expert_jobs(peer_u[i], peer_v[i])
        xt = _peer_dense(xt, norm_ffn[i], u3, vt3, a1t, cnt, a2t, rank2, tl["td"], tl["et"])
    return _rmsnorm(xt, norm_final, tl["tm"]).reshape(batch, seq, d)
```

```python
import functools
import math

import jax
import jax.numpy as jnp
from jax import lax
from jax.experimental import pallas as pl
from jax.experimental.pallas import tpu as pltpu

F32 = jnp.float32
BF16 = jnp.bfloat16

D_MODEL = 1024
EPS = 1e-6
HEAD_DIM = 64
DA_HEADS = 4
DA_WIDTH = 512
ROPE_THETA = 500000.0
ROT_DIM = 16
SC_WIDTH = 256
CONV_W = 3
GLA_HEADS = 4
GLA_DK = 32
GLA_DV = 64
GLA_KW = 128
GLA_VW = 256
GLA_GATE_RANK = 16
GLA_GATE_TEMP = 16.0
GLA_CHUNK = 64
GLA_SUB = 16
GLA_UNROLL = 4
IN_WIDTH = 3088
IN_PAD = 3200
PEER_HEADS = 8
PEER_NKEYS = 128
PEER_EXPERTS = PEER_NKEYS * PEER_NKEYS
PEER_TOPK = 16
LANES = 128
BF16_SUBLANES = 16
NEG_INF = float("-inf")

COL_DA_Q, COL_DA_K, COL_DA_V = 0, 1, 2
COL_SC_B, COL_SC_C, COL_SC_H = 6, 7, 8
COL_G_V, COL_G_R = 10, 11
COL_G_Q, COL_G_K, COL_G_LR = 18, 19, 24


def _cparams(sem, vmem_mb=40):
    return pltpu.CompilerParams(dimension_semantics=sem, vmem_limit_bytes=vmem_mb * 1024 * 1024)


def _rms(x, g):
    return x * lax.rsqrt(jnp.mean(x * x, axis=-1, keepdims=True) + EPS) * g


def _rms_matmul_kernel(x_ref, g_ref, w_ref, o_ref):
    xn = _rms(x_ref[...], g_ref[...]).astype(BF16)
    o_ref[...] = jnp.dot(xn, w_ref[...], preferred_element_type=F32)


def _rms_matmul(x, g, w, tm):
    t, d = x.shape
    n = w.shape[1]
    return pl.pallas_call(
        _rms_matmul_kernel,
        grid=(t // tm,),
        in_specs=[pl.BlockSpec((tm, d), lambda i: (i, 0)),
                  pl.BlockSpec((1, d), lambda i: (0, 0)),
                  pl.BlockSpec((d, n), lambda i: (0, 0))],
        out_specs=pl.BlockSpec((tm, n), lambda i: (i, 0)),
        out_shape=jax.ShapeDtypeStruct((t, n), F32),
        compiler_params=_cparams(("arbitrary",)),
    )(x, g.reshape(1, d), w)


def _rope_table_kernel(pos_ref, inv_ref, mrot_ref, m1_ref, m2_ref, c_ref, s1_ref, s2_ref):
    ang = pos_ref[...] * inv_ref[...]
    c = jnp.cos(ang)
    s = jnp.sin(ang)
    mrot = mrot_ref[...]
    c_ref[...] = mrot * c + (1.0 - mrot)
    s1_ref[...] = -(m1_ref[...] * s)
    s2_ref[...] = m2_ref[...] * s


def _rope_tables(positions, tm):
    t = positions.size
    pos = jnp.broadcast_to(positions.reshape(t, 1).astype(F32), (t, LANES))
    d = jnp.arange(LANES) % HEAD_DIM
    half = ROT_DIM // 2
    inv = ROPE_THETA ** (-jnp.arange(0, ROT_DIM, 2, dtype=F32) / ROT_DIM)
    inv_row = jnp.where(d < ROT_DIM, inv[d % half], 0.0).astype(F32).reshape(1, LANES)
    mrot = (d < ROT_DIM).astype(F32).reshape(1, LANES)
    m1 = (d < half).astype(F32).reshape(1, LANES)
    m2 = ((d >= half) & (d < ROT_DIM)).astype(F32).reshape(1, LANES)
    row = pl.BlockSpec((1, LANES), lambda i: (0, 0))
    blk = pl.BlockSpec((tm, LANES), lambda i: (i, 0))
    return pl.pallas_call(
        _rope_table_kernel,
        grid=(t // tm,),
        in_specs=[blk, row, row, row, row],
        out_specs=[blk, blk, blk],
        out_shape=[jax.ShapeDtypeStruct((t, LANES), F32)] * 3,
        compiler_params=_cparams(("arbitrary",)),
    )(pos, inv_row, mrot, m1, m2)


def _rope_prep_kernel(q_ref, k_ref, v_ref, c_ref, s1_ref, s2_ref, qt_ref, ko_ref, vt_ref, *, qscale):
    c = c_ref[...]
    s1 = s1_ref[...]
    s2 = s2_ref[...]
    half = ROT_DIM // 2

    def rope(x):
        return x * c + pltpu.roll(x, LANES - half, 1) * s1 + pltpu.roll(x, half, 1) * s2

    for g in range(DA_WIDTH // LANES):
        sl = slice(LANES * g, LANES * (g + 1))
        qt_ref[sl, :] = (rope(q_ref[:, sl]) * qscale).T.astype(BF16)
        ko_ref[:, sl] = rope(k_ref[:, sl]).astype(BF16)
        vt_ref[0, sl, :] = v_ref[:, sl].T.astype(BF16)


def _rope_prep(proj, tabs, tm):
    t = proj.shape[0]
    c, s1, s2 = tabs
    col = lambda j: pl.BlockSpec((tm, DA_WIDTH), lambda i, j=j: (i, j))
    tab = pl.BlockSpec((tm, LANES), lambda i: (i, 0))
    return pl.pallas_call(
        functools.partial(_rope_prep_kernel, qscale=HEAD_DIM ** -0.5 * math.log2(math.e)),
        grid=(t // tm,),
        in_specs=[col(COL_DA_Q), col(COL_DA_K), col(COL_DA_V), tab, tab, tab],
        out_specs=[pl.BlockSpec((DA_WIDTH, tm), lambda i: (0, i)),
                   pl.BlockSpec((tm, DA_WIDTH), lambda i: (i, 0)),
                   pl.BlockSpec((1, DA_WIDTH, tm), lambda i: (i, 0, 0))],
        out_shape=[jax.ShapeDtypeStruct((DA_WIDTH, t), BF16),
                   jax.ShapeDtypeStruct((t, DA_WIDTH), BF16),
                   jax.ShapeDtypeStruct((t // tm, DA_WIDTH, tm), BF16)],
        compiler_params=_cparams(("arbitrary",)),
    )(proj, proj, proj, c, s1, s2)


def _attn_kernel(qt_ref, k_ref, vt_ref, lq1_ref, lk1_ref, lq2_ref, lk2_ref, gain_ref, o_ref,
                 acc1_ref, acc2_ref, m1_ref, l1_ref, m2_ref, l2_ref, s1a_ref, s2a_ref, s1b_ref, s2b_ref,
                 *, tq, lam_init):
    qi = pl.program_id(2)
    qt = qt_ref[...]
    row = lax.broadcasted_iota(jnp.int32, (LANES, 1), 0)
    zero = jnp.zeros_like(qt)
    qa = jnp.where(row < HEAD_DIM, qt, zero)
    qb = jnp.where(row >= HEAD_DIM, qt, zero)
    acc1_ref[...] = jnp.zeros_like(acc1_ref)
    acc2_ref[...] = jnp.zeros_like(acc2_ref)

    for st_ref in (m1_ref, m2_ref):
        st_ref[...] = jnp.full(st_ref.shape, NEG_INF, F32)
    for st_ref in (l1_ref, l2_ref):
        st_ref[...] = jnp.zeros_like(st_ref)
    maps = ((qa, m1_ref, l1_ref, acc1_ref), (qb, m2_ref, l2_ref, acc2_ref))

    def scores(j, bufs):
        off = pl.multiple_of(j * tq, tq)
        kk = k_ref[pl.ds(off, tq), :]
        for (qm, _, _, _), s_ref in zip(maps, bufs):
            s_ref[...] = jnp.dot(kk, qm, preferred_element_type=F32)

    def absorb(j, bufs, masked):
        vt = vt_ref[j]
        for (_, m_ref, l_ref, acc_ref), s_ref in zip(maps, bufs):
            s = s_ref[...]
            if masked:
                r = lax.broadcasted_iota(jnp.int32, (tq, tq), 0)
                c = lax.broadcasted_iota(jnp.int32, (tq, tq), 1)
                s = jnp.where(r <= c, s, NEG_INF)
            m = m_ref[...]
            mn = jnp.maximum(m, jnp.max(s, axis=0, keepdims=True))
            alpha = jnp.exp2(m - mn)
            p = jnp.exp2(s - mn)
            m_ref[...] = mn
            l_ref[...] = alpha * l_ref[...] + jnp.sum(p, axis=0, keepdims=True)
            acc_ref[...] = alpha * acc_ref[...] + jnp.dot(vt, p.astype(BF16),
                                                          preferred_element_type=F32)

    buf_a, buf_b = (s1a_ref, s2a_ref), (s1b_ref, s2b_ref)
    scores(0, buf_a)

    def pair(i, carry):
        j = 2 * i
        scores(j + 1, buf_b)
        absorb(j, buf_a, False)
        scores(j + 2, buf_a)
        absorb(j + 1, buf_b, False)
        return carry

    lax.fori_loop(0, qi // 2, pair, 0)

    @pl.when(qi % 2 == 1)
    def _():
        scores(qi, buf_b)
        absorb(qi - 1, buf_a, False)
        absorb(qi, buf_b, True)

    @pl.when(qi % 2 == 0)
    def _():
        absorb(qi, buf_a, True)

    l1, l2 = l1_ref[...], l2_ref[...]

    lam = (jnp.exp(jnp.sum(lq1_ref[...] * lk1_ref[...], axis=-1, keepdims=True))
           - jnp.exp(jnp.sum(lq2_ref[...] * lk2_ref[...], axis=-1, keepdims=True)) + lam_init)
    o = acc1_ref[...] / l1 - lam * (acc2_ref[...] / l2)
    y = o * lax.rsqrt(jnp.mean(o * o, axis=0, keepdims=True) + EPS) * gain_ref[...]
    o_ref[...] = (y * (1.0 - lam_init)).T


def _diff_attention(qt, kr, vt3, lq1, lk1, lq2, lk2, gain, batch, seq, lam_init, tq):
    t = kr.shape[0]
    nq = seq // tq
    vec = lambda n: pl.BlockSpec((1, n), lambda b, h, i: (0, 0))
    return pl.pallas_call(
        functools.partial(_attn_kernel, tq=tq, lam_init=lam_init),
        grid=(batch, DA_HEADS, nq),
        in_specs=[pl.BlockSpec((LANES, tq), lambda b, h, i: (h, b * nq + i)),
                  pl.BlockSpec((seq, LANES), lambda b, h, i: (b, h)),
                  pl.BlockSpec((nq, LANES, tq), lambda b, h, i: (b, h, 0)),
                  vec(HEAD_DIM), vec(HEAD_DIM), vec(HEAD_DIM), vec(HEAD_DIM),
                  pl.BlockSpec((LANES, 1), lambda b, h, i: (0, 0))],
        out_specs=pl.BlockSpec((tq, LANES), lambda b, h, i: (b * nq + i, h)),
        out_shape=jax.ShapeDtypeStruct((t, DA_WIDTH), F32),
        scratch_shapes=([pltpu.VMEM((LANES, tq), F32)] * 2 + [pltpu.VMEM((1, tq), F32)] * 4
                        + [pltpu.VMEM((tq, tq), F32)] * 4),
        compiler_params=_cparams(("arbitrary", "arbitrary", "arbitrary")),
    )(qt, kr, vt3, lq1.reshape(1, -1), lk1.reshape(1, -1), lq2.reshape(1, -1), lk2.reshape(1, -1),
      gain.reshape(-1, 1))


def _short_conv_kernel(b_ref, c_ref, h_ref, cp_ref, hp_ref, w_ref, o_ref, *, tm, blocks_per_seq):
    i = pl.program_id(0)
    z = c_ref[...] * h_ref[...]
    zp = cp_ref[...] * hp_ref[...]
    first = (i % blocks_per_seq) == 0
    zp = jnp.where(first, jnp.zeros_like(zp), zp)
    row = lax.broadcasted_iota(jnp.int32, (tm, 1), 0)
    z1 = jnp.where(row == 0, zp[7:8, :], pltpu.roll(z, 1, 0))
    z2 = pltpu.roll(z, 2, 0)
    z2 = jnp.where(row == 0, zp[6:7, :], jnp.where(row == 1, zp[7:8, :], z2))
    w = w_ref[...]
    y = w[0:1, :] * z2 + w[1:2, :] * z1 + w[2:3, :] * z
    o_ref[...] = b_ref[...] * y


def _short_conv(proj, w, seq, tm):
    t = proj.shape[0]
    rb = tm // 8
    cur = lambda j: pl.BlockSpec((tm, SC_WIDTH), lambda i, j=j: (i, j))
    prev = lambda j: pl.BlockSpec((8, SC_WIDTH), lambda i, j=j: (jnp.maximum(i * rb - 1, 0), j))
    return pl.pallas_call(
        functools.partial(_short_conv_kernel, tm=tm, blocks_per_seq=seq // tm),
        grid=(t // tm,),
        in_specs=[cur(COL_SC_B), cur(COL_SC_C), cur(COL_SC_H), prev(COL_SC_C), prev(COL_SC_H),
                  pl.BlockSpec((CONV_W, SC_WIDTH), lambda i: (0, 0))],
        out_specs=pl.BlockSpec((tm, SC_WIDTH), lambda i: (i, 0)),
        out_shape=jax.ShapeDtypeStruct((t, SC_WIDTH), F32),
        compiler_params=_cparams(("arbitrary",)),
    )(proj, proj, proj, proj, proj, w)


def _split3(x):
    hi = x.astype(BF16)
    r = x - hi.astype(F32)
    mid = r.astype(BF16)
    lo = (r - mid.astype(F32)).astype(BF16)
    return hi, mid, lo


def _gla_kernel(q_ref, k_ref, v_ref, r_ref, lr_ref, wg_ref, bg_ref, gn_ref, o_ref, st_ref, *, tc):
    ch = GLA_CHUNK

    @pl.when(pl.program_id(1) == 0)
    def _():
        st_ref[...] = jnp.zeros_like(st_ref)

    ii = lax.broadcasted_iota(jnp.int32, (ch, ch), 0)
    jj = lax.broadcasted_iota(jnp.int32, (ch, ch), 1)
    tri = (ii >= jj).astype(BF16)
    rk = lax.broadcasted_iota(jnp.int32, (GLA_VW, GLA_KW), 0)
    ck = lax.broadcasted_iota(jnp.int32, (GLA_VW, GLA_KW), 1)
    bd_k = (rk // GLA_DV) == (ck // GLA_DK)
    rv = lax.broadcasted_iota(jnp.int32, (GLA_VW, GLA_VW), 0)
    cv = lax.broadcasted_iota(jnp.int32, (GLA_VW, GLA_VW), 1)
    bd_v = (rv // GLA_DV) == (cv // GLA_DV)
    seg = bd_v.astype(BF16)
    sb = GLA_SUB
    nsb = ch // sb
    hs = GLA_HEADS * sb
    iota = lambda shape, axis: lax.broadcasted_iota(jnp.int32, shape, axis)
    sub_row = iota((sb, 1), 0)
    head_expand = ((iota((GLA_KW, GLA_VW), 0) // GLA_DK)
                   == (iota((GLA_KW, GLA_VW), 1) // GLA_DV)).astype(BF16)
    sum_j = ((iota((ch, ch * sb), 1) // sb) == iota((ch, ch * sb), 0)).astype(BF16)
    bd_ks = (iota((hs, GLA_KW), 0) // sb) == (iota((hs, GLA_KW), 1) // GLA_DK)
    bd_vs = (iota((hs, GLA_VW), 0) // sb) == (iota((hs, GLA_VW), 1) // GLA_DV)
    off_w = (nsb - 1) * hs
    att_keep = iota((ch, off_w), 0) >= (iota((ch, off_w), 1) // hs + 1) * sb
    dn_t = (((1,), (1,)), ((), ()))
    scale = GLA_DK ** -0.5

    def chunk(c, carry):
        off = pl.multiple_of(c * ch, ch)
        rows = pl.ds(off, ch)
        x = jnp.dot(lr_ref[rows, :].astype(BF16), wg_ref[...], preferred_element_type=F32) + bg_ref[...]
        la = (jnp.minimum(x, 0.0) - jnp.log1p(jnp.exp(-jnp.abs(x)))) * (1.0 / GLA_GATE_TEMP)
        hi, mid, lo = _split3(la)
        b = (jnp.dot(tri, hi, preferred_element_type=F32) + jnp.dot(tri, mid, preferred_element_type=F32)
             + jnp.dot(tri, lo, preferred_element_type=F32))
        b_last = b[ch - 1:ch, :]
        q = q_ref[rows, :] * scale
        k = k_ref[rows, :]
        v = v_ref[rows, :]
        vb = v.astype(BF16)
        pieces, vrep = [], []
        for sbi in range(nsb):
            rs = slice(sbi * sb, (sbi + 1) * sb)
            qs, ks, bs = q[rs, :], k[rs, :], b[rs, :]
            for i in range(sb):
                dec = jnp.exp(jnp.where(sub_row <= i, bs[i:i + 1, :] - bs, NEG_INF))
                pieces.append(qs[i:i + 1, :] * ks * dec)
            vrep += [v[rs, :]] * sb
        x3 = jnp.concatenate(pieces, axis=0)
        x3h = x3.astype(BF16)
        x3l = (x3 - x3h.astype(F32)).astype(BF16)
        w = (jnp.dot(x3h, head_expand, preferred_element_type=F32)
             + jnp.dot(x3l, head_expand, preferred_element_type=F32))
        wv = (w * jnp.concatenate(vrep, axis=0)).astype(BF16)
        o = jnp.dot(sum_j, wv, preferred_element_type=F32)
        qts, kbds, vbds = [], [], []
        zeros_k = jnp.zeros((hs, GLA_KW), BF16)
        for sbi in range(nsb - 1):
            rs = slice(sbi * sb, (sbi + 1) * sb)
            end = (sbi + 1) * sb
            b_end = b[end - 1:end, :]
            qts.append((q * jnp.exp(jnp.minimum(b - b_end, 0.0))).astype(BF16))
            kt = (k[rs, :] * jnp.exp(b_end - b[rs, :])).astype(BF16)
            kbd = jnp.where(bd_ks, jnp.concatenate([kt] * GLA_HEADS, axis=0), jnp.zeros((), BF16))
            kbds.append(jnp.concatenate([kbd if s2 == sbi else zeros_k for s2 in range(nsb - 1)], axis=1))
            vbds.append(jnp.where(bd_vs, jnp.concatenate([vb[rs, :]] * GLA_HEADS, axis=0),
                                  jnp.zeros((), BF16)))
        att = lax.dot_general(jnp.concatenate(qts, axis=1), jnp.concatenate(kbds, axis=0), dn_t,
                              preferred_element_type=F32)
        att = jnp.where(att_keep, att, 0.0).astype(BF16)
        o = o + jnp.dot(att, jnp.concatenate(vbds, axis=0), preferred_element_type=F32)
        st = st_ref[...]
        bq = (q * jnp.exp(b)).astype(BF16)
        o = o + lax.dot_general(bq, st.astype(BF16), dn_t, preferred_element_type=F32)
        kl = (k * jnp.exp(b_last - b)).astype(BF16)
        ut = jnp.dot(v.T.astype(BF16), kl, preferred_element_type=F32)
        st_ref[...] = st * jnp.exp(b_last) + jnp.where(bd_k, ut, 0.0)
        o2 = o * o
        o2h = o2.astype(BF16)
        o2l = (o2 - o2h.astype(F32)).astype(BF16)
        ms = (jnp.dot(o2h, seg, preferred_element_type=F32)
              + jnp.dot(o2l, seg, preferred_element_type=F32)) * (1.0 / GLA_DV)
        r = r_ref[rows, :]
        gate = r * (1.0 / (1.0 + jnp.exp(-r)))
        o_ref[rows, :] = o * lax.rsqrt(ms + EPS) * gn_ref[...] * gate
        return carry

    lax.fori_loop(0, tc // ch, chunk, 0, unroll=GLA_UNROLL)


def _gla(proj, wg, bg, gn, batch, seq, tc):
    t = proj.shape[0]
    nb = seq // tc
    blk = lambda w, j: pl.BlockSpec((tc, w), lambda b, i, j=j: (b * nb + i, j))
    const = lambda r, c: pl.BlockSpec((r, c), lambda b, i: (0, 0))
    wg_pad = jnp.zeros((LANES, GLA_KW), F32).at[:GLA_GATE_RANK].set(wg).astype(BF16)
    gn_row = jnp.tile(gn, GLA_HEADS).reshape(1, GLA_VW)
    return pl.pallas_call(
        functools.partial(_gla_kernel, tc=tc),
        grid=(batch, nb),
        in_specs=[blk(GLA_KW, COL_G_Q), blk(GLA_KW, COL_G_K), blk(GLA_VW, COL_G_V), blk(GLA_VW, COL_G_R),
                  blk(LANES, COL_G_LR), const(LANES, GLA_KW), const(1, GLA_KW), const(1, GLA_VW)],
        out_specs=pl.BlockSpec((tc, GLA_VW), lambda b, i: (b * nb + i, 0)),
        out_shape=jax.ShapeDtypeStruct((t, GLA_VW), F32),
        scratch_shapes=[pltpu.VMEM((GLA_VW, GLA_KW), F32)],
        compiler_params=_cparams(("arbitrary", "arbitrary")),
    )(proj, proj, proj, proj, proj, wg_pad, bg.reshape(1, GLA_KW), gn_row)


def _out_proj_kernel(x_ref, a_ref, s_ref, g_ref, w_ref, o_ref):
    acc = jnp.dot(a_ref[...].astype(BF16), w_ref[0:DA_WIDTH, :], preferred_element_type=F32)
    acc += jnp.dot(s_ref[...].astype(BF16), w_ref[DA_WIDTH:DA_WIDTH + SC_WIDTH, :],
                   preferred_element_type=F32)
    acc += jnp.dot(g_ref[...].astype(BF16), w_ref[DA_WIDTH + SC_WIDTH:, :], preferred_element_type=F32)
    o_ref[...] = x_ref[...] + acc


def _out_proj(x, o_da, o_sc, o_g, w, tm):
    t, d = x.shape
    blk = lambda n: pl.BlockSpec((tm, n), lambda i: (i, 0))
    return pl.pallas_call(
        _out_proj_kernel,
        grid=(t // tm,),
        in_specs=[blk(d), blk(DA_WIDTH), blk(SC_WIDTH), blk(GLA_VW),
                  pl.BlockSpec(w.shape, lambda i: (0, 0))],
        out_specs=blk(d),
        out_shape=jax.ShapeDtypeStruct((t, d), F32),
        compiler_params=_cparams(("arbitrary",)),
    )(x, o_da, o_sc, o_g, w)


SUBLANES = 8


def _merge_exchange_network(n):
    pairs = []
    t = max(1, math.ceil(math.log2(n)))
    p = 2 ** (t - 1)
    while p > 0:
        q, r, d = 2 ** (t - 1), 0, p
        while d > 0:
            pairs += [(i, i + d) for i in range(n - d) if (i & p) == r]
            d, q, r = q - p, q // 2, p
        p //= 2
    return pairs


def _top_desc(x, n):
    groups = [x[SUBLANES * i:SUBLANES * (i + 1), :] for i in range(x.shape[0] // SUBLANES)]
    for i, j in _merge_exchange_network(len(groups)):
        groups[i], groups[j] = jnp.maximum(groups[i], groups[j]), jnp.minimum(groups[i], groups[j])
    vals = []
    for r in range(n):
        m = jnp.max(groups[0], axis=0, keepdims=True)
        vals.append(m)
        hit = groups[0] == m
        live = min(len(groups), n - r)
        for k in range(live - 1):
            groups[k] = jnp.where(hit, groups[k + 1], groups[k])
        if live == len(groups) and live > 0 and r + 1 < n:
            groups[live - 1] = jnp.where(hit, NEG_INF, groups[live - 1])
    return vals


def _ranks_of(x, vals):
    rank = jnp.full(x.shape, float(len(vals)), F32)
    for r in reversed(range(len(vals))):
        rank = jnp.where(x == vals[r], float(r), rank)
    return rank


def _peer_route_kernel(x_ref, g_ref, wq_ref, k1_ref, k2_ref, a1_ref, cnt_ref, a2_ref, rank2_ref):
    k = PEER_TOPK
    tt = x_ref.shape[0]
    xn = _rms(x_ref[...], g_ref[...]).astype(BF16)
    q = jnp.dot(xn, wq_ref[...], preferred_element_type=F32)
    dn = (((1,), (1,)), ((), ()))
    for h in range(PEER_HEADS):
        qa = q[:, (2 * h) * LANES:(2 * h + 1) * LANES].astype(BF16)
        qb = q[:, (2 * h + 1) * LANES:(2 * h + 2) * LANES].astype(BF16)
        s1 = lax.dot_general(k1_ref[...], qa, dn, preferred_element_type=F32)
        s2 = lax.dot_general(k2_ref[...], qb, dn, preferred_element_type=F32)
        v1 = _top_desc(s1, k)
        v2 = _top_desc(s2, k)
        rank2 = _ranks_of(s2, v2)
        cands = [v1[i] + v2[j] for i in range(k) for j in range(k // (i + 1))]
        pad = (-len(cands)) % 8
        cands += [jnp.full((1, tt), NEG_INF, F32)] * pad
        top = _top_desc(jnp.concatenate(cands, axis=0), k)
        z = jnp.ones((1, tt), F32)
        for r in range(1, k):
            z = z + jnp.exp(top[r] - top[0])
        thr = top[k - 1]
        v1s = jnp.concatenate(v1, axis=0)
        cnt_rank = jnp.zeros_like(v1s)
        for j in range(k):
            cnt_rank = cnt_rank + ((v1s + v2[j]) >= thr).astype(F32)
        cnt = jnp.zeros_like(s1)
        for i in range(k):
            cnt = jnp.where(s1 == v1[i], cnt_rank[i:i + 1, :], cnt)
        cnt_ref[h] = cnt
        rank2_ref[h] = rank2.astype(BF16)
        a1_ref[h] = jnp.exp(s1 - v1[0]) * (1.0 / z)
        a2_ref[h] = jnp.exp(s2 - v2[0]).astype(BF16)


def _peer_route(x, g, wq, k1, k2, tt):
    t, d = x.shape
    shape = (PEER_HEADS, PEER_NKEYS, t)
    out = pl.BlockSpec((PEER_HEADS, PEER_NKEYS, tt), lambda i: (0, 0, i))
    f32 = jax.ShapeDtypeStruct(shape, F32)
    bf16 = jax.ShapeDtypeStruct(shape, BF16)
    return pl.pallas_call(
        _peer_route_kernel,
        grid=(t // tt,),
        in_specs=[pl.BlockSpec((tt, d), lambda i: (i, 0)),
                  pl.BlockSpec((1, d), lambda i: (0, 0)),
                  pl.BlockSpec(wq.shape, lambda i: (0, 0)),
                  pl.BlockSpec(k1.shape, lambda i: (0, 0)),
                  pl.BlockSpec(k2.shape, lambda i: (0, 0))],
        out_specs=[out, out, out, out],
        out_shape=[f32, f32, bf16, bf16],
        compiler_params=_cparams(("arbitrary",)),
    )(x, g.reshape(1, d), wq, k1, k2)


def _peer_dense_kernel(x_ref, g_ref, u_ref, vt_ref, a1_ref, cnt_ref, a2_ref, rank2_ref, o_ref,
                       xnt_ref, acc_ref, hta_ref, htb_ref, pta_ref, ptb_ref):
    e = pl.program_id(1)

    @pl.when(e == 0)
    def _():
        xn = _rms(x_ref[...], g_ref[...])
        xnt_ref[...] = xn.T.astype(BF16)
        acc_ref[...] = jnp.zeros_like(acc_ref)

    inv_sqrt2 = 1.0 / math.sqrt(2.0)
    zero = jnp.zeros((), BF16)
    n_q, job_rows, _ = u_ref.shape
    sub = xnt_ref.shape[1] // 2
    n1_per_job = job_rows // PEER_NKEYS
    pack = BF16_SUBLANES
    groups = PEER_NKEYS // pack
    col_a, col_b = slice(0, sub), slice(sub, 2 * sub)
    ht_refs = {col_a: hta_ref, col_b: htb_ref}
    pt_refs = {col_a: pta_ref, col_b: ptb_ref}

    def matmul1(q, cols):
        ht_refs[cols][...] = jnp.dot(u_ref[q], xnt_ref[:, cols], preferred_element_type=F32)

    def gate(q, cols):
        ht_ref, pt_ref = ht_refs[cols], pt_refs[cols]
        for rl in range(n1_per_job):
            r = q * n1_per_job + rl
            w = None
            for h in range(PEER_HEADS):
                cnt = jnp.broadcast_to(cnt_ref[h, pl.ds(r, 1), cols], (pack, sub)).astype(BF16)
                a1 = jnp.broadcast_to(a1_ref[h, pl.ds(r, 1), cols], (pack, sub)).astype(BF16)
                sel = rank2_ref[h, :, :, cols] < cnt[None]
                term = a1[None] * jnp.where(sel, a2_ref[h, :, :, cols], zero)
                w = term if w is None else w + term
            rows = slice(rl * PEER_NKEYS, (rl + 1) * PEER_NKEYS)
            hb = ht_ref[rows, :]
            gelu = 0.5 * hb * (1.0 + lax.erf(hb * inv_sqrt2))
            p = w * gelu.astype(BF16).reshape(groups, pack, sub)
            pt_ref[rows, :] = p.reshape(PEER_NKEYS, sub)

    def matmul2(q, cols):
        acc_ref[:, cols] += jnp.dot(vt_ref[q], pt_refs[cols][...], preferred_element_type=F32)

    def pair(q, has_prev, has_next):
        matmul1(q, col_b)
        gate(q, col_a)
        if has_prev:
            matmul2(q - 1, col_b)
        if has_next:
            matmul1(q + 1, col_a)
        gate(q, col_b)
        matmul2(q, col_a)

    matmul1(0, col_a)
    if n_q == 1:
        pair(0, False, False)
    else:
        pair(0, False, True)

        def body(q, carry):
            pair(q, True, True)
            return carry

        lax.fori_loop(1, n_q - 1, body, 0)
        pair(n_q - 1, True, False)
    matmul2(n_q - 1, col_b)

    @pl.when(e == pl.num_programs(1) - 1)
    def _():
        o_ref[...] = x_ref[...] + acc_ref[...].T


PEER_JOB_ROWS = 512


def _expert_jobs(u, v):
    n_exp, d = u.shape
    n_jobs = n_exp // PEER_JOB_ROWS
    u3 = u.astype(BF16).reshape(n_jobs, PEER_JOB_ROWS, d)
    vt3 = jnp.transpose(v.astype(BF16).reshape(n_jobs, PEER_JOB_ROWS, d), (0, 2, 1))
    return u3, vt3


def _peer_dense(x, g, u, vt, a1t, cnt, a2t, rank2, tt, et):
    t, d = x.shape
    n_jobs, job_rows, _ = u.shape
    n_q = et // job_rows
    n1_per_tile = et // PEER_NKEYS
    groups = PEER_NKEYS // BF16_SUBLANES
    grouped = (PEER_HEADS, groups, BF16_SUBLANES, t)
    full = pl.BlockSpec((PEER_HEADS, groups, BF16_SUBLANES, tt), lambda i, e: (0, 0, 0, i))
    part = pl.BlockSpec((PEER_HEADS, n1_per_tile, tt), lambda i, e: (0, e, i))
    sub = tt // 2
    return pl.pallas_call(
        _peer_dense_kernel,
        grid=(t // tt, n_jobs // n_q),
        in_specs=[pl.BlockSpec((tt, d), lambda i, e: (i, 0)),
                  pl.BlockSpec((1, d), lambda i, e: (0, 0)),
                  pl.BlockSpec((n_q, job_rows, d), lambda i, e: (e, 0, 0)),
                  pl.BlockSpec((n_q, d, job_rows), lambda i, e: (e, 0, 0)),
                  part, part, full, full],
        out_specs=pl.BlockSpec((tt, d), lambda i, e: (i, 0)),
        out_shape=jax.ShapeDtypeStruct((t, d), F32),
        scratch_shapes=[pltpu.VMEM((d, tt), BF16), pltpu.VMEM((d, tt), F32),
                        pltpu.VMEM((job_rows, sub), F32), pltpu.VMEM((job_rows, sub), F32),
                        pltpu.VMEM((job_rows, sub), BF16), pltpu.VMEM((job_rows, sub), BF16)],
        compiler_params=_cparams(("arbitrary", "arbitrary"), vmem_mb=52),
    )(x, g.reshape(1, d), u, vt, a1t, cnt, a2t.reshape(grouped), rank2.reshape(grouped))


def _rmsnorm_kernel(x_ref, g_ref, o_ref):
    o_ref[...] = _rms(x_ref[...], g_ref[...])


def _rmsnorm(x, g, tm):
    t, d = x.shape
    return pl.pallas_call(
        _rmsnorm_kernel,
        grid=(t // tm,),
        in_specs=[pl.BlockSpec((tm, d), lambda i: (i, 0)), pl.BlockSpec((1, d), lambda i: (0, 0))],
        out_specs=pl.BlockSpec((tm, d), lambda i: (i, 0)),
        out_shape=jax.ShapeDtypeStruct((t, d), F32),
        compiler_params=_cparams(("arbitrary",)),
    )(x, g.reshape(1, d))


def _tiles(seq):
    pick = lambda want: min(want, seq)
    return dict(tm=pick(256), tq=pick(512), tconv=pick(512), tgla=pick(512), tt=pick(256), td=pick(512),
                et=2048)


def kernel(x, positions, norm_mix, w_in, lam_q1, lam_k1, lam_q2, lam_k2, diff_norm, conv_w, gla_w_gate2, gla_b_gate, gla_norm, w_out, norm_ffn, peer_w_q, peer_keys1, peer_keys2, peer_u, peer_v, norm_final):
    batch, seq, d = x.shape
    t = batch * seq
    depth = w_in.shape[0]
    tl = _tiles(seq)
    xt = x.reshape(t, d)
    tabs = _rope_tables(positions, tl["tm"])
    for i in range(depth):
        w_in_p = jnp.pad(w_in[i], ((0, 0), (0, IN_PAD - IN_WIDTH))).astype(BF16)
        proj = _rms_matmul(xt, norm_mix[i], w_in_p, tl["tm"])
        qt, kr, vt3 = _rope_prep(proj, tabs, tl["tq"])
        lam_init = 0.8 - 0.6 * math.exp(-0.3 * i)
        o_da = _diff_attention(qt, kr, vt3, lam_q1[i], lam_k1[i], lam_q2[i], lam_k2[i], diff_norm[i],
                               batch, seq, lam_init, tl["tq"])
        o_sc = _short_conv(proj, conv_w[i], seq, tl["tconv"])
        o_g = _gla(proj, gla_w_gate2[i], gla_b_gate[i], gla_norm[i], batch, seq, tl["tgla"])
        xt = _out_proj(xt, o_da, o_sc, o_g, w_out[i].astype(BF16), tl["tm"])
        a1t, cnt, a2t, rank2 = _peer_route(xt, norm_ffn[i], peer_w_q[i].astype(BF16),
                                           peer_keys1[i].astype(BF16), peer_keys2[i].astype(BF16), tl["tt"])
        u3, vt3 = _expert_jobs(peer_u[i], peer_v[i])
        xt = _peer_dense(xt, norm_ffn[i], u3, vt3, a1t, cnt, a2t, rank2, tl["td"], tl["et"])
    return _rmsnorm(xt, norm_final, tl["tm"]).reshape(batch, seq, d)
```

```python
import functools
import math

import jax
import jax.numpy as jnp
from jax import lax
from jax.experimental import pallas as pl
from jax.experimental.pallas import tpu as pltpu

F32 = jnp.float32
BF16 = jnp.bfloat16

D_MODEL = 1024
EPS = 1e-6
HEAD_DIM = 64
DA_HEADS = 4
DA_WIDTH = 512
ROPE_THETA = 500000.0
ROT_DIM = 16
SC_WIDTH = 256
CONV_W = 3
GLA_HEADS = 4
GLA_DK = 32
GLA_DV = 64
GLA_KW = 128
GLA_VW = 256
GLA_GATE_RANK = 16
GLA_GATE_TEMP = 16.0
GLA_CHUNK = 64
GLA_SUB = 16
GLA_GROUP = 8
IN_WIDTH = 3088
IN_PAD = 3200
PEER_HEADS = 8
PEER_NKEYS = 128
PEER_EXPERTS = PEER_NKEYS * PEER_NKEYS
PEER_TOPK = 16
LANES = 128
BF16_SUBLANES = 16
NEG_INF = float("-inf")

COL_DA_Q, COL_DA_K, COL_DA_V = 0, 1, 2
COL_SC_B, COL_SC_C, COL_SC_H = 6, 7, 8
COL_G_V, COL_G_R = 10, 11
COL_G_Q, COL_G_K, COL_G_LR = 18, 19, 24


def _cparams(sem, vmem_mb=40):
    return pltpu.CompilerParams(dimension_semantics=sem, vmem_limit_bytes=vmem_mb * 1024 * 1024)


def _rms(x, g):
    return x * lax.rsqrt(jnp.mean(x * x, axis=-1, keepdims=True) + EPS) * g


def _rms_matmul_kernel(x_ref, g_ref, w_ref, o_ref):
    xn = _rms(x_ref[...], g_ref[...]).astype(BF16)
    o_ref[...] = jnp.dot(xn, w_ref[...], preferred_element_type=F32)


def _rms_matmul(x, g, w, tm):
    t, d = x.shape
    n = w.shape[1]
    return pl.pallas_call(
        _rms_matmul_kernel,
        grid=(t // tm,),
        in_specs=[pl.BlockSpec((tm, d), lambda i: (i, 0)),
                  pl.BlockSpec((1, d), lambda i: (0, 0)),
                  pl.BlockSpec((d, n), lambda i: (0, 0))],
        out_specs=pl.BlockSpec((tm, n), lambda i: (i, 0)),
        out_shape=jax.ShapeDtypeStruct((t, n), F32),
        compiler_params=_cparams(("arbitrary",)),
    )(x, g.reshape(1, d), w)


def _rope_table_kernel(pos_ref, inv_ref, mrot_ref, m1_ref, m2_ref, c_ref, s1_ref, s2_ref):
    ang = pos_ref[...] * inv_ref[...]
    c = jnp.cos(ang)
    s = jnp.sin(ang)
    mrot = mrot_ref[...]
    c_ref[...] = mrot * c + (1.0 - mrot)
    s1_ref[...] = -(m1_ref[...] * s)
    s2_ref[...] = m2_ref[...] * s


def _rope_tables(positions, tm):
    t = positions.size
    pos = jnp.broadcast_to(positions.reshape(t, 1).astype(F32), (t, LANES))
    d = jnp.arange(LANES) % HEAD_DIM
    half = ROT_DIM // 2
    inv = ROPE_THETA ** (-jnp.arange(0, ROT_DIM, 2, dtype=F32) / ROT_DIM)
    inv_row = jnp.where(d < ROT_DIM, inv[d % half], 0.0).astype(F32).reshape(1, LANES)
    mrot = (d < ROT_DIM).astype(F32).reshape(1, LANES)
    m1 = (d < half).astype(F32).reshape(1, LANES)
    m2 = ((d >= half) & (d < ROT_DIM)).astype(F32).reshape(1, LANES)
    row = pl.BlockSpec((1, LANES), lambda i: (0, 0))
    blk = pl.BlockSpec((tm, LANES), lambda i: (i, 0))
    return pl.pallas_call(
        _rope_table_kernel,
        grid=(t // tm,),
        in_specs=[blk, row, row, row, row],
        out_specs=[blk, blk, blk],
        out_shape=[jax.ShapeDtypeStruct((t, LANES), F32)] * 3,
        compiler_params=_cparams(("arbitrary",)),
    )(pos, inv_row, mrot, m1, m2)


def _rope_prep_kernel(q_ref, k_ref, v_ref, c_ref, s1_ref, s2_ref, qt_ref, ko_ref, vt_ref, *, qscale):
    c = c_ref[...]
    s1 = s1_ref[...]
    s2 = s2_ref[...]
    half = ROT_DIM // 2

    def rope(x):
        return x * c + pltpu.roll(x, LANES - half, 1) * s1 + pltpu.roll(x, half, 1) * s2

    for g in range(DA_WIDTH // LANES):
        sl = slice(LANES * g, LANES * (g + 1))
        qt_ref[sl, :] = (rope(q_ref[:, sl]) * qscale).T.astype(BF16)
        ko_ref[:, sl] = rope(k_ref[:, sl]).astype(BF16)
        vt_ref[0, sl, :] = v_ref[:, sl].T.astype(BF16)


def _rope_prep(proj, tabs, tm):
    t = proj.shape[0]
    c, s1, s2 = tabs
    col = lambda j: pl.BlockSpec((tm, DA_WIDTH), lambda i, j=j: (i, j))
    tab = pl.BlockSpec((tm, LANES), lambda i: (i, 0))
    return pl.pallas_call(
        functools.partial(_rope_prep_kernel, qscale=HEAD_DIM ** -0.5 * math.log2(math.e)),
        grid=(t // tm,),
        in_specs=[col(COL_DA_Q), col(COL_DA_K), col(COL_DA_V), tab, tab, tab],
        out_specs=[pl.BlockSpec((DA_WIDTH, tm), lambda i: (0, i)),
                   pl.BlockSpec((tm, DA_WIDTH), lambda i: (i, 0)),
                   pl.BlockSpec((1, DA_WIDTH, tm), lambda i: (i, 0, 0))],
        out_shape=[jax.ShapeDtypeStruct((DA_WIDTH, t), BF16),
                   jax.ShapeDtypeStruct((t, DA_WIDTH), BF16),
                   jax.ShapeDtypeStruct((t // tm, DA_WIDTH, tm), BF16)],
        compiler_params=_cparams(("arbitrary",)),
    )(proj, proj, proj, c, s1, s2)


def _attn_kernel(qt_ref, k_ref, vt_ref, lq1_ref, lk1_ref, lq2_ref, lk2_ref, gain_ref, o_ref,
                 acc1_ref, acc2_ref, m1_ref, l1_ref, m2_ref, l2_ref, s1a_ref, s2a_ref, s1b_ref, s2b_ref,
                 *, tq, lam_init):
    qi = pl.program_id(2)
    qt = qt_ref[...]
    row = lax.broadcasted_iota(jnp.int32, (LANES, 1), 0)
    zero = jnp.zeros_like(qt)
    qa = jnp.where(row < HEAD_DIM, qt, zero)
    qb = jnp.where(row >= HEAD_DIM, qt, zero)
    acc1_ref[...] = jnp.zeros_like(acc1_ref)
    acc2_ref[...] = jnp.zeros_like(acc2_ref)

    for st_ref in (m1_ref, m2_ref):
        st_ref[...] = jnp.full(st_ref.shape, NEG_INF, F32)
    for st_ref in (l1_ref, l2_ref):
        st_ref[...] = jnp.zeros_like(st_ref)
    maps = ((qa, m1_ref, l1_ref, acc1_ref), (qb, m2_ref, l2_ref, acc2_ref))

    def scores(j, bufs):
        off = pl.multiple_of(j * tq, tq)
        kk = k_ref[pl.ds(off, tq), :]
        for (qm, _, _, _), s_ref in zip(maps, bufs):
            s_ref[...] = jnp.dot(kk, qm, preferred_element_type=F32)

    def absorb(j, bufs, masked):
        vt = vt_ref[j]
        for (_, m_ref, l_ref, acc_ref), s_ref in zip(maps, bufs):
            s = s_ref[...]
            if masked:
                r = lax.broadcasted_iota(jnp.int32, (tq, tq), 0)
                c = lax.broadcasted_iota(jnp.int32, (tq, tq), 1)
                s = jnp.where(r <= c, s, NEG_INF)
            m = m_ref[...]
            mn = jnp.maximum(m, jnp.max(s, axis=0, keepdims=True))
            alpha = jnp.exp2(m - mn)
            p = jnp.exp2(s - mn)
            m_ref[...] = mn
            l_ref[...] = alpha * l_ref[...] + jnp.sum(p, axis=0, keepdims=True)
            acc_ref[...] = alpha * acc_ref[...] + jnp.dot(vt, p.astype(BF16),
                                                          preferred_element_type=F32)

    buf_a, buf_b = (s1a_ref, s2a_ref), (s1b_ref, s2b_ref)
    scores(0, buf_a)

    def pair(i, carry):
        j = 2 * i
        scores(j + 1, buf_b)
        absorb(j, buf_a, False)
        scores(j + 2, buf_a)
        absorb(j + 1, buf_b, False)
        return carry

    lax.fori_loop(0, qi // 2, pair, 0)

    @pl.when(qi % 2 == 1)
    def _():
        scores(qi, buf_b)
        absorb(qi - 1, buf_a, False)
        absorb(qi, buf_b, True)

    @pl.when(qi % 2 == 0)
    def _():
        absorb(qi, buf_a, True)

    l1, l2 = l1_ref[...], l2_ref[...]

    lam = (jnp.exp(jnp.sum(lq1_ref[...] * lk1_ref[...], axis=-1, keepdims=True))
           - jnp.exp(jnp.sum(lq2_ref[...] * lk2_ref[...], axis=-1, keepdims=True)) + lam_init)
    o = acc1_ref[...] / l1 - lam * (acc2_ref[...] / l2)
    y = o * lax.rsqrt(jnp.mean(o * o, axis=0, keepdims=True) + EPS) * gain_ref[...]
    o_ref[...] = (y * (1.0 - lam_init)).T


def _diff_attention(qt, kr, vt3, lq1, lk1, lq2, lk2, gain, batch, seq, lam_init, tq):
    t = kr.shape[0]
    nq = seq // tq
    vec = lambda n: pl.BlockSpec((1, n), lambda b, h, i: (0, 0))
    return pl.pallas_call(
        functools.partial(_attn_kernel, tq=tq, lam_init=lam_init),
        grid=(batch, DA_HEADS, nq),
        in_specs=[pl.BlockSpec((LANES, tq), lambda b, h, i: (h, b * nq + i)),
                  pl.BlockSpec((seq, LANES), lambda b, h, i: (b, h)),
                  pl.BlockSpec((nq, LANES, tq), lambda b, h, i: (b, h, 0)),
                  vec(HEAD_DIM), vec(HEAD_DIM), vec(HEAD_DIM), vec(HEAD_DIM),
                  pl.BlockSpec((LANES, 1), lambda b, h, i: (0, 0))],
        out_specs=pl.BlockSpec((tq, LANES), lambda b, h, i: (b * nq + i, h)),
        out_shape=jax.ShapeDtypeStruct((t, DA_WIDTH), F32),
        scratch_shapes=([pltpu.VMEM((LANES, tq), F32)] * 2 + [pltpu.VMEM((1, tq), F32)] * 4
                        + [pltpu.VMEM((tq, tq), F32)] * 4),
        compiler_params=_cparams(("arbitrary", "arbitrary", "arbitrary")),
    )(qt, kr, vt3, lq1.reshape(1, -1), lk1.reshape(1, -1), lq2.reshape(1, -1), lk2.reshape(1, -1),
      gain.reshape(-1, 1))


def _short_conv_kernel(b_ref, c_ref, h_ref, cp_ref, hp_ref, w_ref, o_ref, *, tm, blocks_per_seq):
    i = pl.program_id(0)
    z = c_ref[...] * h_ref[...]
    zp = cp_ref[...] * hp_ref[...]
    first = (i % blocks_per_seq) == 0
    zp = jnp.where(first, jnp.zeros_like(zp), zp)
    row = lax.broadcasted_iota(jnp.int32, (tm, 1), 0)
    z1 = jnp.where(row == 0, zp[7:8, :], pltpu.roll(z, 1, 0))
    z2 = pltpu.roll(z, 2, 0)
    z2 = jnp.where(row == 0, zp[6:7, :], jnp.where(row == 1, zp[7:8, :], z2))
    w = w_ref[...]
    y = w[0:1, :] * z2 + w[1:2, :] * z1 + w[2:3, :] * z
    o_ref[...] = b_ref[...] * y


def _short_conv(proj, w, seq, tm):
    t = proj.shape[0]
    rb = tm // 8
    cur = lambda j: pl.BlockSpec((tm, SC_WIDTH), lambda i, j=j: (i, j))
    prev = lambda j: pl.BlockSpec((8, SC_WIDTH), lambda i, j=j: (jnp.maximum(i * rb - 1, 0), j))
    return pl.pallas_call(
        functools.partial(_short_conv_kernel, tm=tm, blocks_per_seq=seq // tm),
        grid=(t // tm,),
        in_specs=[cur(COL_SC_B), cur(COL_SC_C), cur(COL_SC_H), prev(COL_SC_C), prev(COL_SC_H),
                  pl.BlockSpec((CONV_W, SC_WIDTH), lambda i: (0, 0))],
        out_specs=pl.BlockSpec((tm, SC_WIDTH), lambda i: (i, 0)),
        out_shape=jax.ShapeDtypeStruct((t, SC_WIDTH), F32),
        compiler_params=_cparams(("arbitrary",)),
    )(proj, proj, proj, proj, proj, w)


def _split3(x):
    hi = x.astype(BF16)
    r = x - hi.astype(F32)
    mid = r.astype(BF16)
    lo = (r - mid.astype(F32)).astype(BF16)
    return hi, mid, lo


def _gla_kernel(q_ref, k_ref, v_ref, r_ref, lr_ref, wg_ref, bg_ref, gn_ref, o_ref, st_ref, *, tc):
    ch = GLA_CHUNK

    @pl.when(pl.program_id(1) == 0)
    def _():
        st_ref[...] = jnp.zeros_like(st_ref)

    ii = lax.broadcasted_iota(jnp.int32, (ch, ch), 0)
    jj = lax.broadcasted_iota(jnp.int32, (ch, ch), 1)
    tri = (ii >= jj).astype(BF16)
    rk = lax.broadcasted_iota(jnp.int32, (GLA_VW, GLA_KW), 0)
    ck = lax.broadcasted_iota(jnp.int32, (GLA_VW, GLA_KW), 1)
    bd_k = (rk // GLA_DV) == (ck // GLA_DK)
    rv = lax.broadcasted_iota(jnp.int32, (GLA_VW, GLA_VW), 0)
    cv = lax.broadcasted_iota(jnp.int32, (GLA_VW, GLA_VW), 1)
    bd_v = (rv // GLA_DV) == (cv // GLA_DV)
    seg = bd_v.astype(BF16)
    sb = GLA_SUB
    nsb = ch // sb
    hs = GLA_HEADS * sb
    iota = lambda shape, axis: lax.broadcasted_iota(jnp.int32, shape, axis)
    sub_row = iota((sb, 1), 0)
    head_expand = ((iota((GLA_KW, GLA_VW), 0) // GLA_DK)
                   == (iota((GLA_KW, GLA_VW), 1) // GLA_DV)).astype(BF16)
    sum_j = ((iota((ch, ch * sb), 1) // sb) == iota((ch, ch * sb), 0)).astype(BF16)
    bd_ks = (iota((hs, GLA_KW), 0) // sb) == (iota((hs, GLA_KW), 1) // GLA_DK)
    bd_vs = (iota((hs, GLA_VW), 0) // sb) == (iota((hs, GLA_VW), 1) // GLA_DV)
    off_w = (nsb - 1) * hs
    att_keep = iota((ch, off_w), 0) >= (iota((ch, off_w), 1) // hs + 1) * sb
    dn_t = (((1,), (1,)), ((), ()))
    scale = GLA_DK ** -0.5

    def stage_decay(rows):
        x = jnp.dot(lr_ref[rows, :].astype(BF16), wg_ref[...], preferred_element_type=F32) + bg_ref[...]
        la = (jnp.minimum(x, 0.0) - jnp.log1p(jnp.exp(-jnp.abs(x)))) * (1.0 / GLA_GATE_TEMP)
        hi, mid, lo = _split3(la)
        return (jnp.dot(tri, hi, preferred_element_type=F32) + jnp.dot(tri, mid, preferred_element_type=F32)
                + jnp.dot(tri, lo, preferred_element_type=F32))

    def stage_pairwise(q, k, v, b):
        pieces, vrep = [], []
        for sbi in range(nsb):
            rs = slice(sbi * sb, (sbi + 1) * sb)
            qs, ks, bs = q[rs, :], k[rs, :], b[rs, :]
            for i in range(sb):
                dec = jnp.exp(jnp.where(sub_row <= i, bs[i:i + 1, :] - bs, NEG_INF))
                pieces.append(qs[i:i + 1, :] * ks * dec)
            vrep += [v[rs, :]] * sb
        x3 = jnp.concatenate(pieces, axis=0)
        x3h = x3.astype(BF16)
        return x3h, (x3 - x3h.astype(F32)).astype(BF16), jnp.concatenate(vrep, axis=0)

    def stage_head_sums(x3h, x3l):
        return (jnp.dot(x3h, head_expand, preferred_element_type=F32)
                + jnp.dot(x3l, head_expand, preferred_element_type=F32))

    def stage_cross_operands(q, k, vb, b):
        qts, kbds, vbds = [], [], []
        zeros_k = jnp.zeros((hs, GLA_KW), BF16)
        for sbi in range(nsb - 1):
            rs = slice(sbi * sb, (sbi + 1) * sb)
            end = (sbi + 1) * sb
            b_end = b[end - 1:end, :]
            qts.append((q * jnp.exp(jnp.minimum(b - b_end, 0.0))).astype(BF16))
            kt = (k[rs, :] * jnp.exp(b_end - b[rs, :])).astype(BF16)
            kbd = jnp.where(bd_ks, jnp.concatenate([kt] * GLA_HEADS, axis=0), jnp.zeros((), BF16))
            kbds.append(jnp.concatenate([kbd if s2 == sbi else zeros_k for s2 in range(nsb - 1)], axis=1))
            vbds.append(jnp.where(bd_vs, jnp.concatenate([vb[rs, :]] * GLA_HEADS, axis=0),
                                  jnp.zeros((), BF16)))
        return jnp.concatenate(qts, axis=1), jnp.concatenate(kbds, axis=0), jnp.concatenate(vbds, axis=0)

    def stage_cross_scores(q_all, k_all):
        att = lax.dot_general(q_all, k_all, dn_t, preferred_element_type=F32)
        return jnp.where(att_keep, att, 0.0).astype(BF16)

    def stage_state_update(k, v, b):
        b_last = b[ch - 1:ch, :]
        kl = (k * jnp.exp(b_last - b)).astype(BF16)
        ut = jnp.dot(v.T.astype(BF16), kl, preferred_element_type=F32)
        return jnp.exp(b_last), jnp.where(bd_k, ut, 0.0)

    def stage_mean_square(o):
        o2 = o * o
        o2h = o2.astype(BF16)
        o2l = (o2 - o2h.astype(F32)).astype(BF16)
        return (jnp.dot(o2h, seg, preferred_element_type=F32)
                + jnp.dot(o2l, seg, preferred_element_type=F32)) * (1.0 / GLA_DV)

    def stage_store(rows, o, ms):
        r = r_ref[rows, :]
        gate = r * (1.0 / (1.0 + jnp.exp(-r)))
        o_ref[rows, :] = o * lax.rsqrt(ms + EPS) * gn_ref[...] * gate

    each = lambda f, *cols: [f(*args) for args in zip(*cols)]
    gsize = min(GLA_GROUP, tc // ch)

    def group(g, carry):
        rows = [pl.ds(pl.multiple_of((g * gsize + c) * ch, ch), ch) for c in range(gsize)]
        b = each(stage_decay, rows)
        q = [q_ref[r, :] * scale for r in rows]
        k = [k_ref[r, :] for r in rows]
        v = [v_ref[r, :] for r in rows]
        vb = [x.astype(BF16) for x in v]
        x3h, x3l, vrep = zip(*each(stage_pairwise, q, k, v, b))
        w = each(stage_head_sums, x3h, x3l)
        wv = each(lambda w_, v_: (w_ * v_).astype(BF16), w, vrep)
        o = each(lambda wv_: jnp.dot(sum_j, wv_, preferred_element_type=F32), wv)
        q_all, k_all, v_all = zip(*each(stage_cross_operands, q, k, vb, b))
        att = each(stage_cross_scores, q_all, k_all)
        o = each(lambda o_, a_, v_: o_ + jnp.dot(a_, v_, preferred_element_type=F32), o, att, v_all)
        bq = each(lambda q_, b_: (q_ * jnp.exp(b_)).astype(BF16), q, b)
        decay, ut = zip(*each(stage_state_update, k, v, b))
        st = st_ref[...]
        states = []
        for c in range(gsize):
            states.append(st.astype(BF16))
            st = st * decay[c] + ut[c]
        st_ref[...] = st
        o = each(lambda o_, q_, s_: o_ + lax.dot_general(q_, s_, dn_t, preferred_element_type=F32),
                 o, bq, states)
        ms = each(stage_mean_square, o)
        each(stage_store, rows, o, ms)
        return carry

    lax.fori_loop(0, tc // ch // gsize, group, 0)


def _gla(proj, wg, bg, gn, batch, seq, tc):
    t = proj.shape[0]
    nb = seq // tc
    blk = lambda w, j: pl.BlockSpec((tc, w), lambda b, i, j=j: (b * nb + i, j))
    const = lambda r, c: pl.BlockSpec((r, c), lambda b, i: (0, 0))
    wg_pad = jnp.zeros((LANES, GLA_KW), F32).at[:GLA_GATE_RANK].set(wg).astype(BF16)
    gn_row = jnp.tile(gn, GLA_HEADS).reshape(1, GLA_VW)
    return pl.pallas_call(
        functools.partial(_gla_kernel, tc=tc),
        grid=(batch, nb),
        in_specs=[blk(GLA_KW, COL_G_Q), blk(GLA_KW, COL_G_K), blk(GLA_VW, COL_G_V), blk(GLA_VW, COL_G_R),
                  blk(LANES, COL_G_LR), const(LANES, GLA_KW), const(1, GLA_KW), const(1, GLA_VW)],
        out_specs=pl.BlockSpec((tc, GLA_VW), lambda b, i: (b * nb + i, 0)),
        out_shape=jax.ShapeDtypeStruct((t, GLA_VW), F32),
        scratch_shapes=[pltpu.VMEM((GLA_VW, GLA_KW), F32)],
        compiler_params=_cparams(("arbitrary", "arbitrary")),
    )(proj, proj, proj, proj, proj, wg_pad, bg.reshape(1, GLA_KW), gn_row)


def _out_proj_kernel(x_ref, a_ref, s_ref, g_ref, w_ref, o_ref):
    acc = jnp.dot(a_ref[...].astype(BF16), w_ref[0:DA_WIDTH, :], preferred_element_type=F32)
    acc += jnp.dot(s_ref[...].astype(BF16), w_ref[DA_WIDTH:DA_WIDTH + SC_WIDTH, :],
                   preferred_element_type=F32)
    acc += jnp.dot(g_ref[...].astype(BF16), w_ref[DA_WIDTH + SC_WIDTH:, :], preferred_element_type=F32)
    o_ref[...] = x_ref[...] + acc


def _out_proj(x, o_da, o_sc, o_g, w, tm):
    t, d = x.shape
    blk = lambda n: pl.BlockSpec((tm, n), lambda i: (i, 0))
    return pl.pallas_call(
        _out_proj_kernel,
        grid=(t // tm,),
        in_specs=[blk(d), blk(DA_WIDTH), blk(SC_WIDTH), blk(GLA_VW),
                  pl.BlockSpec(w.shape, lambda i: (0, 0))],
        out_specs=blk(d),
        out_shape=jax.ShapeDtypeStruct((t, d), F32),
        compiler_params=_cparams(("arbitrary",)),
    )(x, o_da, o_sc, o_g, w)


SUBLANES = 8


def _merge_exchange_network(n):
    pairs = []
    t = max(1, math.ceil(math.log2(n)))
    p = 2 ** (t - 1)
    while p > 0:
        q, r, d = 2 ** (t - 1), 0, p
        while d > 0:
            pairs += [(i, i + d) for i in range(n - d) if (i & p) == r]
            d, q, r = q - p, q // 2, p
        p //= 2
    return pairs


def _top_desc(x, n):
    groups = [x[SUBLANES * i:SUBLANES * (i + 1), :] for i in range(x.shape[0] // SUBLANES)]
    for i, j in _merge_exchange_network(len(groups)):
        groups[i], groups[j] = jnp.maximum(groups[i], groups[j]), jnp.minimum(groups[i], groups[j])
    vals = []
    for r in range(n):
        m = jnp.max(groups[0], axis=0, keepdims=True)
        vals.append(m)
        hit = groups[0] == m
        live = min(len(groups), n - r)
        for k in range(live - 1):
            groups[k] = jnp.where(hit, groups[k + 1], groups[k])
        if live == len(groups) and live > 0 and r + 1 < n:
            groups[live - 1] = jnp.where(hit, NEG_INF, groups[live - 1])
    return vals


def _ranks_of(x, vals):
    rank = jnp.full(x.shape, float(len(vals)), F32)
    for r in reversed(range(len(vals))):
        rank = jnp.where(x == vals[r], float(r), rank)
    return rank


def _peer_route_kernel(x_ref, g_ref, wq_ref, k1_ref, k2_ref, a1_ref, cnt_ref, a2_ref, rank2_ref):
    k = PEER_TOPK
    tt = x_ref.shape[0]
    xn = _rms(x_ref[...], g_ref[...]).astype(BF16)
    q = jnp.dot(xn, wq_ref[...], preferred_element_type=F32)
    dn = (((1,), (1,)), ((), ()))
    for h in range(PEER_HEADS):
        qa = q[:, (2 * h) * LANES:(2 * h + 1) * LANES].astype(BF16)
        qb = q[:, (2 * h + 1) * LANES:(2 * h + 2) * LANES].astype(BF16)
        s1 = lax.dot_general(k1_ref[...], qa, dn, preferred_element_type=F32)
        s2 = lax.dot_general(k2_ref[...], qb, dn, preferred_element_type=F32)
        v1 = _top_desc(s1, k)
        v2 = _top_desc(s2, k)
        rank2 = _ranks_of(s2, v2)
        cands = [v1[i] + v2[j] for i in range(k) for j in range(k // (i + 1))]
        pad = (-len(cands)) % 8
        cands += [jnp.full((1, tt), NEG_INF, F32)] * pad
        top = _top_desc(jnp.concatenate(cands, axis=0), k)
        z = jnp.ones((1, tt), F32)
        for r in range(1, k):
            z = z + jnp.exp(top[r] - top[0])
        thr = top[k - 1]
        v1s = jnp.concatenate(v1, axis=0)
        cnt_rank = jnp.zeros_like(v1s)
        for j in range(k):
            cnt_rank = cnt_rank + ((v1s + v2[j]) >= thr).astype(F32)
        cnt = jnp.zeros_like(s1)
        for i in range(k):
            cnt = jnp.where(s1 == v1[i], cnt_rank[i:i + 1, :], cnt)
        cnt_ref[h] = cnt
        rank2_ref[h] = rank2.astype(BF16)
        a1_ref[h] = jnp.exp(s1 - v1[0]) * (1.0 / z)
        a2_ref[h] = jnp.exp(s2 - v2[0]).astype(BF16)


def _peer_route(x, g, wq, k1, k2, tt):
    t, d = x.shape
    shape = (PEER_HEADS, PEER_NKEYS, t)
    out = pl.BlockSpec((PEER_HEADS, PEER_NKEYS, tt), lambda i: (0, 0, i))
    f32 = jax.ShapeDtypeStruct(shape, F32)
    bf16 = jax.ShapeDtypeStruct(shape, BF16)
    return pl.pallas_call(
        _peer_route_kernel,
        grid=(t // tt,),
        in_specs=[pl.BlockSpec((tt, d), lambda i: (i, 0)),
                  pl.BlockSpec((1, d), lambda i: (0, 0)),
                  pl.BlockSpec(wq.shape, lambda i: (0, 0)),
                  pl.BlockSpec(k1.shape, lambda i: (0, 0)),
                  pl.BlockSpec(k2.shape, lambda i: (0, 0))],
        out_specs=[out, out, out, out],
        out_shape=[f32, f32, bf16, bf16],
        compiler_params=_cparams(("arbitrary",)),
    )(x, g.reshape(1, d), wq, k1, k2)


def _peer_dense_kernel(x_ref, g_ref, u_ref, vt_ref, a1_ref, cnt_ref, a2_ref, rank2_ref, o_ref,
                       xnt_ref, acc_ref, hta_ref, htb_ref, pta_ref, ptb_ref):
    e = pl.program_id(1)

    @pl.when(e == 0)
    def _():
        xn = _rms(x_ref[...], g_ref[...])
        xnt_ref[...] = xn.T.astype(BF16)
        acc_ref[...] = jnp.zeros_like(acc_ref)

    inv_sqrt2 = 1.0 / math.sqrt(2.0)
    zero = jnp.zeros((), BF16)
    n_q, job_rows, _ = u_ref.shape
    sub = xnt_ref.shape[1] // 2
    n1_per_job = job_rows // PEER_NKEYS
    pack = BF16_SUBLANES
    groups = PEER_NKEYS // pack
    col_a, col_b = slice(0, sub), slice(sub, 2 * sub)
    ht_refs = {col_a: hta_ref, col_b: htb_ref}
    pt_refs = {col_a: pta_ref, col_b: ptb_ref}

    def matmul1(q, cols):
        ht_refs[cols][...] = jnp.dot(u_ref[q], xnt_ref[:, cols], preferred_element_type=F32)

    def gate(q, cols):
        ht_ref, pt_ref = ht_refs[cols], pt_refs[cols]
        for rl in range(n1_per_job):
            r = q * n1_per_job + rl
            w = None
            for h in range(PEER_HEADS):
                cnt = jnp.broadcast_to(cnt_ref[h, pl.ds(r, 1), cols], (pack, sub)).astype(BF16)
                a1 = jnp.broadcast_to(a1_ref[h, pl.ds(r, 1), cols], (pack, sub)).astype(BF16)
                sel = rank2_ref[h, :, :, cols] < cnt[None]
                term = a1[None] * jnp.where(sel, a2_ref[h, :, :, cols], zero)
                w = term if w is None else w + term
            rows = slice(rl * PEER_NKEYS, (rl + 1) * PEER_NKEYS)
            hb = ht_ref[rows, :]
            gelu = 0.5 * hb * (1.0 + lax.erf(hb * inv_sqrt2))
            p = w * gelu.astype(BF16).reshape(groups, pack, sub)
            pt_ref[rows, :] = p.reshape(PEER_NKEYS, sub)

    def matmul2(q, cols):
        acc_ref[:, cols] += jnp.dot(vt_ref[q], pt_refs[cols][...], preferred_element_type=F32)

    def pair(q, has_prev, has_next):
        matmul1(q, col_b)
        gate(q, col_a)
        if has_prev:
            matmul2(q - 1, col_b)
        if has_next:
            matmul1(q + 1, col_a)
        gate(q, col_b)
        matmul2(q, col_a)

    matmul1(0, col_a)
    if n_q == 1:
        pair(0, False, False)
    else:
        pair(0, False, True)

        def body(q, carry):
            pair(q, True, True)
            return carry

        lax.fori_loop(1, n_q - 1, body, 0)
        pair(n_q - 1, True, False)
    matmul2(n_q - 1, col_b)

    @pl.when(e == pl.num_programs(1) - 1)
    def _():
        o_ref[...] = x_ref[...] + acc_ref[...].T


PEER_JOB_ROWS = 512


def _expert_jobs(u, v):
    n_exp, d = u.shape
    n_jobs = n_exp // PEER_JOB_ROWS
    u3 = u.astype(BF16).reshape(n_jobs, PEER_JOB_ROWS, d)
    vt3 = jnp.transpose(v.astype(BF16).reshape(n_jobs, PEER_JOB_ROWS, d), (0, 2, 1))
    return u3, vt3


def _peer_dense(x, g, u, vt, a1t, cnt, a2t, rank2, tt, et):
    t, d = x.shape
    n_jobs, job_rows, _ = u.shape
    n_q = et // job_rows
    n1_per_tile = et // PEER_NKEYS
    groups = PEER_NKEYS // BF16_SUBLANES
    grouped = (PEER_HEADS, groups, BF16_SUBLANES, t)
    full = pl.BlockSpec((PEER_HEADS, groups, BF16_SUBLANES, tt), lambda i, e: (0, 0, 0, i))
    part = pl.BlockSpec((PEER_HEADS, n1_per_tile, tt), lambda i, e: (0, e, i))
    sub = tt // 2
    return pl.pallas_call(
        _peer_dense_kernel,
        grid=(t // tt, n_jobs // n_q),
        in_specs=[pl.BlockSpec((tt, d), lambda i, e: (i, 0)),
                  pl.BlockSpec((1, d), lambda i, e: (0, 0)),
                  pl.BlockSpec((n_q, job_rows, d), lambda i, e: (e, 0, 0)),
                  pl.BlockSpec((n_q, d, job_rows), lambda i, e: (e, 0, 0)),
                  part, part, full, full],
        out_specs=pl.BlockSpec((tt, d), lambda i, e: (i, 0)),
        out_shape=jax.ShapeDtypeStruct((t, d), F32),
        scratch_shapes=[pltpu.VMEM((d, tt), BF16), pltpu.VMEM((d, tt), F32),
                        pltpu.VMEM((job_rows, sub), F32), pltpu.VMEM((job_rows, sub), F32),
                        pltpu.VMEM((job_rows, sub), BF16), pltpu.VMEM((job_rows, sub), BF16)],
        compiler_params=_cparams(("arbitrary", "arbitrary"), vmem_mb=52),
    )(x, g.reshape(1, d), u, vt, a1t, cnt, a2t.reshape(grouped), rank2.reshape(grouped))


def _rmsnorm_kernel(x_ref, g_ref, o_ref):
    o_ref[...] = _rms(x_ref[...], g_ref[...])


def _rmsnorm(x, g, tm):
    t, d = x.shape
    return pl.pallas_call(
        _rmsnorm_kernel,
        grid=(t // tm,),
        in_specs=[pl.BlockSpec((tm, d), lambda i: (i, 0)), pl.BlockSpec((1, d), lambda i: (0, 0))],
        out_specs=pl.BlockSpec((tm, d), lambda i: (i, 0)),
        out_shape=jax.ShapeDtypeStruct((t, d), F32),
        compiler_params=_cparams(("arbitrary",)),
    )(x, g.reshape(1, d))


def _tiles(seq):
    pick = lambda want: min(want, seq)
    return dict(tm=pick(256), tq=pick(512), tconv=pick(512), tgla=pick(512), tt=pick(256), td=pick(512),
                et=2048)


def kernel(x, positions, norm_mix, w_in, lam_q1, lam_k1, lam_q2, lam_k2, diff_norm, conv_w, gla_w_gate2, gla_b_gate, gla_norm, w_out, norm_ffn, peer_w_q, peer_keys1, peer_keys2, peer_u, peer_v, norm_final):
    batch, seq, d = x.shape
    t = batch * seq
    depth = w_in.shape[0]
    tl = _tiles(seq)
    xt = x.reshape(t, d)
    tabs = _rope_tables(positions, tl["tm"])
    for i in range(depth):
        w_in_p = jnp.pad(w_in[i], ((0, 0), (0, IN_PAD - IN_WIDTH))).astype(BF16)
        proj = _rms_matmul(xt, norm_mix[i], w_in_p, tl["tm"])
        qt, kr, vt3 = _rope_prep(proj, tabs, tl["tq"])
        lam_init = 0.8 - 0.6 * math.exp(-0.3 * i)
        o_da = _diff_attention(qt, kr, vt3, lam_q1[i], lam_k1[i], lam_q2[i], lam_k2[i], diff_norm[i],
                               batch, seq, lam_init, tl["tq"])
        o_sc = _short_conv(proj, conv_w[i], seq, tl["tconv"])
        o_g = _gla(proj, gla_w_gate2[i], gla_b_gate[i], gla_norm[i], batch, seq, tl["tgla"])
        xt = _out_proj(xt, o_da, o_sc, o_g, w_out[i].astype(BF16), tl["tm"])
        a1t, cnt, a2t, rank2 = _peer_route(xt, norm_ffn[i], peer_w_q[i].astype(BF16),
                                           peer_keys1[i].astype(BF16), peer_keys2[i].astype(BF16), tl["tt"])
        u3, vt3 = _expert_jobs(peer_u[i], peer_v[i])
        xt = _peer_dense(xt, norm_ffn[i], u3, vt3, a1t, cnt, a2t, rank2, tl["td"], tl["et"])
    return _rmsnorm(xt, norm_final, tl["tm"]).reshape(batch, seq, d)
```

```python
import functools
import math

import jax
import jax.numpy as jnp
from jax import lax
from jax.experimental import pallas as pl
from jax.experimental.pallas import tpu as pltpu

F32 = jnp.float32
BF16 = jnp.bfloat16

D_MODEL = 1024
EPS = 1e-6
HEAD_DIM = 64
DA_HEADS = 4
DA_WIDTH = 512
ROPE_THETA = 500000.0
ROT_DIM = 16
SC_WIDTH = 256
CONV_W = 3
GLA_HEADS = 4
GLA_DK = 32
GLA_DV = 64
GLA_KW = 128
GLA_VW = 256
GLA_GATE_RANK = 16
GLA_GATE_TEMP = 16.0
GLA_CHUNK = 64
GLA_SUB = 16
GLA_GROUP = 8
IN_WIDTH = 3088
IN_PAD = 3200
PEER_HEADS = 8
PEER_NKEYS = 128
PEER_EXPERTS = PEER_NKEYS * PEER_NKEYS
PEER_TOPK = 16
LANES = 128
BF16_SUBLANES = 16
NEG_INF = float("-inf")

COL_DA_Q, COL_DA_K, COL_DA_V = 0, 1, 2
COL_SC_B, COL_SC_C, COL_SC_H = 6, 7, 8
COL_G_V, COL_G_R = 10, 11
COL_G_Q, COL_G_K, COL_G_LR = 18, 19, 24


def _cparams(sem, vmem_mb=40):
    return pltpu.CompilerParams(dimension_semantics=sem, vmem_limit_bytes=vmem_mb * 1024 * 1024)


def _rms(x, g):
    return x * lax.rsqrt(jnp.mean(x * x, axis=-1, keepdims=True) + EPS) * g


def _rms_matmul_kernel(x_ref, g_ref, w_ref, o_ref):
    xn = _rms(x_ref[...], g_ref[...]).astype(BF16)
    o_ref[...] = jnp.dot(xn, w_ref[...], preferred_element_type=F32)


def _rms_matmul(x, g, w, tm):
    t, d = x.shape
    n = w.shape[1]
    return pl.pallas_call(
        _rms_matmul_kernel,
        grid=(t // tm,),
        in_specs=[pl.BlockSpec((tm, d), lambda i: (i, 0)),
                  pl.BlockSpec((1, d), lambda i: (0, 0)),
                  pl.BlockSpec((d, n), lambda i: (0, 0))],
        out_specs=pl.BlockSpec((tm, n), lambda i: (i, 0)),
        out_shape=jax.ShapeDtypeStruct((t, n), F32),
        compiler_params=_cparams(("arbitrary",)),
    )(x, g.reshape(1, d), w)


def _rope_table_kernel(pos_ref, inv_ref, mrot_ref, m1_ref, m2_ref, c_ref, s1_ref, s2_ref):
    ang = pos_ref[...] * inv_ref[...]
    c = jnp.cos(ang)
    s = jnp.sin(ang)
    mrot = mrot_ref[...]
    c_ref[...] = mrot * c + (1.0 - mrot)
    s1_ref[...] = -(m1_ref[...] * s)
    s2_ref[...] = m2_ref[...] * s


def _rope_tables(positions, tm):
    t = positions.size
    pos = jnp.broadcast_to(positions.reshape(t, 1).astype(F32), (t, LANES))
    d = jnp.arange(LANES) % HEAD_DIM
    half = ROT_DIM // 2
    inv = ROPE_THETA ** (-jnp.arange(0, ROT_DIM, 2, dtype=F32) / ROT_DIM)
    inv_row = jnp.where(d < ROT_DIM, inv[d % half], 0.0).astype(F32).reshape(1, LANES)
    mrot = (d < ROT_DIM).astype(F32).reshape(1, LANES)
    m1 = (d < half).astype(F32).reshape(1, LANES)
    m2 = ((d >= half) & (d < ROT_DIM)).astype(F32).reshape(1, LANES)
    row = pl.BlockSpec((1, LANES), lambda i: (0, 0))
    blk = pl.BlockSpec((tm, LANES), lambda i: (i, 0))
    return pl.pallas_call(
        _rope_table_kernel,
        grid=(t // tm,),
        in_specs=[blk, row, row, row, row],
        out_specs=[blk, blk, blk],
        out_shape=[jax.ShapeDtypeStruct((t, LANES), F32)] * 3,
        compiler_params=_cparams(("arbitrary",)),
    )(pos, inv_row, mrot, m1, m2)


def _rope_prep_kernel(q_ref, k_ref, v_ref, c_ref, s1_ref, s2_ref, qt_ref, ko_ref, vt_ref, *, qscale):
    c = c_ref[...]
    s1 = s1_ref[...]
    s2 = s2_ref[...]
    half = ROT_DIM // 2

    def rope(x):
        return x * c + pltpu.roll(x, LANES - half, 1) * s1 + pltpu.roll(x, half, 1) * s2

    for g in range(DA_WIDTH // LANES):
        sl = slice(LANES * g, LANES * (g + 1))
        qt_ref[sl, :] = (rope(q_ref[:, sl]) * qscale).T.astype(BF16)
        ko_ref[:, sl] = rope(k_ref[:, sl]).astype(BF16)
        vt_ref[0, sl, :] = v_ref[:, sl].T.astype(BF16)


def _rope_prep(proj, tabs, tm):
    t = proj.shape[0]
    c, s1, s2 = tabs
    col = lambda j: pl.BlockSpec((tm, DA_WIDTH), lambda i, j=j: (i, j))
    tab = pl.BlockSpec((tm, LANES), lambda i: (i, 0))
    return pl.pallas_call(
        functools.partial(_rope_prep_kernel, qscale=HEAD_DIM ** -0.5 * math.log2(math.e)),
        grid=(t // tm,),
        in_specs=[col(COL_DA_Q), col(COL_DA_K), col(COL_DA_V), tab, tab, tab],
        out_specs=[pl.BlockSpec((DA_WIDTH, tm), lambda i: (0, i)),
                   pl.BlockSpec((tm, DA_WIDTH), lambda i: (i, 0)),
                   pl.BlockSpec((1, DA_WIDTH, tm), lambda i: (i, 0, 0))],
        out_shape=[jax.ShapeDtypeStruct((DA_WIDTH, t), BF16),
                   jax.ShapeDtypeStruct((t, DA_WIDTH), BF16),
                   jax.ShapeDtypeStruct((t // tm, DA_WIDTH, tm), BF16)],
        compiler_params=_cparams(("arbitrary",)),
    )(proj, proj, proj, c, s1, s2)


def _attn_kernel(qt_ref, k_ref, vt_ref, lq1_ref, lk1_ref, lq2_ref, lk2_ref, gain_ref, o_ref,
                 acc1_ref, acc2_ref, m1_ref, l1_ref, m2_ref, l2_ref, s1a_ref, s2a_ref, s1b_ref, s2b_ref,
                 *, tq, lam_init):
    qi = pl.program_id(2)
    qt = qt_ref[...]
    row = lax.broadcasted_iota(jnp.int32, (LANES, 1), 0)
    zero = jnp.zeros_like(qt)
    qa = jnp.where(row < HEAD_DIM, qt, zero)
    qb = jnp.where(row >= HEAD_DIM, qt, zero)
    acc1_ref[...] = jnp.zeros_like(acc1_ref)
    acc2_ref[...] = jnp.zeros_like(acc2_ref)

    for st_ref in (m1_ref, m2_ref):
        st_ref[...] = jnp.full(st_ref.shape, NEG_INF, F32)
    for st_ref in (l1_ref, l2_ref):
        st_ref[...] = jnp.zeros_like(st_ref)
    maps = ((qa, m1_ref, l1_ref, acc1_ref), (qb, m2_ref, l2_ref, acc2_ref))

    def scores(j, bufs):
        off = pl.multiple_of(j * tq, tq)
        kk = k_ref[pl.ds(off, tq), :]
        for (qm, _, _, _), s_ref in zip(maps, bufs):
            s_ref[...] = jnp.dot(kk, qm, preferred_element_type=F32)

    def absorb(j, bufs, masked):
        vt = vt_ref[j]
        for (_, m_ref, l_ref, acc_ref), s_ref in zip(maps, bufs):
            s = s_ref[...]
            if masked:
                r = lax.broadcasted_iota(jnp.int32, (tq, tq), 0)
                c = lax.broadcasted_iota(jnp.int32, (tq, tq), 1)
                s = jnp.where(r <= c, s, NEG_INF)
            m = m_ref[...]
            mn = jnp.maximum(m, jnp.max(s, axis=0, keepdims=True))
            alpha = jnp.exp2(m - mn)
            p = jnp.exp2(s - mn)
            m_ref[...] = mn
            l_ref[...] = alpha * l_ref[...] + jnp.sum(p, axis=0, keepdims=True)
            acc_ref[...] = alpha * acc_ref[...] + jnp.dot(vt, p.astype(BF16),
                                                          preferred_element_type=F32)

    buf_a, buf_b = (s1a_ref, s2a_ref), (s1b_ref, s2b_ref)
    scores(0, buf_a)

    def pair(i, carry):
        j = 2 * i
        scores(j + 1, buf_b)
        absorb(j, buf_a, False)
        scores(j + 2, buf_a)
        absorb(j + 1, buf_b, False)
        return carry

    lax.fori_loop(0, qi // 2, pair, 0)

    @pl.when(qi % 2 == 1)
    def _():
        scores(qi, buf_b)
        absorb(qi - 1, buf_a, False)
        absorb(qi, buf_b, True)

    @pl.when(qi % 2 == 0)
    def _():
        absorb(qi, buf_a, True)

    l1, l2 = l1_ref[...], l2_ref[...]

    lam = (jnp.exp(jnp.sum(lq1_ref[...] * lk1_ref[...], axis=-1, keepdims=True))
           - jnp.exp(jnp.sum(lq2_ref[...] * lk2_ref[...], axis=-1, keepdims=True)) + lam_init)
    o = acc1_ref[...] / l1 - lam * (acc2_ref[...] / l2)
    y = o * lax.rsqrt(jnp.mean(o * o, axis=0, keepdims=True) + EPS) * gain_ref[...]
    o_ref[...] = (y * (1.0 - lam_init)).T


def _diff_attention(qt, kr, vt3, lq1, lk1, lq2, lk2, gain, batch, seq, lam_init, tq):
    t = kr.shape[0]
    nq = seq // tq
    vec = lambda n: pl.BlockSpec((1, n), lambda b, h, i: (0, 0))
    return pl.pallas_call(
        functools.partial(_attn_kernel, tq=tq, lam_init=lam_init),
        grid=(batch, DA_HEADS, nq),
        in_specs=[pl.BlockSpec((LANES, tq), lambda b, h, i: (h, b * nq + i)),
                  pl.BlockSpec((seq, LANES), lambda b, h, i: (b, h)),
                  pl.BlockSpec((nq, LANES, tq), lambda b, h, i: (b, h, 0)),
                  vec(HEAD_DIM), vec(HEAD_DIM), vec(HEAD_DIM), vec(HEAD_DIM),
                  pl.BlockSpec((LANES, 1), lambda b, h, i: (0, 0))],
        out_specs=pl.BlockSpec((tq, LANES), lambda b, h, i: (b * nq + i, h)),
        out_shape=jax.ShapeDtypeStruct((t, DA_WIDTH), F32),
        scratch_shapes=([pltpu.VMEM((LANES, tq), F32)] * 2 + [pltpu.VMEM((1, tq), F32)] * 4
                        + [pltpu.VMEM((tq, tq), F32)] * 4),
        compiler_params=_cparams(("arbitrary", "arbitrary", "arbitrary")),
    )(qt, kr, vt3, lq1.reshape(1, -1), lk1.reshape(1, -1), lq2.reshape(1, -1), lk2.reshape(1, -1),
      gain.reshape(-1, 1))


def _short_conv_kernel(b_ref, c_ref, h_ref, cp_ref, hp_ref, w_ref, o_ref, *, tm, blocks_per_seq):
    i = pl.program_id(0)
    z = c_ref[...] * h_ref[...]
    zp = cp_ref[...] * hp_ref[...]
    first = (i % blocks_per_seq) == 0
    zp = jnp.where(first, jnp.zeros_like(zp), zp)
    row = lax.broadcasted_iota(jnp.int32, (tm, 1), 0)
    z1 = jnp.where(row == 0, zp[7:8, :], pltpu.roll(z, 1, 0))
    z2 = pltpu.roll(z, 2, 0)
    z2 = jnp.where(row == 0, zp[6:7, :], jnp.where(row == 1, zp[7:8, :], z2))
    w = w_ref[...]
    y = w[0:1, :] * z2 + w[1:2, :] * z1 + w[2:3, :] * z
    o_ref[...] = b_ref[...] * y


def _short_conv(proj, w, seq, tm):
    t = proj.shape[0]
    rb = tm // 8
    cur = lambda j: pl.BlockSpec((tm, SC_WIDTH), lambda i, j=j: (i, j))
    prev = lambda j: pl.BlockSpec((8, SC_WIDTH), lambda i, j=j: (jnp.maximum(i * rb - 1, 0), j))
    return pl.pallas_call(
        functools.partial(_short_conv_kernel, tm=tm, blocks_per_seq=seq // tm),
        grid=(t // tm,),
        in_specs=[cur(COL_SC_B), cur(COL_SC_C), cur(COL_SC_H), prev(COL_SC_C), prev(COL_SC_H),
                  pl.BlockSpec((CONV_W, SC_WIDTH), lambda i: (0, 0))],
        out_specs=pl.BlockSpec((tm, SC_WIDTH), lambda i: (i, 0)),
        out_shape=jax.ShapeDtypeStruct((t, SC_WIDTH), F32),
        compiler_params=_cparams(("arbitrary",)),
    )(proj, proj, proj, proj, proj, w)


def _split3(x):
    hi = x.astype(BF16)
    r = x - hi.astype(F32)
    mid = r.astype(BF16)
    lo = (r - mid.astype(F32)).astype(BF16)
    return hi, mid, lo


def _gla_kernel(q_ref, k_ref, v_ref, r_ref, lr_ref, wg_ref, bg_ref, gn_ref, o_ref, st_ref, *, tc):
    ch = GLA_CHUNK

    @pl.when(pl.program_id(1) == 0)
    def _():
        st_ref[...] = jnp.zeros_like(st_ref)

    ii = lax.broadcasted_iota(jnp.int32, (ch, ch), 0)
    jj = lax.broadcasted_iota(jnp.int32, (ch, ch), 1)
    tri = (ii >= jj).astype(BF16)
    rk = lax.broadcasted_iota(jnp.int32, (GLA_VW, GLA_KW), 0)
    ck = lax.broadcasted_iota(jnp.int32, (GLA_VW, GLA_KW), 1)
    bd_k = (rk // GLA_DV) == (ck // GLA_DK)
    rv = lax.broadcasted_iota(jnp.int32, (GLA_VW, GLA_VW), 0)
    cv = lax.broadcasted_iota(jnp.int32, (GLA_VW, GLA_VW), 1)
    bd_v = (rv // GLA_DV) == (cv // GLA_DV)
    seg = bd_v.astype(BF16)
    sb = GLA_SUB
    nsb = ch // sb
    hs = GLA_HEADS * sb
    iota = lambda shape, axis: lax.broadcasted_iota(jnp.int32, shape, axis)
    sub_row = iota((sb, 1), 0)
    head_expand = ((iota((GLA_KW, GLA_VW), 0) // GLA_DK)
                   == (iota((GLA_KW, GLA_VW), 1) // GLA_DV)).astype(BF16)
    sum_j = ((iota((ch, ch * sb), 1) // sb) == iota((ch, ch * sb), 0)).astype(BF16)
    bd_ks = (iota((hs, GLA_KW), 0) // sb) == (iota((hs, GLA_KW), 1) // GLA_DK)
    bd_vs = (iota((hs, GLA_VW), 0) // sb) == (iota((hs, GLA_VW), 1) // GLA_DV)
    off_w = (nsb - 1) * hs
    att_keep = iota((ch, off_w), 0) >= (iota((ch, off_w), 1) // hs + 1) * sb
    dn_t = (((1,), (1,)), ((), ()))
    scale = GLA_DK ** -0.5

    def stage_decay(rows):
        x = jnp.dot(lr_ref[rows, :].astype(BF16), wg_ref[...], preferred_element_type=F32) + bg_ref[...]
        la = (jnp.minimum(x, 0.0) - jnp.log1p(jnp.exp(-jnp.abs(x)))) * (1.0 / GLA_GATE_TEMP)
        hi, mid, lo = _split3(la)
        return (jnp.dot(tri, hi, preferred_element_type=F32) + jnp.dot(tri, mid, preferred_element_type=F32)
                + jnp.dot(tri, lo, preferred_element_type=F32))

    def stage_pairwise(q, k, v, b):
        pieces, vrep = [], []
        for sbi in range(nsb):
            rs = slice(sbi * sb, (sbi + 1) * sb)
            qs, ks, bs = q[rs, :], k[rs, :], b[rs, :]
            for i in range(sb):
                dec = jnp.exp(jnp.where(sub_row <= i, bs[i:i + 1, :] - bs, NEG_INF))
                pieces.append(qs[i:i + 1, :] * ks * dec)
            vrep += [v[rs, :]] * sb
        x3 = jnp.concatenate(pieces, axis=0)
        x3h = x3.astype(BF16)
        return x3h, (x3 - x3h.astype(F32)).astype(BF16), jnp.concatenate(vrep, axis=0)

    def stage_head_sums(x3h, x3l):
        return (jnp.dot(x3h, head_expand, preferred_element_type=F32)
                + jnp.dot(x3l, head_expand, preferred_element_type=F32))

    def stage_cross_operands(q, k, vb, b):
        qts, kbds, vbds = [], [], []
        zeros_k = jnp.zeros((hs, GLA_KW), BF16)
        for sbi in range(nsb - 1):
            rs = slice(sbi * sb, (sbi + 1) * sb)
            end = (sbi + 1) * sb
            b_end = b[end - 1:end, :]
            qts.append((q * jnp.exp(jnp.minimum(b - b_end, 0.0))).astype(BF16))
            kt = (k[rs, :] * jnp.exp(b_end - b[rs, :])).astype(BF16)
            kbd = jnp.where(bd_ks, jnp.concatenate([kt] * GLA_HEADS, axis=0), jnp.zeros((), BF16))
            kbds.append(jnp.concatenate([kbd if s2 == sbi else zeros_k for s2 in range(nsb - 1)], axis=1))
            vbds.append(jnp.where(bd_vs, jnp.concatenate([vb[rs, :]] * GLA_HEADS, axis=0),
                                  jnp.zeros((), BF16)))
        return jnp.concatenate(qts, axis=1), jnp.concatenate(kbds, axis=0), jnp.concatenate(vbds, axis=0)

    def stage_cross_scores(q_all, k_all):
        att = lax.dot_general(q_all, k_all, dn_t, preferred_element_type=F32)
        return jnp.where(att_keep, att, 0.0).astype(BF16)

    def stage_state_update(k, v, b):
        b_last = b[ch - 1:ch, :]
        kl = (k * jnp.exp(b_last - b)).astype(BF16)
        ut = jnp.dot(v.T.astype(BF16), kl, preferred_element_type=F32)
        return jnp.exp(b_last), jnp.where(bd_k, ut, 0.0)

    def stage_mean_square(o):
        o2 = o * o
        o2h = o2.astype(BF16)
        o2l = (o2 - o2h.astype(F32)).astype(BF16)
        return (jnp.dot(o2h, seg, preferred_element_type=F32)
                + jnp.dot(o2l, seg, preferred_element_type=F32)) * (1.0 / GLA_DV)

    def stage_store(rows, o, ms):
        r = r_ref[rows, :]
        gate = r * (1.0 / (1.0 + jnp.exp(-r)))
        o_ref[rows, :] = o * lax.rsqrt(ms + EPS) * gn_ref[...] * gate

    each = lambda f, *cols: [f(*args) for args in zip(*cols)]
    gsize = min(GLA_GROUP, tc // ch)

    def group(g, carry):
        rows = [pl.ds(pl.multiple_of((g * gsize + c) * ch, ch), ch) for c in range(gsize)]
        b = each(stage_decay, rows)
        q = [q_ref[r, :] * scale for r in rows]
        k = [k_ref[r, :] for r in rows]
        v = [v_ref[r, :] for r in rows]
        vb = [x.astype(BF16) for x in v]
        x3h, x3l, vrep = zip(*each(stage_pairwise, q, k, v, b))
        w = each(stage_head_sums, x3h, x3l)
        wv = each(lambda w_, v_: (w_ * v_).astype(BF16), w, vrep)
        o = each(lambda wv_: jnp.dot(sum_j, wv_, preferred_element_type=F32), wv)
        q_all, k_all, v_all = zip(*each(stage_cross_operands, q, k, vb, b))
        att = each(stage_cross_scores, q_all, k_all)
        o = each(lambda o_, a_, v_: o_ + jnp.dot(a_, v_, preferred_element_type=F32), o, att, v_all)
        bq = each(lambda q_, b_: (q_ * jnp.exp(b_)).astype(BF16), q, b)
        decay, ut = zip(*each(stage_state_update, k, v, b))
        st = st_ref[...]
        states = []
        for c in range(gsize):
            states.append(st.astype(BF16))
            st = st * decay[c] + ut[c]
        st_ref[...] = st
        o = each(lambda o_, q_, s_: o_ + lax.dot_general(q_, s_, dn_t, preferred_element_type=F32),
                 o, bq, states)
        ms = each(stage_mean_square, o)
        each(stage_store, rows, o, ms)
        return carry

    lax.fori_loop(0, tc // ch // gsize, group, 0)


def _gla(proj, wg, bg, gn, batch, seq, tc):
    t = proj.shape[0]
    nb = seq // tc
    blk = lambda w, j: pl.BlockSpec((tc, w), lambda b, i, j=j: (b * nb + i, j))
    const = lambda r, c: pl.BlockSpec((r, c), lambda b, i: (0, 0))
    wg_pad = jnp.zeros((LANES, GLA_KW), F32).at[:GLA_GATE_RANK].set(wg).astype(BF16)
    gn_row = jnp.tile(gn, GLA_HEADS).reshape(1, GLA_VW)
    return pl.pallas_call(
        functools.partial(_gla_kernel, tc=tc),
        grid=(batch, nb),
        in_specs=[blk(GLA_KW, COL_G_Q), blk(GLA_KW, COL_G_K), blk(GLA_VW, COL_G_V), blk(GLA_VW, COL_G_R),
                  blk(LANES, COL_G_LR), const(LANES, GLA_KW), const(1, GLA_KW), const(1, GLA_VW)],
        out_specs=pl.BlockSpec((tc, GLA_VW), lambda b, i: (b * nb + i, 0)),
        out_shape=jax.ShapeDtypeStruct((t, GLA_VW), F32),
        scratch_shapes=[pltpu.VMEM((GLA_VW, GLA_KW), F32)],
        compiler_params=_cparams(("arbitrary", "arbitrary")),
    )(proj, proj, proj, proj, proj, wg_pad, bg.reshape(1, GLA_KW), gn_row)


def _out_proj_kernel(x_ref, a_ref, s_ref, g_ref, w_ref, o_ref):
    acc = jnp.dot(a_ref[...].astype(BF16), w_ref[0:DA_WIDTH, :], preferred_element_type=F32)
    acc += jnp.dot(s_ref[...].astype(BF16), w_ref[DA_WIDTH:DA_WIDTH + SC_WIDTH, :],
                   preferred_element_type=F32)
    acc += jnp.dot(g_ref[...].astype(BF16), w_ref[DA_WIDTH + SC_WIDTH:, :], preferred_element_type=F32)
    o_ref[...] = x_ref[...] + acc


def _out_proj(x, o_da, o_sc, o_g, w, tm):
    t, d = x.shape
    blk = lambda n: pl.BlockSpec((tm, n), lambda i: (i, 0))
    return pl.pallas_call(
        _out_proj_kernel,
        grid=(t // tm,),
        in_specs=[blk(d), blk(DA_WIDTH), blk(SC_WIDTH), blk(GLA_VW),
                  pl.BlockSpec(w.shape, lambda i: (0, 0))],
        out_specs=blk(d),
        out_shape=jax.ShapeDtypeStruct((t, d), F32),
        compiler_params=_cparams(("arbitrary",)),
    )(x, o_da, o_sc, o_g, w)


SUBLANES = 8


def _merge_exchange_network(n):
    pairs = []
    t = max(1, math.ceil(math.log2(n)))
    p = 2 ** (t - 1)
    while p > 0:
        q, r, d = 2 ** (t - 1), 0, p
        while d > 0:
            pairs += [(i, i + d) for i in range(n - d) if (i & p) == r]
            d, q, r = q - p, q // 2, p
        p //= 2
    return pairs


def _top_desc(x, n):
    groups = [x[SUBLANES * i:SUBLANES * (i + 1), :] for i in range(x.shape[0] // SUBLANES)]
    for i, j in _merge_exchange_network(len(groups)):
        groups[i], groups[j] = jnp.maximum(groups[i], groups[j]), jnp.minimum(groups[i], groups[j])
    vals = []
    for r in range(n):
        m = jnp.max(groups[0], axis=0, keepdims=True)
        vals.append(m)
        hit = groups[0] == m
        live = min(len(groups), n - r)
        for k in range(live - 1):
            groups[k] = jnp.where(hit, groups[k + 1], groups[k])
        if live == len(groups) and live > 0 and r + 1 < n:
            groups[live - 1] = jnp.where(hit, NEG_INF, groups[live - 1])
    return vals


def _ranks_of(x, vals):
    rank = jnp.full(x.shape, float(len(vals)), F32)
    for r in reversed(range(len(vals))):
        rank = jnp.where(x == vals[r], float(r), rank)
    return rank


def _peer_route_kernel(x_ref, g_ref, wq_ref, k1_ref, k2_ref, a1_ref, cnt_ref, a2_ref, rank2_ref):
    k = PEER_TOPK
    tt = x_ref.shape[0]
    xn = _rms(x_ref[...], g_ref[...]).astype(BF16)
    q = jnp.dot(xn, wq_ref[...], preferred_element_type=F32)
    dn = (((1,), (1,)), ((), ()))
    for h in range(PEER_HEADS):
        qa = q[:, (2 * h) * LANES:(2 * h + 1) * LANES].astype(BF16)
        qb = q[:, (2 * h + 1) * LANES:(2 * h + 2) * LANES].astype(BF16)
        s1 = lax.dot_general(k1_ref[...], qa, dn, preferred_element_type=F32)
        s2 = lax.dot_general(k2_ref[...], qb, dn, preferred_element_type=F32)
        v1 = _top_desc(s1, k)
        v2 = _top_desc(s2, k)
        rank2 = _ranks_of(s2, v2)
        cands = [v1[i] + v2[j] for i in range(k) for j in range(k // (i + 1))]
        pad = (-len(cands)) % 8
        cands += [jnp.full((1, tt), NEG_INF, F32)] * pad
        top = _top_desc(jnp.concatenate(cands, axis=0), k)
        z = jnp.ones((1, tt), F32)
        for r in range(1, k):
            z = z + jnp.exp(top[r] - top[0])
        thr = top[k - 1]
        v1s = jnp.concatenate(v1, axis=0)
        cnt_rank = jnp.zeros_like(v1s)
        for j in range(k):
            cnt_rank = cnt_rank + ((v1s + v2[j]) >= thr).astype(F32)
        cnt = jnp.zeros_like(s1)
        for i in range(k):
            cnt = jnp.where(s1 == v1[i], cnt_rank[i:i + 1, :], cnt)
        cnt_ref[h] = cnt
        rank2_ref[h] = rank2.astype(BF16)
        a1_ref[h] = jnp.exp(s1 - v1[0]) * (1.0 / z)
        a2_ref[h] = jnp.exp(s2 - v2[0]).astype(BF16)


def _peer_route(x, g, wq, k1, k2, tt):
    t, d = x.shape
    shape = (PEER_HEADS, PEER_NKEYS, t)
    out = pl.BlockSpec((PEER_HEADS, PEER_NKEYS, tt), lambda i: (0, 0, i))
    f32 = jax.ShapeDtypeStruct(shape, F32)
    bf16 = jax.ShapeDtypeStruct(shape, BF16)
    return pl.pallas_call(
        _peer_route_kernel,
        grid=(t // tt,),
        in_specs=[pl.BlockSpec((tt, d), lambda i: (i, 0)),
                  pl.BlockSpec((1, d), lambda i: (0, 0)),
                  pl.BlockSpec(wq.shape, lambda i: (0, 0)),
                  pl.BlockSpec(k1.shape, lambda i: (0, 0)),
                  pl.BlockSpec(k2.shape, lambda i: (0, 0))],
        out_specs=[out, out, out, out],
        out_shape=[f32, f32, bf16, bf16],
        compiler_params=_cparams(("arbitrary",)),
    )(x, g.reshape(1, d), wq, k1, k2)


def _peer_dense_kernel(x_ref, g_ref, u_ref, vt_ref, a1_ref, cnt_ref, a2_ref, rank2_ref, gout_ref, o_ref,
                       xnt_ref, acc_ref, hta_ref, htb_ref, pta_ref, ptb_ref, *, norm_out):
    e = pl.program_id(1)

    @pl.when(e == 0)
    def _():
        xn = _rms(x_ref[...], g_ref[...])
        xnt_ref[...] = xn.T.astype(BF16)
        acc_ref[...] = jnp.zeros_like(acc_ref)

    inv_sqrt2 = 1.0 / math.sqrt(2.0)
    zero = jnp.zeros((), BF16)
    n_q, job_rows, _ = u_ref.shape
    sub = xnt_ref.shape[1] // 2
    n1_per_job = job_rows // PEER_NKEYS
    pack = BF16_SUBLANES
    groups = PEER_NKEYS // pack
    col_a, col_b = slice(0, sub), slice(sub, 2 * sub)
    ht_refs = {col_a: hta_ref, col_b: htb_ref}
    pt_refs = {col_a: pta_ref, col_b: ptb_ref}

    def matmul1(q, cols):
        ht_refs[cols][...] = jnp.dot(u_ref[q], xnt_ref[:, cols], preferred_element_type=F32)

    def gate(q, cols):
        ht_ref, pt_ref = ht_refs[cols], pt_refs[cols]
        for rl in range(n1_per_job):
            r = q * n1_per_job + rl
            w = None
            for h in range(PEER_HEADS):
                cnt = jnp.broadcast_to(cnt_ref[h, pl.ds(r, 1), cols], (pack, sub)).astype(BF16)
                a1 = jnp.broadcast_to(a1_ref[h, pl.ds(r, 1), cols], (pack, sub)).astype(BF16)
                sel = rank2_ref[h, :, :, cols] < cnt[None]
                term = a1[None] * jnp.where(sel, a2_ref[h, :, :, cols], zero)
                w = term if w is None else w + term
            rows = slice(rl * PEER_NKEYS, (rl + 1) * PEER_NKEYS)
            hb = ht_ref[rows, :]
            gelu = 0.5 * hb * (1.0 + lax.erf(hb * inv_sqrt2))
            p = w * gelu.astype(BF16).reshape(groups, pack, sub)
            pt_ref[rows, :] = p.reshape(PEER_NKEYS, sub)

    def matmul2(q, cols):
        acc_ref[:, cols] += jnp.dot(vt_ref[q], pt_refs[cols][...], preferred_element_type=F32)

    def pair(q, has_prev, has_next):
        matmul1(q, col_b)
        gate(q, col_a)
        if has_prev:
            matmul2(q - 1, col_b)
        if has_next:
            matmul1(q + 1, col_a)
        gate(q, col_b)
        matmul2(q, col_a)

    matmul1(0, col_a)
    if n_q == 1:
        pair(0, False, False)
    else:
        pair(0, False, True)

        def body(q, carry):
            pair(q, True, True)
            return carry

        lax.fori_loop(1, n_q - 1, body, 0)
        pair(n_q - 1, True, False)
    matmul2(n_q - 1, col_b)

    @pl.when(e == pl.num_programs(1) - 1)
    def _():
        y = x_ref[...] + acc_ref[...].T
        o_ref[...] = _rms(y, gout_ref[...]) if norm_out else y


PEER_JOB_ROWS = 512


def _expert_jobs(u, v):
    n_exp, d = u.shape
    n_jobs = n_exp // PEER_JOB_ROWS
    u3 = u.astype(BF16).reshape(n_jobs, PEER_JOB_ROWS, d)
    vt3 = jnp.transpose(v.astype(BF16).reshape(n_jobs, PEER_JOB_ROWS, d), (0, 2, 1))
    return u3, vt3


def _peer_dense(x, g, u, vt, a1t, cnt, a2t, rank2, g_out, norm_out, tt, et):
    t, d = x.shape
    n_jobs, job_rows, _ = u.shape
    n_q = et // job_rows
    n1_per_tile = et // PEER_NKEYS
    groups = PEER_NKEYS // BF16_SUBLANES
    grouped = (PEER_HEADS, groups, BF16_SUBLANES, t)
    full = pl.BlockSpec((PEER_HEADS, groups, BF16_SUBLANES, tt), lambda i, e: (0, 0, 0, i))
    part = pl.BlockSpec((PEER_HEADS, n1_per_tile, tt), lambda i, e: (0, e, i))
    sub = tt // 2
    row = pl.BlockSpec((1, d), lambda i, e: (0, 0))
    return pl.pallas_call(
        functools.partial(_peer_dense_kernel, norm_out=norm_out),
        grid=(t // tt, n_jobs // n_q),
        in_specs=[pl.BlockSpec((tt, d), lambda i, e: (i, 0)),
                  row,
                  pl.BlockSpec((n_q, job_rows, d), lambda i, e: (e, 0, 0)),
                  pl.BlockSpec((n_q, d, job_rows), lambda i, e: (e, 0, 0)),
                  part, part, full, full, row],
        out_specs=pl.BlockSpec((tt, d), lambda i, e: (i, 0)),
        out_shape=jax.ShapeDtypeStruct((t, d), F32),
        scratch_shapes=[pltpu.VMEM((d, tt), BF16), pltpu.VMEM((d, tt), F32),
                        pltpu.VMEM((job_rows, sub), F32), pltpu.VMEM((job_rows, sub), F32),
                        pltpu.VMEM((job_rows, sub), BF16), pltpu.VMEM((job_rows, sub), BF16)],
        compiler_params=_cparams(("arbitrary", "arbitrary"), vmem_mb=52),
    )(x, g.reshape(1, d), u, vt, a1t, cnt, a2t.reshape(grouped), rank2.reshape(grouped),
      g_out.reshape(1, d))


def _tiles(seq):
    pick = lambda want: min(want, seq)
    return dict(tm=pick(256), tq=pick(512), tconv=pick(512), tgla=pick(512), tt=pick(256), td=pick(512),
                et=2048)


def kernel(x, positions, norm_mix, w_in, lam_q1, lam_k1, lam_q2, lam_k2, diff_norm, conv_w, gla_w_gate2, gla_b_gate, gla_norm, w_out, norm_ffn, peer_w_q, peer_keys1, peer_keys2, peer_u, peer_v, norm_final):
    batch, seq, d = x.shape
    t = batch * seq
    depth = w_in.shape[0]
    tl = _tiles(seq)
    xt = x.reshape(t, d)
    tabs = _rope_tables(positions, tl["tm"])
    for i in range(depth):
        w_in_p = jnp.pad(w_in[i], ((0, 0), (0, IN_PAD - IN_WIDTH))).astype(BF16)
        proj = _rms_matmul(xt, norm_mix[i], w_in_p, tl["tm"])
        qt, kr, vt3 = _rope_prep(proj, tabs, tl["tq"])
        lam_init = 0.8 - 0.6 * math.exp(-0.3 * i)
        o_da = _diff_attention(qt, kr, vt3, lam_q1[i], lam_k1[i], lam_q2[i], lam_k2[i], diff_norm[i],
                               batch, seq, lam_init, tl["tq"])
        o_sc = _short_conv(proj, conv_w[i], seq, tl["tconv"])
        o_g = _gla(proj, gla_w_gate2[i], gla_b_gate[i], gla_norm[i], batch, seq, tl["tgla"])
        xt = _out_proj(xt, o_da, o_sc, o_g, w_out[i].astype(BF16), tl["tm"])
        a1t, cnt, a2t, rank2 = _peer_route(xt, norm_ffn[i], peer_w_q[i].astype(BF16),
                                           peer_keys1[i].astype(BF16), peer_keys2[i].astype(BF16), tl["tt"])
        u3, ev3 = _expert_jobs(peer_u[i], peer_v[i])
        xt = _peer_dense(xt, norm_ffn[i], u3, ev3, a1t, cnt, a2t, rank2, norm_final, i == depth - 1,
                         tl["td"], tl["et"])
    return xt.reshape(batch, seq, d)
```

```python
import functools
import math

import jax
import jax.numpy as jnp
from jax import lax
from jax.experimental import pallas as pl
from jax.experimental.pallas import tpu as pltpu

F32 = jnp.float32
BF16 = jnp.bfloat16

D_MODEL = 1024
EPS = 1e-6
HEAD_DIM = 64
DA_HEADS = 4
DA_WIDTH = 512
ROPE_THETA = 500000.0
ROT_DIM = 16
SC_WIDTH = 256
CONV_W = 3
GLA_HEADS = 4
GLA_DK = 32
GLA_DV = 64
GLA_KW = 128
GLA_VW = 256
GLA_GATE_RANK = 16
GLA_GATE_TEMP = 16.0
GLA_CHUNK = 64
GLA_SUB = 16
GLA_GROUP = 8
IN_WIDTH = 3088
IN_PAD = 3200
PEER_HEADS = 8
PEER_NKEYS = 128
PEER_EXPERTS = PEER_NKEYS * PEER_NKEYS
PEER_TOPK = 16
LANES = 128
BF16_SUBLANES = 16
NEG_INF = float("-inf")

COL_DA_Q, COL_DA_K, COL_DA_V = 0, 1, 2
COL_SC_B, COL_SC_C, COL_SC_H = 6, 7, 8
COL_G_V, COL_G_R = 10, 11
COL_G_Q, COL_G_K, COL_G_LR = 18, 19, 24


def _cparams(sem, vmem_mb=40):
    return pltpu.CompilerParams(dimension_semantics=sem, vmem_limit_bytes=vmem_mb * 1024 * 1024)


def _rms(x, g):
    return x * lax.rsqrt(jnp.mean(x * x, axis=-1, keepdims=True) + EPS) * g


def _rms_matmul_kernel(x_ref, g_ref, w_ref, o_ref):
    xn = _rms(x_ref[...], g_ref[...]).astype(BF16)
    o_ref[...] = jnp.dot(xn, w_ref[...], preferred_element_type=F32)


def _rms_matmul(x, g, w, tm):
    t, d = x.shape
    n = w.shape[1]
    return pl.pallas_call(
        _rms_matmul_kernel,
        grid=(t // tm,),
        in_specs=[pl.BlockSpec((tm, d), lambda i: (i, 0)),
                  pl.BlockSpec((1, d), lambda i: (0, 0)),
                  pl.BlockSpec((d, n), lambda i: (0, 0))],
        out_specs=pl.BlockSpec((tm, n), lambda i: (i, 0)),
        out_shape=jax.ShapeDtypeStruct((t, n), F32),
        compiler_params=_cparams(("arbitrary",)),
    )(x, g.reshape(1, d), w)


def _rope_table_kernel(pos_ref, inv_ref, mrot_ref, m1_ref, m2_ref, c_ref, s1_ref, s2_ref):
    ang = pos_ref[...] * inv_ref[...]
    c = jnp.cos(ang)
    s = jnp.sin(ang)
    mrot = mrot_ref[...]
    c_ref[...] = mrot * c + (1.0 - mrot)
    s1_ref[...] = -(m1_ref[...] * s)
    s2_ref[...] = m2_ref[...] * s


def _rope_tables(positions, tm):
    t = positions.size
    pos = jnp.broadcast_to(positions.reshape(t, 1).astype(F32), (t, LANES))
    d = jnp.arange(LANES) % HEAD_DIM
    half = ROT_DIM // 2
    inv = ROPE_THETA ** (-jnp.arange(0, ROT_DIM, 2, dtype=F32) / ROT_DIM)
    inv_row = jnp.where(d < ROT_DIM, inv[d % half], 0.0).astype(F32).reshape(1, LANES)
    mrot = (d < ROT_DIM).astype(F32).reshape(1, LANES)
    m1 = (d < half).astype(F32).reshape(1, LANES)
    m2 = ((d >= half) & (d < ROT_DIM)).astype(F32).reshape(1, LANES)
    row = pl.BlockSpec((1, LANES), lambda i: (0, 0))
    blk = pl.BlockSpec((tm, LANES), lambda i: (i, 0))
    return pl.pallas_call(
        _rope_table_kernel,
        grid=(t // tm,),
        in_specs=[blk, row, row, row, row],
        out_specs=[blk, blk, blk],
        out_shape=[jax.ShapeDtypeStruct((t, LANES), F32)] * 3,
        compiler_params=_cparams(("arbitrary",)),
    )(pos, inv_row, mrot, m1, m2)


def _rope_prep_kernel(q_ref, k_ref, v_ref, c_ref, s1_ref, s2_ref, qt_ref, ko_ref, vt_ref, *, qscale):
    c = c_ref[...]
    s1 = s1_ref[...]
    s2 = s2_ref[...]
    half = ROT_DIM // 2

    def rope(x):
        return x * c + pltpu.roll(x, LANES - half, 1) * s1 + pltpu.roll(x, half, 1) * s2

    for g in range(DA_WIDTH // LANES):
        sl = slice(LANES * g, LANES * (g + 1))
        qt_ref[sl, :] = (rope(q_ref[:, sl]) * qscale).T.astype(BF16)
        ko_ref[:, sl] = rope(k_ref[:, sl]).astype(BF16)
        vt_ref[0, sl, :] = v_ref[:, sl].T.astype(BF16)


def _rope_prep(proj, tabs, tm):
    t = proj.shape[0]
    c, s1, s2 = tabs
    col = lambda j: pl.BlockSpec((tm, DA_WIDTH), lambda i, j=j: (i, j))
    tab = pl.BlockSpec((tm, LANES), lambda i: (i, 0))
    return pl.pallas_call(
        functools.partial(_rope_prep_kernel, qscale=HEAD_DIM ** -0.5 * math.log2(math.e)),
        grid=(t // tm,),
        in_specs=[col(COL_DA_Q), col(COL_DA_K), col(COL_DA_V), tab, tab, tab],
        out_specs=[pl.BlockSpec((DA_WIDTH, tm), lambda i: (0, i)),
                   pl.BlockSpec((tm, DA_WIDTH), lambda i: (i, 0)),
                   pl.BlockSpec((1, DA_WIDTH, tm), lambda i: (i, 0, 0))],
        out_shape=[jax.ShapeDtypeStruct((DA_WIDTH, t), BF16),
                   jax.ShapeDtypeStruct((t, DA_WIDTH), BF16),
                   jax.ShapeDtypeStruct((t // tm, DA_WIDTH, tm), BF16)],
        compiler_params=_cparams(("arbitrary",)),
    )(proj, proj, proj, c, s1, s2)


def _attn_kernel(qt_ref, k_ref, vt_ref, lq1_ref, lk1_ref, lq2_ref, lk2_ref, gain_ref, o_ref,
                 acc1_ref, acc2_ref, m1_ref, l1_ref, m2_ref, l2_ref, s1a_ref, s2a_ref, s1b_ref, s2b_ref,
                 *, tq, lam_init):
    qi = pl.program_id(2)
    qt = qt_ref[...]
    row = lax.broadcasted_iota(jnp.int32, (LANES, 1), 0)
    zero = jnp.zeros_like(qt)
    qa = jnp.where(row < HEAD_DIM, qt, zero)
    qb = jnp.where(row >= HEAD_DIM, qt, zero)
    acc1_ref[...] = jnp.zeros_like(acc1_ref)
    acc2_ref[...] = jnp.zeros_like(acc2_ref)

    for st_ref in (m1_ref, m2_ref):
        st_ref[...] = jnp.full(st_ref.shape, NEG_INF, F32)
    for st_ref in (l1_ref, l2_ref):
        st_ref[...] = jnp.zeros_like(st_ref)
    maps = ((qa, m1_ref, l1_ref, acc1_ref), (qb, m2_ref, l2_ref, acc2_ref))

    def scores(j, bufs):
        off = pl.multiple_of(j * tq, tq)
        kk = k_ref[pl.ds(off, tq), :]
        for (qm, _, _, _), s_ref in zip(maps, bufs):
            s_ref[...] = jnp.dot(kk, qm, preferred_element_type=F32)

    def absorb(j, bufs, masked):
        vt = vt_ref[j]
        for (_, m_ref, l_ref, acc_ref), s_ref in zip(maps, bufs):
            s = s_ref[...]
            if masked:
                r = lax.broadcasted_iota(jnp.int32, (tq, tq), 0)
                c = lax.broadcasted_iota(jnp.int32, (tq, tq), 1)
                s = jnp.where(r <= c, s, NEG_INF)
            m = m_ref[...]
            mn = jnp.maximum(m, jnp.max(s, axis=0, keepdims=True))
            alpha = jnp.exp2(m - mn)
            p = jnp.exp2(s - mn)
            m_ref[...] = mn
            l_ref[...] = alpha * l_ref[...] + jnp.sum(p, axis=0, keepdims=True)
            acc_ref[...] = alpha * acc_ref[...] + jnp.dot(vt, p.astype(BF16),
                                                          preferred_element_type=F32)

    buf_a, buf_b = (s1a_ref, s2a_ref), (s1b_ref, s2b_ref)
    scores(0, buf_a)

    def pair(i, carry):
        j = 2 * i
        scores(j + 1, buf_b)
        absorb(j, buf_a, False)
        scores(j + 2, buf_a)
        absorb(j + 1, buf_b, False)
        return carry

    lax.fori_loop(0, qi // 2, pair, 0)

    @pl.when(qi % 2 == 1)
    def _():
        scores(qi, buf_b)
        absorb(qi - 1, buf_a, False)
        absorb(qi, buf_b, True)

    @pl.when(qi % 2 == 0)
    def _():
        absorb(qi, buf_a, True)

    l1, l2 = l1_ref[...], l2_ref[...]

    lam = (jnp.exp(jnp.sum(lq1_ref[...] * lk1_ref[...], axis=-1, keepdims=True))
           - jnp.exp(jnp.sum(lq2_ref[...] * lk2_ref[...], axis=-1, keepdims=True)) + lam_init)
    o = acc1_ref[...] / l1 - lam * (acc2_ref[...] / l2)
    y = o * lax.rsqrt(jnp.mean(o * o, axis=0, keepdims=True) + EPS) * gain_ref[...]
    o_ref[...] = (y * (1.0 - lam_init)).T


def _diff_attention(qt, kr, vt3, lq1, lk1, lq2, lk2, gain, batch, seq, lam_init, tq):
    t = kr.shape[0]
    nq = seq // tq
    vec = lambda n: pl.BlockSpec((1, n), lambda b, h, i: (0, 0))
    return pl.pallas_call(
        functools.partial(_attn_kernel, tq=tq, lam_init=lam_init),
        grid=(batch, DA_HEADS, nq),
        in_specs=[pl.BlockSpec((LANES, tq), lambda b, h, i: (h, b * nq + i)),
                  pl.BlockSpec((seq, LANES), lambda b, h, i: (b, h)),
                  pl.BlockSpec((nq, LANES, tq), lambda b, h, i: (b, h, 0)),
                  vec(HEAD_DIM), vec(HEAD_DIM), vec(HEAD_DIM), vec(HEAD_DIM),
                  pl.BlockSpec((LANES, 1), lambda b, h, i: (0, 0))],
        out_specs=pl.BlockSpec((tq, LANES), lambda b, h, i: (b * nq + i, h)),
        out_shape=jax.ShapeDtypeStruct((t, DA_WIDTH), F32),
        scratch_shapes=([pltpu.VMEM((LANES, tq), F32)] * 2 + [pltpu.VMEM((1, tq), F32)] * 4
                        + [pltpu.VMEM((tq, tq), F32)] * 4),
        compiler_params=_cparams(("arbitrary", "arbitrary", "arbitrary")),
    )(qt, kr, vt3, lq1.reshape(1, -1), lk1.reshape(1, -1), lq2.reshape(1, -1), lk2.reshape(1, -1),
      gain.reshape(-1, 1))


def _short_conv_kernel(b_ref, c_ref, h_ref, cp_ref, hp_ref, w_ref, o_ref, *, tm, blocks_per_seq):
    i = pl.program_id(0)
    z = c_ref[...] * h_ref[...]
    zp = cp_ref[...] * hp_ref[...]
    first = (i % blocks_per_seq) == 0
    zp = jnp.where(first, jnp.zeros_like(zp), zp)
    row = lax.broadcasted_iota(jnp.int32, (tm, 1), 0)
    z1 = jnp.where(row == 0, zp[7:8, :], pltpu.roll(z, 1, 0))
    z2 = pltpu.roll(z, 2, 0)
    z2 = jnp.where(row == 0, zp[6:7, :], jnp.where(row == 1, zp[7:8, :], z2))
    w = w_ref[...]
    y = w[0:1, :] * z2 + w[1:2, :] * z1 + w[2:3, :] * z
    o_ref[...] = b_ref[...] * y


def _short_conv(proj, w, seq, tm):
    t = proj.shape[0]
    rb = tm // 8
    cur = lambda j: pl.BlockSpec((tm, SC_WIDTH), lambda i, j=j: (i, j))
    prev = lambda j: pl.BlockSpec((8, SC_WIDTH), lambda i, j=j: (jnp.maximum(i * rb - 1, 0), j))
    return pl.pallas_call(
        functools.partial(_short_conv_kernel, tm=tm, blocks_per_seq=seq // tm),
        grid=(t // tm,),
        in_specs=[cur(COL_SC_B), cur(COL_SC_C), cur(COL_SC_H), prev(COL_SC_C), prev(COL_SC_H),
                  pl.BlockSpec((CONV_W, SC_WIDTH), lambda i: (0, 0))],
        out_specs=pl.BlockSpec((tm, SC_WIDTH), lambda i: (i, 0)),
        out_shape=jax.ShapeDtypeStruct((t, SC_WIDTH), F32),
        compiler_params=_cparams(("arbitrary",)),
    )(proj, proj, proj, proj, proj, w)


def _split3(x):
    hi = x.astype(BF16)
    r = x - hi.astype(F32)
    mid = r.astype(BF16)
    lo = (r - mid.astype(F32)).astype(BF16)
    return hi, mid, lo


def _gla_kernel(q_ref, k_ref, v_ref, r_ref, lr_ref, wg_ref, bg_ref, gn_ref, o_ref, st_ref, *, tc):
    ch = GLA_CHUNK

    @pl.when(pl.program_id(1) == 0)
    def _():
        st_ref[...] = jnp.zeros_like(st_ref)

    ii = lax.broadcasted_iota(jnp.int32, (ch, ch), 0)
    jj = lax.broadcasted_iota(jnp.int32, (ch, ch), 1)
    tri = (ii >= jj).astype(BF16)
    rk = lax.broadcasted_iota(jnp.int32, (GLA_VW, GLA_KW), 0)
    ck = lax.broadcasted_iota(jnp.int32, (GLA_VW, GLA_KW), 1)
    bd_k = (rk // GLA_DV) == (ck // GLA_DK)
    rv = lax.broadcasted_iota(jnp.int32, (GLA_VW, GLA_VW), 0)
    cv = lax.broadcasted_iota(jnp.int32, (GLA_VW, GLA_VW), 1)
    bd_v = (rv // GLA_DV) == (cv // GLA_DV)
    seg = bd_v.astype(BF16)
    sb = GLA_SUB
    nsb = ch // sb
    hs = GLA_HEADS * sb
    iota = lambda shape, axis: lax.broadcasted_iota(jnp.int32, shape, axis)
    sub_row = iota((sb, 1), 0)
    head_expand = ((iota((GLA_KW, GLA_VW), 0) // GLA_DK)
                   == (iota((GLA_KW, GLA_VW), 1) // GLA_DV)).astype(BF16)
    sum_j = ((iota((ch, ch * sb), 1) // sb) == iota((ch, ch * sb), 0)).astype(BF16)
    bd_ks = (iota((hs, GLA_KW), 0) // sb) == (iota((hs, GLA_KW), 1) // GLA_DK)
    bd_vs = (iota((hs, GLA_VW), 0) // sb) == (iota((hs, GLA_VW), 1) // GLA_DV)
    off_w = (nsb - 1) * hs
    att_keep = iota((ch, off_w), 0) >= (iota((ch, off_w), 1) // hs + 1) * sb
    dn_t = (((1,), (1,)), ((), ()))
    scale = GLA_DK ** -0.5

    def stage_decay(rows):
        x = jnp.dot(lr_ref[rows, :].astype(BF16), wg_ref[...], preferred_element_type=F32) + bg_ref[...]
        la = (jnp.minimum(x, 0.0) - jnp.log1p(jnp.exp(-jnp.abs(x)))) * (1.0 / GLA_GATE_TEMP)
        hi, mid, lo = _split3(la)
        return (jnp.dot(tri, hi, preferred_element_type=F32) + jnp.dot(tri, mid, preferred_element_type=F32)
                + jnp.dot(tri, lo, preferred_element_type=F32))

    def stage_pairwise(q, k, v, b):
        pieces, vrep = [], []
        for sbi in range(nsb):
            rs = slice(sbi * sb, (sbi + 1) * sb)
            qs, ks, bs = q[rs, :], k[rs, :], b[rs, :]
            for i in range(sb):
                dec = jnp.exp(jnp.where(sub_row <= i, bs[i:i + 1, :] - bs, NEG_INF))
                pieces.append(qs[i:i + 1, :] * ks * dec)
            vrep += [v[rs, :]] * sb
        x3 = jnp.concatenate(pieces, axis=0)
        x3h = x3.astype(BF16)
        return x3h, (x3 - x3h.astype(F32)).astype(BF16), jnp.concatenate(vrep, axis=0)

    def stage_head_sums(x3h, x3l):
        return (jnp.dot(x3h, head_expand, preferred_element_type=F32)
                + jnp.dot(x3l, head_expand, preferred_element_type=F32))

    def stage_cross_operands(q, k, vb, b):
        qts, kbds, vbds = [], [], []
        zeros_k = jnp.zeros((hs, GLA_KW), BF16)
        for sbi in range(nsb - 1):
            rs = slice(sbi * sb, (sbi + 1) * sb)
            end = (sbi + 1) * sb
            b_end = b[end - 1:end, :]
            qts.append((q * jnp.exp(jnp.minimum(b - b_end, 0.0))).astype(BF16))
            kt = (k[rs, :] * jnp.exp(b_end - b[rs, :])).astype(BF16)
            kbd = jnp.where(bd_ks, jnp.concatenate([kt] * GLA_HEADS, axis=0), jnp.zeros((), BF16))
            kbds.append(jnp.concatenate([kbd if s2 == sbi else zeros_k for s2 in range(nsb - 1)], axis=1))
            vbds.append(jnp.where(bd_vs, jnp.concatenate([vb[rs, :]] * GLA_HEADS, axis=0),
                                  jnp.zeros((), BF16)))
        return jnp.concatenate(qts, axis=1), jnp.concatenate(kbds, axis=0), jnp.concatenate(vbds, axis=0)

    def stage_cross_scores(q_all, k_all):
        att = lax.dot_general(q_all, k_all, dn_t, preferred_element_type=F32)
        return jnp.where(att_keep, att, 0.0).astype(BF16)

    def stage_state_update(k, v, b):
        b_last = b[ch - 1:ch, :]
        kl = (k * jnp.exp(b_last - b)).astype(BF16)
        ut = jnp.dot(v.T.astype(BF16), kl, preferred_element_type=F32)
        return jnp.exp(b_last), jnp.where(bd_k, ut, 0.0)

    def stage_mean_square(o):
        o2 = o * o
        o2h = o2.astype(BF16)
        o2l = (o2 - o2h.astype(F32)).astype(BF16)
        return (jnp.dot(o2h, seg, preferred_element_type=F32)
                + jnp.dot(o2l, seg, preferred_element_type=F32)) * (1.0 / GLA_DV)

    def stage_store(rows, o, ms):
        r = r_ref[rows, :]
        gate = r * (1.0 / (1.0 + jnp.exp(-r)))
        o_ref[rows, :] = o * lax.rsqrt(ms + EPS) * gn_ref[...] * gate

    each = lambda f, *cols: [f(*args) for args in zip(*cols)]
    gsize = min(GLA_GROUP, tc // ch)

    def group(g, carry):
        rows = [pl.ds(pl.multiple_of((g * gsize + c) * ch, ch), ch) for c in range(gsize)]
        b = each(stage_decay, rows)
        q = [q_ref[r, :] * scale for r in rows]
        k = [k_ref[r, :] for r in rows]
        v = [v_ref[r, :] for r in rows]
        vb = [x.astype(BF16) for x in v]
        x3h, x3l, vrep = zip(*each(stage_pairwise, q, k, v, b))
        w = each(stage_head_sums, x3h, x3l)
        wv = each(lambda w_, v_: (w_ * v_).astype(BF16), w, vrep)
        o = each(lambda wv_: jnp.dot(sum_j, wv_, preferred_element_type=F32), wv)
        q_all, k_all, v_all = zip(*each(stage_cross_operands, q, k, vb, b))
        att = each(stage_cross_scores, q_all, k_all)
        o = each(lambda o_, a_, v_: o_ + jnp.dot(a_, v_, preferred_element_type=F32), o, att, v_all)
        bq = each(lambda q_, b_: (q_ * jnp.exp(b_)).astype(BF16), q, b)
        decay, ut = zip(*each(stage_state_update, k, v, b))
        st = st_ref[...]
        states = []
        for c in range(gsize):
            states.append(st.astype(BF16))
            st = st * decay[c] + ut[c]
        st_ref[...] = st
        o = each(lambda o_, q_, s_: o_ + lax.dot_general(q_, s_, dn_t, preferred_element_type=F32),
                 o, bq, states)
        ms = each(stage_mean_square, o)
        each(stage_store, rows, o, ms)
        return carry

    lax.fori_loop(0, tc // ch // gsize, group, 0)


def _gla(proj, wg, bg, gn, batch, seq, tc):
    t = proj.shape[0]
    nb = seq // tc
    blk = lambda w, j: pl.BlockSpec((tc, w), lambda b, i, j=j: (b * nb + i, j))
    const = lambda r, c: pl.BlockSpec((r, c), lambda b, i: (0, 0))
    wg_pad = jnp.zeros((LANES, GLA_KW), F32).at[:GLA_GATE_RANK].set(wg).astype(BF16)
    gn_row = jnp.tile(gn, GLA_HEADS).reshape(1, GLA_VW)
    return pl.pallas_call(
        functools.partial(_gla_kernel, tc=tc),
        grid=(batch, nb),
        in_specs=[blk(GLA_KW, COL_G_Q), blk(GLA_KW, COL_G_K), blk(GLA_VW, COL_G_V), blk(GLA_VW, COL_G_R),
                  blk(LANES, COL_G_LR), const(LANES, GLA_KW), const(1, GLA_KW), const(1, GLA_VW)],
        out_specs=pl.BlockSpec((tc, GLA_VW), lambda b, i: (b * nb + i, 0)),
        out_shape=jax.ShapeDtypeStruct((t, GLA_VW), F32),
        scratch_shapes=[pltpu.VMEM((GLA_VW, GLA_KW), F32)],
        compiler_params=_cparams(("arbitrary", "arbitrary")),
    )(proj, proj, proj, proj, proj, wg_pad, bg.reshape(1, GLA_KW), gn_row)


def _out_proj_kernel(x_ref, a_ref, s_ref, g_ref, w_ref, o_ref):
    acc = jnp.dot(a_ref[...].astype(BF16), w_ref[0:DA_WIDTH, :], preferred_element_type=F32)
    acc += jnp.dot(s_ref[...].astype(BF16), w_ref[DA_WIDTH:DA_WIDTH + SC_WIDTH, :],
                   preferred_element_type=F32)
    acc += jnp.dot(g_ref[...].astype(BF16), w_ref[DA_WIDTH + SC_WIDTH:, :], preferred_element_type=F32)
    o_ref[...] = x_ref[...] + acc


def _out_proj(x, o_da, o_sc, o_g, w, tm):
    t, d = x.shape
    blk = lambda n: pl.BlockSpec((tm, n), lambda i: (i, 0))
    return pl.pallas_call(
        _out_proj_kernel,
        grid=(t // tm,),
        in_specs=[blk(d), blk(DA_WIDTH), blk(SC_WIDTH), blk(GLA_VW),
                  pl.BlockSpec(w.shape, lambda i: (0, 0))],
        out_specs=blk(d),
        out_shape=jax.ShapeDtypeStruct((t, d), F32),
        compiler_params=_cparams(("arbitrary",)),
    )(x, o_da, o_sc, o_g, w)


SUBLANES = 8


def _merge_exchange_network(n):
    pairs = []
    t = max(1, math.ceil(math.log2(n)))
    p = 2 ** (t - 1)
    while p > 0:
        q, r, d = 2 ** (t - 1), 0, p
        while d > 0:
            pairs += [(i, i + d) for i in range(n - d) if (i & p) == r]
            d, q, r = q - p, q // 2, p
        p //= 2
    return pairs


def _top_desc(x, n):
    groups = [x[SUBLANES * i:SUBLANES * (i + 1), :] for i in range(x.shape[0] // SUBLANES)]
    for i, j in _merge_exchange_network(len(groups)):
        groups[i], groups[j] = jnp.maximum(groups[i], groups[j]), jnp.minimum(groups[i], groups[j])
    vals = []
    for r in range(n):
        m = jnp.max(groups[0], axis=0, keepdims=True)
        vals.append(m)
        hit = groups[0] == m
        live = min(len(groups), n - r)
        for k in range(live - 1):
            groups[k] = jnp.where(hit, groups[k + 1], groups[k])
        if live == len(groups) and live > 0 and r + 1 < n:
            groups[live - 1] = jnp.where(hit, NEG_INF, groups[live - 1])
    return vals


def _ranks_of(x, vals):
    rank = jnp.full(x.shape, float(len(vals)), F32)
    for r in reversed(range(len(vals))):
        rank = jnp.where(x == vals[r], float(r), rank)
    return rank


def _peer_route_kernel(x_ref, g_ref, wq_ref, k1_ref, k2_ref, a1_ref, cnt_ref, a2_ref, rank2_ref):
    k = PEER_TOPK
    tt = x_ref.shape[0]
    xn = _rms(x_ref[...], g_ref[...]).astype(BF16)
    q = jnp.dot(xn, wq_ref[...], preferred_element_type=F32)
    dn = (((1,), (1,)), ((), ()))
    for h in range(PEER_HEADS):
        qa = q[:, (2 * h) * LANES:(2 * h + 1) * LANES].astype(BF16)
        qb = q[:, (2 * h + 1) * LANES:(2 * h + 2) * LANES].astype(BF16)
        s1 = lax.dot_general(k1_ref[...], qa, dn, preferred_element_type=F32)
        s2 = lax.dot_general(k2_ref[...], qb, dn, preferred_element_type=F32)
        v1 = _top_desc(s1, k)
        v2 = _top_desc(s2, k)
        rank2 = _ranks_of(s2, v2)
        cands = [v1[i] + v2[j] for i in range(k) for j in range(k // (i + 1))]
        pad = (-len(cands)) % 8
        cands += [jnp.full((1, tt), NEG_INF, F32)] * pad
        top = _top_desc(jnp.concatenate(cands, axis=0), k)
        z = jnp.ones((1, tt), F32)
        for r in range(1, k):
            z = z + jnp.exp(top[r] - top[0])
        thr = top[k - 1]
        v1s = jnp.concatenate(v1, axis=0)
        cnt_rank = jnp.zeros_like(v1s)
        for j in range(k):
            cnt_rank = cnt_rank + ((v1s + v2[j]) >= thr).astype(F32)
        cnt = jnp.zeros_like(s1)
        for i in range(k):
            cnt = jnp.where(s1 == v1[i], cnt_rank[i:i + 1, :], cnt)
        cnt_ref[h] = cnt
        rank2_ref[h] = rank2.astype(BF16)
        a1_ref[h] = jnp.exp(s1 - v1[0]) * (1.0 / z)
        a2_ref[h] = jnp.exp(s2 - v2[0]).astype(BF16)


def _peer_route(x, g, wq, k1, k2, tt):
    t, d = x.shape
    shape = (PEER_HEADS, PEER_NKEYS, t)
    out = pl.BlockSpec((PEER_HEADS, PEER_NKEYS, tt), lambda i: (0, 0, i))
    f32 = jax.ShapeDtypeStruct(shape, F32)
    bf16 = jax.ShapeDtypeStruct(shape, BF16)
    return pl.pallas_call(
        _peer_route_kernel,
        grid=(t // tt,),
        in_specs=[pl.BlockSpec((tt, d), lambda i: (i, 0)),
                  pl.BlockSpec((1, d), lambda i: (0, 0)),
                  pl.BlockSpec(wq.shape, lambda i: (0, 0)),
                  pl.BlockSpec(k1.shape, lambda i: (0, 0)),
                  pl.BlockSpec(k2.shape, lambda i: (0, 0))],
        out_specs=[out, out, out, out],
        out_shape=[f32, f32, bf16, bf16],
        compiler_params=_cparams(("arbitrary",)),
    )(x, g.reshape(1, d), wq, k1, k2)


def _peer_dense_kernel(x_ref, g_ref, u_ref, vt_ref, a1_ref, cnt_ref, a2_ref, rank2_ref, gout_ref, o_ref,
                       xnt_ref, acc_ref, hta_ref, htb_ref, pta_ref, ptb_ref, *, norm_out):
    e = pl.program_id(1)

    @pl.when(e == 0)
    def _():
        xn = _rms(x_ref[...], g_ref[...])
        xnt_ref[...] = xn.T.astype(BF16)
        acc_ref[...] = jnp.zeros_like(acc_ref)

    inv_sqrt2 = 1.0 / math.sqrt(2.0)
    zero = jnp.zeros((), BF16)
    n_q, job_rows, _ = u_ref.shape
    sub = xnt_ref.shape[1] // 2
    n1_per_job = job_rows // PEER_NKEYS
    pack = BF16_SUBLANES
    groups = PEER_NKEYS // pack
    col_a, col_b = slice(0, sub), slice(sub, 2 * sub)
    ht_refs = {col_a: hta_ref, col_b: htb_ref}
    pt_refs = {col_a: pta_ref, col_b: ptb_ref}

    def matmul1(q, cols):
        ht_refs[cols][...] = jnp.dot(u_ref[q], xnt_ref[:, cols], preferred_element_type=F32)

    def gate(q, cols):
        ht_ref, pt_ref = ht_refs[cols], pt_refs[cols]
        for rl in range(n1_per_job):
            r = q * n1_per_job + rl
            w = None
            for h in range(PEER_HEADS):
                cnt = jnp.broadcast_to(cnt_ref[h, pl.ds(r, 1), cols], (pack, sub)).astype(BF16)
                a1 = jnp.broadcast_to(a1_ref[h, pl.ds(r, 1), cols], (pack, sub)).astype(BF16)
                sel = rank2_ref[h, :, :, cols] < cnt[None]
                term = a1[None] * jnp.where(sel, a2_ref[h, :, :, cols], zero)
                w = term if w is None else w + term
            rows = slice(rl * PEER_NKEYS, (rl + 1) * PEER_NKEYS)
            hb = ht_ref[rows, :]
            gelu = 0.5 * hb * (1.0 + lax.erf(hb * inv_sqrt2))
            p = w * gelu.astype(BF16).reshape(groups, pack, sub)
            pt_ref[rows, :] = p.reshape(PEER_NKEYS, sub)

    def matmul2(q, cols):
        acc_ref[:, cols] += jnp.dot(vt_ref[q], pt_refs[cols][...], preferred_element_type=F32)

    def pair(q, has_prev, has_next):
        matmul1(q, col_b)
        gate(q, col_a)
        if has_prev:
            matmul2(q - 1, col_b)
        if has_next:
            matmul1(q + 1, col_a)
        gate(q, col_b)
        matmul2(q, col_a)

    matmul1(0, col_a)
    if n_q == 1:
        pair(0, False, False)
    else:
        pair(0, False, True)

        def body(q, carry):
            pair(q, True, True)
            return carry

        lax.fori_loop(1, n_q - 1, body, 0)
        pair(n_q - 1, True, False)
    matmul2(n_q - 1, col_b)

    @pl.when(e == pl.num_programs(1) - 1)
    def _():
        y = x_ref[...] + acc_ref[...].T
        o_ref[...] = _rms(y, gout_ref[...]) if norm_out else y


PEER_JOB_ROWS = 512


def _expert_jobs(u, v):
    n_exp, d = u.shape
    n_jobs = n_exp // PEER_JOB_ROWS
    u3 = u.astype(BF16).reshape(n_jobs, PEER_JOB_ROWS, d)
    vt3 = jnp.transpose(v.astype(BF16).reshape(n_jobs, PEER_JOB_ROWS, d), (0, 2, 1))
    return u3, vt3


def _peer_dense(x, g, u, vt, a1t, cnt, a2t, rank2, g_out, norm_out, tt, et):
    t, d = x.shape
    n_jobs, job_rows, _ = u.shape
    n_q = et // job_rows
    n1_per_tile = et // PEER_NKEYS
    groups = PEER_NKEYS // BF16_SUBLANES
    grouped = (PEER_HEADS, groups, BF16_SUBLANES, t)
    full = pl.BlockSpec((PEER_HEADS, groups, BF16_SUBLANES, tt), lambda i, e: (0, 0, 0, i))
    part = pl.BlockSpec((PEER_HEADS, n1_per_tile, tt), lambda i, e: (0, e, i))
    sub = tt // 2
    row = pl.BlockSpec((1, d), lambda i, e: (0, 0))
    return pl.pallas_call(
        functools.partial(_peer_dense_kernel, norm_out=norm_out),
        grid=(t // tt, n_jobs // n_q),
        in_specs=[pl.BlockSpec((tt, d), lambda i, e: (i, 0)),
                  row,
                  pl.BlockSpec((n_q, job_rows, d), lambda i, e: (e, 0, 0)),
                  pl.BlockSpec((n_q, d, job_rows), lambda i, e: (e, 0, 0)),
                  part, part, full, full, row],
        out_specs=pl.BlockSpec((tt, d), lambda i, e: (i, 0)),
        out_shape=jax.ShapeDtypeStruct((t, d), F32),
        scratch_shapes=[pltpu.VMEM((d, tt), BF16), pltpu.VMEM((d, tt), F32),
                        pltpu.VMEM((job_rows, sub), F32), pltpu.VMEM((job_rows, sub), F32),
                        pltpu.VMEM((job_rows, sub), BF16), pltpu.VMEM((job_rows, sub), BF16)],
        compiler_params=_cparams(("arbitrary", "arbitrary"), vmem_mb=58),
    )(x, g.reshape(1, d), u, vt, a1t, cnt, a2t.reshape(grouped), rank2.reshape(grouped),
      g_out.reshape(1, d))


def _tiles(seq):
    pick = lambda want: min(want, seq)
    return dict(tm=pick(256), tq=pick(512), tconv=pick(512), tgla=pick(512), tt=pick(256), td=pick(512),
                et=4096)


def kernel(x, positions, norm_mix, w_in, lam_q1, lam_k1, lam_q2, lam_k2, diff_norm, conv_w, gla_w_gate2, gla_b_gate, gla_norm, w_out, norm_ffn, peer_w_q, peer_keys1, peer_keys2, peer_u, peer_v, norm_final):
    batch, seq, d = x.shape
    t = batch * seq
    depth = w_in.shape[0]
    tl = _tiles(seq)
    xt = x.reshape(t, d)
    tabs = _rope_tables(positions, tl["tm"])
    for i in range(depth):
        w_in_p = jnp.pad(w_in[i], ((0, 0), (0, IN_PAD - IN_WIDTH))).astype(BF16)
        proj = _rms_matmul(xt, norm_mix[i], w_in_p, tl["tm"])
        qt, kr, vt3 = _rope_prep(proj, tabs, tl["tq"])
        lam_init = 0.8 - 0.6 * math.exp(-0.3 * i)
        o_da = _diff_attention(qt, kr, vt3, lam_q1[i], lam_k1[i], lam_q2[i], lam_k2[i], diff_norm[i],
                               batch, seq, lam_init, tl["tq"])
        o_sc = _short_conv(proj, conv_w[i], seq, tl["tconv"])
        o_g = _gla(proj, gla_w_gate2[i], gla_b_gate[i], gla_norm[i], batch, seq, tl["tgla"])
        xt = _out_proj(xt, o_da, o_sc, o_g, w_out[i].astype(BF16), tl["tm"])
        a1t, cnt, a2t, rank2 = _peer_route(xt, norm_ffn[i], peer_w_q[i].astype(BF16),
                                           peer_keys1[i].astype(BF16), peer_keys2[i].astype(BF16), tl["tt"])
        u3, ev3 = _expert_jobs(peer_u[i], peer_v[i])
        xt = _peer_dense(xt, norm_ffn[i], u3, ev3, a1t, cnt, a2t, rank2, norm_final, i == depth - 1,
                         tl["td"], tl["et"])
    return xt.reshape(batch, seq, d)
```

```python
import functools
import math

import jax
import jax.numpy as jnp
from jax import lax
from jax.experimental import pallas as pl
from jax.experimental.pallas import tpu as pltpu

F32 = jnp.float32
BF16 = jnp.bfloat16

D_MODEL = 1024
EPS = 1e-6
HEAD_DIM = 64
DA_HEADS = 4
DA_WIDTH = 512
ROPE_THETA = 500000.0
ROT_DIM = 16
SC_WIDTH = 256
CONV_W = 3
GLA_HEADS = 4
GLA_DK = 32
GLA_DV = 64
GLA_KW = 128
GLA_VW = 256
GLA_GATE_RANK = 16
GLA_GATE_TEMP = 16.0
GLA_CHUNK = 64
GLA_SUB = 16
GLA_GROUP = 8
IN_WIDTH = 3088
IN_PAD = 3200
PEER_HEADS = 8
PEER_NKEYS = 128
PEER_EXPERTS = PEER_NKEYS * PEER_NKEYS
PEER_TOPK = 16
LANES = 128
BF16_SUBLANES = 16
NEG_INF = float("-inf")

QKV_WIDTH = 3 * DA_WIDTH
REST_WIDTH = IN_PAD - QKV_WIDTH
COL_SC_B, COL_SC_C, COL_SC_H = 0, 1, 2
COL_G_V, COL_G_R = 4, 5
COL_G_Q, COL_G_K, COL_G_LR = 6, 7, 12


def _cparams(sem, vmem_mb=40):
    return pltpu.CompilerParams(dimension_semantics=sem, vmem_limit_bytes=vmem_mb * 1024 * 1024)


def _rms(x, g):
    return x * lax.rsqrt(jnp.mean(x * x, axis=-1, keepdims=True) + EPS) * g


def _rope_table_kernel(pos_ref, inv_ref, mrot_ref, m1_ref, m2_ref, c_ref, s1_ref, s2_ref):
    ang = pos_ref[...] * inv_ref[...]
    c = jnp.cos(ang)
    s = jnp.sin(ang)
    mrot = mrot_ref[...]
    c_ref[...] = mrot * c + (1.0 - mrot)
    s1_ref[...] = -(m1_ref[...] * s)
    s2_ref[...] = m2_ref[...] * s


def _rope_tables(positions, tm):
    t = positions.size
    pos = jnp.broadcast_to(positions.reshape(t, 1).astype(F32), (t, LANES))
    d = jnp.arange(LANES) % HEAD_DIM
    half = ROT_DIM // 2
    inv = ROPE_THETA ** (-jnp.arange(0, ROT_DIM, 2, dtype=F32) / ROT_DIM)
    inv_row = jnp.where(d < ROT_DIM, inv[d % half], 0.0).astype(F32).reshape(1, LANES)
    mrot = (d < ROT_DIM).astype(F32).reshape(1, LANES)
    m1 = (d < half).astype(F32).reshape(1, LANES)
    m2 = ((d >= half) & (d < ROT_DIM)).astype(F32).reshape(1, LANES)
    row = pl.BlockSpec((1, LANES), lambda i: (0, 0))
    blk = pl.BlockSpec((tm, LANES), lambda i: (i, 0))
    return pl.pallas_call(
        _rope_table_kernel,
        grid=(t // tm,),
        in_specs=[blk, row, row, row, row],
        out_specs=[blk, blk, blk],
        out_shape=[jax.ShapeDtypeStruct((t, LANES), F32)] * 3,
        compiler_params=_cparams(("arbitrary",)),
    )(pos, inv_row, mrot, m1, m2)


def _in_proj_kernel(x_ref, g_ref, w_ref, c_ref, s1_ref, s2_ref, qt_ref, ko_ref, vt_ref, rest_ref,
                    *, qscale):
    xn = _rms(x_ref[...], g_ref[...]).astype(BF16)
    rest_ref[...] = jnp.dot(xn, w_ref[:, QKV_WIDTH:], preferred_element_type=F32)
    c = c_ref[...]
    s1 = s1_ref[...]
    s2 = s2_ref[...]
    half = ROT_DIM // 2

    def rope(x):
        return x * c + pltpu.roll(x, LANES - half, 1) * s1 + pltpu.roll(x, half, 1) * s2

    qkv = jnp.dot(xn, w_ref[:, :QKV_WIDTH], preferred_element_type=F32)
    for g in range(DA_WIDTH // LANES):
        sl = slice(LANES * g, LANES * (g + 1))
        col = lambda base: slice(base + LANES * g, base + LANES * (g + 1))
        qt_ref[sl, :] = (rope(qkv[:, col(0)]) * qscale).T.astype(BF16)
        ko_ref[:, sl] = rope(qkv[:, col(DA_WIDTH)]).astype(BF16)
        vt_ref[0, sl, :] = qkv[:, col(2 * DA_WIDTH)].T.astype(BF16)


def _in_proj(x, g, w, tabs, tm):
    t, d = x.shape
    c, s1, s2 = tabs
    tab = pl.BlockSpec((tm, LANES), lambda i: (i, 0))
    return pl.pallas_call(
        functools.partial(_in_proj_kernel, qscale=HEAD_DIM ** -0.5 * math.log2(math.e)),
        grid=(t // tm,),
        in_specs=[pl.BlockSpec((tm, d), lambda i: (i, 0)),
                  pl.BlockSpec((1, d), lambda i: (0, 0)),
                  pl.BlockSpec(w.shape, lambda i: (0, 0)),
                  tab, tab, tab],
        out_specs=[pl.BlockSpec((DA_WIDTH, tm), lambda i: (0, i)),
                   pl.BlockSpec((tm, DA_WIDTH), lambda i: (i, 0)),
                   pl.BlockSpec((1, DA_WIDTH, tm), lambda i: (i, 0, 0)),
                   pl.BlockSpec((tm, REST_WIDTH), lambda i: (i, 0))],
        out_shape=[jax.ShapeDtypeStruct((DA_WIDTH, t), BF16),
                   jax.ShapeDtypeStruct((t, DA_WIDTH), BF16),
                   jax.ShapeDtypeStruct((t // tm, DA_WIDTH, tm), BF16),
                   jax.ShapeDtypeStruct((t, REST_WIDTH), F32)],
        compiler_params=_cparams(("arbitrary",)),
    )(x, g.reshape(1, d), w, c, s1, s2)


def _attn_kernel(qt_ref, k_ref, vt_ref, lq1_ref, lk1_ref, lq2_ref, lk2_ref, gain_ref, o_ref,
                 acc1_ref, acc2_ref, m1_ref, l1_ref, m2_ref, l2_ref, s1a_ref, s2a_ref, s1b_ref, s2b_ref,
                 *, tq, lam_init):
    qi = pl.program_id(2)
    qt = qt_ref[...]
    row = lax.broadcasted_iota(jnp.int32, (LANES, 1), 0)
    zero = jnp.zeros_like(qt)
    qa = jnp.where(row < HEAD_DIM, qt, zero)
    qb = jnp.where(row >= HEAD_DIM, qt, zero)
    acc1_ref[...] = jnp.zeros_like(acc1_ref)
    acc2_ref[...] = jnp.zeros_like(acc2_ref)

    for st_ref in (m1_ref, m2_ref):
        st_ref[...] = jnp.full(st_ref.shape, NEG_INF, F32)
    for st_ref in (l1_ref, l2_ref):
        st_ref[...] = jnp.zeros_like(st_ref)
    maps = ((qa, m1_ref, l1_ref, acc1_ref), (qb, m2_ref, l2_ref, acc2_ref))

    def scores(j, bufs):
        off = pl.multiple_of(j * tq, tq)
        kk = k_ref[pl.ds(off, tq), :]
        for (qm, _, _, _), s_ref in zip(maps, bufs):
            s_ref[...] = jnp.dot(kk, qm, preferred_element_type=F32)

    def absorb(j, bufs, masked):
        vt = vt_ref[j]
        for (_, m_ref, l_ref, acc_ref), s_ref in zip(maps, bufs):
            s = s_ref[...]
            if masked:
                r = lax.broadcasted_iota(jnp.int32, (tq, tq), 0)
                c = lax.broadcasted_iota(jnp.int32, (tq, tq), 1)
                s = jnp.where(r <= c, s, NEG_INF)
            m = m_ref[...]
            mn = jnp.maximum(m, jnp.max(s, axis=0, keepdims=True))
            alpha = jnp.exp2(m - mn)
            p = jnp.exp2(s - mn)
            m_ref[...] = mn
            l_ref[...] = alpha * l_ref[...] + jnp.sum(p, axis=0, keepdims=True)
            acc_ref[...] = alpha * acc_ref[...] + jnp.dot(vt, p.astype(BF16),
                                                          preferred_element_type=F32)

    buf_a, buf_b = (s1a_ref, s2a_ref), (s1b_ref, s2b_ref)
    scores(0, buf_a)

    def pair(i, carry):
        j = 2 * i
        scores(j + 1, buf_b)
        absorb(j, buf_a, False)
        scores(j + 2, buf_a)
        absorb(j + 1, buf_b, False)
        return carry

    lax.fori_loop(0, qi // 2, pair, 0)

    @pl.when(qi % 2 == 1)
    def _():
        scores(qi, buf_b)
        absorb(qi - 1, buf_a, False)
        absorb(qi, buf_b, True)

    @pl.when(qi % 2 == 0)
    def _():
        absorb(qi, buf_a, True)

    l1, l2 = l1_ref[...], l2_ref[...]

    lam = (jnp.exp(jnp.sum(lq1_ref[...] * lk1_ref[...], axis=-1, keepdims=True))
           - jnp.exp(jnp.sum(lq2_ref[...] * lk2_ref[...], axis=-1, keepdims=True)) + lam_init)
    o = acc1_ref[...] / l1 - lam * (acc2_ref[...] / l2)
    y = o * lax.rsqrt(jnp.mean(o * o, axis=0, keepdims=True) + EPS) * gain_ref[...]
    o_ref[...] = (y * (1.0 - lam_init)).T


def _diff_attention(qt, kr, vt3, lq1, lk1, lq2, lk2, gain, batch, seq, lam_init, tq):
    t = kr.shape[0]
    nq = seq // tq
    vec = lambda n: pl.BlockSpec((1, n), lambda b, h, i: (0, 0))
    return pl.pallas_call(
        functools.partial(_attn_kernel, tq=tq, lam_init=lam_init),
        grid=(batch, DA_HEADS, nq),
        in_specs=[pl.BlockSpec((LANES, tq), lambda b, h, i: (h, b * nq + i)),
                  pl.BlockSpec((seq, LANES), lambda b, h, i: (b, h)),
                  pl.BlockSpec((nq, LANES, tq), lambda b, h, i: (b, h, 0)),
                  vec(HEAD_DIM), vec(HEAD_DIM), vec(HEAD_DIM), vec(HEAD_DIM),
                  pl.BlockSpec((LANES, 1), lambda b, h, i: (0, 0))],
        out_specs=pl.BlockSpec((tq, LANES), lambda b, h, i: (b * nq + i, h)),
        out_shape=jax.ShapeDtypeStruct((t, DA_WIDTH), F32),
        scratch_shapes=([pltpu.VMEM((LANES, tq), F32)] * 2 + [pltpu.VMEM((1, tq), F32)] * 4
                        + [pltpu.VMEM((tq, tq), F32)] * 4),
        compiler_params=_cparams(("arbitrary", "arbitrary", "arbitrary")),
    )(qt, kr, vt3, lq1.reshape(1, -1), lk1.reshape(1, -1), lq2.reshape(1, -1), lk2.reshape(1, -1),
      gain.reshape(-1, 1))


def _short_conv_kernel(b_ref, c_ref, h_ref, cp_ref, hp_ref, w_ref, o_ref, *, tm, blocks_per_seq):
    i = pl.program_id(0)
    z = c_ref[...] * h_ref[...]
    zp = cp_ref[...] * hp_ref[...]
    first = (i % blocks_per_seq) == 0
    zp = jnp.where(first, jnp.zeros_like(zp), zp)
    row = lax.broadcasted_iota(jnp.int32, (tm, 1), 0)
    z1 = jnp.where(row == 0, zp[7:8, :], pltpu.roll(z, 1, 0))
    z2 = pltpu.roll(z, 2, 0)
    z2 = jnp.where(row == 0, zp[6:7, :], jnp.where(row == 1, zp[7:8, :], z2))
    w = w_ref[...]
    y = w[0:1, :] * z2 + w[1:2, :] * z1 + w[2:3, :] * z
    o_ref[...] = b_ref[...] * y


def _short_conv(proj, w, seq, tm):
    t = proj.shape[0]
    rb = tm // 8
    cur = lambda j: pl.BlockSpec((tm, SC_WIDTH), lambda i, j=j: (i, j))
    prev = lambda j: pl.BlockSpec((8, SC_WIDTH), lambda i, j=j: (jnp.maximum(i * rb - 1, 0), j))
    return pl.pallas_call(
        functools.partial(_short_conv_kernel, tm=tm, blocks_per_seq=seq // tm),
        grid=(t // tm,),
        in_specs=[cur(COL_SC_B), cur(COL_SC_C), cur(COL_SC_H), prev(COL_SC_C), prev(COL_SC_H),
                  pl.BlockSpec((CONV_W, SC_WIDTH), lambda i: (0, 0))],
        out_specs=pl.BlockSpec((tm, SC_WIDTH), lambda i: (i, 0)),
        out_shape=jax.ShapeDtypeStruct((t, SC_WIDTH), F32),
        compiler_params=_cparams(("arbitrary",)),
    )(proj, proj, proj, proj, proj, w)


def _split3(x):
    hi = x.astype(BF16)
    r = x - hi.astype(F32)
    mid = r.astype(BF16)
    lo = (r - mid.astype(F32)).astype(BF16)
    return hi, mid, lo


def _gla_kernel(q_ref, k_ref, v_ref, r_ref, lr_ref, wg_ref, bg_ref, gn_ref, o_ref, st_ref, *, tc):
    ch = GLA_CHUNK

    @pl.when(pl.program_id(1) == 0)
    def _():
        st_ref[...] = jnp.zeros_like(st_ref)

    ii = lax.broadcasted_iota(jnp.int32, (ch, ch), 0)
    jj = lax.broadcasted_iota(jnp.int32, (ch, ch), 1)
    tri = (ii >= jj).astype(BF16)
    rk = lax.broadcasted_iota(jnp.int32, (GLA_VW, GLA_KW), 0)
    ck = lax.broadcasted_iota(jnp.int32, (GLA_VW, GLA_KW), 1)
    bd_k = (rk // GLA_DV) == (ck // GLA_DK)
    rv = lax.broadcasted_iota(jnp.int32, (GLA_VW, GLA_VW), 0)
    cv = lax.broadcasted_iota(jnp.int32, (GLA_VW, GLA_VW), 1)
    bd_v = (rv // GLA_DV) == (cv // GLA_DV)
    seg = bd_v.astype(BF16)
    sb = GLA_SUB
    nsb = ch // sb
    hs = GLA_HEADS * sb
    iota = lambda shape, axis: lax.broadcasted_iota(jnp.int32, shape, axis)
    sub_row = iota((sb, 1), 0)
    head_expand = ((iota((GLA_KW, GLA_VW), 0) // GLA_DK)
                   == (iota((GLA_KW, GLA_VW), 1) // GLA_DV)).astype(BF16)
    sum_j = ((iota((ch, ch * sb), 1) // sb) == iota((ch, ch * sb), 0)).astype(BF16)
    bd_ks = (iota((hs, GLA_KW), 0) // sb) == (iota((hs, GLA_KW), 1) // GLA_DK)
    bd_vs = (iota((hs, GLA_VW), 0) // sb) == (iota((hs, GLA_VW), 1) // GLA_DV)
    off_w = (nsb - 1) * hs
    att_keep = iota((ch, off_w), 0) >= (iota((ch, off_w), 1) // hs + 1) * sb
    dn_t = (((1,), (1,)), ((), ()))
    scale = GLA_DK ** -0.5

    def stage_decay(rows):
        x = jnp.dot(lr_ref[rows, :].astype(BF16), wg_ref[...], preferred_element_type=F32) + bg_ref[...]
        la = (jnp.minimum(x, 0.0) - jnp.log1p(jnp.exp(-jnp.abs(x)))) * (1.0 / GLA_GATE_TEMP)
        hi, mid, lo = _split3(la)
        return (jnp.dot(tri, hi, preferred_element_type=F32) + jnp.dot(tri, mid, preferred_element_type=F32)
                + jnp.dot(tri, lo, preferred_element_type=F32))

    def stage_pairwise(q, k, v, b):
        pieces, vrep = [], []
        for sbi in range(nsb):
            rs = slice(sbi * sb, (sbi + 1) * sb)
            qs, ks, bs = q[rs, :], k[rs, :], b[rs, :]
            for i in range(sb):
                dec = jnp.exp(jnp.where(sub_row <= i, bs[i:i + 1, :] - bs, NEG_INF))
                pieces.append(qs[i:i + 1, :] * ks * dec)
            vrep += [v[rs, :]] * sb
        x3 = jnp.concatenate(pieces, axis=0)
        x3h = x3.astype(BF16)
        return x3h, (x3 - x3h.astype(F32)).astype(BF16), jnp.concatenate(vrep, axis=0)

    def stage_head_sums(x3h, x3l):
        return (jnp.dot(x3h, head_expand, preferred_element_type=F32)
                + jnp.dot(x3l, head_expand, preferred_element_type=F32))

    def stage_cross_operands(q, k, vb, b):
        qts, kbds, vbds = [], [], []
        zeros_k = jnp.zeros((hs, GLA_KW), BF16)
        for sbi in range(nsb - 1):
            rs = slice(sbi * sb, (sbi + 1) * sb)
            end = (sbi + 1) * sb
            b_end = b[end - 1:end, :]
            qts.append((q * jnp.exp(jnp.minimum(b - b_end, 0.0))).astype(BF16))
            kt = (k[rs, :] * jnp.exp(b_end - b[rs, :])).astype(BF16)
            kbd = jnp.where(bd_ks, jnp.concatenate([kt] * GLA_HEADS, axis=0), jnp.zeros((), BF16))
            kbds.append(jnp.concatenate([kbd if s2 == sbi else zeros_k for s2 in range(nsb - 1)], axis=1))
            vbds.append(jnp.where(bd_vs, jnp.concatenate([vb[rs, :]] * GLA_HEADS, axis=0),
                                  jnp.zeros((), BF16)))
        return jnp.concatenate(qts, axis=1), jnp.concatenate(kbds, axis=0), jnp.concatenate(vbds, axis=0)

    def stage_cross_scores(q_all, k_all):
        att = lax.dot_general(q_all, k_all, dn_t, preferred_element_type=F32)
        return jnp.where(att_keep, att, 0.0).astype(BF16)

    def stage_state_update(k, v, b):
        b_last = b[ch - 1:ch, :]
        kl = (k * jnp.exp(b_last - b)).astype(BF16)
        ut = jnp.dot(v.T.astype(BF16), kl, preferred_element_type=F32)
        return jnp.exp(b_last), jnp.where(bd_k, ut, 0.0)

    def stage_mean_square(o):
        o2 = o * o
        o2h = o2.astype(BF16)
        o2l = (o2 - o2h.astype(F32)).astype(BF16)
        return (jnp.dot(o2h, seg, preferred_element_type=F32)
                + jnp.dot(o2l, seg, preferred_element_type=F32)) * (1.0 / GLA_DV)

    def stage_store(rows, o, ms):
        r = r_ref[rows, :]
        gate = r * (1.0 / (1.0 + jnp.exp(-r)))
        o_ref[rows, :] = o * lax.rsqrt(ms + EPS) * gn_ref[...] * gate

    each = lambda f, *cols: [f(*args) for args in zip(*cols)]
    gsize = min(GLA_GROUP, tc // ch)

    def group(g, carry):
        rows = [pl.ds(pl.multiple_of((g * gsize + c) * ch, ch), ch) for c in range(gsize)]
        b = each(stage_decay, rows)
        q = [q_ref[r, :] * scale for r in rows]
        k = [k_ref[r, :] for r in rows]
        v = [v_ref[r, :] for r in rows]
        vb = [x.astype(BF16) for x in v]
        x3h, x3l, vrep = zip(*each(stage_pairwise, q, k, v, b))
        w = each(stage_head_sums, x3h, x3l)
        wv = each(lambda w_, v_: (w_ * v_).astype(BF16), w, vrep)
        o = each(lambda wv_: jnp.dot(sum_j, wv_, preferred_element_type=F32), wv)
        q_all, k_all, v_all = zip(*each(stage_cross_operands, q, k, vb, b))
        att = each(stage_cross_scores, q_all, k_all)
        o = each(lambda o_, a_, v_: o_ + jnp.dot(a_, v_, preferred_element_type=F32), o, att, v_all)
        bq = each(lambda q_, b_: (q_ * jnp.exp(b_)).astype(BF16), q, b)
        decay, ut = zip(*each(stage_state_update, k, v, b))
        st = st_ref[...]
        states = []
        for c in range(gsize):
            states.append(st.astype(BF16))
            st = st * decay[c] + ut[c]
        st_ref[...] = st
        o = each(lambda o_, q_, s_: o_ + lax.dot_general(q_, s_, dn_t, preferred_element_type=F32),
                 o, bq, states)
        ms = each(stage_mean_square, o)
        each(stage_store, rows, o, ms)
        return carry

    lax.fori_loop(0, tc // ch // gsize, group, 0)


def _gla(proj, wg, bg, gn, batch, seq, tc):
    t = proj.shape[0]
    nb = seq // tc
    blk = lambda w, j: pl.BlockSpec((tc, w), lambda b, i, j=j: (b * nb + i, j))
    const = lambda r, c: pl.BlockSpec((r, c), lambda b, i: (0, 0))
    wg_pad = jnp.zeros((LANES, GLA_KW), F32).at[:GLA_GATE_RANK].set(wg).astype(BF16)
    gn_row = jnp.tile(gn, GLA_HEADS).reshape(1, GLA_VW)
    return pl.pallas_call(
        functools.partial(_gla_kernel, tc=tc),
        grid=(batch, nb),
        in_specs=[blk(GLA_KW, COL_G_Q), blk(GLA_KW, COL_G_K), blk(GLA_VW, COL_G_V), blk(GLA_VW, COL_G_R),
                  blk(LANES, COL_G_LR), const(LANES, GLA_KW), const(1, GLA_KW), const(1, GLA_VW)],
        out_specs=pl.BlockSpec((tc, GLA_VW), lambda b, i: (b * nb + i, 0)),
        out_shape=jax.ShapeDtypeStruct((t, GLA_VW), F32),
        scratch_shapes=[pltpu.VMEM((GLA_VW, GLA_KW), F32)],
        compiler_params=_cparams(("arbitrary", "arbitrary")),
    )(proj, proj, proj, proj, proj, wg_pad, bg.reshape(1, GLA_KW), gn_row)


def _out_proj_kernel(x_ref, a_ref, s_ref, g_ref, w_ref, o_ref):
    acc = jnp.dot(a_ref[...].astype(BF16), w_ref[0:DA_WIDTH, :], preferred_element_type=F32)
    acc += jnp.dot(s_ref[...].astype(BF16), w_ref[DA_WIDTH:DA_WIDTH + SC_WIDTH, :],
                   preferred_element_type=F32)
    acc += jnp.dot(g_ref[...].astype(BF16), w_ref[DA_WIDTH + SC_WIDTH:, :], preferred_element_type=F32)
    o_ref[...] = x_ref[...] + acc


def _out_proj(x, o_da, o_sc, o_g, w, tm):
    t, d = x.shape
    blk = lambda n: pl.BlockSpec((tm, n), lambda i: (i, 0))
    return pl.pallas_call(
        _out_proj_kernel,
        grid=(t // tm,),
        in_specs=[blk(d), blk(DA_WIDTH), blk(SC_WIDTH), blk(GLA_VW),
                  pl.BlockSpec(w.shape, lambda i: (0, 0))],
        out_specs=blk(d),
        out_shape=jax.ShapeDtypeStruct((t, d), F32),
        compiler_params=_cparams(("arbitrary",)),
    )(x, o_da, o_sc, o_g, w)


SUBLANES = 8


def _merge_exchange_network(n):
    pairs = []
    t = max(1, math.ceil(math.log2(n)))
    p = 2 ** (t - 1)
    while p > 0:
        q, r, d = 2 ** (t - 1), 0, p
        while d > 0:
            pairs += [(i, i + d) for i in range(n - d) if (i & p) == r]
            d, q, r = q - p, q // 2, p
        p //= 2
    return pairs


def _top_desc(x, n):
    groups = [x[SUBLANES * i:SUBLANES * (i + 1), :] for i in range(x.shape[0] // SUBLANES)]
    for i, j in _merge_exchange_network(len(groups)):
        groups[i], groups[j] = jnp.maximum(groups[i], groups[j]), jnp.minimum(groups[i], groups[j])
    vals = []
    for r in range(n):
        m = jnp.max(groups[0], axis=0, keepdims=True)
        vals.append(m)
        hit = groups[0] == m
        live = min(len(groups), n - r)
        for k in range(live - 1):
            groups[k] = jnp.where(hit, groups[k + 1], groups[k])
        if live == len(groups) and live > 0 and r + 1 < n:
            groups[live - 1] = jnp.where(hit, NEG_INF, groups[live - 1])
    return vals


def _ranks_of(x, vals):
    rank = jnp.full(x.shape, float(len(vals)), F32)
    for r in reversed(range(len(vals))):
        rank = jnp.where(x == vals[r], float(r), rank)
    return rank


def _peer_route_kernel(x_ref, g_ref, wq_ref, k1_ref, k2_ref, a1_ref, cnt_ref, a2_ref, rank2_ref):
    k = PEER_TOPK
    tt = x_ref.shape[0]
    xn = _rms(x_ref[...], g_ref[...]).astype(BF16)
    q = jnp.dot(xn, wq_ref[...], preferred_element_type=F32)
    dn = (((1,), (1,)), ((), ()))
    for h in range(PEER_HEADS):
        qa = q[:, (2 * h) * LANES:(2 * h + 1) * LANES].astype(BF16)
        qb = q[:, (2 * h + 1) * LANES:(2 * h + 2) * LANES].astype(BF16)
        s1 = lax.dot_general(k1_ref[...], qa, dn, preferred_element_type=F32)
        s2 = lax.dot_general(k2_ref[...], qb, dn, preferred_element_type=F32)
        v1 = _top_desc(s1, k)
        v2 = _top_desc(s2, k)
        rank2 = _ranks_of(s2, v2)
        cands = [v1[i] + v2[j] for i in range(k) for j in range(k // (i + 1))]
        pad = (-len(cands)) % 8
        cands += [jnp.full((1, tt), NEG_INF, F32)] * pad
        top = _top_desc(jnp.concatenate(cands, axis=0), k)
        z = jnp.ones((1, tt), F32)
        for r in range(1, k):
            z = z + jnp.exp(top[r] - top[0])
        thr = top[k - 1]
        v1s = jnp.concatenate(v1, axis=0)
        cnt_rank = jnp.zeros_like(v1s)
        for j in range(k):
            cnt_rank = cnt_rank + ((v1s + v2[j]) >= thr).astype(F32)
        cnt = jnp.zeros_like(s1)
        for i in range(k):
            cnt = jnp.where(s1 == v1[i], cnt_rank[i:i + 1, :], cnt)
        cnt_ref[h] = cnt
        rank2_ref[h] = rank2.astype(BF16)
        a1_ref[h] = jnp.exp(s1 - v1[0]) * (1.0 / z)
        a2_ref[h] = jnp.exp(s2 - v2[0]).astype(BF16)


def _peer_route(x, g, wq, k1, k2, tt):
    t, d = x.shape
    shape = (PEER_HEADS, PEER_NKEYS, t)
    out = pl.BlockSpec((PEER_HEADS, PEER_NKEYS, tt), lambda i: (0, 0, i))
    f32 = jax.ShapeDtypeStruct(shape, F32)
    bf16 = jax.ShapeDtypeStruct(shape, BF16)
    return pl.pallas_call(
        _peer_route_kernel,
        grid=(t // tt,),
        in_specs=[pl.BlockSpec((tt, d), lambda i: (i, 0)),
                  pl.BlockSpec((1, d), lambda i: (0, 0)),
                  pl.BlockSpec(wq.shape, lambda i: (0, 0)),
                  pl.BlockSpec(k1.shape, lambda i: (0, 0)),
                  pl.BlockSpec(k2.shape, lambda i: (0, 0))],
        out_specs=[out, out, out, out],
        out_shape=[f32, f32, bf16, bf16],
        compiler_params=_cparams(("arbitrary",)),
    )(x, g.reshape(1, d), wq, k1, k2)


def _peer_dense_kernel(x_ref, g_ref, u_ref, vt_ref, a1_ref, cnt_ref, a2_ref, rank2_ref, gout_ref, o_ref,
                       xnt_ref, acc_ref, hta_ref, htb_ref, pta_ref, ptb_ref, *, norm_out):
    e = pl.program_id(1)

    @pl.when(e == 0)
    def _():
        xn = _rms(x_ref[...], g_ref[...])
        xnt_ref[...] = xn.T.astype(BF16)
        acc_ref[...] = jnp.zeros_like(acc_ref)

    inv_sqrt2 = 1.0 / math.sqrt(2.0)
    zero = jnp.zeros((), BF16)
    n_q, job_rows, _ = u_ref.shape
    sub = xnt_ref.shape[1] // 2
    n1_per_job = job_rows // PEER_NKEYS
    pack = BF16_SUBLANES
    groups = PEER_NKEYS // pack
    col_a, col_b = slice(0, sub), slice(sub, 2 * sub)
    ht_refs = {col_a: hta_ref, col_b: htb_ref}
    pt_refs = {col_a: pta_ref, col_b: ptb_ref}

    def matmul1(q, cols):
        ht_refs[cols][...] = jnp.dot(u_ref[q], xnt_ref[:, cols], preferred_element_type=F32)

    def gate(q, cols):
        ht_ref, pt_ref = ht_refs[cols], pt_refs[cols]
        for rl in range(n1_per_job):
            r = q * n1_per_job + rl
            w = None
            for h in range(PEER_HEADS):
                cnt = jnp.broadcast_to(cnt_ref[h, pl.ds(r, 1), cols], (pack, sub)).astype(BF16)
                a1 = jnp.broadcast_to(a1_ref[h, pl.ds(r, 1), cols], (pack, sub)).astype(BF16)
                sel = rank2_ref[h, :, :, cols] < cnt[None]
                term = a1[None] * jnp.where(sel, a2_ref[h, :, :, cols], zero)
                w = term if w is None else w + term
            rows = slice(rl * PEER_NKEYS, (rl + 1) * PEER_NKEYS)
            hb = ht_ref[rows, :]
            gelu = 0.5 * hb * (1.0 + lax.erf(hb * inv_sqrt2))
            p = w * gelu.astype(BF16).reshape(groups, pack, sub)
            pt_ref[rows, :] = p.reshape(PEER_NKEYS, sub)

    def matmul2(q, cols):
        acc_ref[:, cols] += jnp.dot(vt_ref[q], pt_refs[cols][...], preferred_element_type=F32)

    def pair(q, has_prev, has_next):
        matmul1(q, col_b)
        gate(q, col_a)
        if has_prev:
            matmul2(q - 1, col_b)
        if has_next:
            matmul1(q + 1, col_a)
        gate(q, col_b)
        matmul2(q, col_a)

    matmul1(0, col_a)
    if n_q == 1:
        pair(0, False, False)
    else:
        pair(0, False, True)

        def body(q, carry):
            pair(q, True, True)
            return carry

        lax.fori_loop(1, n_q - 1, body, 0)
        pair(n_q - 1, True, False)
    matmul2(n_q - 1, col_b)

    @pl.when(e == pl.num_programs(1) - 1)
    def _():
        y = x_ref[...] + acc_ref[...].T
        o_ref[...] = _rms(y, gout_ref[...]) if norm_out else y


PEER_JOB_ROWS = 512


def _expert_jobs(u, v):
    n_exp, d = u.shape
    n_jobs = n_exp // PEER_JOB_ROWS
    u3 = u.astype(BF16).reshape(n_jobs, PEER_JOB_ROWS, d)
    vt3 = jnp.transpose(v.astype(BF16).reshape(n_jobs, PEER_JOB_ROWS, d), (0, 2, 1))
    return u3, vt3


def _peer_dense(x, g, u, vt, a1t, cnt, a2t, rank2, g_out, norm_out, tt, et):
    t, d = x.shape
    n_jobs, job_rows, _ = u.shape
    n_q = et // job_rows
    n1_per_tile = et // PEER_NKEYS
    groups = PEER_NKEYS // BF16_SUBLANES
    grouped = (PEER_HEADS, groups, BF16_SUBLANES, t)
    full = pl.BlockSpec((PEER_HEADS, groups, BF16_SUBLANES, tt), lambda i, e: (0, 0, 0, i))
    part = pl.BlockSpec((PEER_HEADS, n1_per_tile, tt), lambda i, e: (0, e, i))
    sub = tt // 2
    row = pl.BlockSpec((1, d), lambda i, e: (0, 0))
    return pl.pallas_call(
        functools.partial(_peer_dense_kernel, norm_out=norm_out),
        grid=(t // tt, n_jobs // n_q),
        in_specs=[pl.BlockSpec((tt, d), lambda i, e: (i, 0)),
                  row,
                  pl.BlockSpec((n_q, job_rows, d), lambda i, e: (e, 0, 0)),
                  pl.BlockSpec((n_q, d, job_rows), lambda i, e: (e, 0, 0)),
                  part, part, full, full, row],
        out_specs=pl.BlockSpec((tt, d), lambda i, e: (i, 0)),
        out_shape=jax.ShapeDtypeStruct((t, d), F32),
        scratch_shapes=[pltpu.VMEM((d, tt), BF16), pltpu.VMEM((d, tt), F32),
                        pltpu.VMEM((job_rows, sub), F32), pltpu.VMEM((job_rows, sub), F32),
                        pltpu.VMEM((job_rows, sub), BF16), pltpu.VMEM((job_rows, sub), BF16)],
        compiler_params=_cparams(("arbitrary", "arbitrary"), vmem_mb=52),
    )(x, g.reshape(1, d), u, vt, a1t, cnt, a2t.reshape(grouped), rank2.reshape(grouped),
      g_out.reshape(1, d))


def _tiles(seq):
    pick = lambda want: min(want, seq)
    return dict(tm=pick(256), tq=pick(512), tconv=pick(512), tgla=pick(512), tt=pick(256), td=pick(512),
                et=2048)


def kernel(x, positions, norm_mix, w_in, lam_q1, lam_k1, lam_q2, lam_k2, diff_norm, conv_w, gla_w_gate2, gla_b_gate, gla_norm, w_out, norm_ffn, peer_w_q, peer_keys1, peer_keys2, peer_u, peer_v, norm_final):
    batch, seq, d = x.shape
    t = batch * seq
    depth = w_in.shape[0]
    tl = _tiles(seq)
    xt = x.reshape(t, d)
    tabs = _rope_tables(positions, tl["tm"])
    for i in range(depth):
        w_in_p = jnp.pad(w_in[i], ((0, 0), (0, IN_PAD - IN_WIDTH))).astype(BF16)
        qt, kr, vt3, proj = _in_proj(xt, norm_mix[i], w_in_p, tabs, tl["tq"])
        lam_init = 0.8 - 0.6 * math.exp(-0.3 * i)
        o_da = _diff_attention(qt, kr, vt3, lam_q1[i], lam_k1[i], lam_q2[i], lam_k2[i], diff_norm[i],
                               batch, seq, lam_init, tl["tq"])
        o_sc = _short_conv(proj, conv_w[i], seq, tl["tconv"])
        o_g = _gla(proj, gla_w_gate2[i], gla_b_gate[i], gla_norm[i], batch, seq, tl["tgla"])
        xt = _out_proj(xt, o_da, o_sc, o_g, w_out[i].astype(BF16), tl["tm"])
        a1t, cnt, a2t, rank2 = _peer_route(xt, norm_ffn[i], peer_w_q[i].astype(BF16),
                                           peer_keys1[i].astype(BF16), peer_keys2[i].astype(BF16), tl["tt"])
        u3, ev3 = _expert_jobs(peer_u[i], peer_v[i])
        xt = _peer_dense(xt, norm_ffn[i], u3, ev3, a1t, cnt, a2t, rank2, norm_final, i == depth - 1,
                         tl["td"], tl["et"])
    return xt.reshape(batch, seq, d)
```

```python
import functools
import math

import jax
import jax.numpy as jnp
from jax import lax
from jax.experimental import pallas as pl
from jax.experimental.pallas import tpu as pltpu

F32 = jnp.float32
BF16 = jnp.bfloat16

D_MODEL = 1024
EPS = 1e-6
HEAD_DIM = 64
DA_HEADS = 4
DA_WIDTH = 512
ROPE_THETA = 500000.0
ROT_DIM = 16
SC_WIDTH = 256
CONV_W = 3
GLA_HEADS = 4
GLA_DK = 32
GLA_DV = 64
GLA_KW = 128
GLA_VW = 256
GLA_GATE_RANK = 16
GLA_GATE_TEMP = 16.0
GLA_CHUNK = 64
GLA_SUB = 16
GLA_GROUP = 8
IN_WIDTH = 3088
IN_PAD = 3200
PEER_HEADS = 8
PEER_NKEYS = 128
PEER_EXPERTS = PEER_NKEYS * PEER_NKEYS
PEER_TOPK = 16
LANES = 128
BF16_SUBLANES = 16
NEG_INF = float("-inf")

QKV_WIDTH = 3 * DA_WIDTH
REST_WIDTH = IN_PAD - QKV_WIDTH
COL_SC_B, COL_SC_C, COL_SC_H = 0, 1, 2
COL_G_V, COL_G_R = 4, 5
COL_G_Q, COL_G_K, COL_G_LR = 6, 7, 12


def _cparams(sem, vmem_mb=40):
    return pltpu.CompilerParams(dimension_semantics=sem, vmem_limit_bytes=vmem_mb * 1024 * 1024)


def _rms(x, g):
    return x * lax.rsqrt(jnp.mean(x * x, axis=-1, keepdims=True) + EPS) * g


def _rope_table_kernel(pos_ref, inv_ref, mrot_ref, m1_ref, m2_ref, c_ref, s1_ref, s2_ref):
    ang = pos_ref[...] * inv_ref[...]
    c = jnp.cos(ang)
    s = jnp.sin(ang)
    mrot = mrot_ref[...]
    c_ref[...] = mrot * c + (1.0 - mrot)
    s1_ref[...] = -(m1_ref[...] * s)
    s2_ref[...] = m2_ref[...] * s


def _rope_tables(positions, tm):
    t = positions.size
    pos = jnp.broadcast_to(positions.reshape(t, 1).astype(F32), (t, LANES))
    d = jnp.arange(LANES) % HEAD_DIM
    half = ROT_DIM // 2
    inv = ROPE_THETA ** (-jnp.arange(0, ROT_DIM, 2, dtype=F32) / ROT_DIM)
    inv_row = jnp.where(d < ROT_DIM, inv[d % half], 0.0).astype(F32).reshape(1, LANES)
    mrot = (d < ROT_DIM).astype(F32).reshape(1, LANES)
    m1 = (d < half).astype(F32).reshape(1, LANES)
    m2 = ((d >= half) & (d < ROT_DIM)).astype(F32).reshape(1, LANES)
    row = pl.BlockSpec((1, LANES), lambda i: (0, 0))
    blk = pl.BlockSpec((tm, LANES), lambda i: (i, 0))
    return pl.pallas_call(
        _rope_table_kernel,
        grid=(t // tm,),
        in_specs=[blk, row, row, row, row],
        out_specs=[blk, blk, blk],
        out_shape=[jax.ShapeDtypeStruct((t, LANES), F32)] * 3,
        compiler_params=_cparams(("arbitrary",)),
    )(pos, inv_row, mrot, m1, m2)


def _in_proj_kernel(x_ref, g_ref, w_ref, c_ref, s1_ref, s2_ref, qt_ref, ko_ref, vt_ref, rest_ref,
                    *, qscale):
    xn = _rms(x_ref[...], g_ref[...]).astype(BF16)
    rest_ref[...] = jnp.dot(xn, w_ref[:, QKV_WIDTH:], preferred_element_type=F32)
    c = c_ref[...]
    s1 = s1_ref[...]
    s2 = s2_ref[...]
    half = ROT_DIM // 2

    def rope(x):
        return x * c + pltpu.roll(x, LANES - half, 1) * s1 + pltpu.roll(x, half, 1) * s2

    qkv = jnp.dot(xn, w_ref[:, :QKV_WIDTH], preferred_element_type=F32)
    for g in range(DA_WIDTH // LANES):
        sl = slice(LANES * g, LANES * (g + 1))
        col = lambda base: slice(base + LANES * g, base + LANES * (g + 1))
        qt_ref[sl, :] = (rope(qkv[:, col(0)]) * qscale).T.astype(BF16)
        ko_ref[:, sl] = rope(qkv[:, col(DA_WIDTH)]).astype(BF16)
        vt_ref[0, sl, :] = qkv[:, col(2 * DA_WIDTH)].T.astype(BF16)


def _in_proj(x, g, w, tabs, tm):
    t, d = x.shape
    c, s1, s2 = tabs
    tab = pl.BlockSpec((tm, LANES), lambda i: (i, 0))
    return pl.pallas_call(
        functools.partial(_in_proj_kernel, qscale=HEAD_DIM ** -0.5 * math.log2(math.e)),
        grid=(t // tm,),
        in_specs=[pl.BlockSpec((tm, d), lambda i: (i, 0)),
                  pl.BlockSpec((1, d), lambda i: (0, 0)),
                  pl.BlockSpec(w.shape, lambda i: (0, 0)),
                  tab, tab, tab],
        out_specs=[pl.BlockSpec((DA_WIDTH, tm), lambda i: (0, i)),
                   pl.BlockSpec((tm, DA_WIDTH), lambda i: (i, 0)),
                   pl.BlockSpec((1, DA_WIDTH, tm), lambda i: (i, 0, 0)),
                   pl.BlockSpec((tm, REST_WIDTH), lambda i: (i, 0))],
        out_shape=[jax.ShapeDtypeStruct((DA_WIDTH, t), BF16),
                   jax.ShapeDtypeStruct((t, DA_WIDTH), BF16),
                   jax.ShapeDtypeStruct((t // tm, DA_WIDTH, tm), BF16),
                   jax.ShapeDtypeStruct((t, REST_WIDTH), F32)],
        compiler_params=_cparams(("arbitrary",)),
    )(x, g.reshape(1, d), w, c, s1, s2)


def _attn_kernel(qt_ref, k_ref, vt_ref, lq1_ref, lk1_ref, lq2_ref, lk2_ref, gain_ref, o_ref,
                 acc1_ref, acc2_ref, m1_ref, l1_ref, m2_ref, l2_ref, s1a_ref, s2a_ref, s1b_ref, s2b_ref,
                 *, tq, lam_init):
    qi = pl.program_id(2)
    qt = qt_ref[...]
    row = lax.broadcasted_iota(jnp.int32, (LANES, 1), 0)
    zero = jnp.zeros_like(qt)
    qa = jnp.where(row < HEAD_DIM, qt, zero)
    qb = jnp.where(row >= HEAD_DIM, qt, zero)
    acc1_ref[...] = jnp.zeros_like(acc1_ref)
    acc2_ref[...] = jnp.zeros_like(acc2_ref)

    for st_ref in (m1_ref, m2_ref):
        st_ref[...] = jnp.full(st_ref.shape, NEG_INF, F32)
    for st_ref in (l1_ref, l2_ref):
        st_ref[...] = jnp.zeros_like(st_ref)
    maps = ((qa, m1_ref, l1_ref, acc1_ref), (qb, m2_ref, l2_ref, acc2_ref))

    def scores(j, bufs):
        off = pl.multiple_of(j * tq, tq)
        kk = k_ref[pl.ds(off, tq), :]
        for (qm, _, _, _), s_ref in zip(maps, bufs):
            s_ref[...] = jnp.dot(kk, qm, preferred_element_type=F32)

    def absorb(j, bufs, masked):
        vt = vt_ref[j]
        for (_, m_ref, l_ref, acc_ref), s_ref in zip(maps, bufs):
            s = s_ref[...]
            if masked:
                r = lax.broadcasted_iota(jnp.int32, (tq, tq), 0)
                c = lax.broadcasted_iota(jnp.int32, (tq, tq), 1)
                s = jnp.where(r <= c, s, NEG_INF)
            m = m_ref[...]
            mn = jnp.maximum(m, jnp.max(s, axis=0, keepdims=True))
            alpha = jnp.exp2(m - mn)
            p = jnp.exp2(s - mn)
            m_ref[...] = mn
            l_ref[...] = alpha * l_ref[...] + jnp.sum(p, axis=0, keepdims=True)
            acc_ref[...] = alpha * acc_ref[...] + jnp.dot(vt, p.astype(BF16),
                                                          preferred_element_type=F32)

    buf_a, buf_b = (s1a_ref, s2a_ref), (s1b_ref, s2b_ref)
    scores(0, buf_a)

    def pair(i, carry):
        j = 2 * i
        scores(j + 1, buf_b)
        absorb(j, buf_a, False)
        scores(j + 2, buf_a)
        absorb(j + 1, buf_b, False)
        return carry

    lax.fori_loop(0, qi // 2, pair, 0)

    @pl.when(qi % 2 == 1)
    def _():
        scores(qi, buf_b)
        absorb(qi - 1, buf_a, False)
        absorb(qi, buf_b, True)

    @pl.when(qi % 2 == 0)
    def _():
        absorb(qi, buf_a, True)

    l1, l2 = l1_ref[...], l2_ref[...]

    lam = (jnp.exp(jnp.sum(lq1_ref[...] * lk1_ref[...], axis=-1, keepdims=True))
           - jnp.exp(jnp.sum(lq2_ref[...] * lk2_ref[...], axis=-1, keepdims=True)) + lam_init)
    o = acc1_ref[...] / l1 - lam * (acc2_ref[...] / l2)
    y = o * lax.rsqrt(jnp.mean(o * o, axis=0, keepdims=True) + EPS) * gain_ref[...]
    o_ref[...] = (y * (1.0 - lam_init)).T


def _diff_attention(qt, kr, vt3, lq1, lk1, lq2, lk2, gain, batch, seq, lam_init, tq):
    t = kr.shape[0]
    nq = seq // tq
    vec = lambda n: pl.BlockSpec((1, n), lambda b, h, i: (0, 0))
    return pl.pallas_call(
        functools.partial(_attn_kernel, tq=tq, lam_init=lam_init),
        grid=(batch, DA_HEADS, nq),
        in_specs=[pl.BlockSpec((LANES, tq), lambda b, h, i: (h, b * nq + i)),
                  pl.BlockSpec((seq, LANES), lambda b, h, i: (b, h)),
                  pl.BlockSpec((nq, LANES, tq), lambda b, h, i: (b, h, 0)),
                  vec(HEAD_DIM), vec(HEAD_DIM), vec(HEAD_DIM), vec(HEAD_DIM),
                  pl.BlockSpec((LANES, 1), lambda b, h, i: (0, 0))],
        out_specs=pl.BlockSpec((tq, LANES), lambda b, h, i: (b * nq + i, h)),
        out_shape=jax.ShapeDtypeStruct((t, DA_WIDTH), F32),
        scratch_shapes=([pltpu.VMEM((LANES, tq), F32)] * 2 + [pltpu.VMEM((1, tq), F32)] * 4
                        + [pltpu.VMEM((tq, tq), F32)] * 4),
        compiler_params=_cparams(("arbitrary", "arbitrary", "arbitrary")),
    )(qt, kr, vt3, lq1.reshape(1, -1), lk1.reshape(1, -1), lq2.reshape(1, -1), lk2.reshape(1, -1),
      gain.reshape(-1, 1))


def _short_conv_kernel(b_ref, c_ref, h_ref, cp_ref, hp_ref, w_ref, o_ref, *, tm, blocks_per_seq):
    i = pl.program_id(0)
    z = c_ref[...] * h_ref[...]
    zp = cp_ref[...] * hp_ref[...]
    first = (i % blocks_per_seq) == 0
    zp = jnp.where(first, jnp.zeros_like(zp), zp)
    row = lax.broadcasted_iota(jnp.int32, (tm, 1), 0)
    z1 = jnp.where(row == 0, zp[7:8, :], pltpu.roll(z, 1, 0))
    z2 = pltpu.roll(z, 2, 0)
    z2 = jnp.where(row == 0, zp[6:7, :], jnp.where(row == 1, zp[7:8, :], z2))
    w = w_ref[...]
    y = w[0:1, :] * z2 + w[1:2, :] * z1 + w[2:3, :] * z
    o_ref[...] = b_ref[...] * y


def _short_conv(proj, w, seq, tm):
    t = proj.shape[0]
    rb = tm // 8
    cur = lambda j: pl.BlockSpec((tm, SC_WIDTH), lambda i, j=j: (i, j))
    prev = lambda j: pl.BlockSpec((8, SC_WIDTH), lambda i, j=j: (jnp.maximum(i * rb - 1, 0), j))
    return pl.pallas_call(
        functools.partial(_short_conv_kernel, tm=tm, blocks_per_seq=seq // tm),
        grid=(t // tm,),
        in_specs=[cur(COL_SC_B), cur(COL_SC_C), cur(COL_SC_H), prev(COL_SC_C), prev(COL_SC_H),
                  pl.BlockSpec((CONV_W, SC_WIDTH), lambda i: (0, 0))],
        out_specs=pl.BlockSpec((tm, SC_WIDTH), lambda i: (i, 0)),
        out_shape=jax.ShapeDtypeStruct((t, SC_WIDTH), F32),
        compiler_params=_cparams(("arbitrary",)),
    )(proj, proj, proj, proj, proj, w)


def _split3(x):
    hi = x.astype(BF16)
    r = x - hi.astype(F32)
    mid = r.astype(BF16)
    lo = (r - mid.astype(F32)).astype(BF16)
    return hi, mid, lo


def _gla_kernel(q_ref, k_ref, v_ref, r_ref, lr_ref, wg_ref, bg_ref, gn_ref, o_ref, st_ref, *, tc):
    ch = GLA_CHUNK

    @pl.when(pl.program_id(1) == 0)
    def _():
        st_ref[...] = jnp.zeros_like(st_ref)

    ii = lax.broadcasted_iota(jnp.int32, (ch, ch), 0)
    jj = lax.broadcasted_iota(jnp.int32, (ch, ch), 1)
    tri = (ii >= jj).astype(BF16)
    rk = lax.broadcasted_iota(jnp.int32, (GLA_VW, GLA_KW), 0)
    ck = lax.broadcasted_iota(jnp.int32, (GLA_VW, GLA_KW), 1)
    bd_k = (rk // GLA_DV) == (ck // GLA_DK)
    rv = lax.broadcasted_iota(jnp.int32, (GLA_VW, GLA_VW), 0)
    cv = lax.broadcasted_iota(jnp.int32, (GLA_VW, GLA_VW), 1)
    bd_v = (rv // GLA_DV) == (cv // GLA_DV)
    seg = bd_v.astype(BF16)
    sb = GLA_SUB
    nsb = ch // sb
    hs = GLA_HEADS * sb
    iota = lambda shape, axis: lax.broadcasted_iota(jnp.int32, shape, axis)
    sub_row = iota((sb, 1), 0)
    head_expand = ((iota((GLA_KW, GLA_VW), 0) // GLA_DK)
                   == (iota((GLA_KW, GLA_VW), 1) // GLA_DV)).astype(BF16)
    sum_j = ((iota((ch, ch * sb), 1) // sb) == iota((ch, ch * sb), 0)).astype(BF16)
    bd_ks = (iota((hs, GLA_KW), 0) // sb) == (iota((hs, GLA_KW), 1) // GLA_DK)
    bd_vs = (iota((hs, GLA_VW), 0) // sb) == (iota((hs, GLA_VW), 1) // GLA_DV)
    off_w = (nsb - 1) * hs
    att_keep = iota((ch, off_w), 0) >= (iota((ch, off_w), 1) // hs + 1) * sb
    dn_t = (((1,), (1,)), ((), ()))
    scale = GLA_DK ** -0.5

    def stage_decay(rows):
        x = jnp.dot(lr_ref[rows, :].astype(BF16), wg_ref[...], preferred_element_type=F32) + bg_ref[...]
        la = (jnp.minimum(x, 0.0) - jnp.log1p(jnp.exp(-jnp.abs(x)))) * (1.0 / GLA_GATE_TEMP)
        hi, mid, lo = _split3(la)
        return (jnp.dot(tri, hi, preferred_element_type=F32) + jnp.dot(tri, mid, preferred_element_type=F32)
                + jnp.dot(tri, lo, preferred_element_type=F32))

    def stage_pairwise(q, k, v, b):
        pieces, vrep = [], []
        for sbi in range(nsb):
            rs = slice(sbi * sb, (sbi + 1) * sb)
            qs, ks, bs = q[rs, :], k[rs, :], b[rs, :]
            for i in range(sb):
                dec = jnp.exp(jnp.where(sub_row <= i, bs[i:i + 1, :] - bs, NEG_INF))
                pieces.append(qs[i:i + 1, :] * ks * dec)
            vrep += [v[rs, :]] * sb
        x3 = jnp.concatenate(pieces, axis=0)
        x3h = x3.astype(BF16)
        return x3h, (x3 - x3h.astype(F32)).astype(BF16), jnp.concatenate(vrep, axis=0)

    def stage_head_sums(x3h, x3l):
        return (jnp.dot(x3h, head_expand, preferred_element_type=F32)
                + jnp.dot(x3l, head_expand, preferred_element_type=F32))

    def stage_cross_operands(q, k, vb, b):
        qts, kbds, vbds = [], [], []
        zeros_k = jnp.zeros((hs, GLA_KW), BF16)
        for sbi in range(nsb - 1):
            rs = slice(sbi * sb, (sbi + 1) * sb)
            end = (sbi + 1) * sb
            b_end = b[end - 1:end, :]
            qts.append((q * jnp.exp(jnp.minimum(b - b_end, 0.0))).astype(BF16))
            kt = (k[rs, :] * jnp.exp(b_end - b[rs, :])).astype(BF16)
            kbd = jnp.where(bd_ks, jnp.concatenate([kt] * GLA_HEADS, axis=0), jnp.zeros((), BF16))
            kbds.append(jnp.concatenate([kbd if s2 == sbi else zeros_k for s2 in range(nsb - 1)], axis=1))
            vbds.append(jnp.where(bd_vs, jnp.concatenate([vb[rs, :]] * GLA_HEADS, axis=0),
                                  jnp.zeros((), BF16)))
        return jnp.concatenate(qts, axis=1), jnp.concatenate(kbds, axis=0), jnp.concatenate(vbds, axis=0)

    def stage_cross_scores(q_all, k_all):
        att = lax.dot_general(q_all, k_all, dn_t, preferred_element_type=F32)
        return jnp.where(att_keep, att, 0.0).astype(BF16)

    def stage_state_update(k, v, b):
        b_last = b[ch - 1:ch, :]
        kl = (k * jnp.exp(b_last - b)).astype(BF16)
        ut = jnp.dot(v.T.astype(BF16), kl, preferred_element_type=F32)
        return jnp.exp(b_last), jnp.where(bd_k, ut, 0.0)

    def stage_mean_square(o):
        o2 = o * o
        o2h = o2.astype(BF16)
        o2l = (o2 - o2h.astype(F32)).astype(BF16)
        return (jnp.dot(o2h, seg, preferred_element_type=F32)
                + jnp.dot(o2l, seg, preferred_element_type=F32)) * (1.0 / GLA_DV)

    def stage_store(rows, o, ms):
        r = r_ref[rows, :]
        gate = r * (1.0 / (1.0 + jnp.exp(-r)))
        o_ref[rows, :] = o * lax.rsqrt(ms + EPS) * gn_ref[...] * gate

    each = lambda f, *cols: [f(*args) for args in zip(*cols)]
    gsize = min(GLA_GROUP, tc // ch)

    def group(g, carry):
        rows = [pl.ds(pl.multiple_of((g * gsize + c) * ch, ch), ch) for c in range(gsize)]
        b = each(stage_decay, rows)
        q = [q_ref[r, :] * scale for r in rows]
        k = [k_ref[r, :] for r in rows]
        v = [v_ref[r, :] for r in rows]
        vb = [x.astype(BF16) for x in v]
        x3h, x3l, vrep = zip(*each(stage_pairwise, q, k, v, b))
        w = each(stage_head_sums, x3h, x3l)
        wv = each(lambda w_, v_: (w_ * v_).astype(BF16), w, vrep)
        o = each(lambda wv_: jnp.dot(sum_j, wv_, preferred_element_type=F32), wv)
        q_all, k_all, v_all = zip(*each(stage_cross_operands, q, k, vb, b))
        att = each(stage_cross_scores, q_all, k_all)
        o = each(lambda o_, a_, v_: o_ + jnp.dot(a_, v_, preferred_element_type=F32), o, att, v_all)
        bq = each(lambda q_, b_: (q_ * jnp.exp(b_)).astype(BF16), q, b)
        decay, ut = zip(*each(stage_state_update, k, v, b))
        st = st_ref[...]
        states = []
        for c in range(gsize):
            states.append(st.astype(BF16))
            st = st * decay[c] + ut[c]
        st_ref[...] = st
        o = each(lambda o_, q_, s_: o_ + lax.dot_general(q_, s_, dn_t, preferred_element_type=F32),
                 o, bq, states)
        ms = each(stage_mean_square, o)
        each(stage_store, rows, o, ms)
        return carry

    lax.fori_loop(0, tc // ch // gsize, group, 0)


def _gla(proj, wg, bg, gn, batch, seq, tc):
    t = proj.shape[0]
    nb = seq // tc
    blk = lambda w, j: pl.BlockSpec((tc, w), lambda b, i, j=j: (b * nb + i, j))
    const = lambda r, c: pl.BlockSpec((r, c), lambda b, i: (0, 0))
    wg_pad = jnp.zeros((LANES, GLA_KW), F32).at[:GLA_GATE_RANK].set(wg).astype(BF16)
    gn_row = jnp.tile(gn, GLA_HEADS).reshape(1, GLA_VW)
    return pl.pallas_call(
        functools.partial(_gla_kernel, tc=tc),
        grid=(batch, nb),
        in_specs=[blk(GLA_KW, COL_G_Q), blk(GLA_KW, COL_G_K), blk(GLA_VW, COL_G_V), blk(GLA_VW, COL_G_R),
                  blk(LANES, COL_G_LR), const(LANES, GLA_KW), const(1, GLA_KW), const(1, GLA_VW)],
        out_specs=pl.BlockSpec((tc, GLA_VW), lambda b, i: (b * nb + i, 0)),
        out_shape=jax.ShapeDtypeStruct((t, GLA_VW), F32),
        scratch_shapes=[pltpu.VMEM((GLA_VW, GLA_KW), F32)],
        compiler_params=_cparams(("arbitrary", "arbitrary")),
    )(proj, proj, proj, proj, proj, wg_pad, bg.reshape(1, GLA_KW), gn_row)


def _out_proj_kernel(x_ref, a_ref, s_ref, g_ref, w_ref, o_ref):
    acc = jnp.dot(a_ref[...].astype(BF16), w_ref[0:DA_WIDTH, :], preferred_element_type=F32)
    acc += jnp.dot(s_ref[...].astype(BF16), w_ref[DA_WIDTH:DA_WIDTH + SC_WIDTH, :],
                   preferred_element_type=F32)
    acc += jnp.dot(g_ref[...].astype(BF16), w_ref[DA_WIDTH + SC_WIDTH:, :], preferred_element_type=F32)
    o_ref[...] = x_ref[...] + acc


def _out_proj(x, o_da, o_sc, o_g, w, tm):
    t, d = x.shape
    blk = lambda n: pl.BlockSpec((tm, n), lambda i: (i, 0))
    return pl.pallas_call(
        _out_proj_kernel,
        grid=(t // tm,),
        in_specs=[blk(d), blk(DA_WIDTH), blk(SC_WIDTH), blk(GLA_VW),
                  pl.BlockSpec(w.shape, lambda i: (0, 0))],
        out_specs=blk(d),
        out_shape=jax.ShapeDtypeStruct((t, d), F32),
        compiler_params=_cparams(("arbitrary",)),
    )(x, o_da, o_sc, o_g, w)


SUBLANES = 8


def _merge_exchange_network(n):
    pairs = []
    t = max(1, math.ceil(math.log2(n)))
    p = 2 ** (t - 1)
    while p > 0:
        q, r, d = 2 ** (t - 1), 0, p
        while d > 0:
            pairs += [(i, i + d) for i in range(n - d) if (i & p) == r]
            d, q, r = q - p, q // 2, p
        p //= 2
    return pairs


def _top_desc(x, n):
    groups = [x[SUBLANES * i:SUBLANES * (i + 1), :] for i in range(x.shape[0] // SUBLANES)]
    for i, j in _merge_exchange_network(len(groups)):
        groups[i], groups[j] = jnp.maximum(groups[i], groups[j]), jnp.minimum(groups[i], groups[j])
    vals = []
    for r in range(n):
        m = jnp.max(groups[0], axis=0, keepdims=True)
        vals.append(m)
        hit = groups[0] == m
        live = min(len(groups), n - r)
        for k in range(live - 1):
            groups[k] = jnp.where(hit, groups[k + 1], groups[k])
        if live == len(groups) and live > 0 and r + 1 < n:
            groups[live - 1] = jnp.where(hit, NEG_INF, groups[live - 1])
    return vals


def _ranks_of(x, vals):
    rank = jnp.full(x.shape, float(len(vals)), F32)
    for r in reversed(range(len(vals))):
        rank = jnp.where(x == vals[r], float(r), rank)
    return rank


def _peer_route_kernel(x_ref, g_ref, wq_ref, k1_ref, k2_ref, a1_ref, cnt_ref, a2_ref, rank2_ref):
    k = PEER_TOPK
    tt = x_ref.shape[0]
    xn = _rms(x_ref[...], g_ref[...]).astype(BF16)
    q = jnp.dot(xn, wq_ref[...], preferred_element_type=F32)
    dn = (((1,), (1,)), ((), ()))
    for h in range(PEER_HEADS):
        qa = q[:, (2 * h) * LANES:(2 * h + 1) * LANES].astype(BF16)
        qb = q[:, (2 * h + 1) * LANES:(2 * h + 2) * LANES].astype(BF16)
        s1 = lax.dot_general(k1_ref[...], qa, dn, preferred_element_type=F32)
        s2 = lax.dot_general(k2_ref[...], qb, dn, preferred_element_type=F32)
        v1 = _top_desc(s1, k)
        v2 = _top_desc(s2, k)
        rank2 = _ranks_of(s2, v2)
        cands = [v1[i] + v2[j] for i in range(k) for j in range(k // (i + 1))]
        pad = (-len(cands)) % 8
        cands += [jnp.full((1, tt), NEG_INF, F32)] * pad
        top = _top_desc(jnp.concatenate(cands, axis=0), k)
        z = jnp.ones((1, tt), F32)
        for r in range(1, k):
            z = z + jnp.exp(top[r] - top[0])
        thr = top[k - 1]
        v1s = jnp.concatenate(v1, axis=0)
        cnt_rank = jnp.zeros_like(v1s)
        for j in range(k):
            cnt_rank = cnt_rank + ((v1s + v2[j]) >= thr).astype(F32)
        cnt = jnp.zeros_like(s1)
        for i in range(k):
            cnt = jnp.where(s1 == v1[i], cnt_rank[i:i + 1, :], cnt)
        cnt_ref[h] = cnt
        rank2_ref[h] = rank2.astype(BF16)
        a1_ref[h] = jnp.exp(s1 - v1[0]) * (1.0 / z)
        a2_ref[h] = jnp.exp(s2 - v2[0]).astype(BF16)


def _peer_route(x, g, wq, k1, k2, tt):
    t, d = x.shape
    shape = (PEER_HEADS, PEER_NKEYS, t)
    out = pl.BlockSpec((PEER_HEADS, PEER_NKEYS, tt), lambda i: (0, 0, i))
    f32 = jax.ShapeDtypeStruct(shape, F32)
    bf16 = jax.ShapeDtypeStruct(shape, BF16)
    return pl.pallas_call(
        _peer_route_kernel,
        grid=(t // tt,),
        in_specs=[pl.BlockSpec((tt, d), lambda i: (i, 0)),
                  pl.BlockSpec((1, d), lambda i: (0, 0)),
                  pl.BlockSpec(wq.shape, lambda i: (0, 0)),
                  pl.BlockSpec(k1.shape, lambda i: (0, 0)),
                  pl.BlockSpec(k2.shape, lambda i: (0, 0))],
        out_specs=[out, out, out, out],
        out_shape=[f32, f32, bf16, bf16],
        compiler_params=_cparams(("arbitrary",)),
    )(x, g.reshape(1, d), wq, k1, k2)


def _peer_dense_kernel(x_ref, g_ref, u_ref, vt_ref, a1_ref, cnt_ref, a2_ref, rank2_ref, gout_ref, o_ref,
                       xnt_ref, acc_ref, hta_ref, htb_ref, pta_ref, ptb_ref, *, norm_out):
    e = pl.program_id(1)

    @pl.when(e == 0)
    def _():
        xn = _rms(x_ref[...], g_ref[...])
        xnt_ref[...] = xn.T.astype(BF16)
        acc_ref[...] = jnp.zeros_like(acc_ref)

    inv_sqrt2 = 1.0 / math.sqrt(2.0)
    zero = jnp.zeros((), BF16)
    n_q, job_rows, _ = u_ref.shape
    sub = xnt_ref.shape[1] // 2
    n1_per_job = job_rows // PEER_NKEYS
    pack = BF16_SUBLANES
    groups = PEER_NKEYS // pack
    col_a, col_b = slice(0, sub), slice(sub, 2 * sub)
    ht_refs = {col_a: hta_ref, col_b: htb_ref}
    pt_refs = {col_a: pta_ref, col_b: ptb_ref}

    def matmul1(q, cols):
        ht_refs[cols][...] = jnp.dot(u_ref[q], xnt_ref[:, cols], preferred_element_type=F32)

    def gate(q, cols):
        ht_ref, pt_ref = ht_refs[cols], pt_refs[cols]
        for rl in range(n1_per_job):
            r = q * n1_per_job + rl
            w = None
            for h in range(PEER_HEADS):
                cnt = jnp.broadcast_to(cnt_ref[h, pl.ds(r, 1), cols], (pack, sub)).astype(BF16)
                a1 = jnp.broadcast_to(a1_ref[h, pl.ds(r, 1), cols], (pack, sub)).astype(BF16)
                sel = rank2_ref[h, :, :, cols] < cnt[None]
                term = a1[None] * jnp.where(sel, a2_ref[h, :, :, cols], zero)
                w = term if w is None else w + term
            rows = slice(rl * PEER_NKEYS, (rl + 1) * PEER_NKEYS)
            hb = ht_ref[rows, :]
            gelu = 0.5 * hb * (1.0 + lax.erf(hb * inv_sqrt2))
            p = w * gelu.astype(BF16).reshape(groups, pack, sub)
            pt_ref[rows, :] = p.reshape(PEER_NKEYS, sub)

    def matmul2(q, cols):
        acc_ref[:, cols] += jnp.dot(vt_ref[q], pt_refs[cols][...], preferred_element_type=F32)

    def pair(q, has_prev, has_next):
        matmul1(q, col_b)
        gate(q, col_a)
        if has_prev:
            matmul2(q - 1, col_b)
        if has_next:
            matmul1(q + 1, col_a)
        gate(q, col_b)
        matmul2(q, col_a)

    matmul1(0, col_a)
    if n_q == 1:
        pair(0, False, False)
    else:
        pair(0, False, True)

        def body(q, carry):
            pair(q, True, True)
            return carry

        lax.fori_loop(1, n_q - 1, body, 0)
        pair(n_q - 1, True, False)
    matmul2(n_q - 1, col_b)

    @pl.when(e == pl.num_programs(1) - 1)
    def _():
        y = x_ref[...] + acc_ref[...].T
        o_ref[...] = _rms(y, gout_ref[...]) if norm_out else y


PEER_JOB_ROWS = 512


def _expert_jobs_kernel(u_ref, v_ref, uo_ref, vo_ref):
    uo_ref[0] = u_ref[0].astype(BF16)
    vo_ref[0] = v_ref[0].T.astype(BF16)


def _expert_jobs(u, v, layer):
    _, n_exp, d = u.shape
    n_jobs = n_exp // PEER_JOB_ROWS
    blk = pl.BlockSpec((1, PEER_JOB_ROWS, d), lambda j: (layer, j, 0))
    return pl.pallas_call(
        _expert_jobs_kernel,
        grid=(n_jobs,),
        in_specs=[blk, blk],
        out_specs=[pl.BlockSpec((1, PEER_JOB_ROWS, d), lambda j: (j, 0, 0)),
                   pl.BlockSpec((1, d, PEER_JOB_ROWS), lambda j: (j, 0, 0))],
        out_shape=[jax.ShapeDtypeStruct((n_jobs, PEER_JOB_ROWS, d), BF16),
                   jax.ShapeDtypeStruct((n_jobs, d, PEER_JOB_ROWS), BF16)],
        compiler_params=_cparams(("arbitrary",)),
    )(u, v)


def _peer_dense(x, g, u, vt, a1t, cnt, a2t, rank2, g_out, norm_out, tt, et):
    t, d = x.shape
    n_jobs, job_rows, _ = u.shape
    n_q = et // job_rows
    n1_per_tile = et // PEER_NKEYS
    groups = PEER_NKEYS // BF16_SUBLANES
    grouped = (PEER_HEADS, groups, BF16_SUBLANES, t)
    full = pl.BlockSpec((PEER_HEADS, groups, BF16_SUBLANES, tt), lambda i, e: (0, 0, 0, i))
    part = pl.BlockSpec((PEER_HEADS, n1_per_tile, tt), lambda i, e: (0, e, i))
    sub = tt // 2
    row = pl.BlockSpec((1, d), lambda i, e: (0, 0))
    return pl.pallas_call(
        functools.partial(_peer_dense_kernel, norm_out=norm_out),
        grid=(t // tt, n_jobs // n_q),
        in_specs=[pl.BlockSpec((tt, d), lambda i, e: (i, 0)),
                  row,
                  pl.BlockSpec((n_q, job_rows, d), lambda i, e: (e, 0, 0)),
                  pl.BlockSpec((n_q, d, job_rows), lambda i, e: (e, 0, 0)),
                  part, part, full, full, row],
        out_specs=pl.BlockSpec((tt, d), lambda i, e: (i, 0)),
        out_shape=jax.ShapeDtypeStruct((t, d), F32),
        scratch_shapes=[pltpu.VMEM((d, tt), BF16), pltpu.VMEM((d, tt), F32),
                        pltpu.VMEM((job_rows, sub), F32), pltpu.VMEM((job_rows, sub), F32),
                        pltpu.VMEM((job_rows, sub), BF16), pltpu.VMEM((job_rows, sub), BF16)],
        compiler_params=_cparams(("arbitrary", "arbitrary"), vmem_mb=52),
    )(x, g.reshape(1, d), u, vt, a1t, cnt, a2t.reshape(grouped), rank2.reshape(grouped),
      g_out.reshape(1, d))


def _tiles(seq):
    pick = lambda want: min(want, seq)
    return dict(tm=pick(256), tq=pick(512), tconv=pick(512), tgla=pick(512), tt=pick(256), td=pick(512),
                et=2048)


def kernel(x, positions, norm_mix, w_in, lam_q1, lam_k1, lam_q2, lam_k2, diff_norm, conv_w, gla_w_gate2, gla_b_gate, gla_norm, w_out, norm_ffn, peer_w_q, peer_keys1, peer_keys2, peer_u, peer_v, norm_final):
    batch, seq, d = x.shape
    t = batch * seq
    depth = w_in.shape[0]
    tl = _tiles(seq)
    xt = x.reshape(t, d)
    tabs = _rope_tables(positions, tl["tm"])
    for i in range(depth):
        w_in_p = jnp.pad(w_in[i], ((0, 0), (0, IN_PAD - IN_WIDTH))).astype(BF16)
        qt, kr, vt3, proj = _in_proj(xt, norm_mix[i], w_in_p, tabs, tl["tq"])
        lam_init = 0.8 - 0.6 * math.exp(-0.3 * i)
        o_da = _diff_attention(qt, kr, vt3, lam_q1[i], lam_k1[i], lam_q2[i], lam_k2[i], diff_norm[i],
                               batch, seq, lam_init, tl["tq"])
        o_sc = _short_conv(proj, conv_w[i], seq, tl["tconv"])
        o_g = _gla(proj, gla_w_gate2[i], gla_b_gate[i], gla_norm[i], batch, seq, tl["tgla"])
        xt = _out_proj(xt, o_da, o_sc, o_g, w_out[i].astype(BF16), tl["tm"])
        a1t, cnt, a2t, rank2 = _peer_route(xt, norm_ffn[i], peer_w_q[i].astype(BF16),
                                           peer_keys1[i].astype(BF16), peer_keys2[i].astype(BF16), tl["tt"])
        u3, ev3 = _expert_jobs(peer_u, peer_v, i)
        xt = _peer_dense(xt, norm_ffn[i], u3, ev3, a1t, cnt, a2t, rank2, norm_final, i == depth - 1,
                         tl["td"], tl["et"])
    return xt.reshape(batch, seq, d)
```

```python
import functools
import math

import jax
import jax.numpy as jnp
from jax import lax
from jax.experimental import pallas as pl
from jax.experimental.pallas import tpu as pltpu

F32 = jnp.float32
BF16 = jnp.bfloat16

EPS = 1e-6
HEAD_DIM = 64
DA_HEADS = 4
DA_WIDTH = 512
ROPE_THETA = 500000.0
ROT_DIM = 16
SC_WIDTH = 256
CONV_W = 3
GLA_HEADS = 4
GLA_DK = 32
GLA_DV = 64
GLA_KW = 128
GLA_VW = 256
GLA_GATE_RANK = 16
GLA_GATE_TEMP = 16.0
GLA_CHUNK = 64
GLA_SUB = 16
GLA_GROUP = 8
IN_WIDTH = 3088
IN_PAD = 3200
PEER_HEADS = 8
PEER_NKEYS = 128
PEER_TOPK = 16
PEER_JOB_ROWS = 512
LANES = 128
BF16_SUBLANES = 16
NEG_INF = float("-inf")
INV_SQRT2 = 1.0 / math.sqrt(2.0)

QKV_WIDTH = 3 * DA_WIDTH
REST_WIDTH = IN_PAD - QKV_WIDTH
COL_SC_B, COL_SC_C, COL_SC_H = 0, 1, 2
COL_G_V, COL_G_R = 4, 5
COL_G_Q, COL_G_K, COL_G_LR = 6, 7, 12


VMEM_LIMIT_MB = 40
VMEM_LIMIT_DENSE_MB = 52


def _cparams(sem, vmem_mb=VMEM_LIMIT_MB):
    return pltpu.CompilerParams(dimension_semantics=sem, vmem_limit_bytes=vmem_mb * 1024 * 1024)


def _rms(x, g):
    return x * lax.rsqrt(jnp.mean(x * x, axis=-1, keepdims=True) + EPS) * g


def _rope_table_kernel(pos_ref, inv_ref, mrot_ref, m1_ref, m2_ref, c_ref, s1_ref, s2_ref):
    ang = pos_ref[...] * inv_ref[...]
    c = jnp.cos(ang)
    s = jnp.sin(ang)
    mrot = mrot_ref[...]
    c_ref[...] = mrot * c + (1.0 - mrot)
    s1_ref[...] = -(m1_ref[...] * s)
    s2_ref[...] = m2_ref[...] * s


def _rope_tables(positions, tm):
    t = positions.size
    pos = jnp.broadcast_to(positions.reshape(t, 1).astype(F32), (t, LANES))
    d = jnp.arange(LANES) % HEAD_DIM
    half = ROT_DIM // 2
    inv = ROPE_THETA ** (-jnp.arange(0, ROT_DIM, 2, dtype=F32) / ROT_DIM)
    inv_row = jnp.where(d < ROT_DIM, inv[d % half], 0.0).astype(F32).reshape(1, LANES)
    mrot = (d < ROT_DIM).astype(F32).reshape(1, LANES)
    m1 = (d < half).astype(F32).reshape(1, LANES)
    m2 = ((d >= half) & (d < ROT_DIM)).astype(F32).reshape(1, LANES)
    row = pl.BlockSpec((1, LANES), lambda i: (0, 0))
    blk = pl.BlockSpec((tm, LANES), lambda i: (i, 0))
    return pl.pallas_call(
        _rope_table_kernel,
        grid=(t // tm,),
        in_specs=[blk, row, row, row, row],
        out_specs=[blk, blk, blk],
        out_shape=[jax.ShapeDtypeStruct((t, LANES), F32)] * 3,
        compiler_params=_cparams(("arbitrary",)),
    )(pos, inv_row, mrot, m1, m2)


def _in_proj_kernel(x_ref, g_ref, w_ref, c_ref, s1_ref, s2_ref, qt_ref, ko_ref, vt_ref, rest_ref,
                    *, qscale):
    xn = _rms(x_ref[...], g_ref[...]).astype(BF16)
    rest_ref[...] = jnp.dot(xn, w_ref[:, QKV_WIDTH:], preferred_element_type=F32)
    c = c_ref[...]
    s1 = s1_ref[...]
    s2 = s2_ref[...]
    half = ROT_DIM // 2

    def rope(x):
        return x * c + pltpu.roll(x, LANES - half, 1) * s1 + pltpu.roll(x, half, 1) * s2

    qkv = jnp.dot(xn, w_ref[:, :QKV_WIDTH], preferred_element_type=F32)
    for g in range(DA_WIDTH // LANES):
        sl = slice(LANES * g, LANES * (g + 1))
        col = lambda base: slice(base + LANES * g, base + LANES * (g + 1))
        qt_ref[sl, :] = (rope(qkv[:, col(0)]) * qscale).T.astype(BF16)
        ko_ref[:, sl] = rope(qkv[:, col(DA_WIDTH)]).astype(BF16)
        vt_ref[0, sl, :] = qkv[:, col(2 * DA_WIDTH)].T.astype(BF16)


def _in_proj(x, g, w, tabs, tm):
    t, d = x.shape
    c, s1, s2 = tabs
    tab = pl.BlockSpec((tm, LANES), lambda i: (i, 0))
    return pl.pallas_call(
        functools.partial(_in_proj_kernel, qscale=HEAD_DIM ** -0.5 * math.log2(math.e)),
        grid=(t // tm,),
        in_specs=[pl.BlockSpec((tm, d), lambda i: (i, 0)),
                  pl.BlockSpec((1, d), lambda i: (0, 0)),
                  pl.BlockSpec(w.shape, lambda i: (0, 0)),
                  tab, tab, tab],
        out_specs=[pl.BlockSpec((DA_WIDTH, tm), lambda i: (0, i)),
                   pl.BlockSpec((tm, DA_WIDTH), lambda i: (i, 0)),
                   pl.BlockSpec((1, DA_WIDTH, tm), lambda i: (i, 0, 0)),
                   pl.BlockSpec((tm, REST_WIDTH), lambda i: (i, 0))],
        out_shape=[jax.ShapeDtypeStruct((DA_WIDTH, t), BF16),
                   jax.ShapeDtypeStruct((t, DA_WIDTH), BF16),
                   jax.ShapeDtypeStruct((t // tm, DA_WIDTH, tm), BF16),
                   jax.ShapeDtypeStruct((t, REST_WIDTH), F32)],
        compiler_params=_cparams(("arbitrary",)),
    )(x, g.reshape(1, d), w, c, s1, s2)


def _attn_kernel(qt_ref, k_ref, vt_ref, lq1_ref, lk1_ref, lq2_ref, lk2_ref, gain_ref, o_ref,
                 acc1_ref, acc2_ref, m1_ref, l1_ref, m2_ref, l2_ref, s1a_ref, s2a_ref, s1b_ref, s2b_ref,
                 *, tq, lam_init):
    qi = pl.program_id(2)
    qt = qt_ref[...]
    row = lax.broadcasted_iota(jnp.int32, (LANES, 1), 0)
    zero = jnp.zeros_like(qt)
    qa = jnp.where(row < HEAD_DIM, qt, zero)
    qb = jnp.where(row >= HEAD_DIM, qt, zero)
    acc1_ref[...] = jnp.zeros_like(acc1_ref)
    acc2_ref[...] = jnp.zeros_like(acc2_ref)

    for st_ref in (m1_ref, m2_ref):
        st_ref[...] = jnp.full(st_ref.shape, NEG_INF, F32)
    for st_ref in (l1_ref, l2_ref):
        st_ref[...] = jnp.zeros_like(st_ref)
    maps = ((qa, m1_ref, l1_ref, acc1_ref), (qb, m2_ref, l2_ref, acc2_ref))

    def scores(j, bufs):
        off = pl.multiple_of(j * tq, tq)
        kk = k_ref[pl.ds(off, tq), :]
        for (qm, _, _, _), s_ref in zip(maps, bufs):
            s_ref[...] = jnp.dot(kk, qm, preferred_element_type=F32)

    def absorb(j, bufs, masked):
        vt = vt_ref[j]
        for (_, m_ref, l_ref, acc_ref), s_ref in zip(maps, bufs):
            s = s_ref[...]
            if masked:
                r = lax.broadcasted_iota(jnp.int32, (tq, tq), 0)
                c = lax.broadcasted_iota(jnp.int32, (tq, tq), 1)
                s = jnp.where(r <= c, s, NEG_INF)
            m = m_ref[...]
            mn = jnp.maximum(m, jnp.max(s, axis=0, keepdims=True))
            alpha = jnp.exp2(m - mn)
            p = jnp.exp2(s - mn)
            m_ref[...] = mn
            l_ref[...] = alpha * l_ref[...] + jnp.sum(p, axis=0, keepdims=True)
            acc_ref[...] = alpha * acc_ref[...] + jnp.dot(vt, p.astype(BF16),
                                                          preferred_element_type=F32)

    buf_a, buf_b = (s1a_ref, s2a_ref), (s1b_ref, s2b_ref)
    scores(0, buf_a)

    def pair(i, carry):
        j = 2 * i
        scores(j + 1, buf_b)
        absorb(j, buf_a, False)
        scores(j + 2, buf_a)
        absorb(j + 1, buf_b, False)
        return carry

    lax.fori_loop(0, qi // 2, pair, 0)

    @pl.when(qi % 2 == 1)
    def _():
        scores(qi, buf_b)
        absorb(qi - 1, buf_a, False)
        absorb(qi, buf_b, True)

    @pl.when(qi % 2 == 0)
    def _():
        absorb(qi, buf_a, True)

    l1, l2 = l1_ref[...], l2_ref[...]

    lam = (jnp.exp(jnp.sum(lq1_ref[...] * lk1_ref[...], axis=-1, keepdims=True))
           - jnp.exp(jnp.sum(lq2_ref[...] * lk2_ref[...], axis=-1, keepdims=True)) + lam_init)
    o = acc1_ref[...] / l1 - lam * (acc2_ref[...] / l2)
    y = o * lax.rsqrt(jnp.mean(o * o, axis=0, keepdims=True) + EPS) * gain_ref[...]
    o_ref[...] = (y * (1.0 - lam_init)).T


def _diff_attention(qt, kr, vt3, lq1, lk1, lq2, lk2, gain, batch, seq, lam_init, tq):
    t = kr.shape[0]
    nq = seq // tq
    vec = lambda n: pl.BlockSpec((1, n), lambda b, h, i: (0, 0))
    return pl.pallas_call(
        functools.partial(_attn_kernel, tq=tq, lam_init=lam_init),
        grid=(batch, DA_HEADS, nq),
        in_specs=[pl.BlockSpec((LANES, tq), lambda b, h, i: (h, b * nq + i)),
                  pl.BlockSpec((seq, LANES), lambda b, h, i: (b, h)),
                  pl.BlockSpec((nq, LANES, tq), lambda b, h, i: (b, h, 0)),
                  vec(HEAD_DIM), vec(HEAD_DIM), vec(HEAD_DIM), vec(HEAD_DIM),
                  pl.BlockSpec((LANES, 1), lambda b, h, i: (0, 0))],
        out_specs=pl.BlockSpec((tq, LANES), lambda b, h, i: (b * nq + i, h)),
        out_shape=jax.ShapeDtypeStruct((t, DA_WIDTH), F32),
        scratch_shapes=([pltpu.VMEM((LANES, tq), F32)] * 2 + [pltpu.VMEM((1, tq), F32)] * 4
                        + [pltpu.VMEM((tq, tq), F32)] * 4),
        compiler_params=_cparams(("arbitrary", "arbitrary", "arbitrary")),
    )(qt, kr, vt3, lq1.reshape(1, -1), lk1.reshape(1, -1), lq2.reshape(1, -1), lk2.reshape(1, -1),
      gain.reshape(-1, 1))


def _short_conv_kernel(b_ref, c_ref, h_ref, cp_ref, hp_ref, w_ref, o_ref, *, tm, blocks_per_seq):
    i = pl.program_id(0)
    z = c_ref[...] * h_ref[...]
    zp = cp_ref[...] * hp_ref[...]
    first = (i % blocks_per_seq) == 0
    zp = jnp.where(first, jnp.zeros_like(zp), zp)
    row = lax.broadcasted_iota(jnp.int32, (tm, 1), 0)
    z1 = jnp.where(row == 0, zp[7:8, :], pltpu.roll(z, 1, 0))
    z2 = pltpu.roll(z, 2, 0)
    z2 = jnp.where(row == 0, zp[6:7, :], jnp.where(row == 1, zp[7:8, :], z2))
    w = w_ref[...]
    y = w[0:1, :] * z2 + w[1:2, :] * z1 + w[2:3, :] * z
    o_ref[...] = b_ref[...] * y


def _short_conv(proj, w, seq, tm):
    t = proj.shape[0]
    rb = tm // 8
    cur = lambda j: pl.BlockSpec((tm, SC_WIDTH), lambda i, j=j: (i, j))
    prev = lambda j: pl.BlockSpec((8, SC_WIDTH), lambda i, j=j: (jnp.maximum(i * rb - 1, 0), j))
    return pl.pallas_call(
        functools.partial(_short_conv_kernel, tm=tm, blocks_per_seq=seq // tm),
        grid=(t // tm,),
        in_specs=[cur(COL_SC_B), cur(COL_SC_C), cur(COL_SC_H), prev(COL_SC_C), prev(COL_SC_H),
                  pl.BlockSpec((CONV_W, SC_WIDTH), lambda i: (0, 0))],
        out_specs=pl.BlockSpec((tm, SC_WIDTH), lambda i: (i, 0)),
        out_shape=jax.ShapeDtypeStruct((t, SC_WIDTH), F32),
        compiler_params=_cparams(("arbitrary",)),
    )(proj, proj, proj, proj, proj, w)


def _split3(x):
    hi = x.astype(BF16)
    r = x - hi.astype(F32)
    mid = r.astype(BF16)
    lo = (r - mid.astype(F32)).astype(BF16)
    return hi, mid, lo


def _gla_kernel(q_ref, k_ref, v_ref, r_ref, lr_ref, wg_ref, bg_ref, gn_ref, o_ref, st_ref, *, tc):
    ch = GLA_CHUNK

    @pl.when(pl.program_id(1) == 0)
    def _():
        st_ref[...] = jnp.zeros_like(st_ref)

    ii = lax.broadcasted_iota(jnp.int32, (ch, ch), 0)
    jj = lax.broadcasted_iota(jnp.int32, (ch, ch), 1)
    tri = (ii >= jj).astype(BF16)
    rk = lax.broadcasted_iota(jnp.int32, (GLA_VW, GLA_KW), 0)
    ck = lax.broadcasted_iota(jnp.int32, (GLA_VW, GLA_KW), 1)
    bd_k = (rk // GLA_DV) == (ck // GLA_DK)
    rv = lax.broadcasted_iota(jnp.int32, (GLA_VW, GLA_VW), 0)
    cv = lax.broadcasted_iota(jnp.int32, (GLA_VW, GLA_VW), 1)
    bd_v = (rv // GLA_DV) == (cv // GLA_DV)
    seg = bd_v.astype(BF16)
    sb = GLA_SUB
    nsb = ch // sb
    hs = GLA_HEADS * sb
    iota = lambda shape, axis: lax.broadcasted_iota(jnp.int32, shape, axis)
    sub_row = iota((sb, 1), 0)
    head_expand = ((iota((GLA_KW, GLA_VW), 0) // GLA_DK)
                   == (iota((GLA_KW, GLA_VW), 1) // GLA_DV)).astype(BF16)
    sum_j = ((iota((ch, ch * sb), 1) // sb) == iota((ch, ch * sb), 0)).astype(BF16)
    bd_ks = (iota((hs, GLA_KW), 0) // sb) == (iota((hs, GLA_KW), 1) // GLA_DK)
    bd_vs = (iota((hs, GLA_VW), 0) // sb) == (iota((hs, GLA_VW), 1) // GLA_DV)
    off_w = (nsb - 1) * hs
    att_keep = iota((ch, off_w), 0) >= (iota((ch, off_w), 1) // hs + 1) * sb
    dn_t = (((1,), (1,)), ((), ()))
    scale = GLA_DK ** -0.5

    def stage_decay(rows):
        x = jnp.dot(lr_ref[rows, :].astype(BF16), wg_ref[...], preferred_element_type=F32) + bg_ref[...]
        la = (jnp.minimum(x, 0.0) - jnp.log1p(jnp.exp(-jnp.abs(x)))) * (1.0 / GLA_GATE_TEMP)
        hi, mid, lo = _split3(la)
        return (jnp.dot(tri, hi, preferred_element_type=F32) + jnp.dot(tri, mid, preferred_element_type=F32)
                + jnp.dot(tri, lo, preferred_element_type=F32))

    def stage_pairwise(q, k, v, b):
        pieces, vrep = [], []
        for sbi in range(nsb):
            rs = slice(sbi * sb, (sbi + 1) * sb)
            qs, ks, bs = q[rs, :], k[rs, :], b[rs, :]
            for i in range(sb):
                dec = jnp.exp(jnp.where(sub_row <= i, bs[i:i + 1, :] - bs, NEG_INF))
                pieces.append(qs[i:i + 1, :] * ks * dec)
            vrep += [v[rs, :]] * sb
        x3 = jnp.concatenate(pieces, axis=0)
        x3h = x3.astype(BF16)
        return x3h, (x3 - x3h.astype(F32)).astype(BF16), jnp.concatenate(vrep, axis=0)

    def stage_head_sums(x3h, x3l):
        return (jnp.dot(x3h, head_expand, preferred_element_type=F32)
                + jnp.dot(x3l, head_expand, preferred_element_type=F32))

    def stage_cross_operands(q, k, vb, b):
        qts, kbds, vbds = [], [], []
        zeros_k = jnp.zeros((hs, GLA_KW), BF16)
        for sbi in range(nsb - 1):
            rs = slice(sbi * sb, (sbi + 1) * sb)
            end = (sbi + 1) * sb
            b_end = b[end - 1:end, :]
            qts.append((q * jnp.exp(jnp.minimum(b - b_end, 0.0))).astype(BF16))
            kt = (k[rs, :] * jnp.exp(b_end - b[rs, :])).astype(BF16)
            kbd = jnp.where(bd_ks, jnp.concatenate([kt] * GLA_HEADS, axis=0), jnp.zeros((), BF16))
            kbds.append(jnp.concatenate([kbd if s2 == sbi else zeros_k for s2 in range(nsb - 1)], axis=1))
            vbds.append(jnp.where(bd_vs, jnp.concatenate([vb[rs, :]] * GLA_HEADS, axis=0),
                                  jnp.zeros((), BF16)))
        return jnp.concatenate(qts, axis=1), jnp.concatenate(kbds, axis=0), jnp.concatenate(vbds, axis=0)

    def stage_cross_scores(q_all, k_all):
        att = lax.dot_general(q_all, k_all, dn_t, preferred_element_type=F32)
        return jnp.where(att_keep, att, 0.0).astype(BF16)

    def stage_state_update(k, v, b):
        b_last = b[ch - 1:ch, :]
        kl = (k * jnp.exp(b_last - b)).astype(BF16)
        ut = jnp.dot(v.T.astype(BF16), kl, preferred_element_type=F32)
        return jnp.exp(b_last), jnp.where(bd_k, ut, 0.0)

    def stage_mean_square(o):
        o2 = o * o
        o2h = o2.astype(BF16)
        o2l = (o2 - o2h.astype(F32)).astype(BF16)
        return (jnp.dot(o2h, seg, preferred_element_type=F32)
                + jnp.dot(o2l, seg, preferred_element_type=F32)) * (1.0 / GLA_DV)

    def stage_store(rows, o, ms):
        r = r_ref[rows, :]
        gate = r * (1.0 / (1.0 + jnp.exp(-r)))
        o_ref[rows, :] = o * lax.rsqrt(ms + EPS) * gn_ref[...] * gate

    each = lambda f, *cols: [f(*args) for args in zip(*cols)]
    gsize = min(GLA_GROUP, tc // ch)

    def group(g, carry):
        rows = [pl.ds(pl.multiple_of((g * gsize + c) * ch, ch), ch) for c in range(gsize)]
        b = each(stage_decay, rows)
        q = [q_ref[r, :] * scale for r in rows]
        k = [k_ref[r, :] for r in rows]
        v = [v_ref[r, :] for r in rows]
        vb = [x.astype(BF16) for x in v]
        x3h, x3l, vrep = zip(*each(stage_pairwise, q, k, v, b))
        w = each(stage_head_sums, x3h, x3l)
        wv = each(lambda w_, v_: (w_ * v_).astype(BF16), w, vrep)
        o = each(lambda wv_: jnp.dot(sum_j, wv_, preferred_element_type=F32), wv)
        q_all, k_all, v_all = zip(*each(stage_cross_operands, q, k, vb, b))
        att = each(stage_cross_scores, q_all, k_all)
        o = each(lambda o_, a_, v_: o_ + jnp.dot(a_, v_, preferred_element_type=F32), o, att, v_all)
        bq = each(lambda q_, b_: (q_ * jnp.exp(b_)).astype(BF16), q, b)
        decay, ut = zip(*each(stage_state_update, k, v, b))
        st = st_ref[...]
        states = []
        for c in range(gsize):
            states.append(st.astype(BF16))
            st = st * decay[c] + ut[c]
        st_ref[...] = st
        o = each(lambda o_, q_, s_: o_ + lax.dot_general(q_, s_, dn_t, preferred_element_type=F32),
                 o, bq, states)
        ms = each(stage_mean_square, o)
        each(stage_store, rows, o, ms)
        return carry

    lax.fori_loop(0, tc // ch // gsize, group, 0)


def _gla(proj, wg, bg, gn, batch, seq, tc):
    t = proj.shape[0]
    nb = seq // tc
    blk = lambda w, j: pl.BlockSpec((tc, w), lambda b, i, j=j: (b * nb + i, j))
    const = lambda r, c: pl.BlockSpec((r, c), lambda b, i: (0, 0))
    wg_pad = jnp.zeros((LANES, GLA_KW), F32).at[:GLA_GATE_RANK].set(wg).astype(BF16)
    gn_row = jnp.tile(gn, GLA_HEADS).reshape(1, GLA_VW)
    return pl.pallas_call(
        functools.partial(_gla_kernel, tc=tc),
        grid=(batch, nb),
        in_specs=[blk(GLA_KW, COL_G_Q), blk(GLA_KW, COL_G_K), blk(GLA_VW, COL_G_V), blk(GLA_VW, COL_G_R),
                  blk(LANES, COL_G_LR), const(LANES, GLA_KW), const(1, GLA_KW), const(1, GLA_VW)],
        out_specs=pl.BlockSpec((tc, GLA_VW), lambda b, i: (b * nb + i, 0)),
        out_shape=jax.ShapeDtypeStruct((t, GLA_VW), F32),
        scratch_shapes=[pltpu.VMEM((GLA_VW, GLA_KW), F32)],
        compiler_params=_cparams(("arbitrary", "arbitrary")),
    )(proj, proj, proj, proj, proj, wg_pad, bg.reshape(1, GLA_KW), gn_row)


def _out_proj_kernel(x_ref, a_ref, s_ref, g_ref, w_ref, o_ref):
    acc = jnp.dot(a_ref[...].astype(BF16), w_ref[0:DA_WIDTH, :], preferred_element_type=F32)
    acc += jnp.dot(s_ref[...].astype(BF16), w_ref[DA_WIDTH:DA_WIDTH + SC_WIDTH, :],
                   preferred_element_type=F32)
    acc += jnp.dot(g_ref[...].astype(BF16), w_ref[DA_WIDTH + SC_WIDTH:, :], preferred_element_type=F32)
    o_ref[...] = x_ref[...] + acc


def _out_proj(x, o_da, o_sc, o_g, w, tm):
    t, d = x.shape
    blk = lambda n: pl.BlockSpec((tm, n), lambda i: (i, 0))
    return pl.pallas_call(
        _out_proj_kernel,
        grid=(t // tm,),
        in_specs=[blk(d), blk(DA_WIDTH), blk(SC_WIDTH), blk(GLA_VW),
                  pl.BlockSpec(w.shape, lambda i: (0, 0))],
        out_specs=blk(d),
        out_shape=jax.ShapeDtypeStruct((t, d), F32),
        compiler_params=_cparams(("arbitrary",)),
    )(x, o_da, o_sc, o_g, w)


SUBLANES = 8


def _merge_exchange_network(n):
    pairs = []
    t = max(1, math.ceil(math.log2(n)))
    p = 2 ** (t - 1)
    while p > 0:
        q, r, d = 2 ** (t - 1), 0, p
        while d > 0:
            pairs += [(i, i + d) for i in range(n - d) if (i & p) == r]
            d, q, r = q - p, q // 2, p
        p //= 2
    return pairs


def _top_desc(x, n):
    groups = [x[SUBLANES * i:SUBLANES * (i + 1), :] for i in range(x.shape[0] // SUBLANES)]
    for i, j in _merge_exchange_network(len(groups)):
        groups[i], groups[j] = jnp.maximum(groups[i], groups[j]), jnp.minimum(groups[i], groups[j])
    vals = []
    for r in range(n):
        m = jnp.max(groups[0], axis=0, keepdims=True)
        vals.append(m)
        hit = groups[0] == m
        live = min(len(groups), n - r)
        for k in range(live - 1):
            groups[k] = jnp.where(hit, groups[k + 1], groups[k])
        if live == len(groups) and live > 0 and r + 1 < n:
            groups[live - 1] = jnp.where(hit, NEG_INF, groups[live - 1])
    return vals


def _ranks_of(x, vals):
    rank = jnp.full(x.shape, float(len(vals)), F32)
    for r in reversed(range(len(vals))):
        rank = jnp.where(x == vals[r], float(r), rank)
    return rank


def _peer_route_kernel(x_ref, g_ref, wq_ref, k1_ref, k2_ref, a1_ref, cnt_ref, a2_ref, rank2_ref):
    k = PEER_TOPK
    tt = x_ref.shape[0]
    xn = _rms(x_ref[...], g_ref[...]).astype(BF16)
    q = jnp.dot(xn, wq_ref[...], preferred_element_type=F32)
    dn = (((1,), (1,)), ((), ()))
    for h in range(PEER_HEADS):
        qa = q[:, (2 * h) * LANES:(2 * h + 1) * LANES].astype(BF16)
        qb = q[:, (2 * h + 1) * LANES:(2 * h + 2) * LANES].astype(BF16)
        s1 = lax.dot_general(k1_ref[...], qa, dn, preferred_element_type=F32)
        s2 = lax.dot_general(k2_ref[...], qb, dn, preferred_element_type=F32)
        v1 = _top_desc(s1, k)
        v2 = _top_desc(s2, k)
        rank2 = _ranks_of(s2, v2)
        cands = [v1[i] + v2[j] for i in range(k) for j in range(k // (i + 1))]
        pad = (-len(cands)) % 8
        cands += [jnp.full((1, tt), NEG_INF, F32)] * pad
        top = _top_desc(jnp.concatenate(cands, axis=0), k)
        z = jnp.ones((1, tt), F32)
        for r in range(1, k):
            z = z + jnp.exp(top[r] - top[0])
        thr = top[k - 1]
        v1s = jnp.concatenate(v1, axis=0)
        cnt_rank = jnp.zeros_like(v1s)
        for j in range(k):
            cnt_rank = cnt_rank + ((v1s + v2[j]) >= thr).astype(F32)
        cnt = jnp.zeros_like(s1)
        for i in range(k):
            cnt = jnp.where(s1 == v1[i], cnt_rank[i:i + 1, :], cnt)
        cnt_ref[h] = cnt
        rank2_ref[h] = rank2.astype(BF16)
        a1_ref[h] = jnp.exp(s1 - v1[0]) * (1.0 / z)
        a2_ref[h] = jnp.exp(s2 - v2[0]).astype(BF16)


def _peer_route(x, g, wq, k1, k2, tt):
    t, d = x.shape
    shape = (PEER_HEADS, PEER_NKEYS, t)
    out = pl.BlockSpec((PEER_HEADS, PEER_NKEYS, tt), lambda i: (0, 0, i))
    f32 = jax.ShapeDtypeStruct(shape, F32)
    bf16 = jax.ShapeDtypeStruct(shape, BF16)
    return pl.pallas_call(
        _peer_route_kernel,
        grid=(t // tt,),
        in_specs=[pl.BlockSpec((tt, d), lambda i: (i, 0)),
                  pl.BlockSpec((1, d), lambda i: (0, 0)),
                  pl.BlockSpec(wq.shape, lambda i: (0, 0)),
                  pl.BlockSpec(k1.shape, lambda i: (0, 0)),
                  pl.BlockSpec(k2.shape, lambda i: (0, 0))],
        out_specs=[out, out, out, out],
        out_shape=[f32, f32, bf16, bf16],
        compiler_params=_cparams(("arbitrary",)),
    )(x, g.reshape(1, d), wq, k1, k2)


def _peer_dense_kernel(x_ref, g_ref, u_ref, vt_ref, a1_ref, cnt_ref, a2_ref, rank2_ref, gout_ref, o_ref,
                       xnt_ref, acc_ref, hta_ref, htb_ref, pta_ref, ptb_ref, *, norm_out):
    e = pl.program_id(1)

    @pl.when(e == 0)
    def _():
        xn = _rms(x_ref[...], g_ref[...])
        xnt_ref[...] = xn.T.astype(BF16)
        acc_ref[...] = jnp.zeros_like(acc_ref)

    zero = jnp.zeros((), BF16)
    n_q, job_rows, _ = u_ref.shape
    sub = xnt_ref.shape[1] // 2
    n1_per_job = job_rows // PEER_NKEYS
    pack = BF16_SUBLANES
    groups = PEER_NKEYS // pack
    col_a, col_b = slice(0, sub), slice(sub, 2 * sub)
    ht_refs = {col_a: hta_ref, col_b: htb_ref}
    pt_refs = {col_a: pta_ref, col_b: ptb_ref}

    def matmul1(q, cols):
        ht_refs[cols][...] = jnp.dot(u_ref[q], xnt_ref[:, cols], preferred_element_type=F32)

    def gate(q, cols):
        ht_ref, pt_ref = ht_refs[cols], pt_refs[cols]
        for rl in range(n1_per_job):
            r = q * n1_per_job + rl
            w = None
            for h in range(PEER_HEADS):
                cnt = jnp.broadcast_to(cnt_ref[h, pl.ds(r, 1), cols], (pack, sub)).astype(BF16)
                a1 = jnp.broadcast_to(a1_ref[h, pl.ds(r, 1), cols], (pack, sub)).astype(BF16)
                sel = rank2_ref[h, :, :, cols] < cnt[None]
                term = a1[None] * jnp.where(sel, a2_ref[h, :, :, cols], zero)
                w = term if w is None else w + term
            rows = slice(rl * PEER_NKEYS, (rl + 1) * PEER_NKEYS)
            hb = ht_ref[rows, :]
            gelu = hb * (1.0 + lax.erf(hb))
            p = w * gelu.astype(BF16).reshape(groups, pack, sub)
            pt_ref[rows, :] = p.reshape(PEER_NKEYS, sub)

    def matmul2(q, cols):
        acc_ref[:, cols] += jnp.dot(vt_ref[q], pt_refs[cols][...], preferred_element_type=F32)

    def pair(q, has_prev, has_next):
        matmul1(q, col_b)
        gate(q, col_a)
        if has_prev:
            matmul2(q - 1, col_b)
        if has_next:
            matmul1(q + 1, col_a)
        gate(q, col_b)
        matmul2(q, col_a)

    matmul1(0, col_a)
    if n_q == 1:
        pair(0, False, False)
    else:
        pair(0, False, True)

        def body(q, carry):
            pair(q, True, True)
            return carry

        lax.fori_loop(1, n_q - 1, body, 0)
        pair(n_q - 1, True, False)
    matmul2(n_q - 1, col_b)

    @pl.when(e == pl.num_programs(1) - 1)
    def _():
        y = x_ref[...] + acc_ref[...].T
        o_ref[...] = _rms(y, gout_ref[...]) if norm_out else y


def _expert_jobs_kernel(u_ref, v_ref, uo_ref, vo_ref):
    uo_ref[0] = (u_ref[0] * INV_SQRT2).astype(BF16)
    vo_ref[0] = (v_ref[0].T * (0.5 / INV_SQRT2)).astype(BF16)


def _expert_jobs(u, v, layer):
    _, n_exp, d = u.shape
    n_jobs = n_exp // PEER_JOB_ROWS
    blk = pl.BlockSpec((1, PEER_JOB_ROWS, d), lambda j: (layer, j, 0))
    return pl.pallas_call(
        _expert_jobs_kernel,
        grid=(n_jobs,),
        in_specs=[blk, blk],
        out_specs=[pl.BlockSpec((1, PEER_JOB_ROWS, d), lambda j: (j, 0, 0)),
                   pl.BlockSpec((1, d, PEER_JOB_ROWS), lambda j: (j, 0, 0))],
        out_shape=[jax.ShapeDtypeStruct((n_jobs, PEER_JOB_ROWS, d), BF16),
                   jax.ShapeDtypeStruct((n_jobs, d, PEER_JOB_ROWS), BF16)],
        compiler_params=_cparams(("arbitrary",)),
    )(u, v)


def _peer_dense(x, g, u, vt, a1t, cnt, a2t, rank2, g_out, norm_out, tt, et):
    t, d = x.shape
    n_jobs, job_rows, _ = u.shape
    n_q = et // job_rows
    n1_per_tile = et // PEER_NKEYS
    groups = PEER_NKEYS // BF16_SUBLANES
    grouped = (PEER_HEADS, groups, BF16_SUBLANES, t)
    full = pl.BlockSpec((PEER_HEADS, groups, BF16_SUBLANES, tt), lambda i, e: (0, 0, 0, i))
    part = pl.BlockSpec((PEER_HEADS, n1_per_tile, tt), lambda i, e: (0, e, i))
    sub = tt // 2
    row = pl.BlockSpec((1, d), lambda i, e: (0, 0))
    return pl.pallas_call(
        functools.partial(_peer_dense_kernel, norm_out=norm_out),
        grid=(t // tt, n_jobs // n_q),
        in_specs=[pl.BlockSpec((tt, d), lambda i, e: (i, 0)),
                  row,
                  pl.BlockSpec((n_q, job_rows, d), lambda i, e: (e, 0, 0)),
                  pl.BlockSpec((n_q, d, job_rows), lambda i, e: (e, 0, 0)),
                  part, part, full, full, row],
        out_specs=pl.BlockSpec((tt, d), lambda i, e: (i, 0)),
        out_shape=jax.ShapeDtypeStruct((t, d), F32),
        scratch_shapes=[pltpu.VMEM((d, tt), BF16), pltpu.VMEM((d, tt), F32),
                        pltpu.VMEM((job_rows, sub), F32), pltpu.VMEM((job_rows, sub), F32),
                        pltpu.VMEM((job_rows, sub), BF16), pltpu.VMEM((job_rows, sub), BF16)],
        compiler_params=_cparams(("arbitrary", "arbitrary"), vmem_mb=VMEM_LIMIT_DENSE_MB),
    )(x, g.reshape(1, d), u, vt, a1t, cnt, a2t.reshape(grouped), rank2.reshape(grouped),
      g_out.reshape(1, d))


def _tiles(seq):
    pick = lambda want: min(want, seq)
    return dict(tm=pick(256), tq=pick(512), tconv=pick(512), tgla=pick(512), tt=pick(256), td=pick(512),
                et=2048)


def kernel(x, positions, norm_mix, w_in, lam_q1, lam_k1, lam_q2, lam_k2, diff_norm, conv_w, gla_w_gate2, gla_b_gate, gla_norm, w_out, norm_ffn, peer_w_q, peer_keys1, peer_keys2, peer_u, peer_v, norm_final):
    batch, seq, d = x.shape
    t = batch * seq
    depth = w_in.shape[0]
    tl = _tiles(seq)
    xt = x.reshape(t, d)
    tabs = _rope_tables(positions, tl["tm"])
    for i in range(depth):
        w_in_p = jnp.pad(w_in[i], ((0, 0), (0, IN_PAD - IN_WIDTH))).astype(BF16)
        qt, kr, vt3, proj = _in_proj(xt, norm_mix[i], w_in_p, tabs, tl["tq"])
        lam_init = 0.8 - 0.6 * math.exp(-0.3 * i)
        o_da = _diff_attention(qt, kr, vt3, lam_q1[i], lam_k1[i], lam_q2[i], lam_k2[i], diff_norm[i],
                               batch, seq, lam_init, tl["tq"])
        o_sc = _short_conv(proj, conv_w[i], seq, tl["tconv"])
        o_g = _gla(proj, gla_w_gate2[i], gla_b_gate[i], gla_norm[i], batch, seq, tl["tgla"])
        xt = _out_proj(xt, o_da, o_sc, o_g, w_out[i].astype(BF16), tl["tm"])
        a1t, cnt, a2t, rank2 = _peer_route(xt, norm_ffn[i], peer_w_q[i].astype(BF16),
                                           peer_keys1[i].astype(BF16), peer_keys2[i].astype(BF16), tl["tt"])
        u3, ev3 = _expert_jobs(peer_u, peer_v, i)
        xt = _peer_dense(xt, norm_ffn[i], u3, ev3, a1t, cnt, a2t, rank2, norm_final, i == depth - 1,
                         tl["td"], tl["et"])
    return xt.reshape(batch, seq, d)
```

```python
import functools
import math

import jax
import jax.numpy as jnp
from jax import lax
from jax.experimental import pallas as pl
from jax.experimental.pallas import tpu as pltpu

F32 = jnp.float32
BF16 = jnp.bfloat16

EPS = 1e-6
HEAD_DIM = 64
DA_HEADS = 4
DA_WIDTH = 512
ROPE_THETA = 500000.0
ROT_DIM = 16
SC_WIDTH = 256
CONV_W = 3
GLA_HEADS = 4
GLA_DK = 32
GLA_DV = 64
GLA_KW = 128
GLA_VW = 256
GLA_GATE_RANK = 16
GLA_GATE_TEMP = 16.0
GLA_CHUNK = 64
GLA_SUB = 16
GLA_GROUP = 8
IN_WIDTH = 3088
IN_PAD = 3200
PEER_HEADS = 8
PEER_NKEYS = 128
PEER_TOPK = 16
PEER_JOB_ROWS = 512
LANES = 128
BF16_SUBLANES = 16
NEG_INF = float("-inf")
INV_SQRT2 = 1.0 / math.sqrt(2.0)

QKV_WIDTH = 3 * DA_WIDTH
REST_WIDTH = IN_PAD - QKV_WIDTH
COL_SC_B, COL_SC_C, COL_SC_H = 0, 1, 2
COL_G_V, COL_G_R = 4, 5
COL_G_Q, COL_G_K, COL_G_LR = 6, 7, 12


VMEM_LIMIT_MB = 40
VMEM_LIMIT_DENSE_MB = 52


def _cparams(sem, vmem_mb=VMEM_LIMIT_MB):
    return pltpu.CompilerParams(dimension_semantics=sem, vmem_limit_bytes=vmem_mb * 1024 * 1024)


def _rms(x, g):
    return x * lax.rsqrt(jnp.mean(x * x, axis=-1, keepdims=True) + EPS) * g


def _rope_table_kernel(pos_ref, inv_ref, mrot_ref, m1_ref, m2_ref, c_ref, s1_ref, s2_ref):
    ang = pos_ref[...] * inv_ref[...]
    c = jnp.cos(ang)
    s = jnp.sin(ang)
    mrot = mrot_ref[...]
    c_ref[...] = mrot * c + (1.0 - mrot)
    s1_ref[...] = -(m1_ref[...] * s)
    s2_ref[...] = m2_ref[...] * s


def _rope_tables(positions, tm):
    t = positions.size
    pos = jnp.broadcast_to(positions.reshape(t, 1).astype(F32), (t, LANES))
    d = jnp.arange(LANES) % HEAD_DIM
    half = ROT_DIM // 2
    inv = ROPE_THETA ** (-jnp.arange(0, ROT_DIM, 2, dtype=F32) / ROT_DIM)
    inv_row = jnp.where(d < ROT_DIM, inv[d % half], 0.0).astype(F32).reshape(1, LANES)
    mrot = (d < ROT_DIM).astype(F32).reshape(1, LANES)
    m1 = (d < half).astype(F32).reshape(1, LANES)
    m2 = ((d >= half) & (d < ROT_DIM)).astype(F32).reshape(1, LANES)
    row = pl.BlockSpec((1, LANES), lambda i: (0, 0))
    blk = pl.BlockSpec((tm, LANES), lambda i: (i, 0))
    return pl.pallas_call(
        _rope_table_kernel,
        grid=(t // tm,),
        in_specs=[blk, row, row, row, row],
        out_specs=[blk, blk, blk],
        out_shape=[jax.ShapeDtypeStruct((t, LANES), F32)] * 3,
        compiler_params=_cparams(("arbitrary",)),
    )(pos, inv_row, mrot, m1, m2)


def _in_proj_kernel(x_ref, g_ref, w_ref, c_ref, s1_ref, s2_ref, qt_ref, ko_ref, vt_ref, rest_ref,
                    *, qscale):
    xn = _rms(x_ref[...], g_ref[...]).astype(BF16)
    rest_ref[...] = jnp.dot(xn, w_ref[:, QKV_WIDTH:], preferred_element_type=F32)
    c = c_ref[...]
    s1 = s1_ref[...]
    s2 = s2_ref[...]
    half = ROT_DIM // 2

    def rope(x):
        return x * c + pltpu.roll(x, LANES - half, 1) * s1 + pltpu.roll(x, half, 1) * s2

    qkv = jnp.dot(xn, w_ref[:, :QKV_WIDTH], preferred_element_type=F32)
    for g in range(DA_WIDTH // LANES):
        sl = slice(LANES * g, LANES * (g + 1))
        col = lambda base: slice(base + LANES * g, base + LANES * (g + 1))
        qt_ref[sl, :] = (rope(qkv[:, col(0)]) * qscale).T.astype(BF16)
        ko_ref[:, sl] = rope(qkv[:, col(DA_WIDTH)]).astype(BF16)
        vt_ref[0, sl, :] = qkv[:, col(2 * DA_WIDTH)].T.astype(BF16)


def _in_proj(x, g, w, tabs, tm):
    t, d = x.shape
    c, s1, s2 = tabs
    tab = pl.BlockSpec((tm, LANES), lambda i: (i, 0))
    return pl.pallas_call(
        functools.partial(_in_proj_kernel, qscale=HEAD_DIM ** -0.5 * math.log2(math.e)),
        grid=(t // tm,),
        in_specs=[pl.BlockSpec((tm, d), lambda i: (i, 0)),
                  pl.BlockSpec((1, d), lambda i: (0, 0)),
                  pl.BlockSpec(w.shape, lambda i: (0, 0)),
                  tab, tab, tab],
        out_specs=[pl.BlockSpec((DA_WIDTH, tm), lambda i: (0, i)),
                   pl.BlockSpec((tm, DA_WIDTH), lambda i: (i, 0)),
                   pl.BlockSpec((1, DA_WIDTH, tm), lambda i: (i, 0, 0)),
                   pl.BlockSpec((tm, REST_WIDTH), lambda i: (i, 0))],
        out_shape=[jax.ShapeDtypeStruct((DA_WIDTH, t), BF16),
                   jax.ShapeDtypeStruct((t, DA_WIDTH), BF16),
                   jax.ShapeDtypeStruct((t // tm, DA_WIDTH, tm), BF16),
                   jax.ShapeDtypeStruct((t, REST_WIDTH), F32)],
        compiler_params=_cparams(("arbitrary",)),
    )(x, g.reshape(1, d), w, c, s1, s2)


def _attn_kernel(qt_ref, k_ref, vt_ref, lq1_ref, lk1_ref, lq2_ref, lk2_ref, gain_ref, o_ref,
                 acc1_ref, acc2_ref, m1_ref, l1_ref, m2_ref, l2_ref, s1a_ref, s2a_ref, s1b_ref, s2b_ref,
                 *, tq, lam_init):
    qi = pl.program_id(2)
    qt = qt_ref[...]
    row = lax.broadcasted_iota(jnp.int32, (LANES, 1), 0)
    zero = jnp.zeros_like(qt)
    qa = jnp.where(row < HEAD_DIM, qt, zero)
    qb = jnp.where(row >= HEAD_DIM, qt, zero)
    acc1_ref[...] = jnp.zeros_like(acc1_ref)
    acc2_ref[...] = jnp.zeros_like(acc2_ref)

    for st_ref in (m1_ref, m2_ref):
        st_ref[...] = jnp.full(st_ref.shape, NEG_INF, F32)
    for st_ref in (l1_ref, l2_ref):
        st_ref[...] = jnp.zeros_like(st_ref)
    maps = ((qa, m1_ref, l1_ref, acc1_ref), (qb, m2_ref, l2_ref, acc2_ref))

    def scores(j, bufs):
        off = pl.multiple_of(j * tq, tq)
        kk = k_ref[pl.ds(off, tq), :]
        for (qm, _, _, _), s_ref in zip(maps, bufs):
            s_ref[...] = jnp.dot(kk, qm, preferred_element_type=F32)

    def absorb(j, bufs, masked):
        vt = vt_ref[j]
        for (_, m_ref, l_ref, acc_ref), s_ref in zip(maps, bufs):
            s = s_ref[...]
            if masked:
                r = lax.broadcasted_iota(jnp.int32, (tq, tq), 0)
                c = lax.broadcasted_iota(jnp.int32, (tq, tq), 1)
                s = jnp.where(r <= c, s, NEG_INF)
            m = m_ref[...]
            mn = jnp.maximum(m, jnp.max(s, axis=0, keepdims=True))
            alpha = jnp.exp2(m - mn)
            p = jnp.exp2(s - mn)
            m_ref[...] = mn
            l_ref[...] = alpha * l_ref[...] + jnp.sum(p, axis=0, keepdims=True)
            acc_ref[...] = alpha * acc_ref[...] + jnp.dot(vt, p.astype(BF16),
                                                          preferred_element_type=F32)

    buf_a, buf_b = (s1a_ref, s2a_ref), (s1b_ref, s2b_ref)
    scores(0, buf_a)

    def pair(i, carry):
        j = 2 * i
        scores(j + 1, buf_b)
        absorb(j, buf_a, False)
        scores(j + 2, buf_a)
        absorb(j + 1, buf_b, False)
        return carry

    lax.fori_loop(0, qi // 2, pair, 0)

    @pl.when(qi % 2 == 1)
    def _():
        scores(qi, buf_b)
        absorb(qi - 1, buf_a, False)
        absorb(qi, buf_b, True)

    @pl.when(qi % 2 == 0)
    def _():
        absorb(qi, buf_a, True)

    l1, l2 = l1_ref[...], l2_ref[...]

    lam = (jnp.exp(jnp.sum(lq1_ref[...] * lk1_ref[...], axis=-1, keepdims=True))
           - jnp.exp(jnp.sum(lq2_ref[...] * lk2_ref[...], axis=-1, keepdims=True)) + lam_init)
    o = acc1_ref[...] / l1 - lam * (acc2_ref[...] / l2)
    y = o * lax.rsqrt(jnp.mean(o * o, axis=0, keepdims=True) + EPS) * gain_ref[...]
    o_ref[...] = (y * (1.0 - lam_init)).T


def _diff_attention(qt, kr, vt3, lq1, lk1, lq2, lk2, gain, batch, seq, lam_init, tq):
    t = kr.shape[0]
    nq = seq // tq
    vec = lambda n: pl.BlockSpec((1, n), lambda b, h, i: (0, 0))
    return pl.pallas_call(
        functools.partial(_attn_kernel, tq=tq, lam_init=lam_init),
        grid=(batch, DA_HEADS, nq),
        in_specs=[pl.BlockSpec((LANES, tq), lambda b, h, i: (h, b * nq + i)),
                  pl.BlockSpec((seq, LANES), lambda b, h, i: (b, h)),
                  pl.BlockSpec((nq, LANES, tq), lambda b, h, i: (b, h, 0)),
                  vec(HEAD_DIM), vec(HEAD_DIM), vec(HEAD_DIM), vec(HEAD_DIM),
                  pl.BlockSpec((LANES, 1), lambda b, h, i: (0, 0))],
        out_specs=pl.BlockSpec((tq, LANES), lambda b, h, i: (b * nq + i, h)),
        out_shape=jax.ShapeDtypeStruct((t, DA_WIDTH), F32),
        scratch_shapes=([pltpu.VMEM((LANES, tq), F32)] * 2 + [pltpu.VMEM((1, tq), F32)] * 4
                        + [pltpu.VMEM((tq, tq), F32)] * 4),
        compiler_params=_cparams(("arbitrary", "arbitrary", "arbitrary")),
    )(qt, kr, vt3, lq1.reshape(1, -1), lk1.reshape(1, -1), lq2.reshape(1, -1), lk2.reshape(1, -1),
      gain.reshape(-1, 1))


def _short_conv_kernel(b_ref, c_ref, h_ref, cp_ref, hp_ref, w_ref, o_ref, *, tm, blocks_per_seq):
    i = pl.program_id(0)
    z = c_ref[...] * h_ref[...]
    zp = cp_ref[...] * hp_ref[...]
    first = (i % blocks_per_seq) == 0
    zp = jnp.where(first, jnp.zeros_like(zp), zp)
    row = lax.broadcasted_iota(jnp.int32, (tm, 1), 0)
    z1 = jnp.where(row == 0, zp[7:8, :], pltpu.roll(z, 1, 0))
    z2 = pltpu.roll(z, 2, 0)
    z2 = jnp.where(row == 0, zp[6:7, :], jnp.where(row == 1, zp[7:8, :], z2))
    w = w_ref[...]
    y = w[0:1, :] * z2 + w[1:2, :] * z1 + w[2:3, :] * z
    o_ref[...] = b_ref[...] * y


def _short_conv(proj, w, seq, tm):
    t = proj.shape[0]
    rb = tm // 8
    cur = lambda j: pl.BlockSpec((tm, SC_WIDTH), lambda i, j=j: (i, j))
    prev = lambda j: pl.BlockSpec((8, SC_WIDTH), lambda i, j=j: (jnp.maximum(i * rb - 1, 0), j))
    return pl.pallas_call(
        functools.partial(_short_conv_kernel, tm=tm, blocks_per_seq=seq // tm),
        grid=(t // tm,),
        in_specs=[cur(COL_SC_B), cur(COL_SC_C), cur(COL_SC_H), prev(COL_SC_C), prev(COL_SC_H),
                  pl.BlockSpec((CONV_W, SC_WIDTH), lambda i: (0, 0))],
        out_specs=pl.BlockSpec((tm, SC_WIDTH), lambda i: (i, 0)),
        out_shape=jax.ShapeDtypeStruct((t, SC_WIDTH), F32),
        compiler_params=_cparams(("arbitrary",)),
    )(proj, proj, proj, proj, proj, w)


def _split3(x):
    hi = x.astype(BF16)
    r = x - hi.astype(F32)
    mid = r.astype(BF16)
    lo = (r - mid.astype(F32)).astype(BF16)
    return hi, mid, lo


def _gla_kernel(q_ref, k_ref, v_ref, r_ref, lr_ref, wg_ref, bg_ref, gn_ref, o_ref, st_ref, *, tc):
    ch = GLA_CHUNK

    @pl.when(pl.program_id(1) == 0)
    def _():
        st_ref[...] = jnp.zeros_like(st_ref)

    ii = lax.broadcasted_iota(jnp.int32, (ch, ch), 0)
    jj = lax.broadcasted_iota(jnp.int32, (ch, ch), 1)
    tri = (ii >= jj).astype(BF16)
    rk = lax.broadcasted_iota(jnp.int32, (GLA_VW, GLA_KW), 0)
    ck = lax.broadcasted_iota(jnp.int32, (GLA_VW, GLA_KW), 1)
    bd_k = (rk // GLA_DV) == (ck // GLA_DK)
    rv = lax.broadcasted_iota(jnp.int32, (GLA_VW, GLA_VW), 0)
    cv = lax.broadcasted_iota(jnp.int32, (GLA_VW, GLA_VW), 1)
    bd_v = (rv // GLA_DV) == (cv // GLA_DV)
    seg = bd_v.astype(BF16)
    sb = GLA_SUB
    nsb = ch // sb
    hs = GLA_HEADS * sb
    iota = lambda shape, axis: lax.broadcasted_iota(jnp.int32, shape, axis)
    sub_row = iota((sb, 1), 0)
    head_expand = ((iota((GLA_KW, GLA_VW), 0) // GLA_DK)
                   == (iota((GLA_KW, GLA_VW), 1) // GLA_DV)).astype(BF16)
    sum_j = ((iota((ch, ch * sb), 1) // sb) == iota((ch, ch * sb), 0)).astype(BF16)
    bd_ks = (iota((hs, GLA_KW), 0) // sb) == (iota((hs, GLA_KW), 1) // GLA_DK)
    bd_vs = (iota((hs, GLA_VW), 0) // sb) == (iota((hs, GLA_VW), 1) // GLA_DV)
    off_w = (nsb - 1) * hs
    att_keep = iota((ch, off_w), 0) >= (iota((ch, off_w), 1) // hs + 1) * sb
    dn_t = (((1,), (1,)), ((), ()))
    scale = GLA_DK ** -0.5

    def stage_decay(rows):
        x = jnp.dot(lr_ref[rows, :].astype(BF16), wg_ref[...], preferred_element_type=F32) + bg_ref[...]
        la = (jnp.minimum(x, 0.0) - jnp.log1p(jnp.exp(-jnp.abs(x)))) * (1.0 / GLA_GATE_TEMP)
        hi, mid, lo = _split3(la)
        return (jnp.dot(tri, hi, preferred_element_type=F32) + jnp.dot(tri, mid, preferred_element_type=F32)
                + jnp.dot(tri, lo, preferred_element_type=F32))

    def stage_pairwise(q, k, v, b):
        pieces, vrep = [], []
        for sbi in range(nsb):
            rs = slice(sbi * sb, (sbi + 1) * sb)
            qs, ks, bs = q[rs, :], k[rs, :], b[rs, :]
            for i in range(sb):
                dec = jnp.exp(jnp.where(sub_row <= i, bs[i:i + 1, :] - bs, NEG_INF))
                pieces.append(qs[i:i + 1, :] * ks * dec)
            vrep += [v[rs, :]] * sb
        x3 = jnp.concatenate(pieces, axis=0)
        x3h = x3.astype(BF16)
        return x3h, (x3 - x3h.astype(F32)).astype(BF16), jnp.concatenate(vrep, axis=0)

    def stage_head_sums(x3h, x3l):
        return (jnp.dot(x3h, head_expand, preferred_element_type=F32)
                + jnp.dot(x3l, head_expand, preferred_element_type=F32))

    def stage_cross_operands(q, k, vb, b):
        qts, kbds, vbds = [], [], []
        zeros_k = jnp.zeros((hs, GLA_KW), BF16)
        for sbi in range(nsb - 1):
            rs = slice(sbi * sb, (sbi + 1) * sb)
            end = (sbi + 1) * sb
            b_end = b[end - 1:end, :]
            qts.append((q * jnp.exp(jnp.minimum(b - b_end, 0.0))).astype(BF16))
            kt = (k[rs, :] * jnp.exp(b_end - b[rs, :])).astype(BF16)
            kbd = jnp.where(bd_ks, jnp.concatenate([kt] * GLA_HEADS, axis=0), jnp.zeros((), BF16))
            kbds.append(jnp.concatenate([kbd if s2 == sbi else zeros_k for s2 in range(nsb - 1)], axis=1))
            vbds.append(jnp.where(bd_vs, jnp.concatenate([vb[rs, :]] * GLA_HEADS, axis=0),
                                  jnp.zeros((), BF16)))
        return jnp.concatenate(qts, axis=1), jnp.concatenate(kbds, axis=0), jnp.concatenate(vbds, axis=0)

    def stage_cross_scores(q_all, k_all):
        att = lax.dot_general(q_all, k_all, dn_t, preferred_element_type=F32)
        return jnp.where(att_keep, att, 0.0).astype(BF16)

    def stage_state_update(k, v, b):
        b_last = b[ch - 1:ch, :]
        kl = (k * jnp.exp(b_last - b)).astype(BF16)
        ut = jnp.dot(v.T.astype(BF16), kl, preferred_element_type=F32)
        return jnp.exp(b_last), jnp.where(bd_k, ut, 0.0)

    def stage_mean_square(o):
        o2 = o * o
        o2h = o2.astype(BF16)
        o2l = (o2 - o2h.astype(F32)).astype(BF16)
        return (jnp.dot(o2h, seg, preferred_element_type=F32)
                + jnp.dot(o2l, seg, preferred_element_type=F32)) * (1.0 / GLA_DV)

    def stage_store(rows, o, ms):
        r = r_ref[rows, :]
        gate = r * (1.0 / (1.0 + jnp.exp(-r)))
        o_ref[rows, :] = o * lax.rsqrt(ms + EPS) * gn_ref[...] * gate

    each = lambda f, *cols: [f(*args) for args in zip(*cols)]
    gsize = min(GLA_GROUP, tc // ch)

    def group(g, carry):
        rows = [pl.ds(pl.multiple_of((g * gsize + c) * ch, ch), ch) for c in range(gsize)]
        b = each(stage_decay, rows)
        q = [q_ref[r, :] * scale for r in rows]
        k = [k_ref[r, :] for r in rows]
        v = [v_ref[r, :] for r in rows]
        vb = [x.astype(BF16) for x in v]
        x3h, x3l, vrep = zip(*each(stage_pairwise, q, k, v, b))
        w = each(stage_head_sums, x3h, x3l)
        wv = each(lambda w_, v_: (w_ * v_).astype(BF16), w, vrep)
        o = each(lambda wv_: jnp.dot(sum_j, wv_, preferred_element_type=F32), wv)
        q_all, k_all, v_all = zip(*each(stage_cross_operands, q, k, vb, b))
        att = each(stage_cross_scores, q_all, k_all)
        o = each(lambda o_, a_, v_: o_ + jnp.dot(a_, v_, preferred_element_type=F32), o, att, v_all)
        bq = each(lambda q_, b_: (q_ * jnp.exp(b_)).astype(BF16), q, b)
        decay, ut = zip(*each(stage_state_update, k, v, b))
        st = st_ref[...]
        states = []
        for c in range(gsize):
            states.append(st.astype(BF16))
            st = st * decay[c] + ut[c]
        st_ref[...] = st
        o = each(lambda o_, q_, s_: o_ + lax.dot_general(q_, s_, dn_t, preferred_element_type=F32),
                 o, bq, states)
        ms = each(stage_mean_square, o)
        each(stage_store, rows, o, ms)
        return carry

    lax.fori_loop(0, tc // ch // gsize, group, 0)


def _gla(proj, wg, bg, gn, batch, seq, tc):
    t = proj.shape[0]
    nb = seq // tc
    blk = lambda w, j: pl.BlockSpec((tc, w), lambda b, i, j=j: (b * nb + i, j))
    const = lambda r, c: pl.BlockSpec((r, c), lambda b, i: (0, 0))
    wg_pad = jnp.zeros((LANES, GLA_KW), F32).at[:GLA_GATE_RANK].set(wg).astype(BF16)
    gn_row = jnp.tile(gn, GLA_HEADS).reshape(1, GLA_VW)
    return pl.pallas_call(
        functools.partial(_gla_kernel, tc=tc),
        grid=(batch, nb),
        in_specs=[blk(GLA_KW, COL_G_Q), blk(GLA_KW, COL_G_K), blk(GLA_VW, COL_G_V), blk(GLA_VW, COL_G_R),
                  blk(LANES, COL_G_LR), const(LANES, GLA_KW), const(1, GLA_KW), const(1, GLA_VW)],
        out_specs=pl.BlockSpec((tc, GLA_VW), lambda b, i: (b * nb + i, 0)),
        out_shape=jax.ShapeDtypeStruct((t, GLA_VW), F32),
        scratch_shapes=[pltpu.VMEM((GLA_VW, GLA_KW), F32)],
        compiler_params=_cparams(("arbitrary", "arbitrary")),
    )(proj, proj, proj, proj, proj, wg_pad, bg.reshape(1, GLA_KW), gn_row)


def _out_proj_kernel(x_ref, a_ref, s_ref, g_ref, w_ref, o_ref):
    acc = jnp.dot(a_ref[...].astype(BF16), w_ref[0:DA_WIDTH, :], preferred_element_type=F32)
    acc += jnp.dot(s_ref[...].astype(BF16), w_ref[DA_WIDTH:DA_WIDTH + SC_WIDTH, :],
                   preferred_element_type=F32)
    acc += jnp.dot(g_ref[...].astype(BF16), w_ref[DA_WIDTH + SC_WIDTH:, :], preferred_element_type=F32)
    o_ref[...] = x_ref[...] + acc


def _out_proj(x, o_da, o_sc, o_g, w, tm):
    t, d = x.shape
    blk = lambda n: pl.BlockSpec((tm, n), lambda i: (i, 0))
    return pl.pallas_call(
        _out_proj_kernel,
        grid=(t // tm,),
        in_specs=[blk(d), blk(DA_WIDTH), blk(SC_WIDTH), blk(GLA_VW),
                  pl.BlockSpec(w.shape, lambda i: (0, 0))],
        out_specs=blk(d),
        out_shape=jax.ShapeDtypeStruct((t, d), F32),
        compiler_params=_cparams(("arbitrary",)),
    )(x, o_da, o_sc, o_g, w)


SUBLANES = 8


def _merge_exchange_network(n):
    pairs = []
    t = max(1, math.ceil(math.log2(n)))
    p = 2 ** (t - 1)
    while p > 0:
        q, r, d = 2 ** (t - 1), 0, p
        while d > 0:
            pairs += [(i, i + d) for i in range(n - d) if (i & p) == r]
            d, q, r = q - p, q // 2, p
        p //= 2
    return pairs


def _top_desc(x, n):
    groups = [x[SUBLANES * i:SUBLANES * (i + 1), :] for i in range(x.shape[0] // SUBLANES)]
    for i, j in _merge_exchange_network(len(groups)):
        groups[i], groups[j] = jnp.maximum(groups[i], groups[j]), jnp.minimum(groups[i], groups[j])
    vals = []
    for r in range(n):
        m = jnp.max(groups[0], axis=0, keepdims=True)
        vals.append(m)
        hit = groups[0] == m
        live = min(len(groups), n - r)
        for k in range(live - 1):
            groups[k] = jnp.where(hit, groups[k + 1], groups[k])
        if live == len(groups) and live > 0 and r + 1 < n:
            groups[live - 1] = jnp.where(hit, NEG_INF, groups[live - 1])
    return vals


def _count_true_prefix(test, rows):
    assert len(rows) == 16
    pick = lambda c, a, b: jnp.where(c, a, b)
    c8 = test(rows[7])
    c4 = test(pick(c8, rows[11], rows[3]))
    c2 = test(pick(c8, pick(c4, rows[13], rows[9]), pick(c4, rows[5], rows[1])))
    hi = pick(c4, pick(c2, rows[14], rows[12]), pick(c2, rows[10], rows[8]))
    lo = pick(c4, pick(c2, rows[6], rows[4]), pick(c2, rows[2], rows[0]))
    c1 = test(pick(c8, hi, lo))
    count = (pick(c8, 8.0, 0.0) + pick(c4, 4.0, 0.0)) + (pick(c2, 2.0, 0.0) + pick(c1, 1.0, 0.0))
    return pick(test(rows[15]), 16.0, count)


def _ranks_of(x, vals):
    return _count_true_prefix(lambda v: v > x, vals)


def _peer_route_kernel(x_ref, g_ref, wq_ref, k1_ref, k2_ref, a1_ref, cnt_ref, a2_ref, rank2_ref):
    k = PEER_TOPK
    tt = x_ref.shape[0]
    xn = _rms(x_ref[...], g_ref[...]).astype(BF16)
    q = jnp.dot(xn, wq_ref[...], preferred_element_type=F32)
    dn = (((1,), (1,)), ((), ()))
    for h in range(PEER_HEADS):
        qa = q[:, (2 * h) * LANES:(2 * h + 1) * LANES].astype(BF16)
        qb = q[:, (2 * h + 1) * LANES:(2 * h + 2) * LANES].astype(BF16)
        s1 = lax.dot_general(k1_ref[...], qa, dn, preferred_element_type=F32)
        s2 = lax.dot_general(k2_ref[...], qb, dn, preferred_element_type=F32)
        v1 = _top_desc(s1, k)
        v2 = _top_desc(s2, k)
        rank2 = _ranks_of(s2, v2)
        cands = [v1[i] + v2[j] for i in range(k) for j in range(k // (i + 1))]
        pad = (-len(cands)) % 8
        cands += [jnp.full((1, tt), NEG_INF, F32)] * pad
        top = _top_desc(jnp.concatenate(cands, axis=0), k)
        z = jnp.ones((1, tt), F32)
        for r in range(1, k):
            z = z + jnp.exp(top[r] - top[0])
        thr = top[k - 1]
        cnt = _count_true_prefix(lambda v: (s1 + v) >= thr, v2)
        cnt_ref[h] = jnp.where(s1 >= v1[k - 1], cnt, 0.0)
        rank2_ref[h] = rank2.astype(BF16)
        a1_ref[h] = jnp.exp(s1 - v1[0]) * (1.0 / z)
        a2_ref[h] = jnp.exp(s2 - v2[0]).astype(BF16)


def _peer_route(x, g, wq, k1, k2, tt):
    t, d = x.shape
    shape = (PEER_HEADS, PEER_NKEYS, t)
    out = pl.BlockSpec((PEER_HEADS, PEER_NKEYS, tt), lambda i: (0, 0, i))
    f32 = jax.ShapeDtypeStruct(shape, F32)
    bf16 = jax.ShapeDtypeStruct(shape, BF16)
    return pl.pallas_call(
        _peer_route_kernel,
        grid=(t // tt,),
        in_specs=[pl.BlockSpec((tt, d), lambda i: (i, 0)),
                  pl.BlockSpec((1, d), lambda i: (0, 0)),
                  pl.BlockSpec(wq.shape, lambda i: (0, 0)),
                  pl.BlockSpec(k1.shape, lambda i: (0, 0)),
                  pl.BlockSpec(k2.shape, lambda i: (0, 0))],
        out_specs=[out, out, out, out],
        out_shape=[f32, f32, bf16, bf16],
        compiler_params=_cparams(("arbitrary",)),
    )(x, g.reshape(1, d), wq, k1, k2)


def _peer_dense_kernel(x_ref, g_ref, u_ref, vt_ref, a1_ref, cnt_ref, a2_ref, rank2_ref, gout_ref, o_ref,
                       xnt_ref, acc_ref, hta_ref, htb_ref, pta_ref, ptb_ref, *, norm_out):
    e = pl.program_id(1)

    @pl.when(e == 0)
    def _():
        xn = _rms(x_ref[...], g_ref[...])
        xnt_ref[...] = xn.T.astype(BF16)
        acc_ref[...] = jnp.zeros_like(acc_ref)

    zero = jnp.zeros((), BF16)
    n_q, job_rows, _ = u_ref.shape
    sub = xnt_ref.shape[1] // 2
    n1_per_job = job_rows // PEER_NKEYS
    pack = BF16_SUBLANES
    groups = PEER_NKEYS // pack
    col_a, col_b = slice(0, sub), slice(sub, 2 * sub)
    ht_refs = {col_a: hta_ref, col_b: htb_ref}
    pt_refs = {col_a: pta_ref, col_b: ptb_ref}

    def matmul1(q, cols):
        ht_refs[cols][...] = jnp.dot(u_ref[q], xnt_ref[:, cols], preferred_element_type=F32)

    def gate(q, cols):
        ht_ref, pt_ref = ht_refs[cols], pt_refs[cols]
        for rl in range(n1_per_job):
            r = q * n1_per_job + rl
            w = None
            for h in range(PEER_HEADS):
                cnt = jnp.broadcast_to(cnt_ref[h, pl.ds(r, 1), cols], (pack, sub)).astype(BF16)
                a1 = jnp.broadcast_to(a1_ref[h, pl.ds(r, 1), cols], (pack, sub)).astype(BF16)
                sel = rank2_ref[h, :, :, cols] < cnt[None]
                term = a1[None] * jnp.where(sel, a2_ref[h, :, :, cols], zero)
                w = term if w is None else w + term
            rows = slice(rl * PEER_NKEYS, (rl + 1) * PEER_NKEYS)
            hb = ht_ref[rows, :]
            gelu = hb * (1.0 + lax.erf(hb))
            p = w * gelu.astype(BF16).reshape(groups, pack, sub)
            pt_ref[rows, :] = p.reshape(PEER_NKEYS, sub)

    def matmul2(q, cols):
        acc_ref[:, cols] += jnp.dot(vt_ref[q], pt_refs[cols][...], preferred_element_type=F32)

    def pair(q, has_prev, has_next):
        matmul1(q, col_b)
        gate(q, col_a)
        if has_prev:
            matmul2(q - 1, col_b)
        if has_next:
            matmul1(q + 1, col_a)
        gate(q, col_b)
        matmul2(q, col_a)

    matmul1(0, col_a)
    if n_q == 1:
        pair(0, False, False)
    else:
        pair(0, False, True)

        def body(q, carry):
            pair(q, True, True)
            return carry

        lax.fori_loop(1, n_q - 1, body, 0)
        pair(n_q - 1, True, False)
    matmul2(n_q - 1, col_b)

    @pl.when(e == pl.num_programs(1) - 1)
    def _():
        y = x_ref[...] + acc_ref[...].T
        o_ref[...] = _rms(y, gout_ref[...]) if norm_out else y


def _expert_jobs_kernel(u_ref, v_ref, uo_ref, vo_ref):
    uo_ref[0] = (u_ref[0] * INV_SQRT2).astype(BF16)
    vo_ref[0] = (v_ref[0].T * (0.5 / INV_SQRT2)).astype(BF16)


def _expert_jobs(u, v, layer):
    _, n_exp, d = u.shape
    n_jobs = n_exp // PEER_JOB_ROWS
    blk = pl.BlockSpec((1, PEER_JOB_ROWS, d), lambda j: (layer, j, 0))
    return pl.pallas_call(
        _expert_jobs_kernel,
        grid=(n_jobs,),
        in_specs=[blk, blk],
        out_specs=[pl.BlockSpec((1, PEER_JOB_ROWS, d), lambda j: (j, 0, 0)),
                   pl.BlockSpec((1, d, PEER_JOB_ROWS), lambda j: (j, 0, 0))],
        out_shape=[jax.ShapeDtypeStruct((n_jobs, PEER_JOB_ROWS, d), BF16),
                   jax.ShapeDtypeStruct((n_jobs, d, PEER_JOB_ROWS), BF16)],
        compiler_params=_cparams(("arbitrary",)),
    )(u, v)


def _peer_dense(x, g, u, vt, a1t, cnt, a2t, rank2, g_out, norm_out, tt, et):
    t, d = x.shape
    n_jobs, job_rows, _ = u.shape
    n_q = et // job_rows
    n1_per_tile = et // PEER_NKEYS
    groups = PEER_NKEYS // BF16_SUBLANES
    grouped = (PEER_HEADS, groups, BF16_SUBLANES, t)
    full = pl.BlockSpec((PEER_HEADS, groups, BF16_SUBLANES, tt), lambda i, e: (0, 0, 0, i))
    part = pl.BlockSpec((PEER_HEADS, n1_per_tile, tt), lambda i, e: (0, e, i))
    sub = tt // 2
    row = pl.BlockSpec((1, d), lambda i, e: (0, 0))
    return pl.pallas_call(
        functools.partial(_peer_dense_kernel, norm_out=norm_out),
        grid=(t // tt, n_jobs // n_q),
        in_specs=[pl.BlockSpec((tt, d), lambda i, e: (i, 0)),
                  row,
                  pl.BlockSpec((n_q, job_rows, d), lambda i, e: (e, 0, 0)),
                  pl.BlockSpec((n_q, d, job_rows), lambda i, e: (e, 0, 0)),
                  part, part, full, full, row],
        out_specs=pl.BlockSpec((tt, d), lambda i, e: (i, 0)),
        out_shape=jax.ShapeDtypeStruct((t, d), F32),
        scratch_shapes=[pltpu.VMEM((d, tt), BF16), pltpu.VMEM((d, tt), F32),
                        pltpu.VMEM((job_rows, sub), F32), pltpu.VMEM((job_rows, sub), F32),
                        pltpu.VMEM((job_rows, sub), BF16), pltpu.VMEM((job_rows, sub), BF16)],
        compiler_params=_cparams(("arbitrary", "arbitrary"), vmem_mb=VMEM_LIMIT_DENSE_MB),
    )(x, g.reshape(1, d), u, vt, a1t, cnt, a2t.reshape(grouped), rank2.reshape(grouped),
      g_out.reshape(1, d))


def _tiles(seq):
    pick = lambda want: min(want, seq)
    return dict(tm=pick(256), tq=pick(512), tconv=pick(512), tgla=pick(512), tt=pick(256), td=pick(512),
                et=2048)


def kernel(x, positions, norm_mix, w_in, lam_q1, lam_k1, lam_q2, lam_k2, diff_norm, conv_w, gla_w_gate2, gla_b_gate, gla_norm, w_out, norm_ffn, peer_w_q, peer_keys1, peer_keys2, peer_u, peer_v, norm_final):
    batch, seq, d = x.shape
    t = batch * seq
    depth = w_in.shape[0]
    tl = _tiles(seq)
    xt = x.reshape(t, d)
    tabs = _rope_tables(positions, tl["tm"])
    for i in range(depth):
        w_in_p = jnp.pad(w_in[i], ((0, 0), (0, IN_PAD - IN_WIDTH))).astype(BF16)
        qt, kr, vt3, proj = _in_proj(xt, norm_mix[i], w_in_p, tabs, tl["tq"])
        lam_init = 0.8 - 0.6 * math.exp(-0.3 * i)
        o_da = _diff_attention(qt, kr, vt3, lam_q1[i], lam_k1[i], lam_q2[i], lam_k2[i], diff_norm[i],
                               batch, seq, lam_init, tl["tq"])
        o_sc = _short_conv(proj, conv_w[i], seq, tl["tconv"])
        o_g = _gla(proj, gla_w_gate2[i], gla_b_gate[i], gla_norm[i], batch, seq, tl["tgla"])
        xt = _out_proj(xt, o_da, o_sc, o_g, w_out[i].astype(BF16), tl["tm"])
        a1t, cnt, a2t, rank2 = _peer_route(xt, norm_ffn[i], peer_w_q[i].astype(BF16),
                                           peer_keys1[i].astype(BF16), peer_keys2[i].astype(BF16), tl["tt"])
        u3, ev3 = _expert_jobs(peer_u, peer_v, i)
        xt = _peer_dense(xt, norm_ffn[i], u3, ev3, a1t, cnt, a2t, rank2, norm_final, i == depth - 1,
                         tl["td"], tl["et"])
    return xt.reshape(batch, seq, d)
```

```python
import functools
import math

import jax
import jax.numpy as jnp
from jax import lax
from jax.experimental import pallas as pl
from jax.experimental.pallas import tpu as pltpu

F32 = jnp.float32
BF16 = jnp.bfloat16

EPS = 1e-6
HEAD_DIM = 64
DA_HEADS = 4
DA_WIDTH = 512
ROPE_THETA = 500000.0
ROT_DIM = 16
SC_WIDTH = 256
CONV_W = 3
GLA_HEADS = 4
GLA_DK = 32
GLA_DV = 64
GLA_KW = 128
GLA_VW = 256
GLA_GATE_RANK = 16
GLA_GATE_TEMP = 16.0
GLA_CHUNK = 64
GLA_SUB = 16
GLA_GROUP = 8
IN_WIDTH = 3088
IN_PAD = 3200
PEER_HEADS = 8
PEER_NKEYS = 128
PEER_TOPK = 16
PEER_JOB_ROWS = 512
LANES = 128
SUBLANES = 8
BF16_SUBLANES = 16
NEG_INF = float("-inf")
INV_SQRT2 = 1.0 / math.sqrt(2.0)

QKV_WIDTH = 3 * DA_WIDTH
REST_WIDTH = IN_PAD - QKV_WIDTH
COL_SC_B, COL_SC_C, COL_SC_H = 0, 1, 2
COL_G_V, COL_G_R = 4, 5
COL_G_Q, COL_G_K, COL_G_LR = 6, 7, 12


VMEM_LIMIT_MB = 40
VMEM_LIMIT_DENSE_MB = 52


def _cparams(sem, vmem_mb=VMEM_LIMIT_MB):
    return pltpu.CompilerParams(dimension_semantics=sem, vmem_limit_bytes=vmem_mb * 1024 * 1024)


def _rms(x, g):
    return x * lax.rsqrt(jnp.mean(x * x, axis=-1, keepdims=True) + EPS) * g


def _rope_table_kernel(pos_ref, inv_ref, mrot_ref, m1_ref, m2_ref, c_ref, s1_ref, s2_ref):
    ang = pos_ref[...] * inv_ref[...]
    c = jnp.cos(ang)
    s = jnp.sin(ang)
    mrot = mrot_ref[...]
    c_ref[...] = mrot * c + (1.0 - mrot)
    s1_ref[...] = -(m1_ref[...] * s)
    s2_ref[...] = m2_ref[...] * s


def _rope_tables(positions, tm):
    t = positions.size
    pos = jnp.broadcast_to(positions.reshape(t, 1).astype(F32), (t, LANES))
    d = jnp.arange(LANES) % HEAD_DIM
    half = ROT_DIM // 2
    inv = ROPE_THETA ** (-jnp.arange(0, ROT_DIM, 2, dtype=F32) / ROT_DIM)
    inv_row = jnp.where(d < ROT_DIM, inv[d % half], 0.0).astype(F32).reshape(1, LANES)
    mrot = (d < ROT_DIM).astype(F32).reshape(1, LANES)
    m1 = (d < half).astype(F32).reshape(1, LANES)
    m2 = ((d >= half) & (d < ROT_DIM)).astype(F32).reshape(1, LANES)
    row = pl.BlockSpec((1, LANES), lambda i: (0, 0))
    blk = pl.BlockSpec((tm, LANES), lambda i: (i, 0))
    return pl.pallas_call(
        _rope_table_kernel,
        grid=(t // tm,),
        in_specs=[blk, row, row, row, row],
        out_specs=[blk, blk, blk],
        out_shape=[jax.ShapeDtypeStruct((t, LANES), F32)] * 3,
        compiler_params=_cparams(("arbitrary",)),
    )(pos, inv_row, mrot, m1, m2)


def _in_proj_kernel(x_ref, g_ref, w_ref, c_ref, s1_ref, s2_ref, qt_ref, ko_ref, vt_ref, rest_ref,
                    *, qscale):
    xn = _rms(x_ref[...], g_ref[...]).astype(BF16)
    rest_ref[...] = jnp.dot(xn, w_ref[:, QKV_WIDTH:], preferred_element_type=F32)
    c = c_ref[...]
    s1 = s1_ref[...]
    s2 = s2_ref[...]
    half = ROT_DIM // 2

    def rope(x):
        return x * c + pltpu.roll(x, LANES - half, 1) * s1 + pltpu.roll(x, half, 1) * s2

    qkv = jnp.dot(xn, w_ref[:, :QKV_WIDTH], preferred_element_type=F32)
    for g in range(DA_WIDTH // LANES):
        sl = slice(LANES * g, LANES * (g + 1))
        col = lambda base: slice(base + LANES * g, base + LANES * (g + 1))
        qt_ref[sl, :] = (rope(qkv[:, col(0)]) * qscale).T.astype(BF16)
        ko_ref[:, sl] = rope(qkv[:, col(DA_WIDTH)]).astype(BF16)
        vt_ref[0, sl, :] = qkv[:, col(2 * DA_WIDTH)].T.astype(BF16)


def _in_proj(x, g, w, tabs, tm):
    t, d = x.shape
    c, s1, s2 = tabs
    tab = pl.BlockSpec((tm, LANES), lambda i: (i, 0))
    return pl.pallas_call(
        functools.partial(_in_proj_kernel, qscale=HEAD_DIM ** -0.5 * math.log2(math.e)),
        grid=(t // tm,),
        in_specs=[pl.BlockSpec((tm, d), lambda i: (i, 0)),
                  pl.BlockSpec((1, d), lambda i: (0, 0)),
                  pl.BlockSpec(w.shape, lambda i: (0, 0)),
                  tab, tab, tab],
        out_specs=[pl.BlockSpec((DA_WIDTH, tm), lambda i: (0, i)),
                   pl.BlockSpec((tm, DA_WIDTH), lambda i: (i, 0)),
                   pl.BlockSpec((1, DA_WIDTH, tm), lambda i: (i, 0, 0)),
                   pl.BlockSpec((tm, REST_WIDTH), lambda i: (i, 0))],
        out_shape=[jax.ShapeDtypeStruct((DA_WIDTH, t), BF16),
                   jax.ShapeDtypeStruct((t, DA_WIDTH), BF16),
                   jax.ShapeDtypeStruct((t // tm, DA_WIDTH, tm), BF16),
                   jax.ShapeDtypeStruct((t, REST_WIDTH), F32)],
        compiler_params=_cparams(("arbitrary",)),
    )(x, g.reshape(1, d), w, c, s1, s2)


def _attn_kernel(qt_ref, k_ref, vt_ref, lq1_ref, lk1_ref, lq2_ref, lk2_ref, gain_ref, o_ref,
                 acc1_ref, acc2_ref, m1_ref, l1_ref, m2_ref, l2_ref, s1a_ref, s2a_ref, s1b_ref, s2b_ref,
                 *, tq, lam_init):
    qi = pl.program_id(2)
    qt = qt_ref[...]
    row = lax.broadcasted_iota(jnp.int32, (LANES, 1), 0)
    zero = jnp.zeros_like(qt)
    qa = jnp.where(row < HEAD_DIM, qt, zero)
    qb = jnp.where(row >= HEAD_DIM, qt, zero)
    acc1_ref[...] = jnp.zeros_like(acc1_ref)
    acc2_ref[...] = jnp.zeros_like(acc2_ref)

    for st_ref in (m1_ref, m2_ref):
        st_ref[...] = jnp.full(st_ref.shape, NEG_INF, F32)
    for st_ref in (l1_ref, l2_ref):
        st_ref[...] = jnp.zeros_like(st_ref)
    maps = ((qa, m1_ref, l1_ref, acc1_ref), (qb, m2_ref, l2_ref, acc2_ref))

    def scores(j, bufs):
        off = pl.multiple_of(j * tq, tq)
        kk = k_ref[pl.ds(off, tq), :]
        for (qm, _, _, _), s_ref in zip(maps, bufs):
            s_ref[...] = jnp.dot(kk, qm, preferred_element_type=F32)

    def absorb(j, bufs, masked):
        vt = vt_ref[j]
        for (_, m_ref, l_ref, acc_ref), s_ref in zip(maps, bufs):
            s = s_ref[...]
            if masked:
                r = lax.broadcasted_iota(jnp.int32, (tq, tq), 0)
                c = lax.broadcasted_iota(jnp.int32, (tq, tq), 1)
                s = jnp.where(r <= c, s, NEG_INF)
            m = m_ref[...]
            mn = jnp.maximum(m, jnp.max(s, axis=0, keepdims=True))
            alpha = jnp.exp2(m - mn)
            p = jnp.exp2(s - mn)
            m_ref[...] = mn
            l_ref[...] = alpha * l_ref[...] + jnp.sum(p, axis=0, keepdims=True)
            acc_ref[...] = alpha * acc_ref[...] + jnp.dot(vt, p.astype(BF16),
                                                          preferred_element_type=F32)

    buf_a, buf_b = (s1a_ref, s2a_ref), (s1b_ref, s2b_ref)
    scores(0, buf_a)

    def pair(i, carry):
        j = 2 * i
        scores(j + 1, buf_b)
        absorb(j, buf_a, False)
        scores(j + 2, buf_a)
        absorb(j + 1, buf_b, False)
        return carry

    lax.fori_loop(0, qi // 2, pair, 0)

    @pl.when(qi % 2 == 1)
    def _():
        scores(qi, buf_b)
        absorb(qi - 1, buf_a, False)
        absorb(qi, buf_b, True)

    @pl.when(qi % 2 == 0)
    def _():
        absorb(qi, buf_a, True)

    l1, l2 = l1_ref[...], l2_ref[...]

    lam = (jnp.exp(jnp.sum(lq1_ref[...] * lk1_ref[...], axis=-1, keepdims=True))
           - jnp.exp(jnp.sum(lq2_ref[...] * lk2_ref[...], axis=-1, keepdims=True)) + lam_init)
    o = acc1_ref[...] / l1 - lam * (acc2_ref[...] / l2)
    y = o * lax.rsqrt(jnp.mean(o * o, axis=0, keepdims=True) + EPS) * gain_ref[...]
    o_ref[...] = (y * (1.0 - lam_init)).T


def _diff_attention(qt, kr, vt3, lq1, lk1, lq2, lk2, gain, batch, seq, lam_init, tq):
    t = kr.shape[0]
    nq = seq // tq
    vec = lambda n: pl.BlockSpec((1, n), lambda b, h, i: (0, 0))
    return pl.pallas_call(
        functools.partial(_attn_kernel, tq=tq, lam_init=lam_init),
        grid=(batch, DA_HEADS, nq),
        in_specs=[pl.BlockSpec((LANES, tq), lambda b, h, i: (h, b * nq + i)),
                  pl.BlockSpec((seq, LANES), lambda b, h, i: (b, h)),
                  pl.BlockSpec((nq, LANES, tq), lambda b, h, i: (b, h, 0)),
                  vec(HEAD_DIM), vec(HEAD_DIM), vec(HEAD_DIM), vec(HEAD_DIM),
                  pl.BlockSpec((LANES, 1), lambda b, h, i: (0, 0))],
        out_specs=pl.BlockSpec((tq, LANES), lambda b, h, i: (b * nq + i, h)),
        out_shape=jax.ShapeDtypeStruct((t, DA_WIDTH), F32),
        scratch_shapes=([pltpu.VMEM((LANES, tq), F32)] * 2 + [pltpu.VMEM((1, tq), F32)] * 4
                        + [pltpu.VMEM((tq, tq), F32)] * 4),
        compiler_params=_cparams(("arbitrary", "arbitrary", "arbitrary")),
    )(qt, kr, vt3, lq1.reshape(1, -1), lk1.reshape(1, -1), lq2.reshape(1, -1), lk2.reshape(1, -1),
      gain.reshape(-1, 1))


def _short_conv_tile(b, c, h, cp, hp, w, first):
    z = c * h
    zp = jnp.where(first, jnp.zeros_like(cp), cp * hp)
    row = lax.broadcasted_iota(jnp.int32, (z.shape[0], 1), 0)
    z1 = jnp.where(row == 0, zp[7:8, :], pltpu.roll(z, 1, 0))
    z2 = pltpu.roll(z, 2, 0)
    z2 = jnp.where(row == 0, zp[6:7, :], jnp.where(row == 1, zp[7:8, :], z2))
    return b * (w[0:1, :] * z2 + w[1:2, :] * z1 + w[2:3, :] * z)


def _split3(x):
    hi = x.astype(BF16)
    r = x - hi.astype(F32)
    mid = r.astype(BF16)
    lo = (r - mid.astype(F32)).astype(BF16)
    return hi, mid, lo


def _gla_kernel(q_ref, k_ref, v_ref, r_ref, lr_ref, wg_ref, bg_ref, gn_ref, o_ref, st_ref, *, tc):
    ch = GLA_CHUNK

    @pl.when(pl.program_id(1) == 0)
    def _():
        st_ref[...] = jnp.zeros_like(st_ref)

    ii = lax.broadcasted_iota(jnp.int32, (ch, ch), 0)
    jj = lax.broadcasted_iota(jnp.int32, (ch, ch), 1)
    tri = (ii >= jj).astype(BF16)
    rk = lax.broadcasted_iota(jnp.int32, (GLA_VW, GLA_KW), 0)
    ck = lax.broadcasted_iota(jnp.int32, (GLA_VW, GLA_KW), 1)
    bd_k = (rk // GLA_DV) == (ck // GLA_DK)
    rv = lax.broadcasted_iota(jnp.int32, (GLA_VW, GLA_VW), 0)
    cv = lax.broadcasted_iota(jnp.int32, (GLA_VW, GLA_VW), 1)
    bd_v = (rv // GLA_DV) == (cv // GLA_DV)
    seg = bd_v.astype(BF16)
    sb = GLA_SUB
    nsb = ch // sb
    hs = GLA_HEADS * sb
    iota = lambda shape, axis: lax.broadcasted_iota(jnp.int32, shape, axis)
    sub_row = iota((sb, 1), 0)
    head_expand = ((iota((GLA_KW, GLA_VW), 0) // GLA_DK)
                   == (iota((GLA_KW, GLA_VW), 1) // GLA_DV)).astype(BF16)
    sum_j = ((iota((ch, ch * sb), 1) // sb) == iota((ch, ch * sb), 0)).astype(BF16)
    bd_ks = (iota((hs, GLA_KW), 0) // sb) == (iota((hs, GLA_KW), 1) // GLA_DK)
    bd_vs = (iota((hs, GLA_VW), 0) // sb) == (iota((hs, GLA_VW), 1) // GLA_DV)
    off_w = (nsb - 1) * hs
    att_keep = iota((ch, off_w), 0) >= (iota((ch, off_w), 1) // hs + 1) * sb
    dn_t = (((1,), (1,)), ((), ()))
    scale = GLA_DK ** -0.5

    def stage_decay(rows):
        x = jnp.dot(lr_ref[rows, :].astype(BF16), wg_ref[...], preferred_element_type=F32) + bg_ref[...]
        la = (jnp.minimum(x, 0.0) - jnp.log1p(jnp.exp(-jnp.abs(x)))) * (1.0 / GLA_GATE_TEMP)
        hi, mid, lo = _split3(la)
        return (jnp.dot(tri, hi, preferred_element_type=F32) + jnp.dot(tri, mid, preferred_element_type=F32)
                + jnp.dot(tri, lo, preferred_element_type=F32))

    def stage_pairwise(q, k, v, b):
        pieces, vrep = [], []
        for sbi in range(nsb):
            rs = slice(sbi * sb, (sbi + 1) * sb)
            qs, ks, bs = q[rs, :], k[rs, :], b[rs, :]
            for i in range(sb):
                dec = jnp.exp(jnp.where(sub_row <= i, bs[i:i + 1, :] - bs, NEG_INF))
                pieces.append(qs[i:i + 1, :] * ks * dec)
            vrep += [v[rs, :]] * sb
        x3 = jnp.concatenate(pieces, axis=0)
        x3h = x3.astype(BF16)
        return x3h, (x3 - x3h.astype(F32)).astype(BF16), jnp.concatenate(vrep, axis=0)

    def stage_head_sums(x3h, x3l):
        return (jnp.dot(x3h, head_expand, preferred_element_type=F32)
                + jnp.dot(x3l, head_expand, preferred_element_type=F32))

    def stage_cross_operands(q, k, vb, b):
        qts, kbds, vbds = [], [], []
        zeros_k = jnp.zeros((hs, GLA_KW), BF16)
        for sbi in range(nsb - 1):
            rs = slice(sbi * sb, (sbi + 1) * sb)
            end = (sbi + 1) * sb
            b_end = b[end - 1:end, :]
            qts.append((q * jnp.exp(jnp.minimum(b - b_end, 0.0))).astype(BF16))
            kt = (k[rs, :] * jnp.exp(b_end - b[rs, :])).astype(BF16)
            kbd = jnp.where(bd_ks, jnp.concatenate([kt] * GLA_HEADS, axis=0), jnp.zeros((), BF16))
            kbds.append(jnp.concatenate([kbd if s2 == sbi else zeros_k for s2 in range(nsb - 1)], axis=1))
            vbds.append(jnp.where(bd_vs, jnp.concatenate([vb[rs, :]] * GLA_HEADS, axis=0),
                                  jnp.zeros((), BF16)))
        return jnp.concatenate(qts, axis=1), jnp.concatenate(kbds, axis=0), jnp.concatenate(vbds, axis=0)

    def stage_cross_scores(q_all, k_all):
        att = lax.dot_general(q_all, k_all, dn_t, preferred_element_type=F32)
        return jnp.where(att_keep, att, 0.0).astype(BF16)

    def stage_state_update(k, v, b):
        b_last = b[ch - 1:ch, :]
        kl = (k * jnp.exp(b_last - b)).astype(BF16)
        ut = jnp.dot(v.T.astype(BF16), kl, preferred_element_type=F32)
        return jnp.exp(b_last), jnp.where(bd_k, ut, 0.0)

    def stage_mean_square(o):
        o2 = o * o
        o2h = o2.astype(BF16)
        o2l = (o2 - o2h.astype(F32)).astype(BF16)
        return (jnp.dot(o2h, seg, preferred_element_type=F32)
                + jnp.dot(o2l, seg, preferred_element_type=F32)) * (1.0 / GLA_DV)

    def stage_store(rows, o, ms):
        r = r_ref[rows, :]
        gate = r * (1.0 / (1.0 + jnp.exp(-r)))
        o_ref[rows, :] = o * lax.rsqrt(ms + EPS) * gn_ref[...] * gate

    each = lambda f, *cols: [f(*args) for args in zip(*cols)]
    gsize = min(GLA_GROUP, tc // ch)

    def group(g, carry):
        rows = [pl.ds(pl.multiple_of((g * gsize + c) * ch, ch), ch) for c in range(gsize)]
        b = each(stage_decay, rows)
        q = [q_ref[r, :] * scale for r in rows]
        k = [k_ref[r, :] for r in rows]
        v = [v_ref[r, :] for r in rows]
        vb = [x.astype(BF16) for x in v]
        x3h, x3l, vrep = zip(*each(stage_pairwise, q, k, v, b))
        w = each(stage_head_sums, x3h, x3l)
        wv = each(lambda w_, v_: (w_ * v_).astype(BF16), w, vrep)
        o = each(lambda wv_: jnp.dot(sum_j, wv_, preferred_element_type=F32), wv)
        q_all, k_all, v_all = zip(*each(stage_cross_operands, q, k, vb, b))
        att = each(stage_cross_scores, q_all, k_all)
        o = each(lambda o_, a_, v_: o_ + jnp.dot(a_, v_, preferred_element_type=F32), o, att, v_all)
        bq = each(lambda q_, b_: (q_ * jnp.exp(b_)).astype(BF16), q, b)
        decay, ut = zip(*each(stage_state_update, k, v, b))
        st = st_ref[...]
        states = []
        for c in range(gsize):
            states.append(st.astype(BF16))
            st = st * decay[c] + ut[c]
        st_ref[...] = st
        o = each(lambda o_, q_, s_: o_ + lax.dot_general(q_, s_, dn_t, preferred_element_type=F32),
                 o, bq, states)
        ms = each(stage_mean_square, o)
        each(stage_store, rows, o, ms)
        return carry

    lax.fori_loop(0, tc // ch // gsize, group, 0)


def _gla(proj, wg, bg, gn, batch, seq, tc):
    t = proj.shape[0]
    nb = seq // tc
    blk = lambda w, j: pl.BlockSpec((tc, w), lambda b, i, j=j: (b * nb + i, j))
    const = lambda r, c: pl.BlockSpec((r, c), lambda b, i: (0, 0))
    wg_pad = jnp.zeros((LANES, GLA_KW), F32).at[:GLA_GATE_RANK].set(wg).astype(BF16)
    gn_row = jnp.tile(gn, GLA_HEADS).reshape(1, GLA_VW)
    return pl.pallas_call(
        functools.partial(_gla_kernel, tc=tc),
        grid=(batch, nb),
        in_specs=[blk(GLA_KW, COL_G_Q), blk(GLA_KW, COL_G_K), blk(GLA_VW, COL_G_V), blk(GLA_VW, COL_G_R),
                  blk(LANES, COL_G_LR), const(LANES, GLA_KW), const(1, GLA_KW), const(1, GLA_VW)],
        out_specs=pl.BlockSpec((tc, GLA_VW), lambda b, i: (b * nb + i, 0)),
        out_shape=jax.ShapeDtypeStruct((t, GLA_VW), F32),
        scratch_shapes=[pltpu.VMEM((GLA_VW, GLA_KW), F32)],
        compiler_params=_cparams(("arbitrary", "arbitrary")),
    )(proj, proj, proj, proj, proj, wg_pad, bg.reshape(1, GLA_KW), gn_row)


def _out_proj_kernel(x_ref, a_ref, b_ref, c_ref, h_ref, cp_ref, hp_ref, cw_ref, g_ref, w_ref, o_ref,
                     *, blocks_per_seq):
    first = (pl.program_id(0) % blocks_per_seq) == 0
    o_sc = _short_conv_tile(b_ref[...], c_ref[...], h_ref[...], cp_ref[...], hp_ref[...], cw_ref[...], first)
    acc = jnp.dot(a_ref[...].astype(BF16), w_ref[0:DA_WIDTH, :], preferred_element_type=F32)
    acc += jnp.dot(o_sc.astype(BF16), w_ref[DA_WIDTH:DA_WIDTH + SC_WIDTH, :],
                   preferred_element_type=F32)
    acc += jnp.dot(g_ref[...].astype(BF16), w_ref[DA_WIDTH + SC_WIDTH:, :], preferred_element_type=F32)
    o_ref[...] = x_ref[...] + acc


def _out_proj(x, o_da, proj, conv_w, o_g, w, seq, tm):
    t, d = x.shape
    rb = tm // SUBLANES
    blk = lambda n: pl.BlockSpec((tm, n), lambda i: (i, 0))
    cur = lambda j: pl.BlockSpec((tm, SC_WIDTH), lambda i, j=j: (i, j))
    prev = lambda j: pl.BlockSpec((SUBLANES, SC_WIDTH), lambda i, j=j: (jnp.maximum(i * rb - 1, 0), j))
    return pl.pallas_call(
        functools.partial(_out_proj_kernel, blocks_per_seq=seq // tm),
        grid=(t // tm,),
        in_specs=[blk(d), blk(DA_WIDTH),
                  cur(COL_SC_B), cur(COL_SC_C), cur(COL_SC_H), prev(COL_SC_C), prev(COL_SC_H),
                  pl.BlockSpec((CONV_W, SC_WIDTH), lambda i: (0, 0)),
                  blk(GLA_VW), pl.BlockSpec(w.shape, lambda i: (0, 0))],
        out_specs=blk(d),
        out_shape=jax.ShapeDtypeStruct((t, d), F32),
        compiler_params=_cparams(("arbitrary",)),
    )(x, o_da, proj, proj, proj, proj, proj, conv_w, o_g, w)


def _merge_exchange_network(n):
    pairs = []
    t = max(1, math.ceil(math.log2(n)))
    p = 2 ** (t - 1)
    while p > 0:
        q, r, d = 2 ** (t - 1), 0, p
        while d > 0:
            pairs += [(i, i + d) for i in range(n - d) if (i & p) == r]
            d, q, r = q - p, q // 2, p
        p //= 2
    return pairs


def _top_desc(x, n):
    groups = [x[SUBLANES * i:SUBLANES * (i + 1), :] for i in range(x.shape[0] // SUBLANES)]
    for i, j in _merge_exchange_network(len(groups)):
        groups[i], groups[j] = jnp.maximum(groups[i], groups[j]), jnp.minimum(groups[i], groups[j])
    vals = []
    for r in range(n):
        m = jnp.max(groups[0], axis=0, keepdims=True)
        vals.append(m)
        hit = groups[0] == m
        live = min(len(groups), n - r)
        for k in range(live - 1):
            groups[k] = jnp.where(hit, groups[k + 1], groups[k])
        if live == len(groups) and live > 0 and r + 1 < n:
            groups[live - 1] = jnp.where(hit, NEG_INF, groups[live - 1])
    return vals


def _count_true_prefix(test, rows):
    assert len(rows) == 16
    pick = lambda c, a, b: jnp.where(c, a, b)
    c8 = test(rows[7])
    c4 = test(pick(c8, rows[11], rows[3]))
    c2 = test(pick(c8, pick(c4, rows[13], rows[9]), pick(c4, rows[5], rows[1])))
    hi = pick(c4, pick(c2, rows[14], rows[12]), pick(c2, rows[10], rows[8]))
    lo = pick(c4, pick(c2, rows[6], rows[4]), pick(c2, rows[2], rows[0]))
    c1 = test(pick(c8, hi, lo))
    count = (pick(c8, 8.0, 0.0) + pick(c4, 4.0, 0.0)) + (pick(c2, 2.0, 0.0) + pick(c1, 1.0, 0.0))
    return pick(test(rows[15]), 16.0, count)


def _ranks_of(x, vals):
    return _count_true_prefix(lambda v: v > x, vals)


def _peer_route_kernel(x_ref, g_ref, wq_ref, k1_ref, k2_ref, a1_ref, cnt_ref, a2_ref, rank2_ref):
    k = PEER_TOPK
    tt = x_ref.shape[0]
    xn = _rms(x_ref[...], g_ref[...]).astype(BF16)
    q = jnp.dot(xn, wq_ref[...], preferred_element_type=F32)
    dn = (((1,), (1,)), ((), ()))
    for h in range(PEER_HEADS):
        qa = q[:, (2 * h) * LANES:(2 * h + 1) * LANES].astype(BF16)
        qb = q[:, (2 * h + 1) * LANES:(2 * h + 2) * LANES].astype(BF16)
        s1 = lax.dot_general(k1_ref[...], qa, dn, preferred_element_type=F32)
        s2 = lax.dot_general(k2_ref[...], qb, dn, preferred_element_type=F32)
        v1 = _top_desc(s1, k)
        v2 = _top_desc(s2, k)
        rank2 = _ranks_of(s2, v2)
        cands = [v1[i] + v2[j] for i in range(k) for j in range(k // (i + 1))]
        pad = (-len(cands)) % 8
        cands += [jnp.full((1, tt), NEG_INF, F32)] * pad
        top = _top_desc(jnp.concatenate(cands, axis=0), k)
        z = jnp.ones((1, tt), F32)
        for r in range(1, k):
            z = z + jnp.exp(top[r] - top[0])
        thr = top[k - 1]
        cnt = _count_true_prefix(lambda v: (s1 + v) >= thr, v2)
        cnt_ref[h] = jnp.where(s1 >= v1[k - 1], cnt, 0.0)
        rank2_ref[h] = rank2.astype(BF16)
        a1_ref[h] = jnp.exp(s1 - v1[0]) * (1.0 / z)
        a2_ref[h] = jnp.exp(s2 - v2[0]).astype(BF16)


def _peer_route(x, g, wq, k1, k2, tt):
    t, d = x.shape
    shape = (PEER_HEADS, PEER_NKEYS, t)
    out = pl.BlockSpec((PEER_HEADS, PEER_NKEYS, tt), lambda i: (0, 0, i))
    f32 = jax.ShapeDtypeStruct(shape, F32)
    bf16 = jax.ShapeDtypeStruct(shape, BF16)
    return pl.pallas_call(
        _peer_route_kernel,
        grid=(t // tt,),
        in_specs=[pl.BlockSpec((tt, d), lambda i: (i, 0)),
                  pl.BlockSpec((1, d), lambda i: (0, 0)),
                  pl.BlockSpec(wq.shape, lambda i: (0, 0)),
                  pl.BlockSpec(k1.shape, lambda i: (0, 0)),
                  pl.BlockSpec(k2.shape, lambda i: (0, 0))],
        out_specs=[out, out, out, out],
        out_shape=[f32, f32, bf16, bf16],
        compiler_params=_cparams(("arbitrary",)),
    )(x, g.reshape(1, d), wq, k1, k2)


def _peer_dense_kernel(x_ref, g_ref, u_ref, vt_ref, a1_ref, cnt_ref, a2_ref, rank2_ref, gout_ref, o_ref,
                       xnt_ref, acc_ref, hta_ref, htb_ref, pta_ref, ptb_ref, *, norm_out):
    e = pl.program_id(1)

    @pl.when(e == 0)
    def _():
        xn = _rms(x_ref[...], g_ref[...])
        xnt_ref[...] = xn.T.astype(BF16)
        acc_ref[...] = jnp.zeros_like(acc_ref)

    zero = jnp.zeros((), BF16)
    n_q, job_rows, _ = u_ref.shape
    sub = xnt_ref.shape[1] // 2
    n1_per_job = job_rows // PEER_NKEYS
    pack = BF16_SUBLANES
    groups = PEER_NKEYS // pack
    col_a, col_b = slice(0, sub), slice(sub, 2 * sub)
    ht_refs = {col_a: hta_ref, col_b: htb_ref}
    pt_refs = {col_a: pta_ref, col_b: ptb_ref}

    def matmul1(q, cols):
        ht_refs[cols][...] = jnp.dot(u_ref[q], xnt_ref[:, cols], preferred_element_type=F32)

    def gate(q, cols):
        ht_ref, pt_ref = ht_refs[cols], pt_refs[cols]
        for rl in range(n1_per_job):
            r = q * n1_per_job + rl
            w = None
            for h in range(PEER_HEADS):
                cnt = jnp.broadcast_to(cnt_ref[h, pl.ds(r, 1), cols], (pack, sub)).astype(BF16)
                a1 = jnp.broadcast_to(a1_ref[h, pl.ds(r, 1), cols], (pack, sub)).astype(BF16)
                sel = rank2_ref[h, :, :, cols] < cnt[None]
                term = a1[None] * jnp.where(sel, a2_ref[h, :, :, cols], zero)
                w = term if w is None else w + term
            rows = slice(rl * PEER_NKEYS, (rl + 1) * PEER_NKEYS)
            hb = ht_ref[rows, :]
            gelu = hb * (1.0 + lax.erf(hb))
            p = w * gelu.astype(BF16).reshape(groups, pack, sub)
            pt_ref[rows, :] = p.reshape(PEER_NKEYS, sub)

    def matmul2(q, cols):
        acc_ref[:, cols] += jnp.dot(vt_ref[q], pt_refs[cols][...], preferred_element_type=F32)

    def pair(q, has_prev, has_next):
        matmul1(q, col_b)
        gate(q, col_a)
        if has_prev:
            matmul2(q - 1, col_b)
        if has_next:
            matmul1(q + 1, col_a)
        gate(q, col_b)
        matmul2(q, col_a)

    matmul1(0, col_a)
    if n_q == 1:
        pair(0, False, False)
    else:
        pair(0, False, True)

        def body(q, carry):
            pair(q, True, True)
            return carry

        lax.fori_loop(1, n_q - 1, body, 0)
        pair(n_q - 1, True, False)
    matmul2(n_q - 1, col_b)

    @pl.when(e == pl.num_programs(1) - 1)
    def _():
        y = x_ref[...] + acc_ref[...].T
        o_ref[...] = _rms(y, gout_ref[...]) if norm_out else y


def _expert_jobs_kernel(u_ref, v_ref, uo_ref, vo_ref):
    uo_ref[0] = (u_ref[0] * INV_SQRT2).astype(BF16)
    vo_ref[0] = (v_ref[0].T * (0.5 / INV_SQRT2)).astype(BF16)


def _expert_jobs(u, v, layer):
    _, n_exp, d = u.shape
    n_jobs = n_exp // PEER_JOB_ROWS
    blk = pl.BlockSpec((1, PEER_JOB_ROWS, d), lambda j: (layer, j, 0))
    return pl.pallas_call(
        _expert_jobs_kernel,
        grid=(n_jobs,),
        in_specs=[blk, blk],
        out_specs=[pl.BlockSpec((1, PEER_JOB_ROWS, d), lambda j: (j, 0, 0)),
                   pl.BlockSpec((1, d, PEER_JOB_ROWS), lambda j: (j, 0, 0))],
        out_shape=[jax.ShapeDtypeStruct((n_jobs, PEER_JOB_ROWS, d), BF16),
                   jax.ShapeDtypeStruct((n_jobs, d, PEER_JOB_ROWS), BF16)],
        compiler_params=_cparams(("arbitrary",)),
    )(u, v)


def _peer_dense(x, g, u, vt, a1t, cnt, a2t, rank2, g_out, norm_out, tt, et):
    t, d = x.shape
    n_jobs, job_rows, _ = u.shape
    n_q = et // job_rows
    n1_per_tile = et // PEER_NKEYS
    groups = PEER_NKEYS // BF16_SUBLANES
    grouped = (PEER_HEADS, groups, BF16_SUBLANES, t)
    full = pl.BlockSpec((PEER_HEADS, groups, BF16_SUBLANES, tt), lambda i, e: (0, 0, 0, i))
    part = pl.BlockSpec((PEER_HEADS, n1_per_tile, tt), lambda i, e: (0, e, i))
    sub = tt // 2
    row = pl.BlockSpec((1, d), lambda i, e: (0, 0))
    return pl.pallas_call(
        functools.partial(_peer_dense_kernel, norm_out=norm_out),
        grid=(t // tt, n_jobs // n_q),
        in_specs=[pl.BlockSpec((tt, d), lambda i, e: (i, 0)),
                  row,
                  pl.BlockSpec((n_q, job_rows, d), lambda i, e: (e, 0, 0)),
                  pl.BlockSpec((n_q, d, job_rows), lambda i, e: (e, 0, 0)),
                  part, part, full, full, row],
        out_specs=pl.BlockSpec((tt, d), lambda i, e: (i, 0)),
        out_shape=jax.ShapeDtypeStruct((t, d), F32),
        scratch_shapes=[pltpu.VMEM((d, tt), BF16), pltpu.VMEM((d, tt), F32),
                        pltpu.VMEM((job_rows, sub), F32), pltpu.VMEM((job_rows, sub), F32),
                        pltpu.VMEM((job_rows, sub), BF16), pltpu.VMEM((job_rows, sub), BF16)],
        compiler_params=_cparams(("arbitrary", "arbitrary"), vmem_mb=VMEM_LIMIT_DENSE_MB),
    )(x, g.reshape(1, d), u, vt, a1t, cnt, a2t.reshape(grouped), rank2.reshape(grouped),
      g_out.reshape(1, d))


def _tiles(seq):
    pick = lambda want: min(want, seq)
    return dict(tm=pick(256), tq=pick(512), tgla=pick(512), tt=pick(256), td=pick(512),
                et=2048)


def kernel(x, positions, norm_mix, w_in, lam_q1, lam_k1, lam_q2, lam_k2, diff_norm, conv_w, gla_w_gate2, gla_b_gate, gla_norm, w_out, norm_ffn, peer_w_q, peer_keys1, peer_keys2, peer_u, peer_v, norm_final):
    batch, seq, d = x.shape
    t = batch * seq
    depth = w_in.shape[0]
    tl = _tiles(seq)
    xt = x.reshape(t, d)
    tabs = _rope_tables(positions, tl["tm"])
    for i in range(depth):
        w_in_p = jnp.pad(w_in[i], ((0, 0), (0, IN_PAD - IN_WIDTH))).astype(BF16)
        qt, kr, vt3, proj = _in_proj(xt, norm_mix[i], w_in_p, tabs, tl["tq"])
        lam_init = 0.8 - 0.6 * math.exp(-0.3 * i)
        o_da = _diff_attention(qt, kr, vt3, lam_q1[i], lam_k1[i], lam_q2[i], lam_k2[i], diff_norm[i],
                               batch, seq, lam_init, tl["tq"])
        o_g = _gla(proj, gla_w_gate2[i], gla_b_gate[i], gla_norm[i], batch, seq, tl["tgla"])
        xt = _out_proj(xt, o_da, proj, conv_w[i], o_g, w_out[i].astype(BF16), seq, tl["tm"])
        a1t, cnt, a2t, rank2 = _peer_route(xt, norm_ffn[i], peer_w_q[i].astype(BF16),
                                           peer_keys1[i].astype(BF16), peer_keys2[i].astype(BF16), tl["tt"])
        u3, ev3 = _expert_jobs(peer_u, peer_v, i)
        xt = _peer_dense(xt, norm_ffn[i], u3, ev3, a1t, cnt, a2t, rank2, norm_final, i == depth - 1,
                         tl["td"], tl["et"])
    return xt.reshape(batch, seq, d)
```

```python
import functools
import math

import jax
import jax.numpy as jnp
from jax import lax
from jax.experimental import pallas as pl
from jax.experimental.pallas import tpu as pltpu

F32 = jnp.float32
BF16 = jnp.bfloat16

EPS = 1e-6
HEAD_DIM = 64
DA_HEADS = 4
DA_WIDTH = 512
ROPE_THETA = 500000.0
ROT_DIM = 16
SC_WIDTH = 256
CONV_W = 3
GLA_HEADS = 4
GLA_DK = 32
GLA_DV = 64
GLA_KW = 128
GLA_VW = 256
GLA_GATE_RANK = 16
GLA_GATE_TEMP = 16.0
GLA_CHUNK = 64
GLA_SUB = 16
GLA_GROUP = 8
IN_WIDTH = 3088
IN_PAD = 3200
PEER_HEADS = 8
PEER_NKEYS = 128
PEER_TOPK = 16
PEER_JOB_ROWS = 512
LANES = 128
SUBLANES = 8
BF16_SUBLANES = 16
NEG_INF = float("-inf")
INV_SQRT2 = 1.0 / math.sqrt(2.0)

QKV_WIDTH = 3 * DA_WIDTH
REST_WIDTH = IN_PAD - QKV_WIDTH
COL_SC_B, COL_SC_C, COL_SC_H = 0, 1, 2
COL_G_V, COL_G_R = 4, 5
COL_G_Q, COL_G_K, COL_G_LR = 6, 7, 12


VMEM_LIMIT_MB = 40
VMEM_LIMIT_DENSE_MB = 52


def _cparams(sem, vmem_mb=VMEM_LIMIT_MB):
    return pltpu.CompilerParams(dimension_semantics=sem, vmem_limit_bytes=vmem_mb * 1024 * 1024)


def _rms(x, g):
    return x * lax.rsqrt(jnp.mean(x * x, axis=-1, keepdims=True) + EPS) * g


def _rope_table_kernel(pos_ref, inv_ref, mrot_ref, m1_ref, m2_ref, c_ref, s1_ref, s2_ref):
    ang = pos_ref[...] * inv_ref[...]
    c = jnp.cos(ang)
    s = jnp.sin(ang)
    mrot = mrot_ref[...]
    c_ref[...] = mrot * c + (1.0 - mrot)
    s1_ref[...] = -(m1_ref[...] * s)
    s2_ref[...] = m2_ref[...] * s


def _rope_tables(positions, tm):
    t = positions.size
    pos = jnp.broadcast_to(positions.reshape(t, 1).astype(F32), (t, LANES))
    d = jnp.arange(LANES) % HEAD_DIM
    half = ROT_DIM // 2
    inv = ROPE_THETA ** (-jnp.arange(0, ROT_DIM, 2, dtype=F32) / ROT_DIM)
    inv_row = jnp.where(d < ROT_DIM, inv[d % half], 0.0).astype(F32).reshape(1, LANES)
    mrot = (d < ROT_DIM).astype(F32).reshape(1, LANES)
    m1 = (d < half).astype(F32).reshape(1, LANES)
    m2 = ((d >= half) & (d < ROT_DIM)).astype(F32).reshape(1, LANES)
    row = pl.BlockSpec((1, LANES), lambda i: (0, 0))
    blk = pl.BlockSpec((tm, LANES), lambda i: (i, 0))
    return pl.pallas_call(
        _rope_table_kernel,
        grid=(t // tm,),
        in_specs=[blk, row, row, row, row],
        out_specs=[blk, blk, blk],
        out_shape=[jax.ShapeDtypeStruct((t, LANES), F32)] * 3,
        compiler_params=_cparams(("arbitrary",)),
    )(pos, inv_row, mrot, m1, m2)


def _in_proj_kernel(x_ref, g_ref, w_ref, c_ref, s1_ref, s2_ref, qt_ref, ko_ref, vt_ref, rest_ref,
                    *, qscale):
    xn = _rms(x_ref[...], g_ref[...]).astype(BF16)
    rest_ref[...] = jnp.dot(xn, w_ref[:, QKV_WIDTH:], preferred_element_type=F32)
    c = c_ref[...]
    s1 = s1_ref[...]
    s2 = s2_ref[...]
    half = ROT_DIM // 2

    def rope(x):
        return x * c + pltpu.roll(x, LANES - half, 1) * s1 + pltpu.roll(x, half, 1) * s2

    qkv = jnp.dot(xn, w_ref[:, :QKV_WIDTH], preferred_element_type=F32)
    for g in range(DA_WIDTH // LANES):
        sl = slice(LANES * g, LANES * (g + 1))
        col = lambda base: slice(base + LANES * g, base + LANES * (g + 1))
        qt_ref[sl, :] = (rope(qkv[:, col(0)]) * qscale).T.astype(BF16)
        ko_ref[:, sl] = rope(qkv[:, col(DA_WIDTH)]).astype(BF16)
        vt_ref[0, sl, :] = qkv[:, col(2 * DA_WIDTH)].T.astype(BF16)


def _in_proj(x, g, w, tabs, tm):
    t, d = x.shape
    c, s1, s2 = tabs
    tab = pl.BlockSpec((tm, LANES), lambda i: (i, 0))
    return pl.pallas_call(
        functools.partial(_in_proj_kernel, qscale=HEAD_DIM ** -0.5 * math.log2(math.e)),
        grid=(t // tm,),
        in_specs=[pl.BlockSpec((tm, d), lambda i: (i, 0)),
                  pl.BlockSpec((1, d), lambda i: (0, 0)),
                  pl.BlockSpec(w.shape, lambda i: (0, 0)),
                  tab, tab, tab],
        out_specs=[pl.BlockSpec((DA_WIDTH, tm), lambda i: (0, i)),
                   pl.BlockSpec((tm, DA_WIDTH), lambda i: (i, 0)),
                   pl.BlockSpec((1, DA_WIDTH, tm), lambda i: (i, 0, 0)),
                   pl.BlockSpec((tm, REST_WIDTH), lambda i: (i, 0))],
        out_shape=[jax.ShapeDtypeStruct((DA_WIDTH, t), BF16),
                   jax.ShapeDtypeStruct((t, DA_WIDTH), BF16),
                   jax.ShapeDtypeStruct((t // tm, DA_WIDTH, tm), BF16),
                   jax.ShapeDtypeStruct((t, REST_WIDTH), F32)],
        compiler_params=_cparams(("arbitrary",)),
    )(x, g.reshape(1, d), w, c, s1, s2)


def _attn_kernel(qt_ref, k_ref, vt_ref, lq1_ref, lk1_ref, lq2_ref, lk2_ref, gain_ref, o_ref,
                 acc1_ref, acc2_ref, m1_ref, l1_ref, m2_ref, l2_ref, s1a_ref, s2a_ref, s1b_ref, s2b_ref,
                 *, tq, lam_init):
    qi = pl.program_id(2)
    qt = qt_ref[...]
    row = lax.broadcasted_iota(jnp.int32, (LANES, 1), 0)
    zero = jnp.zeros_like(qt)
    qa = jnp.where(row < HEAD_DIM, qt, zero)
    qb = jnp.where(row >= HEAD_DIM, qt, zero)
    acc1_ref[...] = jnp.zeros_like(acc1_ref)
    acc2_ref[...] = jnp.zeros_like(acc2_ref)

    for st_ref in (m1_ref, m2_ref):
        st_ref[...] = jnp.full(st_ref.shape, NEG_INF, F32)
    for st_ref in (l1_ref, l2_ref):
        st_ref[...] = jnp.zeros_like(st_ref)
    maps = ((qa, m1_ref, l1_ref, acc1_ref), (qb, m2_ref, l2_ref, acc2_ref))

    def scores(j, bufs):
        off = pl.multiple_of(j * tq, tq)
        kk = k_ref[pl.ds(off, tq), :]
        for (qm, _, _, _), s_ref in zip(maps, bufs):
            s_ref[...] = jnp.dot(kk, qm, preferred_element_type=F32)

    def absorb(j, bufs, masked):
        vt = vt_ref[j]
        for (_, m_ref, l_ref, acc_ref), s_ref in zip(maps, bufs):
            s = s_ref[...]
            if masked:
                r = lax.broadcasted_iota(jnp.int32, (tq, tq), 0)
                c = lax.broadcasted_iota(jnp.int32, (tq, tq), 1)
                s = jnp.where(r <= c, s, NEG_INF)
            m = m_ref[...]
            mn = jnp.maximum(m, jnp.max(s, axis=0, keepdims=True))
            alpha = jnp.exp2(m - mn)
            p = jnp.exp2(s - mn)
            m_ref[...] = mn
            l_ref[...] = alpha * l_ref[...] + jnp.sum(p, axis=0, keepdims=True)
            acc_ref[...] = alpha * acc_ref[...] + jnp.dot(vt, p.astype(BF16),
                                                          preferred_element_type=F32)

    buf_a, buf_b = (s1a_ref, s2a_ref), (s1b_ref, s2b_ref)
    scores(0, buf_a)

    def pair(i, carry):
        j = 2 * i
        scores(j + 1, buf_b)
        absorb(j, buf_a, False)
        scores(j + 2, buf_a)
        absorb(j + 1, buf_b, False)
        return carry

    lax.fori_loop(0, qi // 2, pair, 0)

    @pl.when(qi % 2 == 1)
    def _():
        scores(qi, buf_b)
        absorb(qi - 1, buf_a, False)
        absorb(qi, buf_b, True)

    @pl.when(qi % 2 == 0)
    def _():
        absorb(qi, buf_a, True)

    l1, l2 = l1_ref[...], l2_ref[...]

    lam = (jnp.exp(jnp.sum(lq1_ref[...] * lk1_ref[...], axis=-1, keepdims=True))
           - jnp.exp(jnp.sum(lq2_ref[...] * lk2_ref[...], axis=-1, keepdims=True)) + lam_init)
    o = acc1_ref[...] / l1 - lam * (acc2_ref[...] / l2)
    y = o * lax.rsqrt(jnp.mean(o * o, axis=0, keepdims=True) + EPS) * gain_ref[...]
    o_ref[...] = (y * (1.0 - lam_init)).T


def _diff_attention(qt, kr, vt3, lq1, lk1, lq2, lk2, gain, batch, seq, lam_init, tq):
    t = kr.shape[0]
    nq = seq // tq
    vec = lambda n: pl.BlockSpec((1, n), lambda b, h, i: (0, 0))
    return pl.pallas_call(
        functools.partial(_attn_kernel, tq=tq, lam_init=lam_init),
        grid=(batch, DA_HEADS, nq),
        in_specs=[pl.BlockSpec((LANES, tq), lambda b, h, i: (h, b * nq + i)),
                  pl.BlockSpec((seq, LANES), lambda b, h, i: (b, h)),
                  pl.BlockSpec((nq, LANES, tq), lambda b, h, i: (b, h, 0)),
                  vec(HEAD_DIM), vec(HEAD_DIM), vec(HEAD_DIM), vec(HEAD_DIM),
                  pl.BlockSpec((LANES, 1), lambda b, h, i: (0, 0))],
        out_specs=pl.BlockSpec((tq, LANES), lambda b, h, i: (b * nq + i, h)),
        out_shape=jax.ShapeDtypeStruct((t, DA_WIDTH), F32),
        scratch_shapes=([pltpu.VMEM((LANES, tq), F32)] * 2 + [pltpu.VMEM((1, tq), F32)] * 4
                        + [pltpu.VMEM((tq, tq), F32)] * 4),
        compiler_params=_cparams(("arbitrary", "arbitrary", "arbitrary")),
    )(qt, kr, vt3, lq1.reshape(1, -1), lk1.reshape(1, -1), lq2.reshape(1, -1), lk2.reshape(1, -1),
      gain.reshape(-1, 1))


def _short_conv_tile(b, c, h, cp, hp, w, first):
    z = c * h
    zp = jnp.where(first, jnp.zeros_like(cp), cp * hp)
    row = lax.broadcasted_iota(jnp.int32, (z.shape[0], 1), 0)
    z1 = jnp.where(row == 0, zp[7:8, :], pltpu.roll(z, 1, 0))
    z2 = pltpu.roll(z, 2, 0)
    z2 = jnp.where(row == 0, zp[6:7, :], jnp.where(row == 1, zp[7:8, :], z2))
    return b * (w[0:1, :] * z2 + w[1:2, :] * z1 + w[2:3, :] * z)


def _split3(x):
    hi = x.astype(BF16)
    r = x - hi.astype(F32)
    mid = r.astype(BF16)
    lo = (r - mid.astype(F32)).astype(BF16)
    return hi, mid, lo


def _gla_kernel(q_ref, k_ref, v_ref, r_ref, lr_ref, wg_ref, bg_ref, gn_ref, o_ref, st_ref, *, tc):
    ch = GLA_CHUNK

    @pl.when(pl.program_id(1) == 0)
    def _():
        st_ref[...] = jnp.zeros_like(st_ref)

    ii = lax.broadcasted_iota(jnp.int32, (ch, ch), 0)
    jj = lax.broadcasted_iota(jnp.int32, (ch, ch), 1)
    tri = (ii >= jj).astype(BF16)
    rk = lax.broadcasted_iota(jnp.int32, (GLA_VW, GLA_KW), 0)
    ck = lax.broadcasted_iota(jnp.int32, (GLA_VW, GLA_KW), 1)
    bd_k = (rk // GLA_DV) == (ck // GLA_DK)
    rv = lax.broadcasted_iota(jnp.int32, (GLA_VW, GLA_VW), 0)
    cv = lax.broadcasted_iota(jnp.int32, (GLA_VW, GLA_VW), 1)
    bd_v = (rv // GLA_DV) == (cv // GLA_DV)
    seg = bd_v.astype(BF16)
    sb = GLA_SUB
    nsb = ch // sb
    hs = GLA_HEADS * sb
    iota = lambda shape, axis: lax.broadcasted_iota(jnp.int32, shape, axis)
    sub_row = iota((sb, 1), 0)
    head_expand = ((iota((GLA_KW, GLA_VW), 0) // GLA_DK)
                   == (iota((GLA_KW, GLA_VW), 1) // GLA_DV)).astype(BF16)
    sum_j = ((iota((ch, ch * sb), 1) // sb) == iota((ch, ch * sb), 0)).astype(BF16)
    bd_ks = (iota((hs, GLA_KW), 0) // sb) == (iota((hs, GLA_KW), 1) // GLA_DK)
    bd_vs = (iota((hs, GLA_VW), 0) // sb) == (iota((hs, GLA_VW), 1) // GLA_DV)
    off_w = (nsb - 1) * hs
    att_keep = iota((ch, off_w), 0) >= (iota((ch, off_w), 1) // hs + 1) * sb
    dn_t = (((1,), (1,)), ((), ()))
    scale = GLA_DK ** -0.5

    def stage_decay(rows):
        x = jnp.dot(lr_ref[rows, :].astype(BF16), wg_ref[...], preferred_element_type=F32) + bg_ref[...]
        la = (jnp.minimum(x, 0.0) - jnp.log1p(jnp.exp(-jnp.abs(x)))) * (1.0 / GLA_GATE_TEMP)
        hi, mid, lo = _split3(la)
        return (jnp.dot(tri, hi, preferred_element_type=F32) + jnp.dot(tri, mid, preferred_element_type=F32)
                + jnp.dot(tri, lo, preferred_element_type=F32))

    def stage_pairwise(q, k, v, b):
        pieces, vrep = [], []
        for sbi in range(nsb):
            rs = slice(sbi * sb, (sbi + 1) * sb)
            qs, ks, bs = q[rs, :], k[rs, :], b[rs, :]
            for i in range(sb):
                dec = jnp.exp(jnp.where(sub_row <= i, bs[i:i + 1, :] - bs, NEG_INF))
                pieces.append(qs[i:i + 1, :] * ks * dec)
            vrep += [v[rs, :]] * sb
        x3 = jnp.concatenate(pieces, axis=0)
        x3h = x3.astype(BF16)
        return x3h, (x3 - x3h.astype(F32)).astype(BF16), jnp.concatenate(vrep, axis=0)

    def stage_head_sums(x3h, x3l):
        return (jnp.dot(x3h, head_expand, preferred_element_type=F32)
                + jnp.dot(x3l, head_expand, preferred_element_type=F32))

    def stage_cross_operands(q, k, vb, b):
        qts, kbds, vbds = [], [], []
        zeros_k = jnp.zeros((hs, GLA_KW), BF16)
        for sbi in range(nsb - 1):
            rs = slice(sbi * sb, (sbi + 1) * sb)
            end = (sbi + 1) * sb
            b_end = b[end - 1:end, :]
            qts.append((q * jnp.exp(jnp.minimum(b - b_end, 0.0))).astype(BF16))
            kt = (k[rs, :] * jnp.exp(b_end - b[rs, :])).astype(BF16)
            kbd = jnp.where(bd_ks, jnp.concatenate([kt] * GLA_HEADS, axis=0), jnp.zeros((), BF16))
            kbds.append(jnp.concatenate([kbd if s2 == sbi else zeros_k for s2 in range(nsb - 1)], axis=1))
            vbds.append(jnp.where(bd_vs, jnp.concatenate([vb[rs, :]] * GLA_HEADS, axis=0),
                                  jnp.zeros((), BF16)))
        return jnp.concatenate(qts, axis=1), jnp.concatenate(kbds, axis=0), jnp.concatenate(vbds, axis=0)

    def stage_cross_scores(q_all, k_all):
        att = lax.dot_general(q_all, k_all, dn_t, preferred_element_type=F32)
        return jnp.where(att_keep, att, 0.0).astype(BF16)

    def stage_state_update(k, v, b):
        b_last = b[ch - 1:ch, :]
        kl = (k * jnp.exp(b_last - b)).astype(BF16)
        ut = jnp.dot(v.T.astype(BF16), kl, preferred_element_type=F32)
        return jnp.exp(b_last), jnp.where(bd_k, ut, 0.0)

    def stage_mean_square(o):
        o2 = o * o
        o2h = o2.astype(BF16)
        o2l = (o2 - o2h.astype(F32)).astype(BF16)
        return (jnp.dot(o2h, seg, preferred_element_type=F32)
                + jnp.dot(o2l, seg, preferred_element_type=F32)) * (1.0 / GLA_DV)

    def stage_store(rows, o, ms):
        r = r_ref[rows, :]
        gate = r * (1.0 / (1.0 + jnp.exp(-r)))
        o_ref[rows, :] = o * lax.rsqrt(ms + EPS) * gn_ref[...] * gate

    each = lambda f, *cols: [f(*args) for args in zip(*cols)]
    gsize = min(GLA_GROUP, tc // ch)

    def group(g, carry):
        rows = [pl.ds(pl.multiple_of((g * gsize + c) * ch, ch), ch) for c in range(gsize)]
        b = each(stage_decay, rows)
        q = [q_ref[r, :] * scale for r in rows]
        k = [k_ref[r, :] for r in rows]
        v = [v_ref[r, :] for r in rows]
        vb = [x.astype(BF16) for x in v]
        x3h, x3l, vrep = zip(*each(stage_pairwise, q, k, v, b))
        w = each(stage_head_sums, x3h, x3l)
        wv = each(lambda w_, v_: (w_ * v_).astype(BF16), w, vrep)
        o = each(lambda wv_: jnp.dot(sum_j, wv_, preferred_element_type=F32), wv)
        q_all, k_all, v_all = zip(*each(stage_cross_operands, q, k, vb, b))
        att = each(stage_cross_scores, q_all, k_all)
        o = each(lambda o_, a_, v_: o_ + jnp.dot(a_, v_, preferred_element_type=F32), o, att, v_all)
        bq = each(lambda q_, b_: (q_ * jnp.exp(b_)).astype(BF16), q, b)
        decay, ut = zip(*each(stage_state_update, k, v, b))
        st = st_ref[...]
        states = []
        for c in range(gsize):
            states.append(st.astype(BF16))
            st = st * decay[c] + ut[c]
        st_ref[...] = st
        o = each(lambda o_, q_, s_: o_ + lax.dot_general(q_, s_, dn_t, preferred_element_type=F32),
                 o, bq, states)
        ms = each(stage_mean_square, o)
        each(stage_store, rows, o, ms)
        return carry

    lax.fori_loop(0, tc // ch // gsize, group, 0)


def _gla(proj, wg, bg, gn, batch, seq, tc):
    t = proj.shape[0]
    nb = seq // tc
    blk = lambda w, j: pl.BlockSpec((tc, w), lambda b, i, j=j: (b * nb + i, j))
    const = lambda r, c: pl.BlockSpec((r, c), lambda b, i: (0, 0))
    wg_pad = jnp.zeros((LANES, GLA_KW), F32).at[:GLA_GATE_RANK].set(wg).astype(BF16)
    gn_row = jnp.tile(gn, GLA_HEADS).reshape(1, GLA_VW)
    return pl.pallas_call(
        functools.partial(_gla_kernel, tc=tc),
        grid=(batch, nb),
        in_specs=[blk(GLA_KW, COL_G_Q), blk(GLA_KW, COL_G_K), blk(GLA_VW, COL_G_V), blk(GLA_VW, COL_G_R),
                  blk(LANES, COL_G_LR), const(LANES, GLA_KW), const(1, GLA_KW), const(1, GLA_VW)],
        out_specs=pl.BlockSpec((tc, GLA_VW), lambda b, i: (b * nb + i, 0)),
        out_shape=jax.ShapeDtypeStruct((t, GLA_VW), F32),
        scratch_shapes=[pltpu.VMEM((GLA_VW, GLA_KW), F32)],
        compiler_params=_cparams(("arbitrary", "arbitrary")),
    )(proj, proj, proj, proj, proj, wg_pad, bg.reshape(1, GLA_KW), gn_row)


def _out_proj_kernel(x_ref, a_ref, b_ref, c_ref, h_ref, cp_ref, hp_ref, cw_ref, g_ref, w_ref, o_ref,
                     *, blocks_per_seq):
    first = (pl.program_id(0) % blocks_per_seq) == 0
    o_sc = _short_conv_tile(b_ref[...], c_ref[...], h_ref[...], cp_ref[...], hp_ref[...], cw_ref[...], first)
    acc = jnp.dot(a_ref[...].astype(BF16), w_ref[0:DA_WIDTH, :], preferred_element_type=F32)
    acc += jnp.dot(o_sc.astype(BF16), w_ref[DA_WIDTH:DA_WIDTH + SC_WIDTH, :],
                   preferred_element_type=F32)
    acc += jnp.dot(g_ref[...].astype(BF16), w_ref[DA_WIDTH + SC_WIDTH:, :], preferred_element_type=F32)
    o_ref[...] = x_ref[...] + acc


def _out_proj(x, o_da, proj, conv_w, o_g, w, seq, tm):
    t, d = x.shape
    rb = tm // SUBLANES
    blk = lambda n: pl.BlockSpec((tm, n), lambda i: (i, 0))
    cur = lambda j: pl.BlockSpec((tm, SC_WIDTH), lambda i, j=j: (i, j))
    prev = lambda j: pl.BlockSpec((SUBLANES, SC_WIDTH), lambda i, j=j: (jnp.maximum(i * rb - 1, 0), j))
    return pl.pallas_call(
        functools.partial(_out_proj_kernel, blocks_per_seq=seq // tm),
        grid=(t // tm,),
        in_specs=[blk(d), blk(DA_WIDTH),
                  cur(COL_SC_B), cur(COL_SC_C), cur(COL_SC_H), prev(COL_SC_C), prev(COL_SC_H),
                  pl.BlockSpec((CONV_W, SC_WIDTH), lambda i: (0, 0)),
                  blk(GLA_VW), pl.BlockSpec(w.shape, lambda i: (0, 0))],
        out_specs=blk(d),
        out_shape=jax.ShapeDtypeStruct((t, d), F32),
        compiler_params=_cparams(("arbitrary",)),
    )(x, o_da, proj, proj, proj, proj, proj, conv_w, o_g, w)


def _merge_exchange_network(n):
    pairs = []
    t = max(1, math.ceil(math.log2(n)))
    p = 2 ** (t - 1)
    while p > 0:
        q, r, d = 2 ** (t - 1), 0, p
        while d > 0:
            pairs += [(i, i + d) for i in range(n - d) if (i & p) == r]
            d, q, r = q - p, q // 2, p
        p //= 2
    return pairs


def _top_desc(x, n):
    groups = [x[SUBLANES * i:SUBLANES * (i + 1), :] for i in range(x.shape[0] // SUBLANES)]
    for i, j in _merge_exchange_network(len(groups)):
        groups[i], groups[j] = jnp.maximum(groups[i], groups[j]), jnp.minimum(groups[i], groups[j])
    vals = []
    for r in range(n):
        m = jnp.max(groups[0], axis=0, keepdims=True)
        vals.append(m)
        hit = groups[0] == m
        live = min(len(groups), n - r)
        for k in range(live - 1):
            groups[k] = jnp.where(hit, groups[k + 1], groups[k])
        if live == len(groups) and live > 0 and r + 1 < n:
            groups[live - 1] = jnp.where(hit, NEG_INF, groups[live - 1])
    return vals


def _count_true_prefix(test, rows):
    assert len(rows) == 16
    pick = lambda c, a, b: jnp.where(c, a, b)
    c8 = test(rows[7])
    c4 = test(pick(c8, rows[11], rows[3]))
    c2 = test(pick(c8, pick(c4, rows[13], rows[9]), pick(c4, rows[5], rows[1])))
    hi = pick(c4, pick(c2, rows[14], rows[12]), pick(c2, rows[10], rows[8]))
    lo = pick(c4, pick(c2, rows[6], rows[4]), pick(c2, rows[2], rows[0]))
    c1 = test(pick(c8, hi, lo))
    count = (pick(c8, 8.0, 0.0) + pick(c4, 4.0, 0.0)) + (pick(c2, 2.0, 0.0) + pick(c1, 1.0, 0.0))
    return pick(test(rows[15]), 16.0, count)


def _ranks_of(x, vals):
    return _count_true_prefix(lambda v: v > x, vals)


def _peer_route_kernel(x_ref, g_ref, wq_ref, k1_ref, k2_ref, a1_ref, cnt_ref, a2_ref, rank2_ref):
    k = PEER_TOPK
    tt = x_ref.shape[0]
    xn = _rms(x_ref[...], g_ref[...]).astype(BF16)
    q = jnp.dot(xn, wq_ref[...], preferred_element_type=F32)
    dn = (((1,), (1,)), ((), ()))
    for h in range(PEER_HEADS):
        qa = q[:, (2 * h) * LANES:(2 * h + 1) * LANES].astype(BF16)
        qb = q[:, (2 * h + 1) * LANES:(2 * h + 2) * LANES].astype(BF16)
        s1 = lax.dot_general(k1_ref[...], qa, dn, preferred_element_type=F32)
        s2 = lax.dot_general(k2_ref[...], qb, dn, preferred_element_type=F32)
        v1 = _top_desc(s1, k)
        v2 = _top_desc(s2, k)
        rank2 = _ranks_of(s2, v2)
        cands = [v1[i] + v2[j] for i in range(k) for j in range(k // (i + 1))]
        pad = (-len(cands)) % 8
        cands += [jnp.full((1, tt), NEG_INF, F32)] * pad
        top = _top_desc(jnp.concatenate(cands, axis=0), k)
        z = jnp.ones((1, tt), F32)
        for r in range(1, k):
            z = z + jnp.exp(top[r] - top[0])
        thr = top[k - 1]
        cnt = _count_true_prefix(lambda v: (s1 + v) >= thr, v2)
        cnt_ref[h] = jnp.where(s1 >= v1[k - 1], cnt, 0.0)
        rank2_ref[h] = rank2.astype(BF16)
        a1_ref[h] = jnp.exp(s1 - v1[0]) * (1.0 / z)
        a2_ref[h] = jnp.exp(s2 - v2[0]).astype(BF16)


def _peer_route(x, g, wq, k1, k2, tt):
    t, d = x.shape
    shape = (PEER_HEADS, PEER_NKEYS, t)
    out = pl.BlockSpec((PEER_HEADS, PEER_NKEYS, tt), lambda i: (0, 0, i))
    f32 = jax.ShapeDtypeStruct(shape, F32)
    bf16 = jax.ShapeDtypeStruct(shape, BF16)
    return pl.pallas_call(
        _peer_route_kernel,
        grid=(t // tt,),
        in_specs=[pl.BlockSpec((tt, d), lambda i: (i, 0)),
                  pl.BlockSpec((1, d), lambda i: (0, 0)),
                  pl.BlockSpec(wq.shape, lambda i: (0, 0)),
                  pl.BlockSpec(k1.shape, lambda i: (0, 0)),
                  pl.BlockSpec(k2.shape, lambda i: (0, 0))],
        out_specs=[out, out, out, out],
        out_shape=[f32, f32, bf16, bf16],
        compiler_params=_cparams(("arbitrary",)),
    )(x, g.reshape(1, d), wq, k1, k2)


def _peer_dense_kernel(x_ref, g_ref, u_ref, vt_ref, a1_ref, cnt_ref, a2_ref, rank2_ref, gout_ref, o_ref,
                       xnt_ref, acc_ref, hta_ref, htb_ref, pta_ref, ptb_ref, *, norm_out):
    e = pl.program_id(1)

    @pl.when(e == 0)
    def _():
        xn = _rms(x_ref[...], g_ref[...])
        xnt_ref[...] = xn.T.astype(BF16)
        acc_ref[...] = jnp.zeros_like(acc_ref)

    zero = jnp.zeros((), BF16)
    n_q, job_rows, _ = u_ref.shape
    sub = xnt_ref.shape[1] // 2
    n1_per_job = job_rows // PEER_NKEYS
    pack = BF16_SUBLANES
    groups = PEER_NKEYS // pack
    col_a, col_b = slice(0, sub), slice(sub, 2 * sub)
    ht_refs = {col_a: hta_ref, col_b: htb_ref}
    pt_refs = {col_a: pta_ref, col_b: ptb_ref}

    def matmul1(q, cols):
        ht_refs[cols][...] = jnp.dot(u_ref[q], xnt_ref[:, cols], preferred_element_type=F32)

    def gate(q, cols):
        ht_ref, pt_ref = ht_refs[cols], pt_refs[cols]
        for rl in range(n1_per_job):
            r = q * n1_per_job + rl
            w = None
            for h in range(PEER_HEADS):
                cnt = jnp.broadcast_to(cnt_ref[h, pl.ds(r, 1), cols], (pack, sub)).astype(BF16)
                a1 = jnp.broadcast_to(a1_ref[h, pl.ds(r, 1), cols], (pack, sub)).astype(BF16)
                sel = rank2_ref[h, :, :, cols] < cnt[None]
                term = a1[None] * jnp.where(sel, a2_ref[h, :, :, cols], zero)
                w = term if w is None else w + term
            rows = slice(rl * PEER_NKEYS, (rl + 1) * PEER_NKEYS)
            hb = ht_ref[rows, :]
            gelu = hb * (1.0 + lax.erf(hb))
            p = w * gelu.astype(BF16).reshape(groups, pack, sub)
            pt_ref[rows, :] = p.reshape(PEER_NKEYS, sub)

    def matmul2(q, cols):
        acc_ref[:, cols] += jnp.dot(vt_ref[q], pt_refs[cols][...], preferred_element_type=F32)

    def pair(q, has_prev, has_next):
        matmul1(q, col_b)
        gate(q, col_a)
        if has_prev:
            matmul2(q - 1, col_b)
        if has_next:
            matmul1(q + 1, col_a)
        gate(q, col_b)
        matmul2(q, col_a)

    matmul1(0, col_a)
    if n_q == 1:
        pair(0, False, False)
    else:
        pair(0, False, True)

        def body(q, carry):
            pair(q, True, True)
            return carry

        lax.fori_loop(1, n_q - 1, body, 0)
        pair(n_q - 1, True, False)
    matmul2(n_q - 1, col_b)

    @pl.when(e == pl.num_programs(1) - 1)
    def _():
        y = x_ref[...] + acc_ref[...].T
        o_ref[...] = _rms(y, gout_ref[...]) if norm_out else y


def _expert_jobs_kernel(u_ref, v_ref, uo_ref, vo_ref):
    uo_ref[0] = (u_ref[0] * INV_SQRT2).astype(BF16)
    vo_ref[0] = (v_ref[0].T * (0.5 / INV_SQRT2)).astype(BF16)


def _expert_jobs(u, v, layer):
    _, n_exp, d = u.shape
    n_jobs = n_exp // PEER_JOB_ROWS
    blk = pl.BlockSpec((1, PEER_JOB_ROWS, d), lambda j: (layer, j, 0))
    return pl.pallas_call(
        _expert_jobs_kernel,
        grid=(n_jobs,),
        in_specs=[blk, blk],
        out_specs=[pl.BlockSpec((1, PEER_JOB_ROWS, d), lambda j: (j, 0, 0)),
                   pl.BlockSpec((1, d, PEER_JOB_ROWS), lambda j: (j, 0, 0))],
        out_shape=[jax.ShapeDtypeStruct((n_jobs, PEER_JOB_ROWS, d), BF16),
                   jax.ShapeDtypeStruct((n_jobs, d, PEER_JOB_ROWS), BF16)],
        compiler_params=_cparams(("arbitrary",)),
    )(u, v)


def _peer_dense(x, g, u, vt, a1t, cnt, a2t, rank2, g_out, norm_out, tt, et):
    t, d = x.shape
    n_jobs, job_rows, _ = u.shape
    n_q = et // job_rows
    n1_per_tile = et // PEER_NKEYS
    groups = PEER_NKEYS // BF16_SUBLANES
    grouped = (PEER_HEADS, groups, BF16_SUBLANES, t)
    full = pl.BlockSpec((PEER_HEADS, groups, BF16_SUBLANES, tt), lambda i, e: (0, 0, 0, i))
    part = pl.BlockSpec((PEER_HEADS, n1_per_tile, tt), lambda i, e: (0, e, i))
    sub = tt // 2
    row = pl.BlockSpec((1, d), lambda i, e: (0, 0))
    return pl.pallas_call(
        functools.partial(_peer_dense_kernel, norm_out=norm_out),
        grid=(t // tt, n_jobs // n_q),
        in_specs=[pl.BlockSpec((tt, d), lambda i, e: (i, 0)),
                  row,
                  pl.BlockSpec((n_q, job_rows, d), lambda i, e: (e, 0, 0)),
                  pl.BlockSpec((n_q, d, job_rows), lambda i, e: (e, 0, 0)),
                  part, part, full, full, row],
        out_specs=pl.BlockSpec((tt, d), lambda i, e: (i, 0)),
        out_shape=jax.ShapeDtypeStruct((t, d), F32),
        scratch_shapes=[pltpu.VMEM((d, tt), BF16), pltpu.VMEM((d, tt), F32),
                        pltpu.VMEM((job_rows, sub), F32), pltpu.VMEM((job_rows, sub), F32),
                        pltpu.VMEM((job_rows, sub), BF16), pltpu.VMEM((job_rows, sub), BF16)],
        compiler_params=_cparams(("arbitrary", "arbitrary"), vmem_mb=VMEM_LIMIT_DENSE_MB),
    )(x, g.reshape(1, d), u, vt, a1t, cnt, a2t.reshape(grouped), rank2.reshape(grouped),
      g_out.reshape(1, d))


def _tiles(seq):
    pick = lambda want: min(want, seq)
    return dict(tm=pick(512), tq=pick(512), tgla=pick(512), tt=pick(256), td=pick(512),
                et=2048)


def kernel(x, positions, norm_mix, w_in, lam_q1, lam_k1, lam_q2, lam_k2, diff_norm, conv_w, gla_w_gate2, gla_b_gate, gla_norm, w_out, norm_ffn, peer_w_q, peer_keys1, peer_keys2, peer_u, peer_v, norm_final):
    batch, seq, d = x.shape
    t = batch * seq
    depth = w_in.shape[0]
    tl = _tiles(seq)
    xt = x.reshape(t, d)
    tabs = _rope_tables(positions, tl["tm"])
    for i in range(depth):
        w_in_p = jnp.pad(w_in[i], ((0, 0), (0, IN_PAD - IN_WIDTH))).astype(BF16)
        qt, kr, vt3, proj = _in_proj(xt, norm_mix[i], w_in_p, tabs, tl["tq"])
        lam_init = 0.8 - 0.6 * math.exp(-0.3 * i)
        o_da = _diff_attention(qt, kr, vt3, lam_q1[i], lam_k1[i], lam_q2[i], lam_k2[i], diff_norm[i],
                               batch, seq, lam_init, tl["tq"])
        o_g = _gla(proj, gla_w_gate2[i], gla_b_gate[i], gla_norm[i], batch, seq, tl["tgla"])
        xt = _out_proj(xt, o_da, proj, conv_w[i], o_g, w_out[i].astype(BF16), seq, tl["tm"])
        a1t, cnt, a2t, rank2 = _peer_route(xt, norm_ffn[i], peer_w_q[i].astype(BF16),
                                           peer_keys1[i].astype(BF16), peer_keys2[i].astype(BF16), tl["tt"])
        u3, ev3 = _expert_jobs(peer_u, peer_v, i)
        xt = _peer_dense(xt, norm_ffn[i], u3, ev3, a1t, cnt, a2t, rank2, norm_final, i == depth - 1,
                         tl["td"], tl["et"])
    return xt.reshape(batch, seq, d)
```

```python
import functools
import math

import jax
import jax.numpy as jnp
from jax import lax
from jax.experimental import pallas as pl
from jax.experimental.pallas import tpu as pltpu

F32 = jnp.float32
BF16 = jnp.bfloat16

EPS = 1e-6
HEAD_DIM = 64
DA_HEADS = 4
DA_WIDTH = 512
ROPE_THETA = 500000.0
ROT_DIM = 16
SC_WIDTH = 256
CONV_W = 3
GLA_HEADS = 4
GLA_DK = 32
GLA_DV = 64
GLA_KW = 128
GLA_VW = 256
GLA_GATE_RANK = 16
GLA_GATE_TEMP = 16.0
GLA_CHUNK = 64
GLA_SUB = 16
GLA_GROUP = 8
IN_WIDTH = 3088
IN_PAD = 3200
PEER_HEADS = 8
PEER_NKEYS = 128
PEER_TOPK = 16
PEER_JOB_ROWS = 512
LANES = 128
SUBLANES = 8
BF16_SUBLANES = 16
NEG_INF = float("-inf")
INV_SQRT2 = 1.0 / math.sqrt(2.0)

QKV_WIDTH = 3 * DA_WIDTH
REST_WIDTH = IN_PAD - QKV_WIDTH
COL_SC_B, COL_SC_C, COL_SC_H = 0, 1, 2
COL_G_V, COL_G_R = 4, 5
COL_G_Q, COL_G_K, COL_G_LR = 6, 7, 12


VMEM_LIMIT_MB = 40
VMEM_LIMIT_DENSE_MB = 52


def _cparams(sem, vmem_mb=VMEM_LIMIT_MB):
    return pltpu.CompilerParams(dimension_semantics=sem, vmem_limit_bytes=vmem_mb * 1024 * 1024)


def _rms(x, g):
    return x * lax.rsqrt(jnp.mean(x * x, axis=-1, keepdims=True) + EPS) * g


def _rope_table_kernel(pos_ref, inv_ref, mrot_ref, m1_ref, m2_ref, c_ref, s1_ref, s2_ref):
    ang = pos_ref[...] * inv_ref[...]
    c = jnp.cos(ang)
    s = jnp.sin(ang)
    mrot = mrot_ref[...]
    c_ref[...] = mrot * c + (1.0 - mrot)
    s1_ref[...] = -(m1_ref[...] * s)
    s2_ref[...] = m2_ref[...] * s


def _rope_tables(positions, tm):
    t = positions.size
    pos = jnp.broadcast_to(positions.reshape(t, 1).astype(F32), (t, LANES))
    d = jnp.arange(LANES) % HEAD_DIM
    half = ROT_DIM // 2
    inv = ROPE_THETA ** (-jnp.arange(0, ROT_DIM, 2, dtype=F32) / ROT_DIM)
    inv_row = jnp.where(d < ROT_DIM, inv[d % half], 0.0).astype(F32).reshape(1, LANES)
    mrot = (d < ROT_DIM).astype(F32).reshape(1, LANES)
    m1 = (d < half).astype(F32).reshape(1, LANES)
    m2 = ((d >= half) & (d < ROT_DIM)).astype(F32).reshape(1, LANES)
    row = pl.BlockSpec((1, LANES), lambda i: (0, 0))
    blk = pl.BlockSpec((tm, LANES), lambda i: (i, 0))
    return pl.pallas_call(
        _rope_table_kernel,
        grid=(t // tm,),
        in_specs=[blk, row, row, row, row],
        out_specs=[blk, blk, blk],
        out_shape=[jax.ShapeDtypeStruct((t, LANES), F32)] * 3,
        compiler_params=_cparams(("arbitrary",)),
    )(pos, inv_row, mrot, m1, m2)


def _in_proj_kernel(x_ref, g_ref, w_ref, c_ref, s1_ref, s2_ref, qt_ref, ko_ref, vt_ref, rest_ref,
                    *, qscale):
    xn = _rms(x_ref[...], g_ref[...]).astype(BF16)
    rest_ref[...] = jnp.dot(xn, w_ref[:, QKV_WIDTH:], preferred_element_type=F32)
    c = c_ref[...]
    s1 = s1_ref[...]
    s2 = s2_ref[...]
    half = ROT_DIM // 2

    def rope(x):
        return x * c + pltpu.roll(x, LANES - half, 1) * s1 + pltpu.roll(x, half, 1) * s2

    qkv = jnp.dot(xn, w_ref[:, :QKV_WIDTH], preferred_element_type=F32)
    for g in range(DA_WIDTH // LANES):
        sl = slice(LANES * g, LANES * (g + 1))
        col = lambda base: slice(base + LANES * g, base + LANES * (g + 1))
        qt_ref[sl, :] = (rope(qkv[:, col(0)]) * qscale).T.astype(BF16)
        ko_ref[:, sl] = rope(qkv[:, col(DA_WIDTH)]).astype(BF16)
        vt_ref[0, sl, :] = qkv[:, col(2 * DA_WIDTH)].T.astype(BF16)


def _in_proj(x, g, w, tabs, tm):
    t, d = x.shape
    c, s1, s2 = tabs
    tab = pl.BlockSpec((tm, LANES), lambda i: (i, 0))
    return pl.pallas_call(
        functools.partial(_in_proj_kernel, qscale=HEAD_DIM ** -0.5 * math.log2(math.e)),
        grid=(t // tm,),
        in_specs=[pl.BlockSpec((tm, d), lambda i: (i, 0)),
                  pl.BlockSpec((1, d), lambda i: (0, 0)),
                  pl.BlockSpec(w.shape, lambda i: (0, 0)),
                  tab, tab, tab],
        out_specs=[pl.BlockSpec((DA_WIDTH, tm), lambda i: (0, i)),
                   pl.BlockSpec((tm, DA_WIDTH), lambda i: (i, 0)),
                   pl.BlockSpec((1, DA_WIDTH, tm), lambda i: (i, 0, 0)),
                   pl.BlockSpec((tm, REST_WIDTH), lambda i: (i, 0))],
        out_shape=[jax.ShapeDtypeStruct((DA_WIDTH, t), BF16),
                   jax.ShapeDtypeStruct((t, DA_WIDTH), BF16),
                   jax.ShapeDtypeStruct((t // tm, DA_WIDTH, tm), BF16),
                   jax.ShapeDtypeStruct((t, REST_WIDTH), F32)],
        compiler_params=_cparams(("arbitrary",)),
    )(x, g.reshape(1, d), w, c, s1, s2)


def _attn_kernel(qt_ref, k_ref, vt_ref, lq1_ref, lk1_ref, lq2_ref, lk2_ref, gain_ref, o_ref,
                 acc1_ref, acc2_ref, m1_ref, l1_ref, m2_ref, l2_ref, s1a_ref, s2a_ref, s1b_ref, s2b_ref,
                 *, tq, lam_init):
    qi = pl.program_id(2)
    qt = qt_ref[...]
    row = lax.broadcasted_iota(jnp.int32, (LANES, 1), 0)
    zero = jnp.zeros_like(qt)
    qa = jnp.where(row < HEAD_DIM, qt, zero)
    qb = jnp.where(row >= HEAD_DIM, qt, zero)
    acc1_ref[...] = jnp.zeros_like(acc1_ref)
    acc2_ref[...] = jnp.zeros_like(acc2_ref)

    for st_ref in (m1_ref, m2_ref):
        st_ref[...] = jnp.full(st_ref.shape, NEG_INF, F32)
    for st_ref in (l1_ref, l2_ref):
        st_ref[...] = jnp.zeros_like(st_ref)
    maps = ((qa, m1_ref, l1_ref, acc1_ref), (qb, m2_ref, l2_ref, acc2_ref))

    def scores(j, bufs):
        off = pl.multiple_of(j * tq, tq)
        kk = k_ref[pl.ds(off, tq), :]
        for (qm, _, _, _), s_ref in zip(maps, bufs):
            s_ref[...] = jnp.dot(kk, qm, preferred_element_type=F32)

    def absorb(j, bufs, masked):
        vt = vt_ref[j]
        for (_, m_ref, l_ref, acc_ref), s_ref in zip(maps, bufs):
            s = s_ref[...]
            if masked:
                r = lax.broadcasted_iota(jnp.int32, (tq, tq), 0)
                c = lax.broadcasted_iota(jnp.int32, (tq, tq), 1)
                s = jnp.where(r <= c, s, NEG_INF)
            m = m_ref[...]
            mn = jnp.maximum(m, jnp.max(s, axis=0, keepdims=True))
            alpha = jnp.exp2(m - mn)
            p = jnp.exp2(s - mn)
            m_ref[...] = mn
            l_ref[...] = alpha * l_ref[...] + jnp.sum(p, axis=0, keepdims=True)
            acc_ref[...] = alpha * acc_ref[...] + jnp.dot(vt, p.astype(BF16),
                                                          preferred_element_type=F32)

    buf_a, buf_b = (s1a_ref, s2a_ref), (s1b_ref, s2b_ref)
    scores(0, buf_a)

    def pair(i, carry):
        j = 2 * i
        scores(j + 1, buf_b)
        absorb(j, buf_a, False)
        scores(j + 2, buf_a)
        absorb(j + 1, buf_b, False)
        return carry

    lax.fori_loop(0, qi // 2, pair, 0)

    @pl.when(qi % 2 == 1)
    def _():
        scores(qi, buf_b)
        absorb(qi - 1, buf_a, False)
        absorb(qi, buf_b, True)

    @pl.when(qi % 2 == 0)
    def _():
        absorb(qi, buf_a, True)

    l1, l2 = l1_ref[...], l2_ref[...]

    lam = (jnp.exp(jnp.sum(lq1_ref[...] * lk1_ref[...], axis=-1, keepdims=True))
           - jnp.exp(jnp.sum(lq2_ref[...] * lk2_ref[...], axis=-1, keepdims=True)) + lam_init)
    o = acc1_ref[...] / l1 - lam * (acc2_ref[...] / l2)
    y = o * lax.rsqrt(jnp.mean(o * o, axis=0, keepdims=True) + EPS) * gain_ref[...]
    o_ref[...] = (y * (1.0 - lam_init)).T


def _diff_attention(qt, kr, vt3, lq1, lk1, lq2, lk2, gain, batch, seq, lam_init, tq):
    t = kr.shape[0]
    nq = seq // tq
    vec = lambda n: pl.BlockSpec((1, n), lambda b, h, i: (0, 0))
    return pl.pallas_call(
        functools.partial(_attn_kernel, tq=tq, lam_init=lam_init),
        grid=(batch, DA_HEADS, nq),
        in_specs=[pl.BlockSpec((LANES, tq), lambda b, h, i: (h, b * nq + i)),
                  pl.BlockSpec((seq, LANES), lambda b, h, i: (b, h)),
                  pl.BlockSpec((nq, LANES, tq), lambda b, h, i: (b, h, 0)),
                  vec(HEAD_DIM), vec(HEAD_DIM), vec(HEAD_DIM), vec(HEAD_DIM),
                  pl.BlockSpec((LANES, 1), lambda b, h, i: (0, 0))],
        out_specs=pl.BlockSpec((tq, LANES), lambda b, h, i: (b * nq + i, h)),
        out_shape=jax.ShapeDtypeStruct((t, DA_WIDTH), F32),
        scratch_shapes=([pltpu.VMEM((LANES, tq), F32)] * 2 + [pltpu.VMEM((1, tq), F32)] * 4
                        + [pltpu.VMEM((tq, tq), F32)] * 4),
        compiler_params=_cparams(("arbitrary", "arbitrary", "arbitrary")),
    )(qt, kr, vt3, lq1.reshape(1, -1), lk1.reshape(1, -1), lq2.reshape(1, -1), lk2.reshape(1, -1),
      gain.reshape(-1, 1))


def _short_conv_tile(b, c, h, cp, hp, w, first):
    z = c * h
    zp = jnp.where(first, jnp.zeros_like(cp), cp * hp)
    row = lax.broadcasted_iota(jnp.int32, (z.shape[0], 1), 0)
    z1 = jnp.where(row == 0, zp[7:8, :], pltpu.roll(z, 1, 0))
    z2 = pltpu.roll(z, 2, 0)
    z2 = jnp.where(row == 0, zp[6:7, :], jnp.where(row == 1, zp[7:8, :], z2))
    return b * (w[0:1, :] * z2 + w[1:2, :] * z1 + w[2:3, :] * z)


def _split3(x):
    hi = x.astype(BF16)
    r = x - hi.astype(F32)
    mid = r.astype(BF16)
    lo = (r - mid.astype(F32)).astype(BF16)
    return hi, mid, lo


def _gla_kernel(q_ref, k_ref, v_ref, r_ref, lr_ref, wg_ref, bg_ref, gn_ref, o_ref, st_ref, *, tc):
    ch = GLA_CHUNK

    @pl.when(pl.program_id(1) == 0)
    def _():
        st_ref[...] = jnp.zeros_like(st_ref)

    ii = lax.broadcasted_iota(jnp.int32, (ch, ch), 0)
    jj = lax.broadcasted_iota(jnp.int32, (ch, ch), 1)
    tri = (ii >= jj).astype(BF16)
    rk = lax.broadcasted_iota(jnp.int32, (GLA_VW, GLA_KW), 0)
    ck = lax.broadcasted_iota(jnp.int32, (GLA_VW, GLA_KW), 1)
    bd_k = (rk // GLA_DV) == (ck // GLA_DK)
    rv = lax.broadcasted_iota(jnp.int32, (GLA_VW, GLA_VW), 0)
    cv = lax.broadcasted_iota(jnp.int32, (GLA_VW, GLA_VW), 1)
    bd_v = (rv // GLA_DV) == (cv // GLA_DV)
    seg = bd_v.astype(BF16)
    sb = GLA_SUB
    nsb = ch // sb
    hs = GLA_HEADS * sb
    iota = lambda shape, axis: lax.broadcasted_iota(jnp.int32, shape, axis)
    sub_row = iota((sb, 1), 0)
    head_expand = ((iota((GLA_KW, GLA_VW), 0) // GLA_DK)
                   == (iota((GLA_KW, GLA_VW), 1) // GLA_DV)).astype(BF16)
    sum_j = ((iota((ch, ch * sb), 1) // sb) == iota((ch, ch * sb), 0)).astype(BF16)
    bd_ks = (iota((hs, GLA_KW), 0) // sb) == (iota((hs, GLA_KW), 1) // GLA_DK)
    bd_vs = (iota((hs, GLA_VW), 0) // sb) == (iota((hs, GLA_VW), 1) // GLA_DV)
    off_w = (nsb - 1) * hs
    att_keep = iota((ch, off_w), 0) >= (iota((ch, off_w), 1) // hs + 1) * sb
    dn_t = (((1,), (1,)), ((), ()))
    scale = GLA_DK ** -0.5

    def stage_decay(rows):
        x = jnp.dot(lr_ref[rows, :].astype(BF16), wg_ref[...], preferred_element_type=F32) + bg_ref[...]
        la = (jnp.minimum(x, 0.0) - jnp.log1p(jnp.exp(-jnp.abs(x)))) * (1.0 / GLA_GATE_TEMP)
        hi, mid, lo = _split3(la)
        return (jnp.dot(tri, hi, preferred_element_type=F32) + jnp.dot(tri, mid, preferred_element_type=F32)
                + jnp.dot(tri, lo, preferred_element_type=F32))

    def stage_pairwise(q, k, v, b):
        pieces, vrep = [], []
        for sbi in range(nsb):
            rs = slice(sbi * sb, (sbi + 1) * sb)
            qs, ks, bs = q[rs, :], k[rs, :], b[rs, :]
            for i in range(sb):
                dec = jnp.exp(jnp.where(sub_row <= i, bs[i:i + 1, :] - bs, NEG_INF))
                pieces.append(qs[i:i + 1, :] * ks * dec)
            vrep += [v[rs, :]] * sb
        x3 = jnp.concatenate(pieces, axis=0)
        x3h = x3.astype(BF16)
        return x3h, (x3 - x3h.astype(F32)).astype(BF16), jnp.concatenate(vrep, axis=0)

    def stage_head_sums(x3h, x3l):
        return (jnp.dot(x3h, head_expand, preferred_element_type=F32)
                + jnp.dot(x3l, head_expand, preferred_element_type=F32))

    def stage_cross_operands(q, k, vb, b):
        qts, kbds, vbds = [], [], []
        zeros_k = jnp.zeros((hs, GLA_KW), BF16)
        for sbi in range(nsb - 1):
            rs = slice(sbi * sb, (sbi + 1) * sb)
            end = (sbi + 1) * sb
            b_end = b[end - 1:end, :]
            qts.append((q * jnp.exp(jnp.minimum(b - b_end, 0.0))).astype(BF16))
            kt = (k[rs, :] * jnp.exp(b_end - b[rs, :])).astype(BF16)
            kbd = jnp.where(bd_ks, jnp.concatenate([kt] * GLA_HEADS, axis=0), jnp.zeros((), BF16))
            kbds.append(jnp.concatenate([kbd if s2 == sbi else zeros_k for s2 in range(nsb - 1)], axis=1))
            vbds.append(jnp.where(bd_vs, jnp.concatenate([vb[rs, :]] * GLA_HEADS, axis=0),
                                  jnp.zeros((), BF16)))
        return jnp.concatenate(qts, axis=1), jnp.concatenate(kbds, axis=0), jnp.concatenate(vbds, axis=0)

    def stage_cross_scores(q_all, k_all):
        att = lax.dot_general(q_all, k_all, dn_t, preferred_element_type=F32)
        return jnp.where(att_keep, att, 0.0).astype(BF16)

    def stage_state_update(k, v, b):
        b_last = b[ch - 1:ch, :]
        kl = (k * jnp.exp(b_last - b)).astype(BF16)
        ut = jnp.dot(v.T.astype(BF16), kl, preferred_element_type=F32)
        return jnp.exp(b_last), jnp.where(bd_k, ut, 0.0)

    def stage_mean_square(o):
        o2 = o * o
        o2h = o2.astype(BF16)
        o2l = (o2 - o2h.astype(F32)).astype(BF16)
        return (jnp.dot(o2h, seg, preferred_element_type=F32)
                + jnp.dot(o2l, seg, preferred_element_type=F32)) * (1.0 / GLA_DV)

    def stage_store(rows, o, ms):
        r = r_ref[rows, :]
        gate = r * (1.0 / (1.0 + jnp.exp(-r)))
        o_ref[rows, :] = o * lax.rsqrt(ms + EPS) * gn_ref[...] * gate

    each = lambda f, *cols: [f(*args) for args in zip(*cols)]
    gsize = min(GLA_GROUP, tc // ch)

    def group(g, carry):
        rows = [pl.ds(pl.multiple_of((g * gsize + c) * ch, ch), ch) for c in range(gsize)]
        b = each(stage_decay, rows)
        q = [q_ref[r, :] * scale for r in rows]
        k = [k_ref[r, :] for r in rows]
        v = [v_ref[r, :] for r in rows]
        vb = [x.astype(BF16) for x in v]
        x3h, x3l, vrep = zip(*each(stage_pairwise, q, k, v, b))
        w = each(stage_head_sums, x3h, x3l)
        wv = each(lambda w_, v_: (w_ * v_).astype(BF16), w, vrep)
        o = each(lambda wv_: jnp.dot(sum_j, wv_, preferred_element_type=F32), wv)
        q_all, k_all, v_all = zip(*each(stage_cross_operands, q, k, vb, b))
        att = each(stage_cross_scores, q_all, k_all)
        o = each(lambda o_, a_, v_: o_ + jnp.dot(a_, v_, preferred_element_type=F32), o, att, v_all)
        bq = each(lambda q_, b_: (q_ * jnp.exp(b_)).astype(BF16), q, b)
        decay, ut = zip(*each(stage_state_update, k, v, b))
        st = st_ref[...]
        states = []
        for c in range(gsize):
            states.append(st.astype(BF16))
            st = st * decay[c] + ut[c]
        st_ref[...] = st
        o = each(lambda o_, q_, s_: o_ + lax.dot_general(q_, s_, dn_t, preferred_element_type=F32),
                 o, bq, states)
        ms = each(stage_mean_square, o)
        each(stage_store, rows, o, ms)
        return carry

    lax.fori_loop(0, tc // ch // gsize, group, 0)


def _gla(proj, wg, bg, gn, batch, seq, tc):
    t = proj.shape[0]
    nb = seq // tc
    blk = lambda w, j: pl.BlockSpec((tc, w), lambda b, i, j=j: (b * nb + i, j))
    const = lambda r, c: pl.BlockSpec((r, c), lambda b, i: (0, 0))
    wg_pad = jnp.zeros((LANES, GLA_KW), F32).at[:GLA_GATE_RANK].set(wg).astype(BF16)
    gn_row = jnp.tile(gn, GLA_HEADS).reshape(1, GLA_VW)
    return pl.pallas_call(
        functools.partial(_gla_kernel, tc=tc),
        grid=(batch, nb),
        in_specs=[blk(GLA_KW, COL_G_Q), blk(GLA_KW, COL_G_K), blk(GLA_VW, COL_G_V), blk(GLA_VW, COL_G_R),
                  blk(LANES, COL_G_LR), const(LANES, GLA_KW), const(1, GLA_KW), const(1, GLA_VW)],
        out_specs=pl.BlockSpec((tc, GLA_VW), lambda b, i: (b * nb + i, 0)),
        out_shape=jax.ShapeDtypeStruct((t, GLA_VW), F32),
        scratch_shapes=[pltpu.VMEM((GLA_VW, GLA_KW), F32)],
        compiler_params=_cparams(("arbitrary", "arbitrary")),
    )(proj, proj, proj, proj, proj, wg_pad, bg.reshape(1, GLA_KW), gn_row)


def _out_proj_kernel(x_ref, a_ref, b_ref, c_ref, h_ref, cp_ref, hp_ref, cw_ref, g_ref, w_ref, o_ref,
                     *, blocks_per_seq):
    first = (pl.program_id(0) % blocks_per_seq) == 0
    o_sc = _short_conv_tile(b_ref[...], c_ref[...], h_ref[...], cp_ref[...], hp_ref[...], cw_ref[...], first)
    acc = jnp.dot(a_ref[...].astype(BF16), w_ref[0:DA_WIDTH, :], preferred_element_type=F32)
    acc += jnp.dot(o_sc.astype(BF16), w_ref[DA_WIDTH:DA_WIDTH + SC_WIDTH, :],
                   preferred_element_type=F32)
    acc += jnp.dot(g_ref[...].astype(BF16), w_ref[DA_WIDTH + SC_WIDTH:, :], preferred_element_type=F32)
    o_ref[...] = x_ref[...] + acc


def _out_proj(x, o_da, proj, conv_w, o_g, w, seq, tm):
    t, d = x.shape
    rb = tm // SUBLANES
    blk = lambda n: pl.BlockSpec((tm, n), lambda i: (i, 0))
    cur = lambda j: pl.BlockSpec((tm, SC_WIDTH), lambda i, j=j: (i, j))
    prev = lambda j: pl.BlockSpec((SUBLANES, SC_WIDTH), lambda i, j=j: (jnp.maximum(i * rb - 1, 0), j))
    return pl.pallas_call(
        functools.partial(_out_proj_kernel, blocks_per_seq=seq // tm),
        grid=(t // tm,),
        in_specs=[blk(d), blk(DA_WIDTH),
                  cur(COL_SC_B), cur(COL_SC_C), cur(COL_SC_H), prev(COL_SC_C), prev(COL_SC_H),
                  pl.BlockSpec((CONV_W, SC_WIDTH), lambda i: (0, 0)),
                  blk(GLA_VW), pl.BlockSpec(w.shape, lambda i: (0, 0))],
        out_specs=blk(d),
        out_shape=jax.ShapeDtypeStruct((t, d), F32),
        compiler_params=_cparams(("arbitrary",)),
    )(x, o_da, proj, proj, proj, proj, proj, conv_w, o_g, w)


def _merge_exchange_network(n):
    pairs = []
    t = max(1, math.ceil(math.log2(n)))
    p = 2 ** (t - 1)
    while p > 0:
        q, r, d = 2 ** (t - 1), 0, p
        while d > 0:
            pairs += [(i, i + d) for i in range(n - d) if (i & p) == r]
            d, q, r = q - p, q // 2, p
        p //= 2
    return pairs


def _top_desc(x, n):
    groups = [x[SUBLANES * i:SUBLANES * (i + 1), :] for i in range(x.shape[0] // SUBLANES)]
    for i, j in _merge_exchange_network(len(groups)):
        groups[i], groups[j] = jnp.maximum(groups[i], groups[j]), jnp.minimum(groups[i], groups[j])
    vals = []
    for r in range(n):
        m = jnp.max(groups[0], axis=0, keepdims=True)
        vals.append(m)
        hit = groups[0] == m
        live = min(len(groups), n - r)
        for k in range(live - 1):
            groups[k] = jnp.where(hit, groups[k + 1], groups[k])
        if live == len(groups) and live > 0 and r + 1 < n:
            groups[live - 1] = jnp.where(hit, NEG_INF, groups[live - 1])
    return vals


def _count_true_prefix(test, rows):
    assert len(rows) == 16
    pick = lambda c, a, b: jnp.where(c, a, b)
    c8 = test(rows[7])
    c4 = test(pick(c8, rows[11], rows[3]))
    c2 = test(pick(c8, pick(c4, rows[13], rows[9]), pick(c4, rows[5], rows[1])))
    hi = pick(c4, pick(c2, rows[14], rows[12]), pick(c2, rows[10], rows[8]))
    lo = pick(c4, pick(c2, rows[6], rows[4]), pick(c2, rows[2], rows[0]))
    c1 = test(pick(c8, hi, lo))
    count = (pick(c8, 8.0, 0.0) + pick(c4, 4.0, 0.0)) + (pick(c2, 2.0, 0.0) + pick(c1, 1.0, 0.0))
    return pick(test(rows[15]), 16.0, count)


def _ranks_of(x, vals):
    return _count_true_prefix(lambda v: v > x, vals)


def _peer_route_kernel(x_ref, g_ref, wq_ref, k1_ref, k2_ref, a1_ref, cnt_ref, a2_ref, rank2_ref):
    k = PEER_TOPK
    tt = x_ref.shape[0]
    xn = _rms(x_ref[...], g_ref[...]).astype(BF16)
    q = jnp.dot(xn, wq_ref[...], preferred_element_type=F32)
    dn = (((1,), (1,)), ((), ()))
    for h in range(PEER_HEADS):
        qa = q[:, (2 * h) * LANES:(2 * h + 1) * LANES].astype(BF16)
        qb = q[:, (2 * h + 1) * LANES:(2 * h + 2) * LANES].astype(BF16)
        s1 = lax.dot_general(k1_ref[...], qa, dn, preferred_element_type=F32)
        s2 = lax.dot_general(k2_ref[...], qb, dn, preferred_element_type=F32)
        v1 = _top_desc(s1, k)
        v2 = _top_desc(s2, k)
        rank2 = _ranks_of(s2, v2)
        cands = [v1[i] + v2[j] for i in range(k) for j in range(k // (i + 1))]
        pad = (-len(cands)) % 8
        cands += [jnp.full((1, tt), NEG_INF, F32)] * pad
        top = _top_desc(jnp.concatenate(cands, axis=0), k)
        z = jnp.ones((1, tt), F32)
        for r in range(1, k):
            z = z + jnp.exp(top[r] - top[0])
        thr = top[k - 1]
        cnt = _count_true_prefix(lambda v: (s1 + v) >= thr, v2)
        cnt_ref[h] = jnp.where(s1 >= v1[k - 1], cnt, 0.0)
        rank2_ref[h] = rank2.astype(BF16)
        a1_ref[h] = jnp.exp(s1 - v1[0]) * (1.0 / z)
        a2_ref[h] = jnp.exp(s2 - v2[0]).astype(BF16)


def _peer_route(x, g, wq, k1, k2, tt):
    t, d = x.shape
    shape = (PEER_HEADS, PEER_NKEYS, t)
    out = pl.BlockSpec((PEER_HEADS, PEER_NKEYS, tt), lambda i: (0, 0, i))
    f32 = jax.ShapeDtypeStruct(shape, F32)
    bf16 = jax.ShapeDtypeStruct(shape, BF16)
    return pl.pallas_call(
        _peer_route_kernel,
        grid=(t // tt,),
        in_specs=[pl.BlockSpec((tt, d), lambda i: (i, 0)),
                  pl.BlockSpec((1, d), lambda i: (0, 0)),
                  pl.BlockSpec(wq.shape, lambda i: (0, 0)),
                  pl.BlockSpec(k1.shape, lambda i: (0, 0)),
                  pl.BlockSpec(k2.shape, lambda i: (0, 0))],
        out_specs=[out, out, out, out],
        out_shape=[f32, f32, bf16, bf16],
        compiler_params=_cparams(("arbitrary",)),
    )(x, g.reshape(1, d), wq, k1, k2)


def _peer_dense_kernel(x_ref, g_ref, u_ref, vt_ref, a1_ref, cnt_ref, a2_ref, rank2_ref, gout_ref, o_ref,
                       xnt_ref, acc_ref, hta_ref, htb_ref, pta_ref, ptb_ref, *, norm_out):
    e = pl.program_id(1)

    @pl.when(e == 0)
    def _():
        xn = _rms(x_ref[...], g_ref[...])
        xnt_ref[...] = xn.T.astype(BF16)
        acc_ref[...] = jnp.zeros_like(acc_ref)

    zero = jnp.zeros((), BF16)
    n_q, job_rows, _ = u_ref.shape
    sub = xnt_ref.shape[1] // 2
    n1_per_job = job_rows // PEER_NKEYS
    pack = BF16_SUBLANES
    groups = PEER_NKEYS // pack
    col_a, col_b = slice(0, sub), slice(sub, 2 * sub)
    ht_refs = {col_a: hta_ref, col_b: htb_ref}
    pt_refs = {col_a: pta_ref, col_b: ptb_ref}

    def matmul1(q, cols):
        ht_refs[cols][...] = jnp.dot(u_ref[q], xnt_ref[:, cols], preferred_element_type=F32)

    def gate(q, cols):
        ht_ref, pt_ref = ht_refs[cols], pt_refs[cols]
        for rl in range(n1_per_job):
            r = q * n1_per_job + rl
            w = None
            for h in range(PEER_HEADS):
                cnt = jnp.broadcast_to(cnt_ref[h, pl.ds(r, 1), cols], (pack, sub)).astype(BF16)
                a1 = jnp.broadcast_to(a1_ref[h, pl.ds(r, 1), cols], (pack, sub)).astype(BF16)
                sel = rank2_ref[h, :, :, cols] < cnt[None]
                term = a1[None] * jnp.where(sel, a2_ref[h, :, :, cols], zero)
                w = term if w is None else w + term
            rows = slice(rl * PEER_NKEYS, (rl + 1) * PEER_NKEYS)
            hb = ht_ref[rows, :]
            gelu = hb * (1.0 + lax.erf(hb))
            p = w * gelu.astype(BF16).reshape(groups, pack, sub)
            pt_ref[rows, :] = p.reshape(PEER_NKEYS, sub)

    def matmul2(q, cols):
        acc_ref[:, cols] += jnp.dot(vt_ref[q], pt_refs[cols][...], preferred_element_type=F32)

    def pair(q, has_prev, has_next):
        matmul1(q, col_b)
        gate(q, col_a)
        if has_prev:
            matmul2(q - 1, col_b)
        if has_next:
            matmul1(q + 1, col_a)
        gate(q, col_b)
        matmul2(q, col_a)

    matmul1(0, col_a)
    if n_q == 1:
        pair(0, False, False)
    else:
        pair(0, False, True)

        def body(q, carry):
            pair(q, True, True)
            return carry

        lax.fori_loop(1, n_q - 1, body, 0)
        pair(n_q - 1, True, False)
    matmul2(n_q - 1, col_b)

    @pl.when(e == pl.num_programs(1) - 1)
    def _():
        y = x_ref[...] + acc_ref[...].T
        o_ref[...] = _rms(y, gout_ref[...]) if norm_out else y


def _expert_jobs_kernel(u_ref, v_ref, uo_ref, vo_ref):
    uo_ref[0] = (u_ref[0] * INV_SQRT2).astype(BF16)
    vo_ref[0] = (v_ref[0].T * (0.5 / INV_SQRT2)).astype(BF16)


def _expert_jobs(u, v, layer):
    _, n_exp, d = u.shape
    n_jobs = n_exp // PEER_JOB_ROWS
    blk = pl.BlockSpec((1, PEER_JOB_ROWS, d), lambda j: (layer, j, 0))
    return pl.pallas_call(
        _expert_jobs_kernel,
        grid=(n_jobs,),
        in_specs=[blk, blk],
        out_specs=[pl.BlockSpec((1, PEER_JOB_ROWS, d), lambda j: (j, 0, 0)),
                   pl.BlockSpec((1, d, PEER_JOB_ROWS), lambda j: (j, 0, 0))],
        out_shape=[jax.ShapeDtypeStruct((n_jobs, PEER_JOB_ROWS, d), BF16),
                   jax.ShapeDtypeStruct((n_jobs, d, PEER_JOB_ROWS), BF16)],
        compiler_params=_cparams(("arbitrary",)),
    )(u, v)


def _peer_dense(x, g, u, vt, a1t, cnt, a2t, rank2, g_out, norm_out, tt, et):
    t, d = x.shape
    n_jobs, job_rows, _ = u.shape
    n_q = et // job_rows
    n1_per_tile = et // PEER_NKEYS
    groups = PEER_NKEYS // BF16_SUBLANES
    grouped = (PEER_HEADS, groups, BF16_SUBLANES, t)
    full = pl.BlockSpec((PEER_HEADS, groups, BF16_SUBLANES, tt), lambda i, e: (0, 0, 0, i))
    part = pl.BlockSpec((PEER_HEADS, n1_per_tile, tt), lambda i, e: (0, e, i))
    sub = tt // 2
    row = pl.BlockSpec((1, d), lambda i, e: (0, 0))
    return pl.pallas_call(
        functools.partial(_peer_dense_kernel, norm_out=norm_out),
        grid=(t // tt, n_jobs // n_q),
        in_specs=[pl.BlockSpec((tt, d), lambda i, e: (i, 0)),
                  row,
                  pl.BlockSpec((n_q, job_rows, d), lambda i, e: (e, 0, 0)),
                  pl.BlockSpec((n_q, d, job_rows), lambda i, e: (e, 0, 0)),
                  part, part, full, full, row],
        out_specs=pl.BlockSpec((tt, d), lambda i, e: (i, 0)),
        out_shape=jax.ShapeDtypeStruct((t, d), F32),
        scratch_shapes=[pltpu.VMEM((d, tt), BF16), pltpu.VMEM((d, tt), F32),
                        pltpu.VMEM((job_rows, sub), F32), pltpu.VMEM((job_rows, sub), F32),
                        pltpu.VMEM((job_rows, sub), BF16), pltpu.VMEM((job_rows, sub), BF16)],
        compiler_params=_cparams(("arbitrary", "arbitrary"), vmem_mb=VMEM_LIMIT_DENSE_MB),
    )(x, g.reshape(1, d), u, vt, a1t, cnt, a2t.reshape(grouped), rank2.reshape(grouped),
      g_out.reshape(1, d))


def _tiles(seq):
    pick = lambda want: min(want, seq)
    return dict(tm=pick(1024), tq=pick(512), tgla=pick(512), tt=pick(256), td=pick(512),
                et=2048)


def kernel(x, positions, norm_mix, w_in, lam_q1, lam_k1, lam_q2, lam_k2, diff_norm, conv_w, gla_w_gate2, gla_b_gate, gla_norm, w_out, norm_ffn, peer_w_q, peer_keys1, peer_keys2, peer_u, peer_v, norm_final):
    batch, seq, d = x.shape
    t = batch * seq
    depth = w_in.shape[0]
    tl = _tiles(seq)
    xt = x.reshape(t, d)
    tabs = _rope_tables(positions, tl["tm"])
    for i in range(depth):
        w_in_p = jnp.pad(w_in[i], ((0, 0), (0, IN_PAD - IN_WIDTH))).astype(BF16)
        qt, kr, vt3, proj = _in_proj(xt, norm_mix[i], w_in_p, tabs, tl["tq"])
        lam_init = 0.8 - 0.6 * math.exp(-0.3 * i)
        o_da = _diff_attention(qt, kr, vt3, lam_q1[i], lam_k1[i], lam_q2[i], lam_k2[i], diff_norm[i],
                               batch, seq, lam_init, tl["tq"])
        o_g = _gla(proj, gla_w_gate2[i], gla_b_gate[i], gla_norm[i], batch, seq, tl["tgla"])
        xt = _out_proj(xt, o_da, proj, conv_w[i], o_g, w_out[i].astype(BF16), seq, tl["tm"])
        a1t, cnt, a2t, rank2 = _peer_route(xt, norm_ffn[i], peer_w_q[i].astype(BF16),
                                           peer_keys1[i].astype(BF16), peer_keys2[i].astype(BF16), tl["tt"])
        u3, ev3 = _expert_jobs(peer_u, peer_v, i)
        xt = _peer_dense(xt, norm_ffn[i], u3, ev3, a1t, cnt, a2t, rank2, norm_final, i == depth - 1,
                         tl["td"], tl["et"])
    return xt.reshape(batch, seq, d)
```
